```python
import jax, jax.numpy as jnp
from jax import lax
import numpy as np

D_MODEL = 2048
BATCH = 4
SEQ = 2048
DEPTH = 1

CHUNK = 64
PLE_DIM = 256
LRU_WIDTH = D_MODEL // 2
LRU_HEADS = 16
LRU_HEAD_DIM = LRU_WIDTH // LRU_HEADS
CONV_WIDTH = 4
LRU_C = 8.0
POOL_WIDTH = D_MODEL // 2
POOL_WINDOWS = (2, 4, 8, 16)
POOL_GROUPS = len(POOL_WINDOWS)
POOL_GROUP_DIM = POOL_WIDTH // POOL_GROUPS
N_BRANCHES = 2
IN_COLS = 2 * LRU_WIDTH + POOL_WIDTH + N_BRANCHES * D_MODEL
N_GROUPS = 4
EXPERTS_PER_GROUP = 4
N_EXPERTS = N_GROUPS * EXPERTS_PER_GROUP
TOP_K = 2
D_EXPERT = D_MODEL // 4
EPS = 1e-6

kernel_name = "hybrid_rglru_pool_hmoe_block"


def _rms_norm(x, g):
    xf = x.astype(jnp.float32)
    y = xf * lax.rsqrt(jnp.mean(xf * xf, axis=-1, keepdims=True) + EPS) * g.astype(jnp.float32)
    return y.astype(x.dtype)


def _combine(c1, c2):
    a1, b1 = c1
    a2, b2 = c2
    return a1 * a2, a2 * b1 + b2


def _chunked_linear_scan(a, b):
    bsz, s, c = a.shape
    nc = s // CHUNK
    a_c = a.reshape(bsz, nc, CHUNK, c)
    b_c = b.reshape(bsz, nc, CHUNK, c)
    a_cum, h_loc = lax.associative_scan(_combine, (a_c, b_c), axis=2)

    def step(h_prev, inp):
        acum, hloc = inp
        hs = hloc + acum * h_prev[:, None, :]
        return hs[:, -1], hs

    h0 = jnp.zeros((bsz, c), jnp.float32)
    _, hs = lax.scan(step, h0, (a_cum.transpose(1, 0, 2, 3), h_loc.transpose(1, 0, 2, 3)))
    return hs.transpose(1, 0, 2, 3).reshape(bsz, s, c)


def _rglru_branch(x_rnn, g_rnn, conv_w, conv_b, w_rg_a, b_rg_a, w_rg_x, b_rg_x, lru_lambda):
    dt = x_rnn.dtype
    bsz, s, c = x_rnn.shape
    xc = lax.conv_general_dilated(x_rnn, conv_w, window_strides=(1,), padding=[(CONV_WIDTH - 1, 0)],
                                  dimension_numbers=("NWC", "WIO", "NWC"),
                                  feature_group_count=c) + conv_b
    xh = xc.reshape(bsz, s, LRU_HEADS, LRU_HEAD_DIM)
    r = jax.nn.sigmoid((jnp.einsum("bshi,hij->bshj", xh, w_rg_a) + b_rg_a).astype(jnp.float32))
    ig = jax.nn.sigmoid((jnp.einsum("bshi,hij->bshj", xh, w_rg_x) + b_rg_x).astype(jnp.float32))
    r = r.reshape(bsz, s, c)
    ig = ig.reshape(bsz, s, c)
    log_a = -LRU_C * r * jax.nn.softplus(-lru_lambda.astype(jnp.float32))
    a = jnp.exp(log_a)
    mult = jnp.sqrt(-jnp.expm1(2.0 * log_a))
    h = _chunked_linear_scan(a, mult * ig * xc.astype(jnp.float32))
    return (h * jax.nn.gelu(g_rnn.astype(jnp.float32))).astype(dt)


def _pool_branch(x_pool, w_pool, pool_scale):
    dt = x_pool.dtype
    bsz, s, c = x_pool.shape
    xf = x_pool.astype(jnp.float32)
    cs = jnp.concatenate([jnp.zeros((bsz, 1, c), jnp.float32), jnp.cumsum(xf, axis=1)], axis=1)
    t = jnp.arange(s)
    outs = []
    for g, w in enumerate(POOL_WINDOWS):
        sl = slice(g * POOL_GROUP_DIM, (g + 1) * POOL_GROUP_DIM)
        lo = jnp.maximum(t + 1 - w, 0)
        win_sum = cs[:, 1:, sl] - cs[:, lo, sl]
        cnt = (t + 1 - lo).astype(jnp.float32)[None, :, None]
        outs.append(win_sum / cnt - xf[..., sl])
    d = jnp.stack(outs, axis=2).astype(dt)
    y = jnp.einsum("bsgc,gcd->bsgd", d, w_pool).reshape(bsz, s, c)
    return y * pool_scale


def _mixer(x, norm1_g, w_in, conv_w, conv_b, w_rg_a, b_rg_a, w_rg_x, b_rg_x, lru_lambda,
           w_pool, pool_scale, w_branch_a, w_branch_b, w_out):
    h = _rms_norm(x, norm1_g)
    z = h @ w_in
    o1 = LRU_WIDTH
    o2 = 2 * LRU_WIDTH
    o3 = o2 + POOL_WIDTH
    o4 = o3 + D_MODEL
    x_rnn, g_rnn, x_pool = z[..., :o1], z[..., o1:o2], z[..., o2:o3]
    gate_a, gate_b = z[..., o3:o4], z[..., o4:]
    y_a = _rglru_branch(x_rnn, g_rnn, conv_w, conv_b, w_rg_a, b_rg_a, w_rg_x, b_rg_x, lru_lambda)
    y_b = _pool_branch(x_pool, w_pool, pool_scale)
    u = jax.nn.sigmoid(gate_a) * (y_a @ w_branch_a) + jax.nn.sigmoid(gate_b) * (y_b @ w_branch_b)
    return u @ w_out


def _hier_moe(x, norm2_g, w_router_group, b_router_group, w_router_expert, b_router_expert,
              w_e_gate, w_e_up, w_e_down):
    dt = x.dtype
    bsz, s, d = x.shape
    ht = _rms_norm(x, norm2_g).reshape(bsz * s, d)
    lg = (ht @ w_router_group + b_router_group).astype(jnp.float32)
    pg = jax.nn.softmax(lg, axis=-1)
    g_idx = jnp.argmax(lg, axis=-1)
    g_w = jnp.take_along_axis(pg, g_idx[:, None], axis=1)[:, 0]
    le = (ht @ w_router_expert + b_router_expert).astype(jnp.float32)
    le = le.reshape(-1, N_GROUPS, EXPERTS_PER_GROUP)
    le_sel = jnp.take_along_axis(le, g_idx[:, None, None], axis=1)[:, 0]
    pe = jax.nn.softmax(le_sel, axis=-1)
    top_v, top_i = lax.top_k(pe, TOP_K)
    top_v = top_v / jnp.sum(top_v, axis=-1, keepdims=True)
    e_idx = g_idx[:, None] * EXPERTS_PER_GROUP + top_i
    wts = g_w[:, None] * top_v
    comb = jnp.sum(jax.nn.one_hot(e_idx, N_EXPERTS, dtype=jnp.float32) * wts[..., None], axis=1)
    hg = jnp.einsum("td,edf->tef", ht, w_e_gate)
    hu = jnp.einsum("td,edf->tef", ht, w_e_up)
    act = jax.nn.silu(hg) * hu * comb[:, :, None].astype(dt)
    y = jnp.einsum("tef,efd->td", act, w_e_down)
    return y.reshape(bsz, s, d)


def _per_layer_embed(x, p_i, norm_ple_g, w_ple_gate, w_ple_proj):
    e = p_i.astype(x.dtype) @ w_ple_proj
    g = jax.nn.sigmoid(_rms_norm(x, norm_ple_g) @ w_ple_gate)
    return g * e


def setup_inputs(seed: int = 0) -> dict:
    key = jax.random.key(seed)
    ks = iter(jax.random.split(key, 40))

    def nrm(shape, scale):
        return jax.random.normal(next(ks), shape, jnp.float32) * scale

    a0 = jax.random.uniform(next(ks), (DEPTH, LRU_WIDTH), jnp.float32, minval=0.9, maxval=0.999)
    sa = a0 ** (1.0 / LRU_C)
    lru_lambda = jnp.log(sa) - jnp.log1p(-sa)
    return {
        "x": nrm((BATCH, SEQ, D_MODEL), 1.0),
        "p": nrm((DEPTH, BATCH, SEQ, PLE_DIM), 1.0),
        "norm1_g": 1.0 + nrm((DEPTH, D_MODEL), 0.02),
        "w_in": nrm((DEPTH, D_MODEL, IN_COLS), D_MODEL ** -0.5),
        "conv_w": nrm((DEPTH, CONV_WIDTH, 1, LRU_WIDTH), CONV_WIDTH ** -0.5),
        "conv_b": nrm((DEPTH, LRU_WIDTH), 0.01),
        "w_rg_a": nrm((DEPTH, LRU_HEADS, LRU_HEAD_DIM, LRU_HEAD_DIM), LRU_HEAD_DIM ** -0.5),
        "b_rg_a": nrm((DEPTH, LRU_HEADS, LRU_HEAD_DIM), 0.01),
        "w_rg_x": nrm((DEPTH, LRU_HEADS, LRU_HEAD_DIM, LRU_HEAD_DIM), LRU_HEAD_DIM ** -0.5),
        "b_rg_x": nrm((DEPTH, LRU_HEADS, LRU_HEAD_DIM), 0.01),
        "lru_lambda": lru_lambda,
        "w_pool": nrm((DEPTH, POOL_GROUPS, POOL_GROUP_DIM, POOL_GROUP_DIM), POOL_GROUP_DIM ** -0.5),
        "pool_scale": 1.0 + nrm((DEPTH, POOL_WIDTH), 0.02),
        "w_branch_a": nrm((DEPTH, LRU_WIDTH, D_MODEL), LRU_WIDTH ** -0.5),
        "w_branch_b": nrm((DEPTH, POOL_WIDTH, D_MODEL), POOL_WIDTH ** -0.5),
        "w_out": nrm((DEPTH, D_MODEL, D_MODEL), D_MODEL ** -0.5),
        "norm2_g": 1.0 + nrm((DEPTH, D_MODEL), 0.02),
        "w_router_group": nrm((DEPTH, D_MODEL, N_GROUPS), D_MODEL ** -0.5),
        "b_router_group": nrm((DEPTH, N_GROUPS), 0.01),
        "w_router_expert": nrm((DEPTH, D_MODEL, N_EXPERTS), D_MODEL ** -0.5),
        "b_router_expert": nrm((DEPTH, N_EXPERTS), 0.01),
        "w_e_gate": nrm((DEPTH, N_EXPERTS, D_MODEL, D_EXPERT), D_MODEL ** -0.5),
        "w_e_up": nrm((DEPTH, N_EXPERTS, D_MODEL, D_EXPERT), D_MODEL ** -0.5),
        "w_e_down": nrm((DEPTH, N_EXPERTS, D_EXPERT, D_MODEL), D_EXPERT ** -0.5),
        "norm_ple_g": 1.0 + nrm((DEPTH, D_MODEL), 0.02),
        "w_ple_gate": nrm((DEPTH, D_MODEL, D_MODEL), D_MODEL ** -0.5),
        "w_ple_proj": nrm((DEPTH, PLE_DIM, D_MODEL), PLE_DIM ** -0.5),
        "final_norm_g": 1.0 + nrm((D_MODEL,), 0.02),
    }


def reference(x, p, norm1_g, w_in, conv_w, conv_b, w_rg_a, b_rg_a, w_rg_x, b_rg_x, lru_lambda,
              w_pool, pool_scale, w_branch_a, w_branch_b, w_out, norm2_g, w_router_group,
              b_router_group, w_router_expert, b_router_expert, w_e_gate, w_e_up, w_e_down,
              norm_ple_g, w_ple_gate, w_ple_proj, final_norm_g):
    for i in range(DEPTH):
        x = x + _mixer(x, norm1_g[i], w_in[i], conv_w[i], conv_b[i], w_rg_a[i], b_rg_a[i],
                       w_rg_x[i], b_rg_x[i], lru_lambda[i], w_pool[i], pool_scale[i],
                       w_branch_a[i], w_branch_b[i], w_out[i])
        x = x + _hier_moe(x, norm2_g[i], w_router_group[i], b_router_group[i], w_router_expert[i],
                          b_router_expert[i], w_e_gate[i], w_e_up[i], w_e_down[i])
        x = x + _per_layer_embed(x, p[i], norm_ple_g[i], w_ple_gate[i], w_ple_proj[i])
    return _rms_norm(x, final_norm_g)
```

```python
import functools

import jax
import jax.numpy as jnp
from jax import lax
from jax.experimental import pallas as pl
from jax.experimental.pallas import tpu as pltpu

EPS = 1e-6
LRU_C = 8.0
CONV_WIDTH = 4
POOL_WINDOWS = (2, 4, 8, 16)
TOP_K = 2

LANES = 128
SUBLANES = 8
MXU_DIM = 256
VMEM_LIMIT_BYTES = 56 * 1024 * 1024

_BF16 = jnp.bfloat16
_F32 = jnp.float32

_PAIR_SLOT_A = (0, 0, 0, 1, 1, 3)
_PAIR_SLOT_B = (1, 2, 3, 3, 2, 2)
_N_PAIRS = len(_PAIR_SLOT_A)


def _rms(x, g):
    ms = jnp.mean(x * x, axis=-1, keepdims=True)
    return x * lax.rsqrt(ms + EPS) * g


def _const_spec(shape):
    nd = len(shape)
    return pl.BlockSpec(shape, lambda *_: (0,) * nd, pipeline_mode=pl.Buffered(1))


def _params(n_axes):
    return pltpu.CompilerParams(dimension_semantics=("arbitrary",) * n_axes,
                                vmem_limit_bytes=VMEM_LIMIT_BYTES)


def _inproj_kernel(x_ref, g_ref, w_ref, z_ref, h_ref):
    @pl.when(pl.program_id(1) == 0)
    def _():
        h_ref[...] = _rms(x_ref[...], g_ref[...]).astype(_BF16)

    z_ref[...] = jnp.dot(h_ref[...], w_ref[...], preferred_element_type=_F32)


def _inproj(x2d, g, w_bf, tm, tn):
    t, d = x2d.shape
    n = w_bf.shape[1]
    return pl.pallas_call(
        _inproj_kernel,
        grid=(t // tm, n // tn),
        in_specs=[pl.BlockSpec((tm, d), lambda i, j: (i, 0)),
                  pl.BlockSpec((1, d), lambda i, j: (0, 0)),
                  pl.BlockSpec((d, tn), lambda i, j: (0, j))],
        out_specs=pl.BlockSpec((tm, tn), lambda i, j: (i, j)),
        out_shape=jax.ShapeDtypeStruct((t, n), _F32),
        scratch_shapes=[pltpu.VMEM((tm, d), _BF16)],
        compiler_params=_params(2),
        name="inproj",
    )(x2d, g, w_bf)


def _scan_pitch(tm):
    p = -(-tm // SUBLANES)
    while p % SUBLANES != 4:
        p += 1
    return p


def _mixer_kernel(z_ref, cw_ref, cb_ref, wa_ref, ba_ref, wx_ref, bx_ref, lam_ref, wp_ref, ps_ref,
                  ya_ref, yb_ref,
                  er_ref, ep_ref, a_ref, b_ref, h_ref, car_ref, *, tm, pitch):
    s = pl.program_id(1)
    c = cb_ref.shape[1]
    n_slab = c // LANES
    hist_r = SUBLANES
    hist_p = 2 * SUBLANES

    @pl.when(s == 0)
    def _():
        er_ref[0:hist_r, :] = jnp.zeros((hist_r, c), _F32)
        ep_ref[0:hist_p, :] = jnp.zeros((hist_p, c), _F32)
        car_ref[...] = jnp.zeros(car_ref.shape, _F32)
        a_ref[:, tm:, :] = jnp.ones((n_slab, SUBLANES * pitch - tm, LANES), _F32)
        b_ref[:, tm:, :] = jnp.zeros((n_slab, SUBLANES * pitch - tm, LANES), _F32)

    @pl.when(s > 0)
    def _():
        er_ref[0:hist_r, :] = er_ref[tm:tm + hist_r, :]
        ep_ref[0:hist_p, :] = ep_ref[tm:tm + hist_p, :]

    er_ref[hist_r:hist_r + tm, :] = z_ref[:, 0:c]
    ep_ref[hist_p:hist_p + tm, :] = z_ref[:, 2 * c:3 * c]

    kvec = -LRU_C * jax.nn.softplus(-lam_ref[...])
    nblk = c // MXU_DIM
    for k in range(nblk):
        cs = slice(k * MXU_DIM, (k + 1) * MXU_DIM)
        xc = cb_ref[:, cs] + cw_ref[CONV_WIDTH - 1:CONV_WIDTH, cs] * er_ref[hist_r:hist_r + tm, cs]
        for j in range(1, CONV_WIDTH):
            xc = xc + cw_ref[CONV_WIDTH - 1 - j:CONV_WIDTH - j, cs] * er_ref[hist_r - j:hist_r - j + tm, cs]
        xcb = xc.astype(_BF16)
        r = jax.nn.sigmoid(jnp.dot(xcb, wa_ref[k], preferred_element_type=_F32) + ba_ref[:, cs])
        ig = jax.nn.sigmoid(jnp.dot(xcb, wx_ref[k], preferred_element_type=_F32) + bx_ref[:, cs])
        log_a = r * kvec[:, cs]
        a = jnp.exp(log_a)
        mult = jnp.sqrt(1.0 - a * a)
        bb = mult * ig * xc
        for q in range(MXU_DIM // LANES):
            slab = k * (MXU_DIM // LANES) + q
            a_ref[slab, 0:tm, :] = a[:, q * LANES:(q + 1) * LANES]
            b_ref[slab, 0:tm, :] = bb[:, q * LANES:(q + 1) * LANES]

    def sweep1(i, carry):
        hs, ps = carry
        nh, npd = [], []
        for sl in range(n_slab):
            av = a_ref[sl, pl.ds(i, SUBLANES, stride=pitch), :]
            bv = b_ref[sl, pl.ds(i, SUBLANES, stride=pitch), :]
            nh.append(av * hs[sl] + bv)
            npd.append(av * ps[sl])
        return tuple(nh), tuple(npd)

    zero = jnp.zeros((SUBLANES, LANES), _F32)
    one = jnp.ones((SUBLANES, LANES), _F32)
    h_end, a_end = lax.fori_loop(0, pitch, sweep1, ((zero,) * n_slab, (one,) * n_slab))

    row = lax.broadcasted_iota(jnp.int32, (SUBLANES, LANES), 0)
    h0 = []
    for sl in range(n_slab):
        hh, aa = h_end[sl], a_end[sl]
        d = 1
        while d < SUBLANES:
            hs_ = jnp.where(row >= d, pltpu.roll(hh, d, 0), 0.0)
            as_ = jnp.where(row >= d, pltpu.roll(aa, d, 0), 1.0)
            hh = aa * hs_ + hh
            aa = aa * as_
            d *= 2
        cin = car_ref[:, sl * LANES:(sl + 1) * LANES]
        full = hh + aa * cin
        h0.append(jnp.where(row >= 1, pltpu.roll(full, 1, 0), cin))
        car_ref[:, sl * LANES:(sl + 1) * LANES] = jnp.broadcast_to(full[SUBLANES - 1:SUBLANES, :],
                                                                  (SUBLANES, LANES))

    def sweep2(i, hs):
        nh = []
        for sl in range(n_slab):
            av = a_ref[sl, pl.ds(i, SUBLANES, stride=pitch), :]
            bv = b_ref[sl, pl.ds(i, SUBLANES, stride=pitch), :]
            hv = av * hs[sl] + bv
            h_ref[sl, pl.ds(i, SUBLANES, stride=pitch), :] = hv
            nh.append(hv)
        return tuple(nh)

    lax.fori_loop(0, pitch, sweep2, tuple(h0))

    for sl in range(n_slab):
        cs = slice(sl * LANES, (sl + 1) * LANES)
        g = z_ref[:, c + sl * LANES:c + (sl + 1) * LANES]
        ya_ref[:, cs] = (h_ref[sl, 0:tm, :] * jax.nn.gelu(g)).astype(_BF16)

    t_idx = (s * tm + lax.broadcasted_iota(jnp.int32, (tm, 1), 0) + 1).astype(_F32)
    n_grp = len(POOL_WINDOWS)
    gd = c // n_grp
    for gi, w in enumerate(POOL_WINDOWS):
        cs = slice(gi * gd, (gi + 1) * gd)
        e = ep_ref[:, cs]
        acc = e
        d = 1
        while d < w:
            acc = acc + pltpu.roll(acc, d, 0)
            d *= 2
        xt = e[hist_p:, :]
        cnt = jnp.minimum(t_idx, float(w))
        dd = acc[hist_p:, :] / cnt - xt
        yb = jnp.dot(dd.astype(_BF16), wp_ref[gi], preferred_element_type=_F32) * ps_ref[:, cs]
        yb_ref[:, cs] = yb.astype(_BF16)


def _mixer(z, bsz, seq, cw, cb, wa_bd, ba, wx_bd, bx, lam, wp, ps, tm):
    t = z.shape[0]
    c = cb.shape[1]
    ns = seq // tm
    pitch = _scan_pitch(tm)
    n_slab = c // LANES
    kern = functools.partial(_mixer_kernel, tm=tm, pitch=pitch)
    return pl.pallas_call(
        kern,
        grid=(bsz, ns),
        in_specs=[pl.BlockSpec((tm, 3 * c), lambda b, s: (b * ns + s, 0)),
                  _const_spec(cw.shape), _const_spec(cb.shape),
                  _const_spec(wa_bd.shape), _const_spec(ba.shape),
                  _const_spec(wx_bd.shape), _const_spec(bx.shape),
                  _const_spec(lam.shape), _const_spec(wp.shape), _const_spec(ps.shape)],
        out_specs=[pl.BlockSpec((tm, c), lambda b, s: (b * ns + s, 0)),
                   pl.BlockSpec((tm, c), lambda b, s: (b * ns + s, 0))],
        out_shape=[jax.ShapeDtypeStruct((t, c), _BF16), jax.ShapeDtypeStruct((t, c), _BF16)],
        scratch_shapes=[pltpu.VMEM((SUBLANES + tm, c), _F32),
                        pltpu.VMEM((2 * SUBLANES + tm, c), _F32),
                        pltpu.VMEM((n_slab, SUBLANES * pitch, LANES), _F32),
                        pltpu.VMEM((n_slab, SUBLANES * pitch, LANES), _F32),
                        pltpu.VMEM((n_slab, SUBLANES * pitch, LANES), _F32),
                        pltpu.VMEM((SUBLANES, c), _F32)],
        compiler_params=_params(2),
        name="mixer",
    )(z, cw, cb, wa_bd, ba, wx_bd, bx, lam, wp, ps)


def _combine_kernel(ya_ref, yb_ref, ga0_ref, ga1_ref, gb0_ref, gb1_ref, x_ref,
                    wa_ref, wb_ref, wo_ref, n2_ref, wr_ref, br_ref,
                    xe_ref, cnt_ref, cnt_s, *, tm, n_groups, epg):
    i = pl.program_id(0)
    d = x_ref.shape[1]
    half = d // 2

    @pl.when(i == 0)
    def _():
        cnt_s[...] = jnp.zeros(cnt_s.shape, _F32)

    pa = jnp.dot(ya_ref[...], wa_ref[...], preferred_element_type=_F32)
    pb = jnp.dot(yb_ref[...], wb_ref[...], preferred_element_type=_F32)
    u0 = jax.nn.sigmoid(ga0_ref[...]) * pa[:, :half] + jax.nn.sigmoid(gb0_ref[...]) * pb[:, :half]
    u1 = jax.nn.sigmoid(ga1_ref[...]) * pa[:, half:] + jax.nn.sigmoid(gb1_ref[...]) * pb[:, half:]
    u = jnp.concatenate([u0, u1], axis=1).astype(_BF16)
    x1 = x_ref[...] + jnp.dot(u, wo_ref[...], preferred_element_type=_F32)
    xe_ref[:, 0:d] = x1

    ht = _rms(x1, n2_ref[...]).astype(_BF16)
    logits = jnp.dot(ht, wr_ref[...], preferred_element_type=_F32) + br_ref[...]

    lane = lax.broadcasted_iota(jnp.int32, (tm, LANES), 1).astype(_F32)
    ninf = -jnp.inf
    big = float(LANES)

    def first_argmax(v):
        m = jnp.max(v, axis=-1, keepdims=True)
        return m, jnp.min(jnp.where(v == m, lane, big), axis=-1, keepdims=True)

    is_g = lane < float(n_groups)
    gmax, gidx = first_argmax(jnp.where(is_g, logits, ninf))
    g_w = 1.0 / jnp.sum(jnp.where(is_g, jnp.exp(logits - gmax), 0.0), axis=-1, keepdims=True)
    lo_lane = float(n_groups) + float(epg) * gidx
    in_grp = (lane >= lo_lane) & (lane < lo_lane + float(epg))
    le = jnp.where(in_grp, logits, ninf)
    m1, i1 = first_argmax(le)
    m2, i2 = first_argmax(jnp.where(lane == i1, ninf, le))
    e21 = jnp.exp(m2 - m1)
    w1 = g_w / (1.0 + e21)
    w2 = w1 * e21
    e1 = i1 - lo_lane
    e2 = i2 - lo_lane
    lo = jnp.minimum(e1, e2)
    hi = jnp.maximum(e1, e2)
    w_lo = jnp.where(e1 < e2, w1, w2)
    w_hi = jnp.where(e1 < e2, w2, w1)
    pair = jnp.where(lo == 0.0, hi - 1.0, jnp.where(lo == 1.0, 6.0 - hi, 5.0))
    swap = pair == 5.0
    w_a = jnp.where(swap, w_hi, w_lo)
    w_b = jnp.where(swap, w_lo, w_hi)
    bucket = float(_N_PAIRS) * gidx + pair

    onehot = lane == bucket
    oh_bf = jnp.where(onehot, 1.0, 0.0).astype(_BF16)
    rr = lax.broadcasted_iota(jnp.int32, (tm, tm), 0)
    cc = lax.broadcasted_iota(jnp.int32, (tm, tm), 1)
    tri = jnp.where(cc < rr, 1.0, 0.0).astype(_BF16)
    before = jnp.dot(tri, oh_bf, preferred_element_type=_F32) + cnt_s[...]
    rank = jnp.sum(jnp.where(onehot, before, 0.0), axis=-1, keepdims=True)
    cnt_s[...] = cnt_s[...] + jnp.sum(jnp.where(onehot, 1.0, 0.0), axis=0, keepdims=True)
    cnt_ref[...] = cnt_s[...]

    info = jnp.where(lane == 0.0, bucket,
                     jnp.where(lane == 1.0, rank,
                               jnp.where(lane == 2.0, w_a, jnp.where(lane == 3.0, w_b, 0.0))))
    xe_ref[:, d:d + LANES] = info


def _combine(ya, yb, z, x2d, wa, wb, wo, n2, wr, br, tm, n_groups, epg):
    t, d = x2d.shape
    c = ya.shape[1]
    half = d // 2
    off = (3 * c) // half
    kern = functools.partial(_combine_kernel, tm=tm, n_groups=n_groups, epg=epg)

    def zspec(j):
        return pl.BlockSpec((tm, half), lambda i, j=j: (i, off + j))

    return pl.pallas_call(
        kern,
        grid=(t // tm,),
        in_specs=[pl.BlockSpec((tm, c), lambda i: (i, 0)),
                  pl.BlockSpec((tm, c), lambda i: (i, 0)),
                  zspec(0), zspec(1), zspec(2), zspec(3),
                  pl.BlockSpec((tm, d), lambda i: (i, 0)),
                  _const_spec(wa.shape), _const_spec(wb.shape), _const_spec(wo.shape),
                  _const_spec(n2.shape), _const_spec(wr.shape), _const_spec(br.shape)],
        out_specs=[pl.BlockSpec((tm, d + LANES), lambda i: (i, 0)),
                   pl.BlockSpec((1, LANES), lambda i: (0, 0))],
        out_shape=[jax.ShapeDtypeStruct((t, d + LANES), _F32),
                   jax.ShapeDtypeStruct((1, LANES), _F32)],
        scratch_shapes=[pltpu.VMEM((1, LANES), _F32)],
        compiler_params=_params(1),
        name="combine",
    )(ya, yb, z, z, z, z, x2d, wa, wb, wo, n2, wr, br)


def _moe_kernel(src_ref, dst_ref, ea_ref, eb_ref, valid_ref,
                xe_hbm, n2_ref, wga_ref, wua_ref, wda_ref, wgb_ref, wub_ref, wdb_ref,
                x2_hbm,
                xbuf, obuf, gsem, ssem, *, tm, n_tiles):
    j = pl.program_id(0)
    slot = j % 2
    d = obuf.shape[2]

    t_rows = x2_hbm.shape[0] - tm

    def gather_start(tile, sl):
        def body(r, _):
            row = src_ref[tile * tm + r]
            pltpu.make_async_copy(xe_hbm.at[pl.ds(row, 1)], xbuf.at[sl, pl.ds(r, 1)], gsem.at[sl]).start()
            return 0
        lax.fori_loop(0, tm, body, 0, unroll=8)

    def gather_wait(sl):
        pltpu.make_async_copy(xe_hbm.at[pl.ds(0, tm)], xbuf.at[sl], gsem.at[sl]).wait()

    def scatter_start(tile, sl):
        def body(r, _):
            row = dst_ref[tile * tm + r]
            pltpu.make_async_copy(obuf.at[sl, pl.ds(r, 1)], x2_hbm.at[pl.ds(row, 1)], ssem.at[sl]).start()
            return 0
        lax.fori_loop(0, tm, body, 0, unroll=8)

    def scatter_wait(sl):
        pltpu.make_async_copy(obuf.at[sl], x2_hbm.at[pl.ds(0, tm)], ssem.at[sl]).wait()

    @pl.when(j == 0)
    def _():
        gather_start(0, 0)
        obuf[1] = jnp.zeros(obuf.shape[1:], _F32)
        tail = pltpu.make_async_copy(obuf.at[1], x2_hbm.at[pl.ds(t_rows, tm)], ssem.at[1])
        tail.start()
        tail.wait()

    @pl.when(jnp.logical_and(j + 1 < n_tiles, valid_ref[jnp.minimum(j + 1, n_tiles - 1)] == 1))
    def _():
        gather_start(j + 1, 1 - slot)

    @pl.when(valid_ref[j] == 1)
    def _():
        gather_wait(slot)
        xe = xbuf[slot]
        x1 = xe[:, 0:d]
        w_a = xe[:, d + 2:d + 3]
        w_b = xe[:, d + 3:d + 4]
        ht = _rms(x1, n2_ref[...]).astype(_BF16)

        def expert(wg_ref, wu_ref, wgt):
            hg = jnp.dot(ht, wg_ref[0], preferred_element_type=_F32)
            hu = jnp.dot(ht, wu_ref[0], preferred_element_type=_F32)
            return (hg * jax.nn.sigmoid(hg) * hu * wgt).astype(_BF16)

        y = jnp.dot(expert(wga_ref, wua_ref, w_a), wda_ref[0], preferred_element_type=_F32)
        y = y + jnp.dot(expert(wgb_ref, wub_ref, w_b), wdb_ref[0], preferred_element_type=_F32)

        @pl.when(j >= 1)
        def _():
            scatter_wait(1 - slot)

        obuf[slot] = x1 + y
        scatter_start(j, slot)

        @pl.when(j == n_tiles - 1)
        def _():
            scatter_wait(slot)

    @pl.when(jnp.logical_and(valid_ref[j] == 0, valid_ref[jnp.maximum(j - 1, 0)] == 1))
    def _():
        scatter_wait(1 - slot)


def _moe(src, dst, ea, eb, valid, xe, n2, wg, wu, wd, tm, t):
    n_tiles = valid.shape[0]
    d = wg.shape[1]
    f = wg.shape[2]
    kern = functools.partial(_moe_kernel, tm=tm, n_tiles=n_tiles)

    def wspec(shape, which):
        if which == 0:
            return pl.BlockSpec((1,) + shape, lambda j, s, dd, a, b, v: (a[j], 0, 0))
        return pl.BlockSpec((1,) + shape, lambda j, s, dd, a, b, v: (b[j], 0, 0))

    grid_spec = pltpu.PrefetchScalarGridSpec(
        num_scalar_prefetch=5,
        grid=(n_tiles,),
        in_specs=[pl.BlockSpec(memory_space=pl.ANY),
                  pl.BlockSpec((1, d), lambda j, *_: (0, 0)),
                  wspec((d, f), 0), wspec((d, f), 0), wspec((f, d), 0),
                  wspec((d, f), 1), wspec((d, f), 1), wspec((f, d), 1)],
        out_specs=pl.BlockSpec(memory_space=pl.ANY),
        scratch_shapes=[pltpu.VMEM((2, tm, d + LANES), _F32),
                        pltpu.VMEM((2, tm, d), _F32),
                        pltpu.SemaphoreType.DMA((2,)),
                        pltpu.SemaphoreType.DMA((2,))],
    )
    return pl.pallas_call(
        kern,
        grid_spec=grid_spec,
        out_shape=jax.ShapeDtypeStruct((t + tm, d), _F32),
        compiler_params=_params(1),
        name="moe",
    )(src, dst, ea, eb, valid, xe, n2, wg, wu, wd, wg, wu, wd)


def _ple_kernel(x_ref, p_ref, ng_ref, wg_ref, wp_ref, nf_ref, o_ref):
    x2 = x_ref[...]
    g = jax.nn.sigmoid(jnp.dot(_rms(x2, ng_ref[...]).astype(_BF16), wg_ref[...],
                               preferred_element_type=_F32))
    e = jnp.dot(p_ref[...].astype(_BF16), wp_ref[...], preferred_element_type=_F32)
    o_ref[...] = _rms(x2 + g * e, nf_ref[...])


def _ple(x2, p2d, ng, wg, wp, nf, tm, t):
    d = x2.shape[1]
    pd = p2d.shape[1]
    return pl.pallas_call(
        _ple_kernel,
        grid=(t // tm,),
        in_specs=[pl.BlockSpec((tm, d), lambda i: (i, 0)),
                  pl.BlockSpec((tm, pd), lambda i: (i, 0)),
                  _const_spec(ng.shape), _const_spec(wg.shape), _const_spec(wp.shape),
                  _const_spec(nf.shape)],
        out_specs=pl.BlockSpec((tm, d), lambda i: (i, 0)),
        out_shape=jax.ShapeDtypeStruct((t, d), _F32),
        compiler_params=_params(1),
        name="ple",
    )(x2, p2d, ng, wg, wp, nf)


def _block_diag(w, per_block):
    h, hd, _ = w.shape
    nb = h // per_block
    w4 = w.reshape(nb, per_block, hd, hd)
    eye = jnp.eye(per_block, dtype=w.dtype)
    out = jnp.einsum("bpij,pq->bpiqj", w4, eye)
    return out.reshape(nb, per_block * hd, per_block * hd)


def _layer(x2d, p2d, bsz, seq, norm1_g, w_in, conv_w, conv_b, w_rg_a, b_rg_a, w_rg_x, b_rg_x, lru_lambda,
           w_pool, pool_scale, w_branch_a, w_branch_b, w_out, norm2_g, w_router_group, b_router_group,
           w_router_expert, b_router_expert, w_e_gate, w_e_up, w_e_down, norm_ple_g, w_ple_gate,
           w_ple_proj, out_norm_g):
    t, d = x2d.shape
    c = conv_b.shape[0]
    heads, hd, _ = w_rg_a.shape
    n_groups = w_router_group.shape[1]
    n_exp = w_router_expert.shape[1]
    epg = n_exp // n_groups
    assert epg == 4 and TOP_K == 2 and hd * (MXU_DIM // hd) == MXU_DIM
    assert w_pool.shape[0] == len(POOL_WINDOWS) and w_pool.shape[1] == MXU_DIM

    tm_in, tn_in = min(512, t), 1024
    tm_mix = min(256, seq)
    tm_cmb = min(256, t)
    tm_moe = min(256, t)
    tm_ple = min(256, t)

    row = lambda v: v.reshape(1, -1).astype(_F32)
    per_block = MXU_DIM // hd

    z = _inproj(x2d, row(norm1_g), w_in.astype(_BF16), tm_in, tn_in)
    ya, yb = _mixer(z, bsz, seq, conv_w.reshape(CONV_WIDTH, c), row(conv_b),
                    _block_diag(w_rg_a, per_block).astype(_BF16), row(b_rg_a),
                    _block_diag(w_rg_x, per_block).astype(_BF16), row(b_rg_x),
                    row(lru_lambda), w_pool.astype(_BF16), row(pool_scale), tm_mix)

    wr = jnp.zeros((d, LANES), _F32).at[:, :n_groups].set(w_router_group)
    wr = wr.at[:, n_groups:n_groups + n_exp].set(w_router_expert).astype(_BF16)
    br = jnp.zeros((1, LANES), _F32).at[0, :n_groups].set(b_router_group)
    br = br.at[0, n_groups:n_groups + n_exp].set(b_router_expert)
    xe, counts = _combine(ya, yb, z, x2d, w_branch_a.astype(_BF16), w_branch_b.astype(_BF16),
                          w_out.astype(_BF16), row(norm2_g), wr, br, tm_cmb, n_groups, epg)

    n_buckets = n_groups * _N_PAIRS
    n_tiles = t // tm_moe + n_buckets
    cnt = counts[0, :n_buckets].astype(jnp.int32)
    nt = (cnt + tm_moe - 1) // tm_moe
    cum = jnp.cumsum(nt)
    total = cum[-1]
    bucket = xe[:, d].astype(jnp.int32)
    rank = xe[:, d + 1].astype(jnp.int32)
    pos = (cum - nt)[bucket] * tm_moe + rank
    slots = jnp.arange(n_tiles * tm_moe, dtype=jnp.int32)
    tok = jnp.arange(t, dtype=jnp.int32)
    dst = (t + slots % tm_moe).at[pos].set(tok)
    src = jnp.where(dst < t, dst, 0)
    tile = jnp.minimum(jnp.arange(n_tiles, dtype=jnp.int32), total - 1)
    tb = jnp.sum((cum[None, :] <= tile[:, None]).astype(jnp.int32), axis=1)
    grp, pr = tb // _N_PAIRS, tb % _N_PAIRS
    ea = grp * epg + jnp.asarray(_PAIR_SLOT_A, jnp.int32)[pr]
    eb = grp * epg + jnp.asarray(_PAIR_SLOT_B, jnp.int32)[pr]
    valid = (jnp.arange(n_tiles, dtype=jnp.int32) < total).astype(jnp.int32)

    x2 = _moe(src, dst, ea, eb, valid, xe, row(norm2_g), w_e_gate.astype(_BF16), w_e_up.astype(_BF16),
              w_e_down.astype(_BF16), tm_moe, t)
    return _ple(x2, p2d, row(norm_ple_g), w_ple_gate.astype(_BF16), w_ple_proj.astype(_BF16),
                row(out_norm_g), tm_ple, t)


def kernel(x, p, norm1_g, w_in, conv_w, conv_b, w_rg_a, b_rg_a, w_rg_x, b_rg_x, lru_lambda, w_pool, pool_scale, w_branch_a, w_branch_b, w_out, norm2_g, w_router_group, b_router_group, w_router_expert, b_router_expert, w_e_gate, w_e_up, w_e_down, norm_ple_g, w_ple_gate, w_ple_proj, final_norm_g):
    bsz, seq, d = x.shape
    depth = p.shape[0]
    assert depth == 1, "the final RMSNorm is fused into the last layer's embedding kernel"
    out = _layer(x.reshape(bsz * seq, d), p[0].reshape(bsz * seq, -1), bsz, seq,
                 norm1_g[0], w_in[0], conv_w[0], conv_b[0], w_rg_a[0], b_rg_a[0], w_rg_x[0], b_rg_x[0],
                 lru_lambda[0], w_pool[0], pool_scale[0], w_branch_a[0], w_branch_b[0], w_out[0],
                 norm2_g[0], w_router_group[0], b_router_group[0], w_router_expert[0],
                 b_router_expert[0], w_e_gate[0], w_e_up[0], w_e_down[0], norm_ple_g[0],
                 w_ple_gate[0], w_ple_proj[0], final_norm_g)
    return out.reshape(bsz, seq, d)
```

```python
import functools

import jax
import jax.numpy as jnp
from jax import lax
from jax.experimental import pallas as pl
from jax.experimental.pallas import tpu as pltpu

EPS = 1e-6
LRU_C = 8.0
CONV_WIDTH = 4
POOL_WINDOWS = (2, 4, 8, 16)
TOP_K = 2

LANES = 128
SUBLANES = 8
MXU_DIM = 256
VMEM_LIMIT_BYTES = 56 * 1024 * 1024

_BF16 = jnp.bfloat16
_F32 = jnp.float32

_PAIR_SLOT_A = (0, 0, 0, 1, 1, 3)
_PAIR_SLOT_B = (1, 2, 3, 3, 2, 2)
_N_PAIRS = len(_PAIR_SLOT_A)


def _rms(x, g):
    ms = jnp.mean(x * x, axis=-1, keepdims=True)
    return x * lax.rsqrt(ms + EPS) * g


def _const_spec(shape):
    nd = len(shape)
    return pl.BlockSpec(shape, lambda *_: (0,) * nd, pipeline_mode=pl.Buffered(1))


def _params(n_axes):
    return pltpu.CompilerParams(dimension_semantics=("arbitrary",) * n_axes,
                                vmem_limit_bytes=VMEM_LIMIT_BYTES)


def _inproj_kernel(x_ref, g_ref, w_ref, z_ref, h_ref):
    @pl.when(pl.program_id(1) == 0)
    def _():
        h_ref[...] = _rms(x_ref[...], g_ref[...]).astype(_BF16)

    z_ref[...] = jnp.dot(h_ref[...], w_ref[...], preferred_element_type=_F32)


def _inproj(x2d, g, w_bf, tm, tn):
    t, d = x2d.shape
    n = w_bf.shape[1]
    return pl.pallas_call(
        _inproj_kernel,
        grid=(t // tm, n // tn),
        in_specs=[pl.BlockSpec((tm, d), lambda i, j: (i, 0)),
                  pl.BlockSpec((1, d), lambda i, j: (0, 0)),
                  pl.BlockSpec((d, tn), lambda i, j: (0, j))],
        out_specs=pl.BlockSpec((tm, tn), lambda i, j: (i, j)),
        out_shape=jax.ShapeDtypeStruct((t, n), _F32),
        scratch_shapes=[pltpu.VMEM((tm, d), _BF16)],
        compiler_params=_params(2),
        name="inproj",
    )(x2d, g, w_bf)


def _scan_pitch(tm):
    p = -(-tm // SUBLANES)
    while p % SUBLANES != 4:
        p += 1
    return p


def _mixer_kernel(z_ref, cw_ref, cb_ref, wa_ref, ba_ref, wx_ref, bx_ref, lam_ref, wp_ref, ps_ref,
                  ya_ref, yb_ref,
                  er_ref, ep_ref, a_ref, b_ref, h_ref, car_ref, *, tm, pitch):
    s = pl.program_id(1)
    c = cb_ref.shape[1]
    n_slab = c // LANES
    hist_r = SUBLANES
    hist_p = 2 * SUBLANES

    @pl.when(s == 0)
    def _():
        er_ref[0:hist_r, :] = jnp.zeros((hist_r, c), _F32)
        ep_ref[0:hist_p, :] = jnp.zeros((hist_p, c), _F32)
        car_ref[...] = jnp.zeros(car_ref.shape, _F32)
        a_ref[:, tm:, :] = jnp.ones((n_slab, SUBLANES * pitch - tm, LANES), _F32)
        b_ref[:, tm:, :] = jnp.zeros((n_slab, SUBLANES * pitch - tm, LANES), _F32)

    @pl.when(s > 0)
    def _():
        er_ref[0:hist_r, :] = er_ref[tm:tm + hist_r, :]
        ep_ref[0:hist_p, :] = ep_ref[tm:tm + hist_p, :]

    er_ref[hist_r:hist_r + tm, :] = z_ref[:, 0:c]
    ep_ref[hist_p:hist_p + tm, :] = z_ref[:, 2 * c:3 * c]

    kvec = -LRU_C * jax.nn.softplus(-lam_ref[...])
    nblk = c // MXU_DIM
    for k in range(nblk):
        cs = slice(k * MXU_DIM, (k + 1) * MXU_DIM)
        xc = cb_ref[:, cs] + cw_ref[CONV_WIDTH - 1:CONV_WIDTH, cs] * er_ref[hist_r:hist_r + tm, cs]
        for j in range(1, CONV_WIDTH):
            xc = xc + cw_ref[CONV_WIDTH - 1 - j:CONV_WIDTH - j, cs] * er_ref[hist_r - j:hist_r - j + tm, cs]
        xcb = xc.astype(_BF16)
        r = jax.nn.sigmoid(jnp.dot(xcb, wa_ref[k], preferred_element_type=_F32) + ba_ref[:, cs])
        ig = jax.nn.sigmoid(jnp.dot(xcb, wx_ref[k], preferred_element_type=_F32) + bx_ref[:, cs])
        log_a = r * kvec[:, cs]
        a = jnp.exp(log_a)
        mult = jnp.sqrt(1.0 - a * a)
        bb = mult * ig * xc
        for q in range(MXU_DIM // LANES):
            slab = k * (MXU_DIM // LANES) + q
            a_ref[slab, 0:tm, :] = a[:, q * LANES:(q + 1) * LANES]
            b_ref[slab, 0:tm, :] = bb[:, q * LANES:(q + 1) * LANES]

    def sweep1(i, carry):
        hs, ps = carry
        nh, npd = [], []
        for sl in range(n_slab):
            av = a_ref[sl, pl.ds(i, SUBLANES, stride=pitch), :]
            bv = b_ref[sl, pl.ds(i, SUBLANES, stride=pitch), :]
            nh.append(av * hs[sl] + bv)
            npd.append(av * ps[sl])
        return tuple(nh), tuple(npd)

    zero = jnp.zeros((SUBLANES, LANES), _F32)
    one = jnp.ones((SUBLANES, LANES), _F32)
    h_end, a_end = lax.fori_loop(0, pitch, sweep1, ((zero,) * n_slab, (one,) * n_slab))

    row = lax.broadcasted_iota(jnp.int32, (SUBLANES, LANES), 0)
    h0 = []
    for sl in range(n_slab):
        hh, aa = h_end[sl], a_end[sl]
        d = 1
        while d < SUBLANES:
            hs_ = jnp.where(row >= d, pltpu.roll(hh, d, 0), 0.0)
            as_ = jnp.where(row >= d, pltpu.roll(aa, d, 0), 1.0)
            hh = aa * hs_ + hh
            aa = aa * as_
            d *= 2
        cin = car_ref[:, sl * LANES:(sl + 1) * LANES]
        full = hh + aa * cin
        h0.append(jnp.where(row >= 1, pltpu.roll(full, 1, 0), cin))
        car_ref[:, sl * LANES:(sl + 1) * LANES] = jnp.broadcast_to(full[SUBLANES - 1:SUBLANES, :],
                                                                  (SUBLANES, LANES))

    def sweep2(i, hs):
        nh = []
        for sl in range(n_slab):
            av = a_ref[sl, pl.ds(i, SUBLANES, stride=pitch), :]
            bv = b_ref[sl, pl.ds(i, SUBLANES, stride=pitch), :]
            hv = av * hs[sl] + bv
            h_ref[sl, pl.ds(i, SUBLANES, stride=pitch), :] = hv
            nh.append(hv)
        return tuple(nh)

    lax.fori_loop(0, pitch, sweep2, tuple(h0))

    for sl in range(n_slab):
        cs = slice(sl * LANES, (sl + 1) * LANES)
        g = z_ref[:, c + sl * LANES:c + (sl + 1) * LANES]
        ya_ref[:, cs] = (h_ref[sl, 0:tm, :] * jax.nn.gelu(g)).astype(_BF16)

    t_idx = (s * tm + lax.broadcasted_iota(jnp.int32, (tm, 1), 0) + 1).astype(_F32)
    n_grp = len(POOL_WINDOWS)
    gd = c // n_grp
    for gi, w in enumerate(POOL_WINDOWS):
        cs = slice(gi * gd, (gi + 1) * gd)
        e = ep_ref[:, cs]
        acc = e
        d = 1
        while d < w:
            acc = acc + pltpu.roll(acc, d, 0)
            d *= 2
        xt = e[hist_p:, :]
        cnt = jnp.minimum(t_idx, float(w))
        dd = acc[hist_p:, :] / cnt - xt
        yb = jnp.dot(dd.astype(_BF16), wp_ref[gi], preferred_element_type=_F32) * ps_ref[:, cs]
        yb_ref[:, cs] = yb.astype(_BF16)


def _mixer(z, bsz, seq, cw, cb, wa_bd, ba, wx_bd, bx, lam, wp, ps, tm):
    t = z.shape[0]
    c = cb.shape[1]
    ns = seq // tm
    pitch = _scan_pitch(tm)
    n_slab = c // LANES
    kern = functools.partial(_mixer_kernel, tm=tm, pitch=pitch)
    return pl.pallas_call(
        kern,
        grid=(bsz, ns),
        in_specs=[pl.BlockSpec((tm, 3 * c), lambda b, s: (b * ns + s, 0)),
                  _const_spec(cw.shape), _const_spec(cb.shape),
                  _const_spec(wa_bd.shape), _const_spec(ba.shape),
                  _const_spec(wx_bd.shape), _const_spec(bx.shape),
                  _const_spec(lam.shape), _const_spec(wp.shape), _const_spec(ps.shape)],
        out_specs=[pl.BlockSpec((tm, c), lambda b, s: (b * ns + s, 0)),
                   pl.BlockSpec((tm, c), lambda b, s: (b * ns + s, 0))],
        out_shape=[jax.ShapeDtypeStruct((t, c), _BF16), jax.ShapeDtypeStruct((t, c), _BF16)],
        scratch_shapes=[pltpu.VMEM((SUBLANES + tm, c), _F32),
                        pltpu.VMEM((2 * SUBLANES + tm, c), _F32),
                        pltpu.VMEM((n_slab, SUBLANES * pitch, LANES), _F32),
                        pltpu.VMEM((n_slab, SUBLANES * pitch, LANES), _F32),
                        pltpu.VMEM((n_slab, SUBLANES * pitch, LANES), _F32),
                        pltpu.VMEM((SUBLANES, c), _F32)],
        compiler_params=_params(2),
        name="mixer",
    )(z, cw, cb, wa_bd, ba, wx_bd, bx, lam, wp, ps)


def _combine_kernel(ya_ref, yb_ref, ga0_ref, ga1_ref, gb0_ref, gb1_ref, x_ref,
                    wa_ref, wb_ref, wo_ref, n2_ref, wr_ref, br_ref,
                    xe_ref, cnt_ref, cnt_s, *, tm, n_groups, epg):
    i = pl.program_id(0)
    d = x_ref.shape[1]
    half = d // 2

    @pl.when(i == 0)
    def _():
        cnt_s[...] = jnp.zeros(cnt_s.shape, _F32)

    pa = jnp.dot(ya_ref[...], wa_ref[...], preferred_element_type=_F32)
    pb = jnp.dot(yb_ref[...], wb_ref[...], preferred_element_type=_F32)
    u0 = jax.nn.sigmoid(ga0_ref[...]) * pa[:, :half] + jax.nn.sigmoid(gb0_ref[...]) * pb[:, :half]
    u1 = jax.nn.sigmoid(ga1_ref[...]) * pa[:, half:] + jax.nn.sigmoid(gb1_ref[...]) * pb[:, half:]
    u = jnp.concatenate([u0, u1], axis=1).astype(_BF16)
    x1 = x_ref[...] + jnp.dot(u, wo_ref[...], preferred_element_type=_F32)
    xe_ref[:, 0:d] = x1

    ht = _rms(x1, n2_ref[...]).astype(_BF16)
    logits = jnp.dot(ht, wr_ref[...], preferred_element_type=_F32) + br_ref[...]

    lane = lax.broadcasted_iota(jnp.int32, (tm, LANES), 1).astype(_F32)
    ninf = -jnp.inf
    big = float(LANES)

    def first_argmax(v):
        m = jnp.max(v, axis=-1, keepdims=True)
        return m, jnp.min(jnp.where(v == m, lane, big), axis=-1, keepdims=True)

    is_g = lane < float(n_groups)
    gmax, gidx = first_argmax(jnp.where(is_g, logits, ninf))
    g_w = 1.0 / jnp.sum(jnp.where(is_g, jnp.exp(logits - gmax), 0.0), axis=-1, keepdims=True)
    lo_lane = float(n_groups) + float(epg) * gidx
    in_grp = (lane >= lo_lane) & (lane < lo_lane + float(epg))
    le = jnp.where(in_grp, logits, ninf)
    m1, i1 = first_argmax(le)
    m2, i2 = first_argmax(jnp.where(lane == i1, ninf, le))
    e21 = jnp.exp(m2 - m1)
    w1 = g_w / (1.0 + e21)
    w2 = w1 * e21
    e1 = i1 - lo_lane
    e2 = i2 - lo_lane
    lo = jnp.minimum(e1, e2)
    hi = jnp.maximum(e1, e2)
    w_lo = jnp.where(e1 < e2, w1, w2)
    w_hi = jnp.where(e1 < e2, w2, w1)
    pair = jnp.where(lo == 0.0, hi - 1.0, jnp.where(lo == 1.0, 6.0 - hi, 5.0))
    swap = pair == 5.0
    w_a = jnp.where(swap, w_hi, w_lo)
    w_b = jnp.where(swap, w_lo, w_hi)
    bucket = float(_N_PAIRS) * gidx + pair

    onehot = lane == bucket
    oh_bf = jnp.where(onehot, 1.0, 0.0).astype(_BF16)
    rr = lax.broadcasted_iota(jnp.int32, (tm, tm), 0)
    cc = lax.broadcasted_iota(jnp.int32, (tm, tm), 1)
    tri = jnp.where(cc < rr, 1.0, 0.0).astype(_BF16)
    before = jnp.dot(tri, oh_bf, preferred_element_type=_F32) + cnt_s[...]
    rank = jnp.sum(jnp.where(onehot, before, 0.0), axis=-1, keepdims=True)
    cnt_s[...] = cnt_s[...] + jnp.sum(jnp.where(onehot, 1.0, 0.0), axis=0, keepdims=True)
    cnt_ref[...] = cnt_s[...]

    info = jnp.where(lane == 0.0, bucket,
                     jnp.where(lane == 1.0, rank,
                               jnp.where(lane == 2.0, w_a, jnp.where(lane == 3.0, w_b, 0.0))))
    xe_ref[:, d:d + LANES] = info


def _combine(ya, yb, z, x2d, wa, wb, wo, n2, wr, br, tm, n_groups, epg):
    t, d = x2d.shape
    c = ya.shape[1]
    half = d // 2
    off = (3 * c) // half
    kern = functools.partial(_combine_kernel, tm=tm, n_groups=n_groups, epg=epg)

    def zspec(j):
        return pl.BlockSpec((tm, half), lambda i, j=j: (i, off + j))

    return pl.pallas_call(
        kern,
        grid=(t // tm,),
        in_specs=[pl.BlockSpec((tm, c), lambda i: (i, 0)),
                  pl.BlockSpec((tm, c), lambda i: (i, 0)),
                  zspec(0), zspec(1), zspec(2), zspec(3),
                  pl.BlockSpec((tm, d), lambda i: (i, 0)),
                  _const_spec(wa.shape), _const_spec(wb.shape), _const_spec(wo.shape),
                  _const_spec(n2.shape), _const_spec(wr.shape), _const_spec(br.shape)],
        out_specs=[pl.BlockSpec((tm, d + LANES), lambda i: (i, 0)),
                   pl.BlockSpec((1, LANES), lambda i: (0, 0))],
        out_shape=[jax.ShapeDtypeStruct((t, d + LANES), _F32),
                   jax.ShapeDtypeStruct((1, LANES), _F32)],
        scratch_shapes=[pltpu.VMEM((1, LANES), _F32)],
        compiler_params=_params(1),
        name="combine",
    )(ya, yb, z, z, z, z, x2d, wa, wb, wo, n2, wr, br)


def _row_gather_start(idx_ref, base, src_hbm, buf, slot, sem, n_rows, unrolled):
    def start(r):
        row = idx_ref[base + r]
        pltpu.make_async_copy(src_hbm.at[pl.ds(row, 1)], buf.at[slot, pl.ds(r, 1)], sem.at[slot]).start()

    if unrolled:
        for r in range(n_rows):
            start(r)
    else:
        def body(r, _):
            start(r)
            return 0
        lax.fori_loop(0, n_rows, body, 0, unroll=8)


def _row_gather_wait(src_hbm, buf, slot, sem, n_rows):
    pltpu.make_async_copy(src_hbm.at[pl.ds(0, n_rows)], buf.at[slot], sem.at[slot]).wait()


def _moe_kernel(src_ref, ea_ref, eb_ref, valid_ref,
                xe_hbm, n2_ref, wga_ref, wua_ref, wda_ref, wgb_ref, wub_ref, wdb_ref,
                x2s_ref, xbuf, gsem, *, tm):
    j = pl.program_id(0)
    slot = j % 2
    d = x2s_ref.shape[1]

    @pl.when(j == 0)
    def _():
        _row_gather_start(src_ref, 0, xe_hbm, xbuf, 0, gsem, tm, unrolled=False)

    @pl.when(valid_ref[j] == 1)
    def _():
        _row_gather_wait(xe_hbm, xbuf, slot, gsem, tm)
        _row_gather_start(src_ref, (j + 1) * tm, xe_hbm, xbuf, 1 - slot, gsem, tm, unrolled=True)
        xe = xbuf[slot]
        x1 = xe[:, 0:d]
        w_a = xe[:, d + 2:d + 3]
        w_b = xe[:, d + 3:d + 4]
        ht = _rms(x1, n2_ref[...]).astype(_BF16)

        def expert(wg_ref, wu_ref, wgt):
            hg = jnp.dot(ht, wg_ref[0], preferred_element_type=_F32)
            hu = jnp.dot(ht, wu_ref[0], preferred_element_type=_F32)
            return (hg * jax.nn.sigmoid(hg) * hu * wgt).astype(_BF16)

        y = jnp.dot(expert(wga_ref, wua_ref, w_a), wda_ref[0], preferred_element_type=_F32)
        y = y + jnp.dot(expert(wgb_ref, wub_ref, w_b), wdb_ref[0], preferred_element_type=_F32)

        x2s_ref[...] = x1 + y

    @pl.when(valid_ref[j] == 0)
    def _():
        x2s_ref[...] = jnp.zeros(x2s_ref.shape, _F32)

        @pl.when(valid_ref[jnp.maximum(j - 1, 0)] == 1)
        def _():
            _row_gather_wait(xe_hbm, xbuf, slot, gsem, tm)


def _moe(src, ea, eb, valid, xe, n2, wg, wu, wd, tm):
    n_tiles = valid.shape[0]
    d = wg.shape[1]
    f = wg.shape[2]
    kern = functools.partial(_moe_kernel, tm=tm)

    def wspec(shape, which):
        if which == 0:
            return pl.BlockSpec((1,) + shape, lambda j, s, a, b, v: (a[j], 0, 0))
        return pl.BlockSpec((1,) + shape, lambda j, s, a, b, v: (b[j], 0, 0))

    grid_spec = pltpu.PrefetchScalarGridSpec(
        num_scalar_prefetch=4,
        grid=(n_tiles,),
        in_specs=[pl.BlockSpec(memory_space=pl.ANY),
                  pl.BlockSpec((1, d), lambda j, *_: (0, 0)),
                  wspec((d, f), 0), wspec((d, f), 0), wspec((f, d), 0),
                  wspec((d, f), 1), wspec((d, f), 1), wspec((f, d), 1)],
        out_specs=pl.BlockSpec((tm, d), lambda j, *_: (j, 0)),
        scratch_shapes=[pltpu.VMEM((2, tm, d + LANES), _F32),
                        pltpu.SemaphoreType.DMA((2,))],
    )
    return pl.pallas_call(
        kern,
        grid_spec=grid_spec,
        out_shape=jax.ShapeDtypeStruct((n_tiles * tm, d), _F32),
        compiler_params=_params(1),
        name="moe",
    )(src, ea, eb, valid, xe, n2, wg, wu, wd, wg, wu, wd)


def _ple_kernel(pos_ref, x2s_hbm, p_ref, ng_ref, wg_ref, wp_ref, nf_ref, o_ref, xbuf, gsem, *, tm, n_steps):
    i = pl.program_id(0)
    slot = i % 2

    @pl.when(i == 0)
    def _():
        _row_gather_start(pos_ref, 0, x2s_hbm, xbuf, 0, gsem, tm, unrolled=False)

    _row_gather_wait(x2s_hbm, xbuf, slot, gsem, tm)
    nxt = jnp.where(i + 1 == n_steps, 0, i + 1)
    _row_gather_start(pos_ref, nxt * tm, x2s_hbm, xbuf, 1 - slot, gsem, tm, unrolled=True)
    x2 = xbuf[slot]
    g = jax.nn.sigmoid(jnp.dot(_rms(x2, ng_ref[...]).astype(_BF16), wg_ref[...],
                               preferred_element_type=_F32))
    e = jnp.dot(p_ref[...].astype(_BF16), wp_ref[...], preferred_element_type=_F32)
    o_ref[...] = _rms(x2 + g * e, nf_ref[...])

    @pl.when(i == n_steps - 1)
    def _():
        _row_gather_wait(x2s_hbm, xbuf, 1 - slot, gsem, tm)


def _ple(pos, x2s, p2d, ng, wg, wp, nf, tm, t):
    d = x2s.shape[1]
    pd = p2d.shape[1]
    n_steps = t // tm
    kern = functools.partial(_ple_kernel, tm=tm, n_steps=n_steps)

    def cspec(shape):
        nd = len(shape)
        return pl.BlockSpec(shape, lambda i, *_: (0,) * nd, pipeline_mode=pl.Buffered(1))

    grid_spec = pltpu.PrefetchScalarGridSpec(
        num_scalar_prefetch=1,
        grid=(n_steps,),
        in_specs=[pl.BlockSpec(memory_space=pl.ANY),
                  pl.BlockSpec((tm, pd), lambda i, *_: (i, 0)),
                  cspec(ng.shape), cspec(wg.shape), cspec(wp.shape), cspec(nf.shape)],
        out_specs=pl.BlockSpec((tm, d), lambda i, *_: (i, 0)),
        scratch_shapes=[pltpu.VMEM((2, tm, d), _F32),
                        pltpu.SemaphoreType.DMA((2,))],
    )
    return pl.pallas_call(
        kern,
        grid_spec=grid_spec,
        out_shape=jax.ShapeDtypeStruct((t, d), _F32),
        compiler_params=_params(1),
        name="ple",
    )(pos, x2s, p2d, ng, wg, wp, nf)


def _block_diag(w, per_block):
    h, hd, _ = w.shape
    nb = h // per_block
    w4 = w.reshape(nb, per_block, hd, hd)
    eye = jnp.eye(per_block, dtype=w.dtype)
    out = jnp.einsum("bpij,pq->bpiqj", w4, eye)
    return out.reshape(nb, per_block * hd, per_block * hd)


def _layer(x2d, p2d, bsz, seq, norm1_g, w_in, conv_w, conv_b, w_rg_a, b_rg_a, w_rg_x, b_rg_x, lru_lambda,
           w_pool, pool_scale, w_branch_a, w_branch_b, w_out, norm2_g, w_router_group, b_router_group,
           w_router_expert, b_router_expert, w_e_gate, w_e_up, w_e_down, norm_ple_g, w_ple_gate,
           w_ple_proj, out_norm_g):
    t, d = x2d.shape
    c = conv_b.shape[0]
    heads, hd, _ = w_rg_a.shape
    n_groups = w_router_group.shape[1]
    n_exp = w_router_expert.shape[1]
    epg = n_exp // n_groups
    assert epg == 4 and TOP_K == 2 and hd * (MXU_DIM // hd) == MXU_DIM
    assert w_pool.shape[0] == len(POOL_WINDOWS) and w_pool.shape[1] == MXU_DIM

    tm_in, tn_in = min(512, t), 1024
    tm_mix = min(256, seq)
    tm_cmb = min(256, t)
    tm_moe = min(256, t)
    tm_ple = min(256, t)

    row = lambda v: v.reshape(1, -1).astype(_F32)
    per_block = MXU_DIM // hd

    z = _inproj(x2d, row(norm1_g), w_in.astype(_BF16), tm_in, tn_in)
    ya, yb = _mixer(z, bsz, seq, conv_w.reshape(CONV_WIDTH, c), row(conv_b),
                    _block_diag(w_rg_a, per_block).astype(_BF16), row(b_rg_a),
                    _block_diag(w_rg_x, per_block).astype(_BF16), row(b_rg_x),
                    row(lru_lambda), w_pool.astype(_BF16), row(pool_scale), tm_mix)

    wr = jnp.zeros((d, LANES), _F32).at[:, :n_groups].set(w_router_group)
    wr = wr.at[:, n_groups:n_groups + n_exp].set(w_router_expert).astype(_BF16)
    br = jnp.zeros((1, LANES), _F32).at[0, :n_groups].set(b_router_group)
    br = br.at[0, n_groups:n_groups + n_exp].set(b_router_expert)
    xe, counts = _combine(ya, yb, z, x2d, w_branch_a.astype(_BF16), w_branch_b.astype(_BF16),
                          w_out.astype(_BF16), row(norm2_g), wr, br, tm_cmb, n_groups, epg)

    n_buckets = n_groups * _N_PAIRS
    n_tiles = t // tm_moe + n_buckets
    cnt = counts[0, :n_buckets].astype(jnp.int32)
    nt = (cnt + tm_moe - 1) // tm_moe
    cum = jnp.cumsum(nt)
    total = cum[-1]
    bucket = xe[:, d].astype(jnp.int32)
    rank = xe[:, d + 1].astype(jnp.int32)
    pos = (cum - nt)[bucket] * tm_moe + rank
    tok = jnp.arange(t, dtype=jnp.int32)
    src = jnp.zeros((n_tiles * tm_moe,), jnp.int32).at[pos].set(tok)
    tile = jnp.minimum(jnp.arange(n_tiles, dtype=jnp.int32), total - 1)
    tb = jnp.sum((cum[None, :] <= tile[:, None]).astype(jnp.int32), axis=1)
    grp, pr = tb // _N_PAIRS, tb % _N_PAIRS
    ea = grp * epg + jnp.asarray(_PAIR_SLOT_A, jnp.int32)[pr]
    eb = grp * epg + jnp.asarray(_PAIR_SLOT_B, jnp.int32)[pr]
    valid = (jnp.arange(n_tiles, dtype=jnp.int32) < total).astype(jnp.int32)

    x2s = _moe(src, ea, eb, valid, xe, row(norm2_g), w_e_gate.astype(_BF16), w_e_up.astype(_BF16),
               w_e_down.astype(_BF16), tm_moe)
    return _ple(pos, x2s, p2d, row(norm_ple_g), w_ple_gate.astype(_BF16), w_ple_proj.astype(_BF16),
                row(out_norm_g), tm_ple, t)


def kernel(x, p, norm1_g, w_in, conv_w, conv_b, w_rg_a, b_rg_a, w_rg_x, b_rg_x, lru_lambda, w_pool, pool_scale, w_branch_a, w_branch_b, w_out, norm2_g, w_router_group, b_router_group, w_router_expert, b_router_expert, w_e_gate, w_e_up, w_e_down, norm_ple_g, w_ple_gate, w_ple_proj, final_norm_g):
    bsz, seq, d = x.shape
    depth = p.shape[0]
    assert depth == 1, "the final RMSNorm is fused into the last layer's embedding kernel"
    out = _layer(x.reshape(bsz * seq, d), p[0].reshape(bsz * seq, -1), bsz, seq,
                 norm1_g[0], w_in[0], conv_w[0], conv_b[0], w_rg_a[0], b_rg_a[0], w_rg_x[0], b_rg_x[0],
                 lru_lambda[0], w_pool[0], pool_scale[0], w_branch_a[0], w_branch_b[0], w_out[0],
                 norm2_g[0], w_router_group[0], b_router_group[0], w_router_expert[0],
                 b_router_expert[0], w_e_gate[0], w_e_up[0], w_e_down[0], norm_ple_g[0],
                 w_ple_gate[0], w_ple_proj[0], final_norm_g)
    return out.reshape(bsz, seq, d)
```

```python
import functools

import jax
import jax.numpy as jnp
from jax import lax
from jax.experimental import pallas as pl
from jax.experimental.pallas import tpu as pltpu

EPS = 1e-6
LRU_C = 8.0
CONV_WIDTH = 4
POOL_WINDOWS = (2, 4, 8, 16)
TOP_K = 2

LANES = 128
SUBLANES = 8
MXU_DIM = 256
VMEM_LIMIT_BYTES = 56 * 1024 * 1024

_BF16 = jnp.bfloat16
_F32 = jnp.float32

_PAIR_SLOT_A = (0, 0, 0, 1, 1, 3)
_PAIR_SLOT_B = (1, 2, 3, 3, 2, 2)
_N_PAIRS = len(_PAIR_SLOT_A)


def _rms(x, g):
    ms = jnp.mean(x * x, axis=-1, keepdims=True)
    return x * lax.rsqrt(ms + EPS) * g


def _const_spec(shape):
    nd = len(shape)
    return pl.BlockSpec(shape, lambda *_: (0,) * nd, pipeline_mode=pl.Buffered(1))


def _params(n_axes):
    return pltpu.CompilerParams(dimension_semantics=("arbitrary",) * n_axes,
                                vmem_limit_bytes=VMEM_LIMIT_BYTES)


def _inproj_kernel(x_ref, g_ref, w_ref, z_ref, h_ref):
    @pl.when(pl.program_id(1) == 0)
    def _():
        h_ref[...] = _rms(x_ref[...], g_ref[...]).astype(_BF16)

    z_ref[...] = jnp.dot(h_ref[...], w_ref[...], preferred_element_type=_F32)


def _inproj(x2d, g, w_bf, tm, tn):
    t, d = x2d.shape
    n = w_bf.shape[1]
    return pl.pallas_call(
        _inproj_kernel,
        grid=(t // tm, n // tn),
        in_specs=[pl.BlockSpec((tm, d), lambda i, j: (i, 0)),
                  pl.BlockSpec((1, d), lambda i, j: (0, 0)),
                  pl.BlockSpec((d, tn), lambda i, j: (0, j))],
        out_specs=pl.BlockSpec((tm, tn), lambda i, j: (i, j)),
        out_shape=jax.ShapeDtypeStruct((t, n), _F32),
        scratch_shapes=[pltpu.VMEM((tm, d), _BF16)],
        compiler_params=_params(2),
        name="inproj",
    )(x2d, g, w_bf)


def _scan_pitch(tm):
    p = -(-tm // SUBLANES)
    while p % SUBLANES != 4:
        p += 1
    return p


def _mixer_kernel(z_ref, cw_ref, cb_ref, wa_ref, ba_ref, wx_ref, bx_ref, lam_ref, wp_ref, ps_ref,
                  ya_ref, yb_ref,
                  er_ref, ep_ref, a_ref, b_ref, h_ref, car_ref, *, tm, pitch):
    s = pl.program_id(1)
    c = cb_ref.shape[1]
    n_slab = c // LANES
    hist_r = SUBLANES
    hist_p = 2 * SUBLANES

    @pl.when(s == 0)
    def _():
        er_ref[0:hist_r, :] = jnp.zeros((hist_r, c), _F32)
        ep_ref[0:hist_p, :] = jnp.zeros((hist_p, c), _F32)
        car_ref[...] = jnp.zeros(car_ref.shape, _F32)
        a_ref[:, tm:, :] = jnp.ones((n_slab, SUBLANES * pitch - tm, LANES), _F32)
        b_ref[:, tm:, :] = jnp.zeros((n_slab, SUBLANES * pitch - tm, LANES), _F32)

    @pl.when(s > 0)
    def _():
        er_ref[0:hist_r, :] = er_ref[tm:tm + hist_r, :]
        ep_ref[0:hist_p, :] = ep_ref[tm:tm + hist_p, :]

    er_ref[hist_r:hist_r + tm, :] = z_ref[:, 0:c]
    ep_ref[hist_p:hist_p + tm, :] = z_ref[:, 2 * c:3 * c]

    kvec = -LRU_C * jax.nn.softplus(-lam_ref[...])
    nblk = c // MXU_DIM
    for k in range(nblk):
        cs = slice(k * MXU_DIM, (k + 1) * MXU_DIM)
        xc = cb_ref[:, cs] + cw_ref[CONV_WIDTH - 1:CONV_WIDTH, cs] * er_ref[hist_r:hist_r + tm, cs]
        for j in range(1, CONV_WIDTH):
            xc = xc + cw_ref[CONV_WIDTH - 1 - j:CONV_WIDTH - j, cs] * er_ref[hist_r - j:hist_r - j + tm, cs]
        xcb = xc.astype(_BF16)
        r = jax.nn.sigmoid(jnp.dot(xcb, wa_ref[k], preferred_element_type=_F32) + ba_ref[:, cs])
        ig = jax.nn.sigmoid(jnp.dot(xcb, wx_ref[k], preferred_element_type=_F32) + bx_ref[:, cs])
        log_a = r * kvec[:, cs]
        a = jnp.exp(log_a)
        mult = jnp.sqrt(1.0 - a * a)
        bb = mult * ig * xc
        for q in range(MXU_DIM // LANES):
            slab = k * (MXU_DIM // LANES) + q
            a_ref[slab, 0:tm, :] = a[:, q * LANES:(q + 1) * LANES]
            b_ref[slab, 0:tm, :] = bb[:, q * LANES:(q + 1) * LANES]

    def sweep1(i, carry):
        hs, ps = carry
        nh, npd = [], []
        for sl in range(n_slab):
            av = a_ref[sl, pl.ds(i, SUBLANES, stride=pitch), :]
            bv = b_ref[sl, pl.ds(i, SUBLANES, stride=pitch), :]
            nh.append(av * hs[sl] + bv)
            npd.append(av * ps[sl])
        return tuple(nh), tuple(npd)

    zero = jnp.zeros((SUBLANES, LANES), _F32)
    one = jnp.ones((SUBLANES, LANES), _F32)
    h_end, a_end = lax.fori_loop(0, pitch, sweep1, ((zero,) * n_slab, (one,) * n_slab))

    row = lax.broadcasted_iota(jnp.int32, (SUBLANES, LANES), 0)
    h0 = []
    for sl in range(n_slab):
        hh, aa = h_end[sl], a_end[sl]
        d = 1
        while d < SUBLANES:
            hs_ = jnp.where(row >= d, pltpu.roll(hh, d, 0), 0.0)
            as_ = jnp.where(row >= d, pltpu.roll(aa, d, 0), 1.0)
            hh = aa * hs_ + hh
            aa = aa * as_
            d *= 2
        cin = car_ref[:, sl * LANES:(sl + 1) * LANES]
        full = hh + aa * cin
        h0.append(jnp.where(row >= 1, pltpu.roll(full, 1, 0), cin))
        car_ref[:, sl * LANES:(sl + 1) * LANES] = jnp.broadcast_to(full[SUBLANES - 1:SUBLANES, :],
                                                                  (SUBLANES, LANES))

    def sweep2(i, hs):
        nh = []
        for sl in range(n_slab):
            av = a_ref[sl, pl.ds(i, SUBLANES, stride=pitch), :]
            bv = b_ref[sl, pl.ds(i, SUBLANES, stride=pitch), :]
            hv = av * hs[sl] + bv
            h_ref[sl, pl.ds(i, SUBLANES, stride=pitch), :] = hv
            nh.append(hv)
        return tuple(nh)

    lax.fori_loop(0, pitch, sweep2, tuple(h0))

    for sl in range(n_slab):
        cs = slice(sl * LANES, (sl + 1) * LANES)
        g = z_ref[:, c + sl * LANES:c + (sl + 1) * LANES]
        ya_ref[:, cs] = (h_ref[sl, 0:tm, :] * jax.nn.gelu(g)).astype(_BF16)

    t_idx = (s * tm + lax.broadcasted_iota(jnp.int32, (tm, 1), 0) + 1).astype(_F32)
    n_grp = len(POOL_WINDOWS)
    gd = c // n_grp
    for gi, w in enumerate(POOL_WINDOWS):
        cs = slice(gi * gd, (gi + 1) * gd)
        e = ep_ref[:, cs]
        acc = e
        d = 1
        while d < w:
            acc = acc + pltpu.roll(acc, d, 0)
            d *= 2
        xt = e[hist_p:, :]
        cnt = jnp.minimum(t_idx, float(w))
        dd = acc[hist_p:, :] / cnt - xt
        yb = jnp.dot(dd.astype(_BF16), wp_ref[gi], preferred_element_type=_F32) * ps_ref[:, cs]
        yb_ref[:, cs] = yb.astype(_BF16)


def _mixer(z, bsz, seq, cw, cb, wa_bd, ba, wx_bd, bx, lam, wp, ps, tm):
    t = z.shape[0]
    c = cb.shape[1]
    ns = seq // tm
    pitch = _scan_pitch(tm)
    n_slab = c // LANES
    kern = functools.partial(_mixer_kernel, tm=tm, pitch=pitch)
    return pl.pallas_call(
        kern,
        grid=(bsz, ns),
        in_specs=[pl.BlockSpec((tm, 3 * c), lambda b, s: (b * ns + s, 0)),
                  _const_spec(cw.shape), _const_spec(cb.shape),
                  _const_spec(wa_bd.shape), _const_spec(ba.shape),
                  _const_spec(wx_bd.shape), _const_spec(bx.shape),
                  _const_spec(lam.shape), _const_spec(wp.shape), _const_spec(ps.shape)],
        out_specs=[pl.BlockSpec((tm, c), lambda b, s: (b * ns + s, 0)),
                   pl.BlockSpec((tm, c), lambda b, s: (b * ns + s, 0))],
        out_shape=[jax.ShapeDtypeStruct((t, c), _BF16), jax.ShapeDtypeStruct((t, c), _BF16)],
        scratch_shapes=[pltpu.VMEM((SUBLANES + tm, c), _F32),
                        pltpu.VMEM((2 * SUBLANES + tm, c), _F32),
                        pltpu.VMEM((n_slab, SUBLANES * pitch, LANES), _F32),
                        pltpu.VMEM((n_slab, SUBLANES * pitch, LANES), _F32),
                        pltpu.VMEM((n_slab, SUBLANES * pitch, LANES), _F32),
                        pltpu.VMEM((SUBLANES, c), _F32)],
        compiler_params=_params(2),
        name="mixer",
    )(z, cw, cb, wa_bd, ba, wx_bd, bx, lam, wp, ps)


def _combine_kernel(ya_ref, yb_ref, ga0_ref, ga1_ref, gb0_ref, gb1_ref, x_ref,
                    wa_ref, wb_ref, wo_ref, n2_ref, wr_ref, br_ref,
                    xe_ref, cnt_ref, cnt_s, *, tm, n_groups, epg):
    i = pl.program_id(0)
    d = x_ref.shape[1]
    half = d // 2

    @pl.when(i == 0)
    def _():
        cnt_s[...] = jnp.zeros(cnt_s.shape, _F32)

    pa = jnp.dot(ya_ref[...], wa_ref[...], preferred_element_type=_F32)
    pb = jnp.dot(yb_ref[...], wb_ref[...], preferred_element_type=_F32)
    u0 = jax.nn.sigmoid(ga0_ref[...]) * pa[:, :half] + jax.nn.sigmoid(gb0_ref[...]) * pb[:, :half]
    u1 = jax.nn.sigmoid(ga1_ref[...]) * pa[:, half:] + jax.nn.sigmoid(gb1_ref[...]) * pb[:, half:]
    u = jnp.concatenate([u0, u1], axis=1).astype(_BF16)
    x1 = x_ref[...] + jnp.dot(u, wo_ref[...], preferred_element_type=_F32)
    xe_ref[:, 0:d] = x1

    ht = _rms(x1, n2_ref[...]).astype(_BF16)
    logits = jnp.dot(ht, wr_ref[...], preferred_element_type=_F32) + br_ref[...]

    lane = lax.broadcasted_iota(jnp.int32, (tm, LANES), 1).astype(_F32)
    ninf = -jnp.inf
    big = float(LANES)

    def first_argmax(v):
        m = jnp.max(v, axis=-1, keepdims=True)
        return m, jnp.min(jnp.where(v == m, lane, big), axis=-1, keepdims=True)

    is_g = lane < float(n_groups)
    gmax, gidx = first_argmax(jnp.where(is_g, logits, ninf))
    g_w = 1.0 / jnp.sum(jnp.where(is_g, jnp.exp(logits - gmax), 0.0), axis=-1, keepdims=True)
    lo_lane = float(n_groups) + float(epg) * gidx
    in_grp = (lane >= lo_lane) & (lane < lo_lane + float(epg))
    le = jnp.where(in_grp, logits, ninf)
    m1, i1 = first_argmax(le)
    m2, i2 = first_argmax(jnp.where(lane == i1, ninf, le))
    e21 = jnp.exp(m2 - m1)
    w1 = g_w / (1.0 + e21)
    w2 = w1 * e21
    e1 = i1 - lo_lane
    e2 = i2 - lo_lane
    lo = jnp.minimum(e1, e2)
    hi = jnp.maximum(e1, e2)
    w_lo = jnp.where(e1 < e2, w1, w2)
    w_hi = jnp.where(e1 < e2, w2, w1)
    pair = jnp.where(lo == 0.0, hi - 1.0, jnp.where(lo == 1.0, 6.0 - hi, 5.0))
    swap = pair == 5.0
    w_a = jnp.where(swap, w_hi, w_lo)
    w_b = jnp.where(swap, w_lo, w_hi)
    bucket = float(_N_PAIRS) * gidx + pair

    onehot = lane == bucket
    oh_bf = jnp.where(onehot, 1.0, 0.0).astype(_BF16)
    rr = lax.broadcasted_iota(jnp.int32, (tm, tm), 0)
    cc = lax.broadcasted_iota(jnp.int32, (tm, tm), 1)
    tri = jnp.where(cc < rr, 1.0, 0.0).astype(_BF16)
    before = jnp.dot(tri, oh_bf, preferred_element_type=_F32) + cnt_s[...]
    rank = jnp.sum(jnp.where(onehot, before, 0.0), axis=-1, keepdims=True)
    cnt_s[...] = cnt_s[...] + jnp.sum(jnp.where(onehot, 1.0, 0.0), axis=0, keepdims=True)
    cnt_ref[...] = cnt_s[...]

    info = jnp.where(lane == 0.0, bucket,
                     jnp.where(lane == 1.0, rank,
                               jnp.where(lane == 2.0, w_a, jnp.where(lane == 3.0, w_b, 0.0))))
    xe_ref[:, d:d + LANES] = info


def _combine(ya, yb, z, x2d, wa, wb, wo, n2, wr, br, tm, n_groups, epg):
    t, d = x2d.shape
    c = ya.shape[1]
    half = d // 2
    off = (3 * c) // half
    kern = functools.partial(_combine_kernel, tm=tm, n_groups=n_groups, epg=epg)

    def zspec(j):
        return pl.BlockSpec((tm, half), lambda i, j=j: (i, off + j))

    return pl.pallas_call(
        kern,
        grid=(t // tm,),
        in_specs=[pl.BlockSpec((tm, c), lambda i: (i, 0)),
                  pl.BlockSpec((tm, c), lambda i: (i, 0)),
                  zspec(0), zspec(1), zspec(2), zspec(3),
                  pl.BlockSpec((tm, d), lambda i: (i, 0)),
                  _const_spec(wa.shape), _const_spec(wb.shape), _const_spec(wo.shape),
                  _const_spec(n2.shape), _const_spec(wr.shape), _const_spec(br.shape)],
        out_specs=[pl.BlockSpec((tm, d + LANES), lambda i: (i, 0)),
                   pl.BlockSpec((1, LANES), lambda i: (0, 0))],
        out_shape=[jax.ShapeDtypeStruct((t, d + LANES), _F32),
                   jax.ShapeDtypeStruct((1, LANES), _F32)],
        scratch_shapes=[pltpu.VMEM((1, LANES), _F32)],
        compiler_params=_params(1),
        name="combine",
    )(ya, yb, z, z, z, z, x2d, wa, wb, wo, n2, wr, br)


def _row_gather_start(idx_ref, base, src_hbm, buf, slot, sem, n_rows, unrolled):
    def start(r):
        row = idx_ref[base + r]
        pltpu.make_async_copy(src_hbm.at[pl.ds(row, 1)], buf.at[slot, pl.ds(r, 1)],
                              sem.at[slot]).start(priority=1)

    if unrolled:
        for r in range(n_rows):
            start(r)
    else:
        def body(r, _):
            start(r)
            return 0
        lax.fori_loop(0, n_rows, body, 0, unroll=8)


def _row_gather_wait(src_hbm, buf, slot, sem, n_rows):
    pltpu.make_async_copy(src_hbm.at[pl.ds(0, n_rows)], buf.at[slot], sem.at[slot]).wait()


def _moe_kernel(src_ref, ea_ref, eb_ref, valid_ref,
                xe_hbm, n2_ref, wga_ref, wua_ref, wda_ref, wgb_ref, wub_ref, wdb_ref,
                x2s_ref, xbuf, gsem, *, tm):
    j = pl.program_id(0)
    slot = j % 2
    d = x2s_ref.shape[1]

    @pl.when(j == 0)
    def _():
        _row_gather_start(src_ref, 0, xe_hbm, xbuf, 0, gsem, tm, unrolled=False)

    @pl.when(valid_ref[j] == 1)
    def _():
        _row_gather_wait(xe_hbm, xbuf, slot, gsem, tm)
        _row_gather_start(src_ref, (j + 1) * tm, xe_hbm, xbuf, 1 - slot, gsem, tm, unrolled=True)
        xe = xbuf[slot]
        x1 = xe[:, 0:d]
        w_a = xe[:, d + 2:d + 3]
        w_b = xe[:, d + 3:d + 4]
        ht = _rms(x1, n2_ref[...]).astype(_BF16)

        def expert(wg_ref, wu_ref, wgt):
            hg = jnp.dot(ht, wg_ref[0], preferred_element_type=_F32)
            hu = jnp.dot(ht, wu_ref[0], preferred_element_type=_F32)
            return (hg * jax.nn.sigmoid(hg) * hu * wgt).astype(_BF16)

        y = jnp.dot(expert(wga_ref, wua_ref, w_a), wda_ref[0], preferred_element_type=_F32)
        y = y + jnp.dot(expert(wgb_ref, wub_ref, w_b), wdb_ref[0], preferred_element_type=_F32)

        x2s_ref[...] = x1 + y

    @pl.when(valid_ref[j] == 0)
    def _():
        x2s_ref[...] = jnp.zeros(x2s_ref.shape, _F32)

        @pl.when(valid_ref[jnp.maximum(j - 1, 0)] == 1)
        def _():
            _row_gather_wait(xe_hbm, xbuf, slot, gsem, tm)


def _moe(src, ea, eb, valid, xe, n2, wg, wu, wd, tm):
    n_tiles = valid.shape[0]
    d = wg.shape[1]
    f = wg.shape[2]
    kern = functools.partial(_moe_kernel, tm=tm)

    def wspec(shape, which):
        if which == 0:
            return pl.BlockSpec((1,) + shape, lambda j, s, a, b, v: (a[j], 0, 0))
        return pl.BlockSpec((1,) + shape, lambda j, s, a, b, v: (b[j], 0, 0))

    grid_spec = pltpu.PrefetchScalarGridSpec(
        num_scalar_prefetch=4,
        grid=(n_tiles,),
        in_specs=[pl.BlockSpec(memory_space=pl.ANY),
                  pl.BlockSpec((1, d), lambda j, *_: (0, 0)),
                  wspec((d, f), 0), wspec((d, f), 0), wspec((f, d), 0),
                  wspec((d, f), 1), wspec((d, f), 1), wspec((f, d), 1)],
        out_specs=pl.BlockSpec((tm, d), lambda j, *_: (j, 0)),
        scratch_shapes=[pltpu.VMEM((2, tm, d + LANES), _F32),
                        pltpu.SemaphoreType.DMA((2,))],
    )
    return pl.pallas_call(
        kern,
        grid_spec=grid_spec,
        out_shape=jax.ShapeDtypeStruct((n_tiles * tm, d), _F32),
        compiler_params=_params(1),
        name="moe",
    )(src, ea, eb, valid, xe, n2, wg, wu, wd, wg, wu, wd)


def _ple_kernel(pos_ref, x2s_hbm, p_ref, ng_ref, wg_ref, wp_ref, nf_ref, o_ref, xbuf, gsem, *, tm, n_steps):
    i = pl.program_id(0)
    slot = i % 2

    @pl.when(i == 0)
    def _():
        _row_gather_start(pos_ref, 0, x2s_hbm, xbuf, 0, gsem, tm, unrolled=False)

    _row_gather_wait(x2s_hbm, xbuf, slot, gsem, tm)
    nxt = jnp.where(i + 1 == n_steps, 0, i + 1)
    _row_gather_start(pos_ref, nxt * tm, x2s_hbm, xbuf, 1 - slot, gsem, tm, unrolled=True)
    x2 = xbuf[slot]
    g = jax.nn.sigmoid(jnp.dot(_rms(x2, ng_ref[...]).astype(_BF16), wg_ref[...],
                               preferred_element_type=_F32))
    e = jnp.dot(p_ref[...].astype(_BF16), wp_ref[...], preferred_element_type=_F32)
    o_ref[...] = _rms(x2 + g * e, nf_ref[...])

    @pl.when(i == n_steps - 1)
    def _():
        _row_gather_wait(x2s_hbm, xbuf, 1 - slot, gsem, tm)


def _ple(pos, x2s, p2d, ng, wg, wp, nf, tm, t):
    d = x2s.shape[1]
    pd = p2d.shape[1]
    n_steps = t // tm
    kern = functools.partial(_ple_kernel, tm=tm, n_steps=n_steps)

    def cspec(shape):
        nd = len(shape)
        return pl.BlockSpec(shape, lambda i, *_: (0,) * nd, pipeline_mode=pl.Buffered(1))

    grid_spec = pltpu.PrefetchScalarGridSpec(
        num_scalar_prefetch=1,
        grid=(n_steps,),
        in_specs=[pl.BlockSpec(memory_space=pl.ANY),
                  pl.BlockSpec((tm, pd), lambda i, *_: (i, 0)),
                  cspec(ng.shape), cspec(wg.shape), cspec(wp.shape), cspec(nf.shape)],
        out_specs=pl.BlockSpec((tm, d), lambda i, *_: (i, 0)),
        scratch_shapes=[pltpu.VMEM((2, tm, d), _F32),
                        pltpu.SemaphoreType.DMA((2,))],
    )
    return pl.pallas_call(
        kern,
        grid_spec=grid_spec,
        out_shape=jax.ShapeDtypeStruct((t, d), _F32),
        compiler_params=_params(1),
        name="ple",
    )(pos, x2s, p2d, ng, wg, wp, nf)


def _block_diag(w, per_block):
    h, hd, _ = w.shape
    nb = h // per_block
    w4 = w.reshape(nb, per_block, hd, hd)
    eye = jnp.eye(per_block, dtype=w.dtype)
    out = jnp.einsum("bpij,pq->bpiqj", w4, eye)
    return out.reshape(nb, per_block * hd, per_block * hd)


def _layer(x2d, p2d, bsz, seq, norm1_g, w_in, conv_w, conv_b, w_rg_a, b_rg_a, w_rg_x, b_rg_x, lru_lambda,
           w_pool, pool_scale, w_branch_a, w_branch_b, w_out, norm2_g, w_router_group, b_router_group,
           w_router_expert, b_router_expert, w_e_gate, w_e_up, w_e_down, norm_ple_g, w_ple_gate,
           w_ple_proj, out_norm_g):
    t, d = x2d.shape
    c = conv_b.shape[0]
    heads, hd, _ = w_rg_a.shape
    n_groups = w_router_group.shape[1]
    n_exp = w_router_expert.shape[1]
    epg = n_exp // n_groups
    assert epg == 4 and TOP_K == 2 and hd * (MXU_DIM // hd) == MXU_DIM
    assert w_pool.shape[0] == len(POOL_WINDOWS) and w_pool.shape[1] == MXU_DIM

    tm_in, tn_in = min(1024, t), 1024
    tm_mix = min(256, seq)
    tm_cmb = min(256, t)
    tm_moe = min(256, t)
    tm_ple = min(256, t)

    row = lambda v: v.reshape(1, -1).astype(_F32)
    per_block = MXU_DIM // hd

    z = _inproj(x2d, row(norm1_g), w_in.astype(_BF16), tm_in, tn_in)
    ya, yb = _mixer(z, bsz, seq, conv_w.reshape(CONV_WIDTH, c), row(conv_b),
                    _block_diag(w_rg_a, per_block).astype(_BF16), row(b_rg_a),
                    _block_diag(w_rg_x, per_block).astype(_BF16), row(b_rg_x),
                    row(lru_lambda), w_pool.astype(_BF16), row(pool_scale), tm_mix)

    wr = jnp.zeros((d, LANES), _F32).at[:, :n_groups].set(w_router_group)
    wr = wr.at[:, n_groups:n_groups + n_exp].set(w_router_expert).astype(_BF16)
    br = jnp.zeros((1, LANES), _F32).at[0, :n_groups].set(b_router_group)
    br = br.at[0, n_groups:n_groups + n_exp].set(b_router_expert)
    xe, counts = _combine(ya, yb, z, x2d, w_branch_a.astype(_BF16), w_branch_b.astype(_BF16),
                          w_out.astype(_BF16), row(norm2_g), wr, br, tm_cmb, n_groups, epg)

    n_buckets = n_groups * _N_PAIRS
    n_tiles = t // tm_moe + n_buckets
    cnt = counts[0, :n_buckets].astype(jnp.int32)
    nt = (cnt + tm_moe - 1) // tm_moe
    cum = jnp.cumsum(nt)
    total = cum[-1]
    bucket = xe[:, d].astype(jnp.int32)
    rank = xe[:, d + 1].astype(jnp.int32)
    pos = (cum - nt)[bucket] * tm_moe + rank
    tok = jnp.arange(t, dtype=jnp.int32)
    src = jnp.zeros((n_tiles * tm_moe,), jnp.int32).at[pos].set(tok)
    tile = jnp.minimum(jnp.arange(n_tiles, dtype=jnp.int32), total - 1)
    tb = jnp.sum((cum[None, :] <= tile[:, None]).astype(jnp.int32), axis=1)
    grp, pr = tb // _N_PAIRS, tb % _N_PAIRS
    ea = grp * epg + jnp.asarray(_PAIR_SLOT_A, jnp.int32)[pr]
    eb = grp * epg + jnp.asarray(_PAIR_SLOT_B, jnp.int32)[pr]
    valid = (jnp.arange(n_tiles, dtype=jnp.int32) < total).astype(jnp.int32)

    x2s = _moe(src, ea, eb, valid, xe, row(norm2_g), w_e_gate.astype(_BF16), w_e_up.astype(_BF16),
               w_e_down.astype(_BF16), tm_moe)
    return _ple(pos, x2s, p2d, row(norm_ple_g), w_ple_gate.astype(_BF16), w_ple_proj.astype(_BF16),
                row(out_norm_g), tm_ple, t)


def kernel(x, p, norm1_g, w_in, conv_w, conv_b, w_rg_a, b_rg_a, w_rg_x, b_rg_x, lru_lambda, w_pool, pool_scale, w_branch_a, w_branch_b, w_out, norm2_g, w_router_group, b_router_group, w_router_expert, b_router_expert, w_e_gate, w_e_up, w_e_down, norm_ple_g, w_ple_gate, w_ple_proj, final_norm_g):
    bsz, seq, d = x.shape
    depth = p.shape[0]
    assert depth == 1, "the final RMSNorm is fused into the last layer's embedding kernel"
    out = _layer(x.reshape(bsz * seq, d), p[0].reshape(bsz * seq, -1), bsz, seq,
                 norm1_g[0], w_in[0], conv_w[0], conv_b[0], w_rg_a[0], b_rg_a[0], w_rg_x[0], b_rg_x[0],
                 lru_lambda[0], w_pool[0], pool_scale[0], w_branch_a[0], w_branch_b[0], w_out[0],
                 norm2_g[0], w_router_group[0], b_router_group[0], w_router_expert[0],
                 b_router_expert[0], w_e_gate[0], w_e_up[0], w_e_down[0], norm_ple_g[0],
                 w_ple_gate[0], w_ple_proj[0], final_norm_g)
    return out.reshape(bsz, seq, d)
```

```python
import functools

import jax
import jax.numpy as jnp
from jax import lax
from jax.experimental import pallas as pl
from jax.experimental.pallas import tpu as pltpu

EPS = 1e-6
LRU_C = 8.0
CONV_WIDTH = 4
POOL_WINDOWS = (2, 4, 8, 16)
TOP_K = 2

LANES = 128
SUBLANES = 8
MXU_DIM = 256
VMEM_LIMIT_BYTES = 56 * 1024 * 1024

_BF16 = jnp.bfloat16
_F32 = jnp.float32

_PAIR_SLOT_A = (0, 0, 0, 1, 1, 3)
_PAIR_SLOT_B = (1, 2, 3, 3, 2, 2)
_N_PAIRS = len(_PAIR_SLOT_A)


def _rms(x, g):
    ms = jnp.mean(x * x, axis=-1, keepdims=True)
    return x * lax.rsqrt(ms + EPS) * g


def _const_spec(shape):
    nd = len(shape)
    return pl.BlockSpec(shape, lambda *_: (0,) * nd, pipeline_mode=pl.Buffered(1))


def _params(n_axes):
    return pltpu.CompilerParams(dimension_semantics=("arbitrary",) * n_axes,
                                vmem_limit_bytes=VMEM_LIMIT_BYTES)


def _inproj_kernel(x_ref, g_ref, w_ref, z_ref, h_ref):
    @pl.when(pl.program_id(1) == 0)
    def _():
        h_ref[...] = _rms(x_ref[...], g_ref[...]).astype(_BF16)

    z_ref[...] = jnp.dot(h_ref[...], w_ref[...], preferred_element_type=_F32)


def _inproj(x2d, g, w_bf, tm, tn):
    t, d = x2d.shape
    n = w_bf.shape[1]
    return pl.pallas_call(
        _inproj_kernel,
        grid=(t // tm, n // tn),
        in_specs=[pl.BlockSpec((tm, d), lambda i, j: (i, 0)),
                  pl.BlockSpec((1, d), lambda i, j: (0, 0)),
                  pl.BlockSpec((d, tn), lambda i, j: (0, j))],
        out_specs=pl.BlockSpec((tm, tn), lambda i, j: (i, j)),
        out_shape=jax.ShapeDtypeStruct((t, n), _F32),
        scratch_shapes=[pltpu.VMEM((tm, d), _BF16)],
        compiler_params=_params(2),
        name="inproj",
    )(x2d, g, w_bf)


def _scan_pitch(tm):
    p = -(-tm // SUBLANES)
    while p % SUBLANES != 4:
        p += 1
    return p


def _mixer_kernel(z_ref, cw_ref, cb_ref, wa_ref, ba_ref, wx_ref, bx_ref, lam_ref, wp_ref, ps_ref,
                  ya_ref, yb_ref,
                  er_ref, ep_ref, a_ref, b_ref, h_ref, car_ref, *, tm, pitch):
    s = pl.program_id(1)
    c = cb_ref.shape[1]
    n_slab = c // LANES
    hist_r = SUBLANES
    hist_p = 2 * SUBLANES

    @pl.when(s == 0)
    def _():
        er_ref[0:hist_r, :] = jnp.zeros((hist_r, c), _F32)
        ep_ref[0:hist_p, :] = jnp.zeros((hist_p, c), _F32)
        car_ref[...] = jnp.zeros(car_ref.shape, _F32)
        a_ref[:, tm:, :] = jnp.ones((n_slab, SUBLANES * pitch - tm, LANES), _F32)
        b_ref[:, tm:, :] = jnp.zeros((n_slab, SUBLANES * pitch - tm, LANES), _F32)

    @pl.when(s > 0)
    def _():
        er_ref[0:hist_r, :] = er_ref[tm:tm + hist_r, :]
        ep_ref[0:hist_p, :] = ep_ref[tm:tm + hist_p, :]

    er_ref[hist_r:hist_r + tm, :] = z_ref[:, 0:c]
    ep_ref[hist_p:hist_p + tm, :] = z_ref[:, 2 * c:3 * c]

    kvec = -LRU_C * jax.nn.softplus(-lam_ref[...])
    nblk = c // MXU_DIM
    for k in range(nblk):
        cs = slice(k * MXU_DIM, (k + 1) * MXU_DIM)
        xc = cb_ref[:, cs] + cw_ref[CONV_WIDTH - 1:CONV_WIDTH, cs] * er_ref[hist_r:hist_r + tm, cs]
        for j in range(1, CONV_WIDTH):
            xc = xc + cw_ref[CONV_WIDTH - 1 - j:CONV_WIDTH - j, cs] * er_ref[hist_r - j:hist_r - j + tm, cs]
        xcb = xc.astype(_BF16)
        r = jax.nn.sigmoid(jnp.dot(xcb, wa_ref[k], preferred_element_type=_F32) + ba_ref[:, cs])
        ig = jax.nn.sigmoid(jnp.dot(xcb, wx_ref[k], preferred_element_type=_F32) + bx_ref[:, cs])
        log_a = r * kvec[:, cs]
        a = jnp.exp(log_a)
        mult = jnp.sqrt(1.0 - a * a)
        bb = mult * ig * xc
        for q in range(MXU_DIM // LANES):
            slab = k * (MXU_DIM // LANES) + q
            a_ref[slab, 0:tm, :] = a[:, q * LANES:(q + 1) * LANES]
            b_ref[slab, 0:tm, :] = bb[:, q * LANES:(q + 1) * LANES]

    def sweep1(i, carry):
        hs, ps = carry
        nh, npd = [], []
        for sl in range(n_slab):
            av = a_ref[sl, pl.ds(i, SUBLANES, stride=pitch), :]
            bv = b_ref[sl, pl.ds(i, SUBLANES, stride=pitch), :]
            nh.append(av * hs[sl] + bv)
            npd.append(av * ps[sl])
        return tuple(nh), tuple(npd)

    zero = jnp.zeros((SUBLANES, LANES), _F32)
    one = jnp.ones((SUBLANES, LANES), _F32)
    h_end, a_end = lax.fori_loop(0, pitch, sweep1, ((zero,) * n_slab, (one,) * n_slab))

    row = lax.broadcasted_iota(jnp.int32, (SUBLANES, LANES), 0)
    h0 = []
    for sl in range(n_slab):
        hh, aa = h_end[sl], a_end[sl]
        d = 1
        while d < SUBLANES:
            hs_ = jnp.where(row >= d, pltpu.roll(hh, d, 0), 0.0)
            as_ = jnp.where(row >= d, pltpu.roll(aa, d, 0), 1.0)
            hh = aa * hs_ + hh
            aa = aa * as_
            d *= 2
        cin = car_ref[:, sl * LANES:(sl + 1) * LANES]
        full = hh + aa * cin
        h0.append(jnp.where(row >= 1, pltpu.roll(full, 1, 0), cin))
        car_ref[:, sl * LANES:(sl + 1) * LANES] = jnp.broadcast_to(full[SUBLANES - 1:SUBLANES, :],
                                                                  (SUBLANES, LANES))

    def sweep2(i, hs):
        nh = []
        for sl in range(n_slab):
            av = a_ref[sl, pl.ds(i, SUBLANES, stride=pitch), :]
            bv = b_ref[sl, pl.ds(i, SUBLANES, stride=pitch), :]
            hv = av * hs[sl] + bv
            h_ref[sl, pl.ds(i, SUBLANES, stride=pitch), :] = hv
            nh.append(hv)
        return tuple(nh)

    lax.fori_loop(0, pitch, sweep2, tuple(h0))

    for sl in range(n_slab):
        cs = slice(sl * LANES, (sl + 1) * LANES)
        g = z_ref[:, c + sl * LANES:c + (sl + 1) * LANES]
        ya_ref[:, cs] = (h_ref[sl, 0:tm, :] * jax.nn.gelu(g)).astype(_BF16)

    t_idx = (s * tm + lax.broadcasted_iota(jnp.int32, (tm, 1), 0) + 1).astype(_F32)
    n_grp = len(POOL_WINDOWS)
    gd = c // n_grp
    for gi, w in enumerate(POOL_WINDOWS):
        cs = slice(gi * gd, (gi + 1) * gd)
        e = ep_ref[:, cs]
        acc = e
        d = 1
        while d < w:
            acc = acc + pltpu.roll(acc, d, 0)
            d *= 2
        xt = e[hist_p:, :]
        cnt = jnp.minimum(t_idx, float(w))
        dd = acc[hist_p:, :] / cnt - xt
        yb = jnp.dot(dd.astype(_BF16), wp_ref[gi], preferred_element_type=_F32) * ps_ref[:, cs]
        yb_ref[:, cs] = yb.astype(_BF16)


def _mixer(z, bsz, seq, cw, cb, wa_bd, ba, wx_bd, bx, lam, wp, ps, tm):
    t = z.shape[0]
    c = cb.shape[1]
    ns = seq // tm
    pitch = _scan_pitch(tm)
    n_slab = c // LANES
    kern = functools.partial(_mixer_kernel, tm=tm, pitch=pitch)
    return pl.pallas_call(
        kern,
        grid=(bsz, ns),
        in_specs=[pl.BlockSpec((tm, 3 * c), lambda b, s: (b * ns + s, 0)),
                  _const_spec(cw.shape), _const_spec(cb.shape),
                  _const_spec(wa_bd.shape), _const_spec(ba.shape),
                  _const_spec(wx_bd.shape), _const_spec(bx.shape),
                  _const_spec(lam.shape), _const_spec(wp.shape), _const_spec(ps.shape)],
        out_specs=[pl.BlockSpec((tm, c), lambda b, s: (b * ns + s, 0)),
                   pl.BlockSpec((tm, c), lambda b, s: (b * ns + s, 0))],
        out_shape=[jax.ShapeDtypeStruct((t, c), _BF16), jax.ShapeDtypeStruct((t, c), _BF16)],
        scratch_shapes=[pltpu.VMEM((SUBLANES + tm, c), _F32),
                        pltpu.VMEM((2 * SUBLANES + tm, c), _F32),
                        pltpu.VMEM((n_slab, SUBLANES * pitch, LANES), _F32),
                        pltpu.VMEM((n_slab, SUBLANES * pitch, LANES), _F32),
                        pltpu.VMEM((n_slab, SUBLANES * pitch, LANES), _F32),
                        pltpu.VMEM((SUBLANES, c), _F32)],
        compiler_params=_params(2),
        name="mixer",
    )(z, cw, cb, wa_bd, ba, wx_bd, bx, lam, wp, ps)


def _combine_kernel(ya_ref, yb_ref, ga0_ref, ga1_ref, gb0_ref, gb1_ref, x_ref,
                    wa_ref, wb_ref, wo_ref, n2_ref, wr_ref, br_ref,
                    xe_ref, cnt_ref, cnt_s, *, tm, n_groups, epg):
    i = pl.program_id(0)
    d = x_ref.shape[1]
    half = d // 2

    @pl.when(i == 0)
    def _():
        cnt_s[...] = jnp.zeros(cnt_s.shape, _F32)

    pa = jnp.dot(ya_ref[...], wa_ref[...], preferred_element_type=_F32)
    pb = jnp.dot(yb_ref[...], wb_ref[...], preferred_element_type=_F32)
    u0 = jax.nn.sigmoid(ga0_ref[...]) * pa[:, :half] + jax.nn.sigmoid(gb0_ref[...]) * pb[:, :half]
    u1 = jax.nn.sigmoid(ga1_ref[...]) * pa[:, half:] + jax.nn.sigmoid(gb1_ref[...]) * pb[:, half:]
    u = jnp.concatenate([u0, u1], axis=1).astype(_BF16)
    x1 = x_ref[...] + jnp.dot(u, wo_ref[...], preferred_element_type=_F32)
    xe_ref[:, 0:d] = x1

    ht = _rms(x1, n2_ref[...]).astype(_BF16)
    logits = jnp.dot(ht, wr_ref[...], preferred_element_type=_F32) + br_ref[...]

    lane = lax.broadcasted_iota(jnp.int32, (tm, LANES), 1).astype(_F32)
    ninf = -jnp.inf
    big = float(LANES)

    def first_argmax(v):
        m = jnp.max(v, axis=-1, keepdims=True)
        return m, jnp.min(jnp.where(v == m, lane, big), axis=-1, keepdims=True)

    is_g = lane < float(n_groups)
    gmax, gidx = first_argmax(jnp.where(is_g, logits, ninf))
    g_w = 1.0 / jnp.sum(jnp.where(is_g, jnp.exp(logits - gmax), 0.0), axis=-1, keepdims=True)
    lo_lane = float(n_groups) + float(epg) * gidx
    in_grp = (lane >= lo_lane) & (lane < lo_lane + float(epg))
    le = jnp.where(in_grp, logits, ninf)
    m1, i1 = first_argmax(le)
    m2, i2 = first_argmax(jnp.where(lane == i1, ninf, le))
    e21 = jnp.exp(m2 - m1)
    w1 = g_w / (1.0 + e21)
    w2 = w1 * e21
    e1 = i1 - lo_lane
    e2 = i2 - lo_lane
    lo = jnp.minimum(e1, e2)
    hi = jnp.maximum(e1, e2)
    w_lo = jnp.where(e1 < e2, w1, w2)
    w_hi = jnp.where(e1 < e2, w2, w1)
    pair = jnp.where(lo == 0.0, hi - 1.0, jnp.where(lo == 1.0, 6.0 - hi, 5.0))
    swap = pair == 5.0
    w_a = jnp.where(swap, w_hi, w_lo)
    w_b = jnp.where(swap, w_lo, w_hi)
    bucket = float(_N_PAIRS) * gidx + pair

    onehot = lane == bucket
    oh_bf = jnp.where(onehot, 1.0, 0.0).astype(_BF16)
    rr = lax.broadcasted_iota(jnp.int32, (tm, tm), 0)
    cc = lax.broadcasted_iota(jnp.int32, (tm, tm), 1)
    tri = jnp.where(cc < rr, 1.0, 0.0).astype(_BF16)
    before = jnp.dot(tri, oh_bf, preferred_element_type=_F32) + cnt_s[...]
    rank = jnp.sum(jnp.where(onehot, before, 0.0), axis=-1, keepdims=True)
    cnt_s[...] = cnt_s[...] + jnp.sum(jnp.where(onehot, 1.0, 0.0), axis=0, keepdims=True)
    cnt_ref[...] = cnt_s[...]

    info = jnp.where(lane == 0.0, bucket,
                     jnp.where(lane == 1.0, rank,
                               jnp.where(lane == 2.0, w_a, jnp.where(lane == 3.0, w_b, 0.0))))
    xe_ref[:, d:d + LANES] = info


def _combine(ya, yb, z, x2d, wa, wb, wo, n2, wr, br, tm, n_groups, epg):
    t, d = x2d.shape
    c = ya.shape[1]
    half = d // 2
    off = (3 * c) // half
    kern = functools.partial(_combine_kernel, tm=tm, n_groups=n_groups, epg=epg)

    def zspec(j):
        return pl.BlockSpec((tm, half), lambda i, j=j: (i, off + j))

    return pl.pallas_call(
        kern,
        grid=(t // tm,),
        in_specs=[pl.BlockSpec((tm, c), lambda i: (i, 0)),
                  pl.BlockSpec((tm, c), lambda i: (i, 0)),
                  zspec(0), zspec(1), zspec(2), zspec(3),
                  pl.BlockSpec((tm, d), lambda i: (i, 0)),
                  _const_spec(wa.shape), _const_spec(wb.shape), _const_spec(wo.shape),
                  _const_spec(n2.shape), _const_spec(wr.shape), _const_spec(br.shape)],
        out_specs=[pl.BlockSpec((tm, d + LANES), lambda i: (i, 0)),
                   pl.BlockSpec((1, LANES), lambda i: (0, 0))],
        out_shape=[jax.ShapeDtypeStruct((t, d + LANES), _F32),
                   jax.ShapeDtypeStruct((1, LANES), _F32)],
        scratch_shapes=[pltpu.VMEM((1, LANES), _F32)],
        compiler_params=_params(1),
        name="combine",
    )(ya, yb, z, z, z, z, x2d, wa, wb, wo, n2, wr, br)


def _row_gather_start(idx_ref, base, src_hbm, buf, slot, sem, n_rows, unrolled, width=None):
    width = src_hbm.shape[1] if width is None else width

    def start(r):
        row = idx_ref[base + r]
        pltpu.make_async_copy(src_hbm.at[pl.ds(row, 1), pl.ds(0, width)],
                              buf.at[slot, pl.ds(r, 1), pl.ds(0, width)],
                              sem.at[slot]).start(priority=1)

    if unrolled:
        for r in range(n_rows):
            start(r)
    else:
        def body(r, _):
            start(r)
            return 0
        lax.fori_loop(0, n_rows, body, 0, unroll=8)


def _row_gather_wait(src_hbm, buf, slot, sem, n_rows, width=None):
    width = src_hbm.shape[1] if width is None else width
    pltpu.make_async_copy(src_hbm.at[pl.ds(0, n_rows), pl.ds(0, width)],
                          buf.at[slot, :, pl.ds(0, width)], sem.at[slot]).wait()


def _moe_kernel(src_ref, ea_ref, eb_ref, valid_ref,
                xe_hbm, n2_ref, wga_ref, wua_ref, wda_ref, wgb_ref, wub_ref, wdb_ref,
                x2s_ref, xbuf, gsem, *, tm):
    j = pl.program_id(0)
    slot = j % 2
    d = x2s_ref.shape[1]

    @pl.when(j == 0)
    def _():
        _row_gather_start(src_ref, 0, xe_hbm, xbuf, 0, gsem, tm, unrolled=False)

    @pl.when(valid_ref[j] == 1)
    def _():
        _row_gather_wait(xe_hbm, xbuf, slot, gsem, tm)
        _row_gather_start(src_ref, (j + 1) * tm, xe_hbm, xbuf, 1 - slot, gsem, tm, unrolled=True)
        xe = xbuf[slot]
        x1 = xe[:, 0:d]
        w_a = xe[:, d + 2:d + 3]
        w_b = xe[:, d + 3:d + 4]
        ht = _rms(x1, n2_ref[...]).astype(_BF16)

        def expert(wg_ref, wu_ref, wgt):
            hg = jnp.dot(ht, wg_ref[0], preferred_element_type=_F32)
            hu = jnp.dot(ht, wu_ref[0], preferred_element_type=_F32)
            return (hg * jax.nn.sigmoid(hg) * hu * wgt).astype(_BF16)

        y = jnp.dot(expert(wga_ref, wua_ref, w_a), wda_ref[0], preferred_element_type=_F32)
        y = y + jnp.dot(expert(wgb_ref, wub_ref, w_b), wdb_ref[0], preferred_element_type=_F32)

        x2s_ref[...] = x1 + y

    @pl.when(valid_ref[j] == 0)
    def _():
        x2s_ref[...] = jnp.zeros(x2s_ref.shape, _F32)

        @pl.when(valid_ref[jnp.maximum(j - 1, 0)] == 1)
        def _():
            _row_gather_wait(xe_hbm, xbuf, slot, gsem, tm)


def _moe(src, ea, eb, valid, xe, n2, wg, wu, wd, tm):
    n_tiles = valid.shape[0]
    d = wg.shape[1]
    f = wg.shape[2]
    kern = functools.partial(_moe_kernel, tm=tm)

    def wspec(shape, which):
        if which == 0:
            return pl.BlockSpec((1,) + shape, lambda j, s, a, b, v: (a[j], 0, 0))
        return pl.BlockSpec((1,) + shape, lambda j, s, a, b, v: (b[j], 0, 0))

    grid_spec = pltpu.PrefetchScalarGridSpec(
        num_scalar_prefetch=4,
        grid=(n_tiles,),
        in_specs=[pl.BlockSpec(memory_space=pl.ANY),
                  pl.BlockSpec((1, d), lambda j, *_: (0, 0)),
                  wspec((d, f), 0), wspec((d, f), 0), wspec((f, d), 0),
                  wspec((d, f), 1), wspec((d, f), 1), wspec((f, d), 1)],
        out_specs=pl.BlockSpec((tm, d), lambda j, *_: (j, 0)),
        scratch_shapes=[pltpu.VMEM((2, tm, d + LANES), _F32),
                        pltpu.SemaphoreType.DMA((2,))],
    )
    return pl.pallas_call(
        kern,
        grid_spec=grid_spec,
        out_shape=jax.ShapeDtypeStruct((n_tiles * tm, d), _F32),
        compiler_params=_params(1),
        name="moe",
    )(src, ea, eb, valid, xe, n2, wg, wu, wd, wg, wu, wd)


def _ple_kernel(pos_ref, x2s_hbm, p_ref, ng_ref, wg_ref, wp_ref, nf_ref, o_ref, xbuf, gsem, *, tm, n_steps):
    i = pl.program_id(0)
    slot = i % 2

    @pl.when(i == 0)
    def _():
        _row_gather_start(pos_ref, 0, x2s_hbm, xbuf, 0, gsem, tm, unrolled=False)

    _row_gather_wait(x2s_hbm, xbuf, slot, gsem, tm)
    nxt = jnp.where(i + 1 == n_steps, 0, i + 1)
    _row_gather_start(pos_ref, nxt * tm, x2s_hbm, xbuf, 1 - slot, gsem, tm, unrolled=True)
    x2 = xbuf[slot]
    g = jax.nn.sigmoid(jnp.dot(_rms(x2, ng_ref[...]).astype(_BF16), wg_ref[...],
                               preferred_element_type=_F32))
    e = jnp.dot(p_ref[...].astype(_BF16), wp_ref[...], preferred_element_type=_F32)
    o_ref[...] = _rms(x2 + g * e, nf_ref[...])

    @pl.when(i == n_steps - 1)
    def _():
        _row_gather_wait(x2s_hbm, xbuf, 1 - slot, gsem, tm)


def _ple(pos, x2s, p2d, ng, wg, wp, nf, tm, t):
    d = x2s.shape[1]
    pd = p2d.shape[1]
    n_steps = t // tm
    kern = functools.partial(_ple_kernel, tm=tm, n_steps=n_steps)

    def cspec(shape):
        nd = len(shape)
        return pl.BlockSpec(shape, lambda i, *_: (0,) * nd, pipeline_mode=pl.Buffered(1))

    grid_spec = pltpu.PrefetchScalarGridSpec(
        num_scalar_prefetch=1,
        grid=(n_steps,),
        in_specs=[pl.BlockSpec(memory_space=pl.ANY),
                  pl.BlockSpec((tm, pd), lambda i, *_: (i, 0)),
                  cspec(ng.shape), cspec(wg.shape), cspec(wp.shape), cspec(nf.shape)],
        out_specs=pl.BlockSpec((tm, d), lambda i, *_: (i, 0)),
        scratch_shapes=[pltpu.VMEM((2, tm, d), _F32),
                        pltpu.SemaphoreType.DMA((2,))],
    )
    return pl.pallas_call(
        kern,
        grid_spec=grid_spec,
        out_shape=jax.ShapeDtypeStruct((t, d), _F32),
        compiler_params=_params(1),
        name="ple",
    )(pos, x2s, p2d, ng, wg, wp, nf)


def _block_diag(w, per_block):
    h, hd, _ = w.shape
    nb = h // per_block
    w4 = w.reshape(nb, per_block, hd, hd)
    eye = jnp.eye(per_block, dtype=w.dtype)
    out = jnp.einsum("bpij,pq->bpiqj", w4, eye)
    return out.reshape(nb, per_block * hd, per_block * hd)


def _layer(x2d, p2d, bsz, seq, norm1_g, w_in, conv_w, conv_b, w_rg_a, b_rg_a, w_rg_x, b_rg_x, lru_lambda,
           w_pool, pool_scale, w_branch_a, w_branch_b, w_out, norm2_g, w_router_group, b_router_group,
           w_router_expert, b_router_expert, w_e_gate, w_e_up, w_e_down, norm_ple_g, w_ple_gate,
           w_ple_proj, out_norm_g):
    t, d = x2d.shape
    c = conv_b.shape[0]
    heads, hd, _ = w_rg_a.shape
    n_groups = w_router_group.shape[1]
    n_exp = w_router_expert.shape[1]
    epg = n_exp // n_groups
    assert epg == 4 and TOP_K == 2 and hd * (MXU_DIM // hd) == MXU_DIM
    assert w_pool.shape[0] == len(POOL_WINDOWS) and w_pool.shape[1] == MXU_DIM

    tm_in, tn_in = min(1024, t), 1024
    tm_mix = min(256, seq)
    tm_cmb = min(256, t)
    tm_moe = min(256, t)
    tm_ple = min(256, t)

    row = lambda v: v.reshape(1, -1).astype(_F32)
    per_block = MXU_DIM // hd

    z = _inproj(x2d, row(norm1_g), w_in.astype(_BF16), tm_in, tn_in)
    ya, yb = _mixer(z, bsz, seq, conv_w.reshape(CONV_WIDTH, c), row(conv_b),
                    _block_diag(w_rg_a, per_block).astype(_BF16), row(b_rg_a),
                    _block_diag(w_rg_x, per_block).astype(_BF16), row(b_rg_x),
                    row(lru_lambda), w_pool.astype(_BF16), row(pool_scale), tm_mix)

    wr = jnp.zeros((d, LANES), _F32).at[:, :n_groups].set(w_router_group)
    wr = wr.at[:, n_groups:n_groups + n_exp].set(w_router_expert).astype(_BF16)
    br = jnp.zeros((1, LANES), _F32).at[0, :n_groups].set(b_router_group)
    br = br.at[0, n_groups:n_groups + n_exp].set(b_router_expert)
    xe, counts = _combine(ya, yb, z, x2d, w_branch_a.astype(_BF16), w_branch_b.astype(_BF16),
                          w_out.astype(_BF16), row(norm2_g), wr, br, tm_cmb, n_groups, epg)

    n_buckets = n_groups * _N_PAIRS
    n_tiles = t // tm_moe + n_buckets
    cnt = counts[0, :n_buckets].astype(jnp.int32)
    nt = (cnt + tm_moe - 1) // tm_moe
    cum = jnp.cumsum(nt)
    total = cum[-1]
    bucket = xe[:, d].astype(jnp.int32)
    rank = xe[:, d + 1].astype(jnp.int32)
    pos = (cum - nt)[bucket] * tm_moe + rank
    tok = jnp.arange(t, dtype=jnp.int32)
    src = (jnp.arange(n_tiles * tm_moe, dtype=jnp.int32) % t).at[pos].set(tok)
    tile = jnp.minimum(jnp.arange(n_tiles, dtype=jnp.int32), total - 1)
    tb = jnp.sum((cum[None, :] <= tile[:, None]).astype(jnp.int32), axis=1)
    grp, pr = tb // _N_PAIRS, tb % _N_PAIRS
    ea = grp * epg + jnp.asarray(_PAIR_SLOT_A, jnp.int32)[pr]
    eb = grp * epg + jnp.asarray(_PAIR_SLOT_B, jnp.int32)[pr]
    valid = (jnp.arange(n_tiles, dtype=jnp.int32) < total).astype(jnp.int32)

    x2s = _moe(src, ea, eb, valid, xe, row(norm2_g), w_e_gate.astype(_BF16), w_e_up.astype(_BF16),
               w_e_down.astype(_BF16), tm_moe)
    return _ple(pos, x2s, p2d, row(norm_ple_g), w_ple_gate.astype(_BF16), w_ple_proj.astype(_BF16),
                row(out_norm_g), tm_ple, t)


def kernel(x, p, norm1_g, w_in, conv_w, conv_b, w_rg_a, b_rg_a, w_rg_x, b_rg_x, lru_lambda, w_pool, pool_scale, w_branch_a, w_branch_b, w_out, norm2_g, w_router_group, b_router_group, w_router_expert, b_router_expert, w_e_gate, w_e_up, w_e_down, norm_ple_g, w_ple_gate, w_ple_proj, final_norm_g):
    bsz, seq, d = x.shape
    depth = p.shape[0]
    assert depth == 1, "the final RMSNorm is fused into the last layer's embedding kernel"
    out = _layer(x.reshape(bsz * seq, d), p[0].reshape(bsz * seq, -1), bsz, seq,
                 norm1_g[0], w_in[0], conv_w[0], conv_b[0], w_rg_a[0], b_rg_a[0], w_rg_x[0], b_rg_x[0],
                 lru_lambda[0], w_pool[0], pool_scale[0], w_branch_a[0], w_branch_b[0], w_out[0],
                 norm2_g[0], w_router_group[0], b_router_group[0], w_router_expert[0],
                 b_router_expert[0], w_e_gate[0], w_e_up[0], w_e_down[0], norm_ple_g[0],
                 w_ple_gate[0], w_ple_proj[0], final_norm_g)
    return out.reshape(bsz, seq, d)
```

```python
import functools

import jax
import jax.numpy as jnp
from jax import lax
from jax.experimental import pallas as pl
from jax.experimental.pallas import tpu as pltpu

EPS = 1e-6
LRU_C = 8.0
CONV_WIDTH = 4
POOL_WINDOWS = (2, 4, 8, 16)
TOP_K = 2

LANES = 128
SUBLANES = 8
MXU_DIM = 256
VMEM_LIMIT_BYTES = 56 * 1024 * 1024

_BF16 = jnp.bfloat16
_F32 = jnp.float32

_PAIR_SLOT_A = (0, 0, 0, 1, 1, 3)
_PAIR_SLOT_B = (1, 2, 3, 3, 2, 2)
_N_PAIRS = len(_PAIR_SLOT_A)


def _rms(x, g):
    ms = jnp.mean(x * x, axis=-1, keepdims=True)
    return x * lax.rsqrt(ms + EPS) * g


def _const_spec(shape):
    nd = len(shape)
    return pl.BlockSpec(shape, lambda *_: (0,) * nd, pipeline_mode=pl.Buffered(1))


def _params(n_axes):
    return pltpu.CompilerParams(dimension_semantics=("arbitrary",) * n_axes,
                                vmem_limit_bytes=VMEM_LIMIT_BYTES)


def _inproj_kernel(x_hbm, g_ref, w_ref, z_ref, h_ref, xs_ref, sem, *, tm, rc):
    i = pl.program_id(0)
    n_chunks = tm // rc

    def chunk_copy(c):
        return pltpu.make_async_copy(x_hbm.at[pl.ds(i * tm + c * rc, rc)], xs_ref.at[c % 2], sem.at[c % 2])

    @pl.when(pl.program_id(1) == 0)
    def _():
        chunk_copy(0).start()
        for c in range(n_chunks):
            if c + 1 < n_chunks:
                chunk_copy(c + 1).start()
            chunk_copy(c).wait()
            h_ref[c * rc:(c + 1) * rc, :] = _rms(xs_ref[c % 2], g_ref[...]).astype(_BF16)

    z_ref[...] = jnp.dot(h_ref[...], w_ref[...].astype(_BF16), preferred_element_type=_F32)


def _inproj(x2d, g, w, tm, tn, rc):
    t, d = x2d.shape
    n = w.shape[1]
    kern = functools.partial(_inproj_kernel, tm=tm, rc=rc)
    return pl.pallas_call(
        kern,
        grid=(t // tm, n // tn),
        in_specs=[pl.BlockSpec(memory_space=pl.ANY),
                  pl.BlockSpec((1, d), lambda i, j: (0, 0)),
                  pl.BlockSpec((d, tn), lambda i, j: (0, j))],
        out_specs=pl.BlockSpec((tm, tn), lambda i, j: (i, j)),
        out_shape=jax.ShapeDtypeStruct((t, n), _F32),
        scratch_shapes=[pltpu.VMEM((tm, d), _BF16),
                        pltpu.VMEM((2, rc, d), _F32),
                        pltpu.SemaphoreType.DMA((2,))],
        compiler_params=_params(2),
        name="inproj",
    )(x2d, g, w)


def _scan_pitch(tm):
    p = -(-tm // SUBLANES)
    while p % SUBLANES != 4:
        p += 1
    return p


def _mixer_kernel(z_ref, cw_ref, cb_ref, wa_ref, ba_ref, wx_ref, bx_ref, lam_ref, wp_ref, ps_ref,
                  ya_ref, yb_ref,
                  er_ref, ep_ref, a_ref, b_ref, h_ref, car_ref, *, tm, pitch):
    s = pl.program_id(1)
    c = cb_ref.shape[1]
    n_slab = c // LANES
    hist_r = SUBLANES
    hist_p = 2 * SUBLANES

    @pl.when(s == 0)
    def _():
        er_ref[0:hist_r, :] = jnp.zeros((hist_r, c), _F32)
        ep_ref[0:hist_p, :] = jnp.zeros((hist_p, c), _F32)
        car_ref[...] = jnp.zeros(car_ref.shape, _F32)
        a_ref[:, tm:, :] = jnp.ones((n_slab, SUBLANES * pitch - tm, LANES), _F32)
        b_ref[:, tm:, :] = jnp.zeros((n_slab, SUBLANES * pitch - tm, LANES), _F32)

    @pl.when(s > 0)
    def _():
        er_ref[0:hist_r, :] = er_ref[tm:tm + hist_r, :]
        ep_ref[0:hist_p, :] = ep_ref[tm:tm + hist_p, :]

    er_ref[hist_r:hist_r + tm, :] = z_ref[:, 0:c]
    ep_ref[hist_p:hist_p + tm, :] = z_ref[:, 2 * c:3 * c]

    kvec = -LRU_C * jax.nn.softplus(-lam_ref[...])
    nblk = c // MXU_DIM
    for k in range(nblk):
        cs = slice(k * MXU_DIM, (k + 1) * MXU_DIM)
        xc = cb_ref[:, cs] + cw_ref[CONV_WIDTH - 1:CONV_WIDTH, cs] * er_ref[hist_r:hist_r + tm, cs]
        for j in range(1, CONV_WIDTH):
            xc = xc + cw_ref[CONV_WIDTH - 1 - j:CONV_WIDTH - j, cs] * er_ref[hist_r - j:hist_r - j + tm, cs]
        xcb = xc.astype(_BF16)
        r = jax.nn.sigmoid(jnp.dot(xcb, wa_ref[k], preferred_element_type=_F32) + ba_ref[:, cs])
        ig = jax.nn.sigmoid(jnp.dot(xcb, wx_ref[k], preferred_element_type=_F32) + bx_ref[:, cs])
        log_a = r * kvec[:, cs]
        a = jnp.exp(log_a)
        mult = jnp.sqrt(1.0 - a * a)
        bb = mult * ig * xc
        for q in range(MXU_DIM // LANES):
            slab = k * (MXU_DIM // LANES) + q
            a_ref[slab, 0:tm, :] = a[:, q * LANES:(q + 1) * LANES]
            b_ref[slab, 0:tm, :] = bb[:, q * LANES:(q + 1) * LANES]

    def sweep1(i, carry):
        hs, ps = carry
        nh, npd = [], []
        for sl in range(n_slab):
            av = a_ref[sl, pl.ds(i, SUBLANES, stride=pitch), :]
            bv = b_ref[sl, pl.ds(i, SUBLANES, stride=pitch), :]
            nh.append(av * hs[sl] + bv)
            npd.append(av * ps[sl])
        return tuple(nh), tuple(npd)

    zero = jnp.zeros((SUBLANES, LANES), _F32)
    one = jnp.ones((SUBLANES, LANES), _F32)
    h_end, a_end = lax.fori_loop(0, pitch, sweep1, ((zero,) * n_slab, (one,) * n_slab))

    row = lax.broadcasted_iota(jnp.int32, (SUBLANES, LANES), 0)
    h0 = []
    for sl in range(n_slab):
        hh, aa = h_end[sl], a_end[sl]
        d = 1
        while d < SUBLANES:
            hs_ = jnp.where(row >= d, pltpu.roll(hh, d, 0), 0.0)
            as_ = jnp.where(row >= d, pltpu.roll(aa, d, 0), 1.0)
            hh = aa * hs_ + hh
            aa = aa * as_
            d *= 2
        cin = car_ref[:, sl * LANES:(sl + 1) * LANES]
        full = hh + aa * cin
        h0.append(jnp.where(row >= 1, pltpu.roll(full, 1, 0), cin))
        car_ref[:, sl * LANES:(sl + 1) * LANES] = jnp.broadcast_to(full[SUBLANES - 1:SUBLANES, :],
                                                                  (SUBLANES, LANES))

    def sweep2(i, hs):
        nh = []
        for sl in range(n_slab):
            av = a_ref[sl, pl.ds(i, SUBLANES, stride=pitch), :]
            bv = b_ref[sl, pl.ds(i, SUBLANES, stride=pitch), :]
            hv = av * hs[sl] + bv
            h_ref[sl, pl.ds(i, SUBLANES, stride=pitch), :] = hv
            nh.append(hv)
        return tuple(nh)

    lax.fori_loop(0, pitch, sweep2, tuple(h0))

    for sl in range(n_slab):
        cs = slice(sl * LANES, (sl + 1) * LANES)
        g = z_ref[:, c + sl * LANES:c + (sl + 1) * LANES]
        ya_ref[:, cs] = (h_ref[sl, 0:tm, :] * jax.nn.gelu(g)).astype(_BF16)

    t_idx = (s * tm + lax.broadcasted_iota(jnp.int32, (tm, 1), 0) + 1).astype(_F32)
    n_grp = len(POOL_WINDOWS)
    gd = c // n_grp
    for gi, w in enumerate(POOL_WINDOWS):
        cs = slice(gi * gd, (gi + 1) * gd)
        e = ep_ref[:, cs]
        acc = e
        d = 1
        while d < w:
            acc = acc + pltpu.roll(acc, d, 0)
            d *= 2
        xt = e[hist_p:, :]
        cnt = jnp.minimum(t_idx, float(w))
        dd = acc[hist_p:, :] / cnt - xt
        yb = jnp.dot(dd.astype(_BF16), wp_ref[gi], preferred_element_type=_F32) * ps_ref[:, cs]
        yb_ref[:, cs] = yb.astype(_BF16)


def _mixer(z, bsz, seq, cw, cb, wa_bd, ba, wx_bd, bx, lam, wp, ps, tm):
    t = z.shape[0]
    c = cb.shape[1]
    ns = seq // tm
    pitch = _scan_pitch(tm)
    n_slab = c // LANES
    kern = functools.partial(_mixer_kernel, tm=tm, pitch=pitch)
    return pl.pallas_call(
        kern,
        grid=(bsz, ns),
        in_specs=[pl.BlockSpec((tm, 3 * c), lambda b, s: (b * ns + s, 0)),
                  _const_spec(cw.shape), _const_spec(cb.shape),
                  _const_spec(wa_bd.shape), _const_spec(ba.shape),
                  _const_spec(wx_bd.shape), _const_spec(bx.shape),
                  _const_spec(lam.shape), _const_spec(wp.shape), _const_spec(ps.shape)],
        out_specs=[pl.BlockSpec((tm, c), lambda b, s: (b * ns + s, 0)),
                   pl.BlockSpec((tm, c), lambda b, s: (b * ns + s, 0))],
        out_shape=[jax.ShapeDtypeStruct((t, c), _BF16), jax.ShapeDtypeStruct((t, c), _BF16)],
        scratch_shapes=[pltpu.VMEM((SUBLANES + tm, c), _F32),
                        pltpu.VMEM((2 * SUBLANES + tm, c), _F32),
                        pltpu.VMEM((n_slab, SUBLANES * pitch, LANES), _F32),
                        pltpu.VMEM((n_slab, SUBLANES * pitch, LANES), _F32),
                        pltpu.VMEM((n_slab, SUBLANES * pitch, LANES), _F32),
                        pltpu.VMEM((SUBLANES, c), _F32)],
        compiler_params=_params(2),
        name="mixer",
    )(z, cw, cb, wa_bd, ba, wx_bd, bx, lam, wp, ps)


def _combine_kernel(ya_ref, yb_ref, ga0_ref, ga1_ref, gb0_ref, gb1_ref, x_ref,
                    wa_ref, wb_ref, wo_ref, n2_ref, wr_ref, br_ref,
                    xe_ref, cnt_ref, cnt_s, *, tm, n_groups, epg):
    i = pl.program_id(0)
    d = x_ref.shape[1]
    half = d // 2

    @pl.when(i == 0)
    def _():
        cnt_s[...] = jnp.zeros(cnt_s.shape, _F32)

    pa = jnp.dot(ya_ref[...], wa_ref[...], preferred_element_type=_F32)
    pb = jnp.dot(yb_ref[...], wb_ref[...], preferred_element_type=_F32)
    u0 = jax.nn.sigmoid(ga0_ref[...]) * pa[:, :half] + jax.nn.sigmoid(gb0_ref[...]) * pb[:, :half]
    u1 = jax.nn.sigmoid(ga1_ref[...]) * pa[:, half:] + jax.nn.sigmoid(gb1_ref[...]) * pb[:, half:]
    u = jnp.concatenate([u0, u1], axis=1).astype(_BF16)
    x1 = x_ref[...] + jnp.dot(u, wo_ref[...], preferred_element_type=_F32)
    xe_ref[:, 0:d] = x1

    ht = _rms(x1, n2_ref[...]).astype(_BF16)
    logits = jnp.dot(ht, wr_ref[...], preferred_element_type=_F32) + br_ref[...]

    lane = lax.broadcasted_iota(jnp.int32, (tm, LANES), 1).astype(_F32)
    ninf = -jnp.inf
    big = float(LANES)

    def first_argmax(v):
        m = jnp.max(v, axis=-1, keepdims=True)
        return m, jnp.min(jnp.where(v == m, lane, big), axis=-1, keepdims=True)

    is_g = lane < float(n_groups)
    gmax, gidx = first_argmax(jnp.where(is_g, logits, ninf))
    g_w = 1.0 / jnp.sum(jnp.where(is_g, jnp.exp(logits - gmax), 0.0), axis=-1, keepdims=True)
    lo_lane = float(n_groups) + float(epg) * gidx
    in_grp = (lane >= lo_lane) & (lane < lo_lane + float(epg))
    le = jnp.where(in_grp, logits, ninf)
    m1, i1 = first_argmax(le)
    m2, i2 = first_argmax(jnp.where(lane == i1, ninf, le))
    e21 = jnp.exp(m2 - m1)
    w1 = g_w / (1.0 + e21)
    w2 = w1 * e21
    e1 = i1 - lo_lane
    e2 = i2 - lo_lane
    lo = jnp.minimum(e1, e2)
    hi = jnp.maximum(e1, e2)
    w_lo = jnp.where(e1 < e2, w1, w2)
    w_hi = jnp.where(e1 < e2, w2, w1)
    pair = jnp.where(lo == 0.0, hi - 1.0, jnp.where(lo == 1.0, 6.0 - hi, 5.0))
    swap = pair == 5.0
    w_a = jnp.where(swap, w_hi, w_lo)
    w_b = jnp.where(swap, w_lo, w_hi)
    bucket = float(_N_PAIRS) * gidx + pair

    onehot = lane == bucket
    oh_bf = jnp.where(onehot, 1.0, 0.0).astype(_BF16)
    rr = lax.broadcasted_iota(jnp.int32, (tm, tm), 0)
    cc = lax.broadcasted_iota(jnp.int32, (tm, tm), 1)
    tri = jnp.where(cc < rr, 1.0, 0.0).astype(_BF16)
    before = jnp.dot(tri, oh_bf, preferred_element_type=_F32) + cnt_s[...]
    rank = jnp.sum(jnp.where(onehot, before, 0.0), axis=-1, keepdims=True)
    cnt_s[...] = cnt_s[...] + jnp.sum(jnp.where(onehot, 1.0, 0.0), axis=0, keepdims=True)
    cnt_ref[...] = cnt_s[...]

    info = jnp.where(lane == 0.0, bucket,
                     jnp.where(lane == 1.0, rank,
                               jnp.where(lane == 2.0, w_a, jnp.where(lane == 3.0, w_b, 0.0))))
    xe_ref[:, d:d + LANES] = info


def _combine(ya, yb, z, x2d, wa, wb, wo, n2, wr, br, tm, n_groups, epg):
    t, d = x2d.shape
    c = ya.shape[1]
    half = d // 2
    off = (3 * c) // half
    kern = functools.partial(_combine_kernel, tm=tm, n_groups=n_groups, epg=epg)

    def zspec(j):
        return pl.BlockSpec((tm, half), lambda i, j=j: (i, off + j))

    return pl.pallas_call(
        kern,
        grid=(t // tm,),
        in_specs=[pl.BlockSpec((tm, c), lambda i: (i, 0)),
                  pl.BlockSpec((tm, c), lambda i: (i, 0)),
                  zspec(0), zspec(1), zspec(2), zspec(3),
                  pl.BlockSpec((tm, d), lambda i: (i, 0)),
                  _const_spec(wa.shape), _const_spec(wb.shape), _const_spec(wo.shape),
                  _const_spec(n2.shape), _const_spec(wr.shape), _const_spec(br.shape)],
        out_specs=[pl.BlockSpec((tm, d + LANES), lambda i: (i, 0)),
                   pl.BlockSpec((1, LANES), lambda i: (0, 0))],
        out_shape=[jax.ShapeDtypeStruct((t, d + LANES), _F32),
                   jax.ShapeDtypeStruct((1, LANES), _F32)],
        scratch_shapes=[pltpu.VMEM((1, LANES), _F32)],
        compiler_params=_params(1),
        name="combine",
    )(ya, yb, z, z, z, z, x2d, wa, wb, wo, n2, wr, br)


def _row_gather_start(idx_ref, base, src_hbm, buf, slot, sem, n_rows, unrolled, width=None):
    width = src_hbm.shape[1] if width is None else width

    def start(r):
        row = idx_ref[base + r]
        pltpu.make_async_copy(src_hbm.at[pl.ds(row, 1), pl.ds(0, width)],
                              buf.at[slot, pl.ds(r, 1), pl.ds(0, width)],
                              sem.at[slot]).start(priority=1)

    if unrolled:
        for r in range(n_rows):
            start(r)
    else:
        def body(r, _):
            start(r)
            return 0
        lax.fori_loop(0, n_rows, body, 0, unroll=8)


def _row_gather_wait(src_hbm, buf, slot, sem, n_rows, width=None):
    width = src_hbm.shape[1] if width is None else width
    pltpu.make_async_copy(src_hbm.at[pl.ds(0, n_rows), pl.ds(0, width)],
                          buf.at[slot, :, pl.ds(0, width)], sem.at[slot]).wait()


def _moe_kernel(src_ref, ea_ref, eb_ref, valid_ref,
                xe_hbm, n2_ref, wga_ref, wua_ref, wda_ref, wgb_ref, wub_ref, wdb_ref,
                x2s_ref, xbuf, gsem, *, tm):
    j = pl.program_id(0)
    slot = j % 2
    d = x2s_ref.shape[1]

    @pl.when(j == 0)
    def _():
        _row_gather_start(src_ref, 0, xe_hbm, xbuf, 0, gsem, tm, unrolled=False)

    @pl.when(valid_ref[j] == 1)
    def _():
        _row_gather_wait(xe_hbm, xbuf, slot, gsem, tm)
        _row_gather_start(src_ref, (j + 1) * tm, xe_hbm, xbuf, 1 - slot, gsem, tm, unrolled=True)
        xe = xbuf[slot]
        x1 = xe[:, 0:d]
        w_a = xe[:, d + 2:d + 3]
        w_b = xe[:, d + 3:d + 4]
        ht = _rms(x1, n2_ref[...]).astype(_BF16)

        def expert(wg_ref, wu_ref, wgt):
            hg = jnp.dot(ht, wg_ref[0], preferred_element_type=_F32)
            hu = jnp.dot(ht, wu_ref[0], preferred_element_type=_F32)
            return (hg * jax.nn.sigmoid(hg) * hu * wgt).astype(_BF16)

        y = jnp.dot(expert(wga_ref, wua_ref, w_a), wda_ref[0], preferred_element_type=_F32)
        y = y + jnp.dot(expert(wgb_ref, wub_ref, w_b), wdb_ref[0], preferred_element_type=_F32)

        x2s_ref[...] = x1 + y

    @pl.when(valid_ref[j] == 0)
    def _():
        x2s_ref[...] = jnp.zeros(x2s_ref.shape, _F32)

        @pl.when(valid_ref[jnp.maximum(j - 1, 0)] == 1)
        def _():
            _row_gather_wait(xe_hbm, xbuf, slot, gsem, tm)


def _moe(src, ea, eb, valid, xe, n2, wg, wu, wd, tm):
    n_tiles = valid.shape[0]
    d = wg.shape[1]
    f = wg.shape[2]
    kern = functools.partial(_moe_kernel, tm=tm)

    def wspec(shape, which):
        if which == 0:
            return pl.BlockSpec((1,) + shape, lambda j, s, a, b, v: (a[j], 0, 0))
        return pl.BlockSpec((1,) + shape, lambda j, s, a, b, v: (b[j], 0, 0))

    grid_spec = pltpu.PrefetchScalarGridSpec(
        num_scalar_prefetch=4,
        grid=(n_tiles,),
        in_specs=[pl.BlockSpec(memory_space=pl.ANY),
                  pl.BlockSpec((1, d), lambda j, *_: (0, 0)),
                  wspec((d, f), 0), wspec((d, f), 0), wspec((f, d), 0),
                  wspec((d, f), 1), wspec((d, f), 1), wspec((f, d), 1)],
        out_specs=pl.BlockSpec((tm, d), lambda j, *_: (j, 0)),
        scratch_shapes=[pltpu.VMEM((2, tm, d + LANES), _F32),
                        pltpu.SemaphoreType.DMA((2,))],
    )
    return pl.pallas_call(
        kern,
        grid_spec=grid_spec,
        out_shape=jax.ShapeDtypeStruct((n_tiles * tm, d), _F32),
        compiler_params=_params(1),
        name="moe",
    )(src, ea, eb, valid, xe, n2, wg, wu, wd, wg, wu, wd)


def _ple_kernel(pos_ref, x2s_hbm, p_ref, ng_ref, wg_ref, wp_ref, nf_ref, o_ref, xbuf, gsem, *, tm, n_steps):
    i = pl.program_id(0)
    slot = i % 2

    @pl.when(i == 0)
    def _():
        _row_gather_start(pos_ref, 0, x2s_hbm, xbuf, 0, gsem, tm, unrolled=False)

    _row_gather_wait(x2s_hbm, xbuf, slot, gsem, tm)
    nxt = jnp.where(i + 1 == n_steps, 0, i + 1)
    _row_gather_start(pos_ref, nxt * tm, x2s_hbm, xbuf, 1 - slot, gsem, tm, unrolled=True)
    x2 = xbuf[slot]
    g = jax.nn.sigmoid(jnp.dot(_rms(x2, ng_ref[...]).astype(_BF16), wg_ref[...],
                               preferred_element_type=_F32))
    e = jnp.dot(p_ref[...].astype(_BF16), wp_ref[...], preferred_element_type=_F32)
    o_ref[...] = _rms(x2 + g * e, nf_ref[...])

    @pl.when(i == n_steps - 1)
    def _():
        _row_gather_wait(x2s_hbm, xbuf, 1 - slot, gsem, tm)


def _ple(pos, x2s, p2d, ng, wg, wp, nf, tm, t):
    d = x2s.shape[1]
    pd = p2d.shape[1]
    n_steps = t // tm
    kern = functools.partial(_ple_kernel, tm=tm, n_steps=n_steps)

    def cspec(shape):
        nd = len(shape)
        return pl.BlockSpec(shape, lambda i, *_: (0,) * nd, pipeline_mode=pl.Buffered(1))

    grid_spec = pltpu.PrefetchScalarGridSpec(
        num_scalar_prefetch=1,
        grid=(n_steps,),
        in_specs=[pl.BlockSpec(memory_space=pl.ANY),
                  pl.BlockSpec((tm, pd), lambda i, *_: (i, 0)),
                  cspec(ng.shape), cspec(wg.shape), cspec(wp.shape), cspec(nf.shape)],
        out_specs=pl.BlockSpec((tm, d), lambda i, *_: (i, 0)),
        scratch_shapes=[pltpu.VMEM((2, tm, d), _F32),
                        pltpu.SemaphoreType.DMA((2,))],
    )
    return pl.pallas_call(
        kern,
        grid_spec=grid_spec,
        out_shape=jax.ShapeDtypeStruct((t, d), _F32),
        compiler_params=_params(1),
        name="ple",
    )(pos, x2s, p2d, ng, wg, wp, nf)


def _block_diag(w, per_block):
    h, hd, _ = w.shape
    nb = h // per_block
    w4 = w.reshape(nb, per_block, hd, hd)
    eye = jnp.eye(per_block, dtype=w.dtype)
    out = jnp.einsum("bpij,pq->bpiqj", w4, eye)
    return out.reshape(nb, per_block * hd, per_block * hd)


def _layer(x2d, p2d, bsz, seq, norm1_g, w_in, conv_w, conv_b, w_rg_a, b_rg_a, w_rg_x, b_rg_x, lru_lambda,
           w_pool, pool_scale, w_branch_a, w_branch_b, w_out, norm2_g, w_router_group, b_router_group,
           w_router_expert, b_router_expert, w_e_gate, w_e_up, w_e_down, norm_ple_g, w_ple_gate,
           w_ple_proj, out_norm_g):
    t, d = x2d.shape
    c = conv_b.shape[0]
    heads, hd, _ = w_rg_a.shape
    n_groups = w_router_group.shape[1]
    n_exp = w_router_expert.shape[1]
    epg = n_exp // n_groups
    assert epg == 4 and TOP_K == 2 and hd * (MXU_DIM // hd) == MXU_DIM
    assert w_pool.shape[0] == len(POOL_WINDOWS) and w_pool.shape[1] == MXU_DIM

    tm_in, tn_in, rc_in = min(2048, t), 512, min(256, t)
    tm_mix = min(256, seq)
    tm_cmb = min(256, t)
    tm_moe = min(256, t)
    tm_ple = min(256, t)

    row = lambda v: v.reshape(1, -1).astype(_F32)
    per_block = MXU_DIM // hd

    z = _inproj(x2d, row(norm1_g), w_in, tm_in, tn_in, rc_in)
    ya, yb = _mixer(z, bsz, seq, conv_w.reshape(CONV_WIDTH, c), row(conv_b),
                    _block_diag(w_rg_a, per_block).astype(_BF16), row(b_rg_a),
                    _block_diag(w_rg_x, per_block).astype(_BF16), row(b_rg_x),
                    row(lru_lambda), w_pool.astype(_BF16), row(pool_scale), tm_mix)

    wr = jnp.zeros((d, LANES), _F32).at[:, :n_groups].set(w_router_group)
    wr = wr.at[:, n_groups:n_groups + n_exp].set(w_router_expert).astype(_BF16)
    br = jnp.zeros((1, LANES), _F32).at[0, :n_groups].set(b_router_group)
    br = br.at[0, n_groups:n_groups + n_exp].set(b_router_expert)
    xe, counts = _combine(ya, yb, z, x2d, w_branch_a.astype(_BF16), w_branch_b.astype(_BF16),
                          w_out.astype(_BF16), row(norm2_g), wr, br, tm_cmb, n_groups, epg)

    n_buckets = n_groups * _N_PAIRS
    n_tiles = t // tm_moe + n_buckets
    cnt = counts[0, :n_buckets].astype(jnp.int32)
    nt = (cnt + tm_moe - 1) // tm_moe
    cum = jnp.cumsum(nt)
    total = cum[-1]
    bucket = xe[:, d].astype(jnp.int32)
    rank = xe[:, d + 1].astype(jnp.int32)
    pos = (cum - nt)[bucket] * tm_moe + rank
    tok = jnp.arange(t, dtype=jnp.int32)
    src = (jnp.arange(n_tiles * tm_moe, dtype=jnp.int32) % t).at[pos].set(tok)
    tile = jnp.minimum(jnp.arange(n_tiles, dtype=jnp.int32), total - 1)
    tb = jnp.sum((cum[None, :] <= tile[:, None]).astype(jnp.int32), axis=1)
    grp, pr = tb // _N_PAIRS, tb % _N_PAIRS
    ea = grp * epg + jnp.asarray(_PAIR_SLOT_A, jnp.int32)[pr]
    eb = grp * epg + jnp.asarray(_PAIR_SLOT_B, jnp.int32)[pr]
    valid = (jnp.arange(n_tiles, dtype=jnp.int32) < total).astype(jnp.int32)

    x2s = _moe(src, ea, eb, valid, xe, row(norm2_g), w_e_gate.astype(_BF16), w_e_up.astype(_BF16),
               w_e_down.astype(_BF16), tm_moe)
    return _ple(pos, x2s, p2d, row(norm_ple_g), w_ple_gate.astype(_BF16), w_ple_proj.astype(_BF16),
                row(out_norm_g), tm_ple, t)


def kernel(x, p, norm1_g, w_in, conv_w, conv_b, w_rg_a, b_rg_a, w_rg_x, b_rg_x, lru_lambda, w_pool, pool_scale, w_branch_a, w_branch_b, w_out, norm2_g, w_router_group, b_router_group, w_router_expert, b_router_expert, w_e_gate, w_e_up, w_e_down, norm_ple_g, w_ple_gate, w_ple_proj, final_norm_g):
    bsz, seq, d = x.shape
    depth = p.shape[0]
    assert depth == 1, "the final RMSNorm is fused into the last layer's embedding kernel"
    out = _layer(x.reshape(bsz * seq, d), p[0].reshape(bsz * seq, -1), bsz, seq,
                 norm1_g[0], w_in[0], conv_w[0], conv_b[0], w_rg_a[0], b_rg_a[0], w_rg_x[0], b_rg_x[0],
                 lru_lambda[0], w_pool[0], pool_scale[0], w_branch_a[0], w_branch_b[0], w_out[0],
                 norm2_g[0], w_router_group[0], b_router_group[0], w_router_expert[0],
                 b_router_expert[0], w_e_gate[0], w_e_up[0], w_e_down[0], norm_ple_g[0],
                 w_ple_gate[0], w_ple_proj[0], final_norm_g)
    return out.reshape(bsz, seq, d)
```

```python
import functools

import jax
import jax.numpy as jnp
from jax import lax
from jax.experimental import pallas as pl
from jax.experimental.pallas import tpu as pltpu

EPS = 1e-6
LRU_C = 8.0
CONV_WIDTH = 4
POOL_WINDOWS = (2, 4, 8, 16)
TOP_K = 2

LANES = 128
SUBLANES = 8
MXU_DIM = 256
VMEM_LIMIT_BYTES = 56 * 1024 * 1024

_BF16 = jnp.bfloat16
_F32 = jnp.float32

_PAIR_SLOT_A = (0, 0, 0, 1, 1, 3)
_PAIR_SLOT_B = (1, 2, 3, 3, 2, 2)
_N_PAIRS = len(_PAIR_SLOT_A)


def _rms(x, g):
    ms = jnp.mean(x * x, axis=-1, keepdims=True)
    return x * lax.rsqrt(ms + EPS) * g


def _const_spec(shape):
    nd = len(shape)
    return pl.BlockSpec(shape, lambda *_: (0,) * nd, pipeline_mode=pl.Buffered(1))


def _params(n_axes):
    return pltpu.CompilerParams(dimension_semantics=("arbitrary",) * n_axes,
                                vmem_limit_bytes=VMEM_LIMIT_BYTES)


def _inproj_kernel(x_hbm, g_ref, w_ref, z_ref, h_ref, xs_ref, sem, *, tm, rc):
    i = pl.program_id(0)
    n_chunks = tm // rc

    def chunk_copy(c):
        return pltpu.make_async_copy(x_hbm.at[pl.ds(i * tm + c * rc, rc)], xs_ref.at[c % 2], sem.at[c % 2])

    @pl.when(pl.program_id(1) == 0)
    def _():
        chunk_copy(0).start()
        for c in range(n_chunks):
            if c + 1 < n_chunks:
                chunk_copy(c + 1).start()
            chunk_copy(c).wait()
            h_ref[c * rc:(c + 1) * rc, :] = _rms(xs_ref[c % 2], g_ref[...]).astype(_BF16)

    z_ref[...] = jnp.dot(h_ref[...], w_ref[...].astype(_BF16), preferred_element_type=_F32)


def _inproj(x2d, g, w, tm, tn, rc):
    t, d = x2d.shape
    n = w.shape[1]
    kern = functools.partial(_inproj_kernel, tm=tm, rc=rc)
    return pl.pallas_call(
        kern,
        grid=(t // tm, n // tn),
        in_specs=[pl.BlockSpec(memory_space=pl.ANY),
                  pl.BlockSpec((1, d), lambda i, j: (0, 0)),
                  pl.BlockSpec((d, tn), lambda i, j: (0, j))],
        out_specs=pl.BlockSpec((tm, tn), lambda i, j: (i, j)),
        out_shape=jax.ShapeDtypeStruct((t, n), _F32),
        scratch_shapes=[pltpu.VMEM((tm, d), _BF16),
                        pltpu.VMEM((2, rc, d), _F32),
                        pltpu.SemaphoreType.DMA((2,))],
        compiler_params=_params(2),
        name="inproj",
    )(x2d, g, w)


def _scan_pitch(tm):
    p = -(-tm // SUBLANES)
    while p % SUBLANES != 4:
        p += 1
    return p


def _mixer_kernel(z_ref, cw_ref, cb_ref, wa_ref, ba_ref, wx_ref, bx_ref, lam_ref, wp_ref, ps_ref,
                  ya_ref, yb_ref,
                  er_ref, ep_ref, a_ref, b_ref, h_ref, car_ref, *, tm, pitch):
    s = pl.program_id(1)
    c = cb_ref.shape[1]
    n_slab = c // LANES
    hist_r = SUBLANES
    hist_p = 2 * SUBLANES

    @pl.when(s == 0)
    def _():
        er_ref[0:hist_r, :] = jnp.zeros((hist_r, c), _F32)
        ep_ref[0:hist_p, :] = jnp.zeros((hist_p, c), _F32)
        car_ref[...] = jnp.zeros(car_ref.shape, _F32)
        a_ref[:, tm:, :] = jnp.ones((n_slab, SUBLANES * pitch - tm, LANES), _F32)
        b_ref[:, tm:, :] = jnp.zeros((n_slab, SUBLANES * pitch - tm, LANES), _F32)

    @pl.when(s > 0)
    def _():
        er_ref[0:hist_r, :] = er_ref[tm:tm + hist_r, :]
        ep_ref[0:hist_p, :] = ep_ref[tm:tm + hist_p, :]

    er_ref[hist_r:hist_r + tm, :] = z_ref[:, 0:c]
    ep_ref[hist_p:hist_p + tm, :] = z_ref[:, 2 * c:3 * c]

    kvec = -LRU_C * jax.nn.softplus(-lam_ref[...])
    nblk = c // MXU_DIM
    for k in range(nblk):
        cs = slice(k * MXU_DIM, (k + 1) * MXU_DIM)
        xc = cb_ref[:, cs] + cw_ref[CONV_WIDTH - 1:CONV_WIDTH, cs] * er_ref[hist_r:hist_r + tm, cs]
        for j in range(1, CONV_WIDTH):
            xc = xc + cw_ref[CONV_WIDTH - 1 - j:CONV_WIDTH - j, cs] * er_ref[hist_r - j:hist_r - j + tm, cs]
        xcb = xc.astype(_BF16)
        r = jax.nn.sigmoid(jnp.dot(xcb, wa_ref[k], preferred_element_type=_F32) + ba_ref[:, cs])
        ig = jax.nn.sigmoid(jnp.dot(xcb, wx_ref[k], preferred_element_type=_F32) + bx_ref[:, cs])
        log_a = r * kvec[:, cs]
        a = jnp.exp(log_a)
        mult = jnp.sqrt(1.0 - a * a)
        bb = mult * ig * xc
        for q in range(MXU_DIM // LANES):
            slab = k * (MXU_DIM // LANES) + q
            a_ref[slab, 0:tm, :] = a[:, q * LANES:(q + 1) * LANES]
            b_ref[slab, 0:tm, :] = bb[:, q * LANES:(q + 1) * LANES]

    def sweep1(i, carry):
        hs, ps = carry
        nh, npd = [], []
        for sl in range(n_slab):
            av = a_ref[sl, pl.ds(i, SUBLANES, stride=pitch), :]
            bv = b_ref[sl, pl.ds(i, SUBLANES, stride=pitch), :]
            nh.append(av * hs[sl] + bv)
            npd.append(av * ps[sl])
        return tuple(nh), tuple(npd)

    zero = jnp.zeros((SUBLANES, LANES), _F32)
    one = jnp.ones((SUBLANES, LANES), _F32)
    h_end, a_end = lax.fori_loop(0, pitch, sweep1, ((zero,) * n_slab, (one,) * n_slab))

    row = lax.broadcasted_iota(jnp.int32, (SUBLANES, LANES), 0)
    h0 = []
    for sl in range(n_slab):
        hh, aa = h_end[sl], a_end[sl]
        d = 1
        while d < SUBLANES:
            hs_ = jnp.where(row >= d, pltpu.roll(hh, d, 0), 0.0)
            as_ = jnp.where(row >= d, pltpu.roll(aa, d, 0), 1.0)
            hh = aa * hs_ + hh
            aa = aa * as_
            d *= 2
        cin = car_ref[:, sl * LANES:(sl + 1) * LANES]
        full = hh + aa * cin
        h0.append(jnp.where(row >= 1, pltpu.roll(full, 1, 0), cin))
        car_ref[:, sl * LANES:(sl + 1) * LANES] = jnp.broadcast_to(full[SUBLANES - 1:SUBLANES, :],
                                                                  (SUBLANES, LANES))

    def sweep2(i, hs):
        nh = []
        for sl in range(n_slab):
            av = a_ref[sl, pl.ds(i, SUBLANES, stride=pitch), :]
            bv = b_ref[sl, pl.ds(i, SUBLANES, stride=pitch), :]
            hv = av * hs[sl] + bv
            h_ref[sl, pl.ds(i, SUBLANES, stride=pitch), :] = hv
            nh.append(hv)
        return tuple(nh)

    lax.fori_loop(0, pitch, sweep2, tuple(h0))

    for sl in range(n_slab):
        cs = slice(sl * LANES, (sl + 1) * LANES)
        g = z_ref[:, c + sl * LANES:c + (sl + 1) * LANES]
        ya_ref[:, cs] = (h_ref[sl, 0:tm, :] * jax.nn.gelu(g)).astype(_BF16)

    t_idx = (s * tm + lax.broadcasted_iota(jnp.int32, (tm, 1), 0) + 1).astype(_F32)
    n_grp = len(POOL_WINDOWS)
    gd = c // n_grp
    for gi, w in enumerate(POOL_WINDOWS):
        cs = slice(gi * gd, (gi + 1) * gd)
        e = ep_ref[:, cs]
        acc = e
        d = 1
        while d < w:
            acc = acc + pltpu.roll(acc, d, 0)
            d *= 2
        xt = e[hist_p:, :]
        cnt = jnp.minimum(t_idx, float(w))
        dd = acc[hist_p:, :] / cnt - xt
        yb = jnp.dot(dd.astype(_BF16), wp_ref[gi], preferred_element_type=_F32) * ps_ref[:, cs]
        yb_ref[:, cs] = yb.astype(_BF16)


def _mixer(z, bsz, seq, cw, cb, wa_bd, ba, wx_bd, bx, lam, wp, ps, tm):
    t = z.shape[0]
    c = cb.shape[1]
    ns = seq // tm
    pitch = _scan_pitch(tm)
    n_slab = c // LANES
    kern = functools.partial(_mixer_kernel, tm=tm, pitch=pitch)
    return pl.pallas_call(
        kern,
        grid=(bsz, ns),
        in_specs=[pl.BlockSpec((tm, 3 * c), lambda b, s: (b * ns + s, 0)),
                  _const_spec(cw.shape), _const_spec(cb.shape),
                  _const_spec(wa_bd.shape), _const_spec(ba.shape),
                  _const_spec(wx_bd.shape), _const_spec(bx.shape),
                  _const_spec(lam.shape), _const_spec(wp.shape), _const_spec(ps.shape)],
        out_specs=[pl.BlockSpec((tm, c), lambda b, s: (b * ns + s, 0)),
                   pl.BlockSpec((tm, c), lambda b, s: (b * ns + s, 0))],
        out_shape=[jax.ShapeDtypeStruct((t, c), _BF16), jax.ShapeDtypeStruct((t, c), _BF16)],
        scratch_shapes=[pltpu.VMEM((SUBLANES + tm, c), _F32),
                        pltpu.VMEM((2 * SUBLANES + tm, c), _F32),
                        pltpu.VMEM((n_slab, SUBLANES * pitch, LANES), _F32),
                        pltpu.VMEM((n_slab, SUBLANES * pitch, LANES), _F32),
                        pltpu.VMEM((n_slab, SUBLANES * pitch, LANES), _F32),
                        pltpu.VMEM((SUBLANES, c), _F32)],
        compiler_params=_params(2),
        name="mixer",
    )(z, cw, cb, wa_bd, ba, wx_bd, bx, lam, wp, ps)


def _combine_kernel(ya_ref, yb_ref, ga0_ref, ga1_ref, gb0_ref, gb1_ref, x_ref,
                    wa_ref, wb_ref, wo_ref, n2_ref, wr_ref, br_ref,
                    xe_ref, cnt_ref, cnt_s, *, tm, n_groups, epg):
    i = pl.program_id(0)
    d = x_ref.shape[1]
    half = d // 2

    @pl.when(i == 0)
    def _():
        cnt_s[...] = jnp.zeros(cnt_s.shape, _F32)

    pa = jnp.dot(ya_ref[...], wa_ref[...], preferred_element_type=_F32)
    pb = jnp.dot(yb_ref[...], wb_ref[...], preferred_element_type=_F32)
    u0 = jax.nn.sigmoid(ga0_ref[...]) * pa[:, :half] + jax.nn.sigmoid(gb0_ref[...]) * pb[:, :half]
    u1 = jax.nn.sigmoid(ga1_ref[...]) * pa[:, half:] + jax.nn.sigmoid(gb1_ref[...]) * pb[:, half:]
    u = jnp.concatenate([u0, u1], axis=1).astype(_BF16)
    x1 = x_ref[...] + jnp.dot(u, wo_ref[...], preferred_element_type=_F32)
    xe_ref[:, 0:d] = x1

    ht = _rms(x1, n2_ref[...]).astype(_BF16)
    logits = jnp.dot(ht, wr_ref[...], preferred_element_type=_F32) + br_ref[...]

    lane = lax.broadcasted_iota(jnp.int32, (tm, LANES), 1).astype(_F32)
    ninf = -jnp.inf
    big = float(LANES)

    def first_argmax(v):
        m = jnp.max(v, axis=-1, keepdims=True)
        return m, jnp.min(jnp.where(v == m, lane, big), axis=-1, keepdims=True)

    is_g = lane < float(n_groups)
    gmax, gidx = first_argmax(jnp.where(is_g, logits, ninf))
    g_w = 1.0 / jnp.sum(jnp.where(is_g, jnp.exp(logits - gmax), 0.0), axis=-1, keepdims=True)
    lo_lane = float(n_groups) + float(epg) * gidx
    in_grp = (lane >= lo_lane) & (lane < lo_lane + float(epg))
    le = jnp.where(in_grp, logits, ninf)
    m1, i1 = first_argmax(le)
    m2, i2 = first_argmax(jnp.where(lane == i1, ninf, le))
    e21 = jnp.exp(m2 - m1)
    w1 = g_w / (1.0 + e21)
    w2 = w1 * e21
    e1 = i1 - lo_lane
    e2 = i2 - lo_lane
    lo = jnp.minimum(e1, e2)
    hi = jnp.maximum(e1, e2)
    w_lo = jnp.where(e1 < e2, w1, w2)
    w_hi = jnp.where(e1 < e2, w2, w1)
    pair = jnp.where(lo == 0.0, hi - 1.0, jnp.where(lo == 1.0, 6.0 - hi, 5.0))
    swap = pair == 5.0
    w_a = jnp.where(swap, w_hi, w_lo)
    w_b = jnp.where(swap, w_lo, w_hi)
    bucket = float(_N_PAIRS) * gidx + pair

    onehot = lane == bucket
    oh_bf = jnp.where(onehot, 1.0, 0.0).astype(_BF16)
    rr = lax.broadcasted_iota(jnp.int32, (tm, tm), 0)
    cc = lax.broadcasted_iota(jnp.int32, (tm, tm), 1)
    tri = jnp.where(cc < rr, 1.0, 0.0).astype(_BF16)
    before = jnp.dot(tri, oh_bf, preferred_element_type=_F32) + cnt_s[...]
    rank = jnp.sum(jnp.where(onehot, before, 0.0), axis=-1, keepdims=True)
    cnt_s[...] = cnt_s[...] + jnp.sum(jnp.where(onehot, 1.0, 0.0), axis=0, keepdims=True)
    cnt_ref[...] = cnt_s[...]

    info = jnp.where(lane == 0.0, bucket,
                     jnp.where(lane == 1.0, rank,
                               jnp.where(lane == 2.0, w_a, jnp.where(lane == 3.0, w_b, 0.0))))
    xe_ref[:, d:d + LANES] = info


def _combine(ya, yb, z, x2d, wa, wb, wo, n2, wr, br, tm, n_groups, epg):
    t, d = x2d.shape
    c = ya.shape[1]
    half = d // 2
    off = (3 * c) // half
    kern = functools.partial(_combine_kernel, tm=tm, n_groups=n_groups, epg=epg)

    def zspec(j):
        return pl.BlockSpec((tm, half), lambda i, j=j: (i, off + j))

    return pl.pallas_call(
        kern,
        grid=(t // tm,),
        in_specs=[pl.BlockSpec((tm, c), lambda i: (i, 0)),
                  pl.BlockSpec((tm, c), lambda i: (i, 0)),
                  zspec(0), zspec(1), zspec(2), zspec(3),
                  pl.BlockSpec((tm, d), lambda i: (i, 0)),
                  _const_spec(wa.shape), _const_spec(wb.shape), _const_spec(wo.shape),
                  _const_spec(n2.shape), _const_spec(wr.shape), _const_spec(br.shape)],
        out_specs=[pl.BlockSpec((tm, d + LANES), lambda i: (i, 0)),
                   pl.BlockSpec((1, LANES), lambda i: (0, 0))],
        out_shape=[jax.ShapeDtypeStruct((t, d + LANES), _F32),
                   jax.ShapeDtypeStruct((1, LANES), _F32)],
        scratch_shapes=[pltpu.VMEM((1, LANES), _F32)],
        compiler_params=_params(1),
        name="combine",
    )(ya, yb, z, z, z, z, x2d, wa, wb, wo, n2, wr, br)


def _row_gather_start(idx_ref, base, src_hbm, buf, slot, sem, n_rows, unrolled, width=None):
    width = src_hbm.shape[1] if width is None else width

    def start(r):
        row = idx_ref[base + r]
        pltpu.make_async_copy(src_hbm.at[pl.ds(row, 1), pl.ds(0, width)],
                              buf.at[slot, pl.ds(r, 1), pl.ds(0, width)],
                              sem.at[slot]).start(priority=1)

    if unrolled:
        for r in range(n_rows):
            start(r)
    else:
        def body(r, _):
            start(r)
            return 0
        lax.fori_loop(0, n_rows, body, 0, unroll=8)


def _row_gather_wait(src_hbm, buf, slot, sem, n_rows, width=None):
    width = src_hbm.shape[1] if width is None else width
    pltpu.make_async_copy(src_hbm.at[pl.ds(0, n_rows), pl.ds(0, width)],
                          buf.at[slot, :, pl.ds(0, width)], sem.at[slot]).wait()


def _moe_kernel(src_ref, ea_ref, eb_ref, valid_ref,
                xe_hbm, n2_ref, wga_ref, wua_ref, wda_ref, wgb_ref, wub_ref, wdb_ref,
                x2s_ref, xbuf, gsem, *, tm):
    j = pl.program_id(0)
    slot = j % 2
    d = x2s_ref.shape[1]

    @pl.when(j == 0)
    def _():
        _row_gather_start(src_ref, 0, xe_hbm, xbuf, 0, gsem, tm, unrolled=False)

    @pl.when(valid_ref[j] == 1)
    def _():
        _row_gather_wait(xe_hbm, xbuf, slot, gsem, tm)
        _row_gather_start(src_ref, (j + 1) * tm, xe_hbm, xbuf, 1 - slot, gsem, tm, unrolled=True)
        xe = xbuf[slot]
        x1 = xe[:, 0:d]
        w_a = xe[:, d + 2:d + 3]
        w_b = xe[:, d + 3:d + 4]
        ht = _rms(x1, n2_ref[...]).astype(_BF16)

        def expert(wg_ref, wu_ref, wgt):
            hg = jnp.dot(ht, wg_ref[0], preferred_element_type=_F32)
            hu = jnp.dot(ht, wu_ref[0], preferred_element_type=_F32)
            return (hg * jax.nn.sigmoid(hg) * hu * wgt).astype(_BF16)

        y = jnp.dot(expert(wga_ref, wua_ref, w_a), wda_ref[0], preferred_element_type=_F32)
        y = y + jnp.dot(expert(wgb_ref, wub_ref, w_b), wdb_ref[0], preferred_element_type=_F32)

        x2s_ref[...] = x1 + y

    @pl.when(valid_ref[j] == 0)
    def _():
        x2s_ref[...] = jnp.zeros(x2s_ref.shape, _F32)

        @pl.when(valid_ref[jnp.maximum(j - 1, 0)] == 1)
        def _():
            _row_gather_wait(xe_hbm, xbuf, slot, gsem, tm)


def _moe(src, ea, eb, valid, xe, n2, wg, wu, wd, tm):
    n_tiles = valid.shape[0]
    d = wg.shape[1]
    f = wg.shape[2]
    kern = functools.partial(_moe_kernel, tm=tm)

    def wspec(shape, which):
        if which == 0:
            return pl.BlockSpec((1,) + shape, lambda j, s, a, b, v: (a[j], 0, 0))
        return pl.BlockSpec((1,) + shape, lambda j, s, a, b, v: (b[j], 0, 0))

    grid_spec = pltpu.PrefetchScalarGridSpec(
        num_scalar_prefetch=4,
        grid=(n_tiles,),
        in_specs=[pl.BlockSpec(memory_space=pl.ANY),
                  pl.BlockSpec((1, d), lambda j, *_: (0, 0)),
                  wspec((d, f), 0), wspec((d, f), 0), wspec((f, d), 0),
                  wspec((d, f), 1), wspec((d, f), 1), wspec((f, d), 1)],
        out_specs=pl.BlockSpec((tm, d), lambda j, *_: (j, 0)),
        scratch_shapes=[pltpu.VMEM((2, tm, d + LANES), _F32),
                        pltpu.SemaphoreType.DMA((2,))],
    )
    return pl.pallas_call(
        kern,
        grid_spec=grid_spec,
        out_shape=jax.ShapeDtypeStruct((n_tiles * tm, d), _F32),
        compiler_params=_params(1),
        name="moe",
    )(src, ea, eb, valid, xe, n2, wg, wu, wd, wg, wu, wd)


def _ple_kernel(pos_ref, x2s_hbm, p_ref, ng_ref, wg_ref, wp_ref, nf_ref, o_ref, xbuf, gsem, *, tm, n_steps):
    i = pl.program_id(0)
    slot = i % 2

    @pl.when(i == 0)
    def _():
        _row_gather_start(pos_ref, 0, x2s_hbm, xbuf, 0, gsem, tm, unrolled=False)

    _row_gather_wait(x2s_hbm, xbuf, slot, gsem, tm)
    nxt = jnp.where(i + 1 == n_steps, 0, i + 1)
    _row_gather_start(pos_ref, nxt * tm, x2s_hbm, xbuf, 1 - slot, gsem, tm, unrolled=True)
    x2 = xbuf[slot]
    g = jax.nn.sigmoid(jnp.dot(_rms(x2, ng_ref[...]).astype(_BF16), wg_ref[...],
                               preferred_element_type=_F32))
    e = jnp.dot(p_ref[...].astype(_BF16), wp_ref[...], preferred_element_type=_F32)
    o_ref[...] = _rms(x2 + g * e, nf_ref[...])

    @pl.when(i == n_steps - 1)
    def _():
        _row_gather_wait(x2s_hbm, xbuf, 1 - slot, gsem, tm)


def _ple(pos, x2s, p2d, ng, wg, wp, nf, tm, t):
    d = x2s.shape[1]
    pd = p2d.shape[1]
    n_steps = t // tm
    kern = functools.partial(_ple_kernel, tm=tm, n_steps=n_steps)

    def cspec(shape):
        nd = len(shape)
        return pl.BlockSpec(shape, lambda i, *_: (0,) * nd, pipeline_mode=pl.Buffered(1))

    grid_spec = pltpu.PrefetchScalarGridSpec(
        num_scalar_prefetch=1,
        grid=(n_steps,),
        in_specs=[pl.BlockSpec(memory_space=pl.ANY),
                  pl.BlockSpec((tm, pd), lambda i, *_: (i, 0)),
                  cspec(ng.shape), cspec(wg.shape), cspec(wp.shape), cspec(nf.shape)],
        out_specs=pl.BlockSpec((tm, d), lambda i, *_: (i, 0)),
        scratch_shapes=[pltpu.VMEM((2, tm, d), _F32),
                        pltpu.SemaphoreType.DMA((2,))],
    )
    return pl.pallas_call(
        kern,
        grid_spec=grid_spec,
        out_shape=jax.ShapeDtypeStruct((t, d), _F32),
        compiler_params=_params(1),
        name="ple",
    )(pos, x2s, p2d, ng, wg, wp, nf)


def _block_diag(w, per_block):
    h, hd, _ = w.shape
    nb = h // per_block
    w4 = w.reshape(nb, per_block, hd, hd)
    rows = [jnp.pad(w4[:, p], ((0, 0), (0, 0), (p * hd, (per_block - 1 - p) * hd))) for p in range(per_block)]
    return jnp.concatenate(rows, axis=1)


def _layer(x2d, p2d, bsz, seq, norm1_g, w_in, conv_w, conv_b, w_rg_a, b_rg_a, w_rg_x, b_rg_x, lru_lambda,
           w_pool, pool_scale, w_branch_a, w_branch_b, w_out, norm2_g, w_router_group, b_router_group,
           w_router_expert, b_router_expert, w_e_gate, w_e_up, w_e_down, norm_ple_g, w_ple_gate,
           w_ple_proj, out_norm_g):
    t, d = x2d.shape
    c = conv_b.shape[0]
    heads, hd, _ = w_rg_a.shape
    n_groups = w_router_group.shape[1]
    n_exp = w_router_expert.shape[1]
    epg = n_exp // n_groups
    assert epg == 4 and TOP_K == 2 and hd * (MXU_DIM // hd) == MXU_DIM
    assert w_pool.shape[0] == len(POOL_WINDOWS) and w_pool.shape[1] == MXU_DIM

    tm_in, tn_in, rc_in = min(2048, t), 512, min(256, t)
    tm_mix = min(256, seq)
    tm_cmb = min(256, t)
    tm_moe = min(128, t)
    tm_ple = min(256, t)

    row = lambda v: v.reshape(1, -1).astype(_F32)
    per_block = MXU_DIM // hd

    z = _inproj(x2d, row(norm1_g), w_in, tm_in, tn_in, rc_in)
    ya, yb = _mixer(z, bsz, seq, conv_w.reshape(CONV_WIDTH, c), row(conv_b),
                    _block_diag(w_rg_a, per_block).astype(_BF16), row(b_rg_a),
                    _block_diag(w_rg_x, per_block).astype(_BF16), row(b_rg_x),
                    row(lru_lambda), w_pool.astype(_BF16), row(pool_scale), tm_mix)

    n_rt = n_groups + n_exp
    wr = jnp.pad(jnp.concatenate([w_router_group, w_router_expert], axis=1),
                 ((0, 0), (0, LANES - n_rt))).astype(_BF16)
    br = jnp.pad(jnp.concatenate([b_router_group, b_router_expert]), (0, LANES - n_rt)).reshape(1, LANES)
    xe, counts = _combine(ya, yb, z, x2d, w_branch_a.astype(_BF16), w_branch_b.astype(_BF16),
                          w_out.astype(_BF16), row(norm2_g), wr, br, tm_cmb, n_groups, epg)

    n_buckets = n_groups * _N_PAIRS
    n_tiles = t // tm_moe + n_buckets
    cnt = counts[0, :n_buckets].astype(jnp.int32)
    nt = (cnt + tm_moe - 1) // tm_moe
    cum = jnp.cumsum(nt)
    total = cum[-1]
    bucket = xe[:, d].astype(jnp.int32)
    rank = xe[:, d + 1].astype(jnp.int32)
    pos = (cum - nt)[bucket] * tm_moe + rank
    tok = jnp.arange(t, dtype=jnp.int32)
    src = (jnp.arange(n_tiles * tm_moe, dtype=jnp.int32) % t).at[pos].set(tok)
    tile = jnp.minimum(jnp.arange(n_tiles, dtype=jnp.int32), total - 1)
    tb = jnp.sum((cum[None, :] <= tile[:, None]).astype(jnp.int32), axis=1)
    grp, pr = tb // _N_PAIRS, tb % _N_PAIRS
    ea = grp * epg + jnp.asarray(_PAIR_SLOT_A, jnp.int32)[pr]
    eb = grp * epg + jnp.asarray(_PAIR_SLOT_B, jnp.int32)[pr]
    valid = (jnp.arange(n_tiles, dtype=jnp.int32) < total).astype(jnp.int32)

    x2s = _moe(src, ea, eb, valid, xe, row(norm2_g), w_e_gate.astype(_BF16), w_e_up.astype(_BF16),
               w_e_down.astype(_BF16), tm_moe)
    return _ple(pos, x2s, p2d, row(norm_ple_g), w_ple_gate.astype(_BF16), w_ple_proj.astype(_BF16),
                row(out_norm_g), tm_ple, t)


def kernel(x, p, norm1_g, w_in, conv_w, conv_b, w_rg_a, b_rg_a, w_rg_x, b_rg_x, lru_lambda, w_pool, pool_scale, w_branch_a, w_branch_b, w_out, norm2_g, w_router_group, b_router_group, w_router_expert, b_router_expert, w_e_gate, w_e_up, w_e_down, norm_ple_g, w_ple_gate, w_ple_proj, final_norm_g):
    bsz, seq, d = x.shape
    depth = p.shape[0]
    assert depth == 1, "the final RMSNorm is fused into the last layer's embedding kernel"
    out = _layer(x.reshape(bsz * seq, d), p[0].reshape(bsz * seq, -1), bsz, seq,
                 norm1_g[0], w_in[0], conv_w[0], conv_b[0], w_rg_a[0], b_rg_a[0], w_rg_x[0], b_rg_x[0],
                 lru_lambda[0], w_pool[0], pool_scale[0], w_branch_a[0], w_branch_b[0], w_out[0],
                 norm2_g[0], w_router_group[0], b_router_group[0], w_router_expert[0],
                 b_router_expert[0], w_e_gate[0], w_e_up[0], w_e_down[0], norm_ple_g[0],
                 w_ple_gate[0], w_ple_proj[0], final_norm_g)
    return out.reshape(bsz, seq, d)
```

```python
import functools

import jax
import jax.numpy as jnp
from jax import lax
from jax.experimental import pallas as pl
from jax.experimental.pallas import tpu as pltpu

EPS = 1e-6
LRU_C = 8.0
CONV_WIDTH = 4
POOL_WINDOWS = (2, 4, 8, 16)
TOP_K = 2

LANES = 128
SUBLANES = 8
MXU_DIM = 256
VMEM_LIMIT_BYTES = 56 * 1024 * 1024

_BF16 = jnp.bfloat16
_F32 = jnp.float32

_PAIR_SLOT_A = (0, 0, 0, 1, 1, 3)
_PAIR_SLOT_B = (1, 2, 3, 3, 2, 2)
_N_PAIRS = len(_PAIR_SLOT_A)


def _rms(x, g):
    ms = jnp.mean(x * x, axis=-1, keepdims=True)
    return x * lax.rsqrt(ms + EPS) * g


def _const_spec(shape):
    nd = len(shape)
    return pl.BlockSpec(shape, lambda *_: (0,) * nd, pipeline_mode=pl.Buffered(1))


def _params(n_axes):
    return pltpu.CompilerParams(dimension_semantics=("arbitrary",) * n_axes,
                                vmem_limit_bytes=VMEM_LIMIT_BYTES)


def _inproj_kernel(x_hbm, g_ref, w_ref, z_ref, h_ref, xs_ref, sem, *, tm, rc):
    i = pl.program_id(0)
    n_chunks = tm // rc

    def chunk_copy(c):
        return pltpu.make_async_copy(x_hbm.at[pl.ds(i * tm + c * rc, rc)], xs_ref.at[c % 2], sem.at[c % 2])

    @pl.when(pl.program_id(1) == 0)
    def _():
        chunk_copy(0).start()
        for c in range(n_chunks):
            if c + 1 < n_chunks:
                chunk_copy(c + 1).start()
            chunk_copy(c).wait()
            h_ref[c * rc:(c + 1) * rc, :] = _rms(xs_ref[c % 2], g_ref[...]).astype(_BF16)

    z_ref[...] = jnp.dot(h_ref[...], w_ref[...].astype(_BF16), preferred_element_type=_F32)


def _inproj(x2d, g, w, tm, tn, rc):
    t, d = x2d.shape
    n = w.shape[1]
    kern = functools.partial(_inproj_kernel, tm=tm, rc=rc)
    return pl.pallas_call(
        kern,
        grid=(t // tm, n // tn),
        in_specs=[pl.BlockSpec(memory_space=pl.ANY),
                  pl.BlockSpec((1, d), lambda i, j: (0, 0)),
                  pl.BlockSpec((d, tn), lambda i, j: (0, j))],
        out_specs=pl.BlockSpec((tm, tn), lambda i, j: (i, j)),
        out_shape=jax.ShapeDtypeStruct((t, n), _F32),
        scratch_shapes=[pltpu.VMEM((tm, d), _BF16),
                        pltpu.VMEM((2, rc, d), _F32),
                        pltpu.SemaphoreType.DMA((2,))],
        compiler_params=_params(2),
        name="inproj",
    )(x2d, g, w)


def _scan_pitch(tm):
    p = -(-tm // SUBLANES)
    while p % SUBLANES != 4:
        p += 1
    return p


def _mixer_kernel(z_ref, cw_ref, cb_ref, wa_ref, ba_ref, wx_ref, bx_ref, lam_ref, wp_ref, ps_ref,
                  ya_ref, yb_ref,
                  er_ref, ep_ref, a_ref, b_ref, h_ref, car_ref, *, tm, pitch):
    s = pl.program_id(1)
    c = cb_ref.shape[1]
    n_slab = c // LANES
    hist_r = SUBLANES
    hist_p = 2 * SUBLANES

    @pl.when(s == 0)
    def _():
        er_ref[0:hist_r, :] = jnp.zeros((hist_r, c), _F32)
        ep_ref[0:hist_p, :] = jnp.zeros((hist_p, c), _F32)
        car_ref[...] = jnp.zeros(car_ref.shape, _F32)
        a_ref[:, tm:, :] = jnp.ones((n_slab, SUBLANES * pitch - tm, LANES), _F32)
        b_ref[:, tm:, :] = jnp.zeros((n_slab, SUBLANES * pitch - tm, LANES), _F32)

    @pl.when(s > 0)
    def _():
        er_ref[0:hist_r, :] = er_ref[tm:tm + hist_r, :]
        ep_ref[0:hist_p, :] = ep_ref[tm:tm + hist_p, :]

    er_ref[hist_r:hist_r + tm, :] = z_ref[:, 0:c]
    ep_ref[hist_p:hist_p + tm, :] = z_ref[:, 2 * c:3 * c]

    kvec = -LRU_C * jax.nn.softplus(-lam_ref[...])
    nblk = c // MXU_DIM
    for k in range(nblk):
        cs = slice(k * MXU_DIM, (k + 1) * MXU_DIM)
        xc = cb_ref[:, cs] + cw_ref[CONV_WIDTH - 1:CONV_WIDTH, cs] * er_ref[hist_r:hist_r + tm, cs]
        for j in range(1, CONV_WIDTH):
            xc = xc + cw_ref[CONV_WIDTH - 1 - j:CONV_WIDTH - j, cs] * er_ref[hist_r - j:hist_r - j + tm, cs]
        xcb = xc.astype(_BF16)
        r = jax.nn.sigmoid(jnp.dot(xcb, wa_ref[k], preferred_element_type=_F32) + ba_ref[:, cs])
        ig = jax.nn.sigmoid(jnp.dot(xcb, wx_ref[k], preferred_element_type=_F32) + bx_ref[:, cs])
        log_a = r * kvec[:, cs]
        a = jnp.exp(log_a)
        mult = jnp.sqrt(1.0 - a * a)
        bb = mult * ig * xc
        for q in range(MXU_DIM // LANES):
            slab = k * (MXU_DIM // LANES) + q
            a_ref[slab, 0:tm, :] = a[:, q * LANES:(q + 1) * LANES]
            b_ref[slab, 0:tm, :] = bb[:, q * LANES:(q + 1) * LANES]

    def sweep1(i, carry):
        hs, ps = carry
        nh, npd = [], []
        for sl in range(n_slab):
            av = a_ref[sl, pl.ds(i, SUBLANES, stride=pitch), :]
            bv = b_ref[sl, pl.ds(i, SUBLANES, stride=pitch), :]
            nh.append(av * hs[sl] + bv)
            npd.append(av * ps[sl])
        return tuple(nh), tuple(npd)

    zero = jnp.zeros((SUBLANES, LANES), _F32)
    one = jnp.ones((SUBLANES, LANES), _F32)
    h_end, a_end = lax.fori_loop(0, pitch, sweep1, ((zero,) * n_slab, (one,) * n_slab))

    row = lax.broadcasted_iota(jnp.int32, (SUBLANES, LANES), 0)
    h0 = []
    for sl in range(n_slab):
        hh, aa = h_end[sl], a_end[sl]
        d = 1
        while d < SUBLANES:
            hs_ = jnp.where(row >= d, pltpu.roll(hh, d, 0), 0.0)
            as_ = jnp.where(row >= d, pltpu.roll(aa, d, 0), 1.0)
            hh = aa * hs_ + hh
            aa = aa * as_
            d *= 2
        cin = car_ref[:, sl * LANES:(sl + 1) * LANES]
        full = hh + aa * cin
        h0.append(jnp.where(row >= 1, pltpu.roll(full, 1, 0), cin))
        car_ref[:, sl * LANES:(sl + 1) * LANES] = jnp.broadcast_to(full[SUBLANES - 1:SUBLANES, :],
                                                                  (SUBLANES, LANES))

    def sweep2(i, hs):
        nh = []
        for sl in range(n_slab):
            av = a_ref[sl, pl.ds(i, SUBLANES, stride=pitch), :]
            bv = b_ref[sl, pl.ds(i, SUBLANES, stride=pitch), :]
            hv = av * hs[sl] + bv
            h_ref[sl, pl.ds(i, SUBLANES, stride=pitch), :] = hv
            nh.append(hv)
        return tuple(nh)

    lax.fori_loop(0, pitch, sweep2, tuple(h0))

    for sl in range(n_slab):
        cs = slice(sl * LANES, (sl + 1) * LANES)
        g = z_ref[:, c + sl * LANES:c + (sl + 1) * LANES]
        ya_ref[:, cs] = (h_ref[sl, 0:tm, :] * jax.nn.gelu(g)).astype(_BF16)

    t_idx = (s * tm + lax.broadcasted_iota(jnp.int32, (tm, 1), 0) + 1).astype(_F32)
    n_grp = len(POOL_WINDOWS)
    gd = c // n_grp
    for gi, w in enumerate(POOL_WINDOWS):
        cs = slice(gi * gd, (gi + 1) * gd)
        e = ep_ref[:, cs]
        acc = e
        d = 1
        while d < w:
            acc = acc + pltpu.roll(acc, d, 0)
            d *= 2
        xt = e[hist_p:, :]
        cnt = jnp.minimum(t_idx, float(w))
        dd = acc[hist_p:, :] / cnt - xt
        yb = jnp.dot(dd.astype(_BF16), wp_ref[gi], preferred_element_type=_F32) * ps_ref[:, cs]
        yb_ref[:, cs] = yb.astype(_BF16)


def _mixer(z, bsz, seq, cw, cb, wa_bd, ba, wx_bd, bx, lam, wp, ps, tm):
    t = z.shape[0]
    c = cb.shape[1]
    ns = seq // tm
    pitch = _scan_pitch(tm)
    n_slab = c // LANES
    kern = functools.partial(_mixer_kernel, tm=tm, pitch=pitch)
    return pl.pallas_call(
        kern,
        grid=(bsz, ns),
        in_specs=[pl.BlockSpec((tm, 3 * c), lambda b, s: (b * ns + s, 0)),
                  _const_spec(cw.shape), _const_spec(cb.shape),
                  _const_spec(wa_bd.shape), _const_spec(ba.shape),
                  _const_spec(wx_bd.shape), _const_spec(bx.shape),
                  _const_spec(lam.shape), _const_spec(wp.shape), _const_spec(ps.shape)],
        out_specs=[pl.BlockSpec((tm, c), lambda b, s: (b * ns + s, 0)),
                   pl.BlockSpec((tm, c), lambda b, s: (b * ns + s, 0))],
        out_shape=[jax.ShapeDtypeStruct((t, c), _BF16), jax.ShapeDtypeStruct((t, c), _BF16)],
        scratch_shapes=[pltpu.VMEM((SUBLANES + tm, c), _F32),
                        pltpu.VMEM((2 * SUBLANES + tm, c), _F32),
                        pltpu.VMEM((n_slab, SUBLANES * pitch, LANES), _F32),
                        pltpu.VMEM((n_slab, SUBLANES * pitch, LANES), _F32),
                        pltpu.VMEM((n_slab, SUBLANES * pitch, LANES), _F32),
                        pltpu.VMEM((SUBLANES, c), _F32)],
        compiler_params=_params(2),
        name="mixer",
    )(z, cw, cb, wa_bd, ba, wx_bd, bx, lam, wp, ps)


def _combine_kernel(ya_ref, yb_ref, ga0_ref, ga1_ref, gb0_ref, gb1_ref, x_ref,
                    wa_ref, wb_ref, wo_ref, n2_ref, wr_ref, br_ref,
                    xe_ref, cnt_ref, cnt_s, *, tm, n_groups, epg):
    i = pl.program_id(0)
    d = x_ref.shape[1]
    half = d // 2

    @pl.when(i == 0)
    def _():
        cnt_s[...] = jnp.zeros(cnt_s.shape, _F32)

    pa = jnp.dot(ya_ref[...], wa_ref[...], preferred_element_type=_F32)
    pb = jnp.dot(yb_ref[...], wb_ref[...], preferred_element_type=_F32)
    u0 = jax.nn.sigmoid(ga0_ref[...]) * pa[:, :half] + jax.nn.sigmoid(gb0_ref[...]) * pb[:, :half]
    u1 = jax.nn.sigmoid(ga1_ref[...]) * pa[:, half:] + jax.nn.sigmoid(gb1_ref[...]) * pb[:, half:]
    u = jnp.concatenate([u0, u1], axis=1).astype(_BF16)
    x1 = x_ref[...] + jnp.dot(u, wo_ref[...], preferred_element_type=_F32)
    xe_ref[:, 0:d] = x1

    ht = _rms(x1, n2_ref[...]).astype(_BF16)
    logits = jnp.dot(ht, wr_ref[...], preferred_element_type=_F32) + br_ref[...]

    lane = lax.broadcasted_iota(jnp.int32, (tm, LANES), 1).astype(_F32)
    ninf = -jnp.inf
    big = float(LANES)

    def first_argmax(v):
        m = jnp.max(v, axis=-1, keepdims=True)
        return m, jnp.min(jnp.where(v == m, lane, big), axis=-1, keepdims=True)

    is_g = lane < float(n_groups)
    gmax, gidx = first_argmax(jnp.where(is_g, logits, ninf))
    g_w = 1.0 / jnp.sum(jnp.where(is_g, jnp.exp(logits - gmax), 0.0), axis=-1, keepdims=True)
    lo_lane = float(n_groups) + float(epg) * gidx
    in_grp = (lane >= lo_lane) & (lane < lo_lane + float(epg))
    le = jnp.where(in_grp, logits, ninf)
    m1, i1 = first_argmax(le)
    m2, i2 = first_argmax(jnp.where(lane == i1, ninf, le))
    e21 = jnp.exp(m2 - m1)
    w1 = g_w / (1.0 + e21)
    w2 = w1 * e21
    e1 = i1 - lo_lane
    e2 = i2 - lo_lane
    lo = jnp.minimum(e1, e2)
    hi = jnp.maximum(e1, e2)
    w_lo = jnp.where(e1 < e2, w1, w2)
    w_hi = jnp.where(e1 < e2, w2, w1)
    pair = jnp.where(lo == 0.0, hi - 1.0, jnp.where(lo == 1.0, 6.0 - hi, 5.0))
    swap = pair == 5.0
    w_a = jnp.where(swap, w_hi, w_lo)
    w_b = jnp.where(swap, w_lo, w_hi)
    bucket = float(_N_PAIRS) * gidx + pair

    onehot = lane == bucket
    oh_bf = jnp.where(onehot, 1.0, 0.0).astype(_BF16)
    rr = lax.broadcasted_iota(jnp.int32, (tm, tm), 0)
    cc = lax.broadcasted_iota(jnp.int32, (tm, tm), 1)
    tri = jnp.where(cc < rr, 1.0, 0.0).astype(_BF16)
    before = jnp.dot(tri, oh_bf, preferred_element_type=_F32) + cnt_s[...]
    rank = jnp.sum(jnp.where(onehot, before, 0.0), axis=-1, keepdims=True)
    cnt_s[...] = cnt_s[...] + jnp.sum(jnp.where(onehot, 1.0, 0.0), axis=0, keepdims=True)
    cnt_ref[...] = cnt_s[...]

    info = jnp.where(lane == 0.0, bucket,
                     jnp.where(lane == 1.0, rank,
                               jnp.where(lane == 2.0, w_a, jnp.where(lane == 3.0, w_b, 0.0))))
    xe_ref[:, d:d + LANES] = info


def _combine(ya, yb, z, x2d, wa, wb, wo, n2, wr, br, tm, n_groups, epg):
    t, d = x2d.shape
    c = ya.shape[1]
    half = d // 2
    off = (3 * c) // half
    kern = functools.partial(_combine_kernel, tm=tm, n_groups=n_groups, epg=epg)

    def zspec(j):
        return pl.BlockSpec((tm, half), lambda i, j=j: (i, off + j))

    return pl.pallas_call(
        kern,
        grid=(t // tm,),
        in_specs=[pl.BlockSpec((tm, c), lambda i: (i, 0)),
                  pl.BlockSpec((tm, c), lambda i: (i, 0)),
                  zspec(0), zspec(1), zspec(2), zspec(3),
                  pl.BlockSpec((tm, d), lambda i: (i, 0)),
                  _const_spec(wa.shape), _const_spec(wb.shape), _const_spec(wo.shape),
                  _const_spec(n2.shape), _const_spec(wr.shape), _const_spec(br.shape)],
        out_specs=[pl.BlockSpec((tm, d + LANES), lambda i: (i, 0)),
                   pl.BlockSpec((1, LANES), lambda i: (0, 0))],
        out_shape=[jax.ShapeDtypeStruct((t, d + LANES), _F32),
                   jax.ShapeDtypeStruct((1, LANES), _F32)],
        scratch_shapes=[pltpu.VMEM((1, LANES), _F32)],
        compiler_params=_params(1),
        name="combine",
    )(ya, yb, z, z, z, z, x2d, wa, wb, wo, n2, wr, br)


def _row_gather_start(idx_ref, base, src_hbm, buf, slot, sem, n_rows, unrolled, width=None):
    width = src_hbm.shape[1] if width is None else width

    def start(r):
        row = idx_ref[base + r]
        pltpu.make_async_copy(src_hbm.at[pl.ds(row, 1), pl.ds(0, width)],
                              buf.at[slot, pl.ds(r, 1), pl.ds(0, width)],
                              sem.at[slot]).start(priority=1)

    if unrolled:
        for r in range(n_rows):
            start(r)
    else:
        def body(r, _):
            start(r)
            return 0
        lax.fori_loop(0, n_rows, body, 0, unroll=8)


def _row_gather_wait(src_hbm, buf, slot, sem, n_rows, width=None):
    width = src_hbm.shape[1] if width is None else width
    pltpu.make_async_copy(src_hbm.at[pl.ds(0, n_rows), pl.ds(0, width)],
                          buf.at[slot, :, pl.ds(0, width)], sem.at[slot]).wait()


MOE_WEIGHT_SLOTS = 6
MOE_PARTS = 4
MOE_CHUNKS = 3 * MOE_PARTS
MOE_PUMP = 3


def _moe_weight_plan(n_groups, epg):
    first_use = {e: min(p for p in range(_N_PAIRS) if e in (_PAIR_SLOT_A[p], _PAIR_SLOT_B[p])) for e in range(epg)}
    last_use = {e: max(p for p in range(_N_PAIRS) if e in (_PAIR_SLOT_A[p], _PAIR_SLOT_B[p])) for e in range(epg)}
    free_after = [-1] * MOE_WEIGHT_SLOTS
    loads, slot_of = [], {}
    for g in range(n_groups):
        for e in sorted(range(epg), key=lambda e: (first_use[e], e)):
            needed_by = g * _N_PAIRS + first_use[e]
            s = min(range(MOE_WEIGHT_SLOTS), key=lambda s: (free_after[s], s))
            assert free_after[s] < needed_by
            loads.append((g * epg + e, s, free_after[s], needed_by))
            free_after[s] = g * _N_PAIRS + last_use[e]
            slot_of[(g, e)] = s
    n_buckets = n_groups * _N_PAIRS
    need = [sum(1 for l in loads if l[3] <= b) for b in range(n_buckets)]
    allow = [sum(1 for l in loads if l[2] < b) for b in range(n_buckets)]
    slot_a = [slot_of[(b // _N_PAIRS, _PAIR_SLOT_A[b % _N_PAIRS])] for b in range(n_buckets)]
    slot_b = [slot_of[(b // _N_PAIRS, _PAIR_SLOT_B[b % _N_PAIRS])] for b in range(n_buckets)]
    return [l[0] for l in loads], [l[1] for l in loads], need, allow, slot_a, slot_b


def _moe_kernel(src_ref, bkt_ref, valid_ref, need_ref, allow_ref, sa_ref, sb_ref, le_ref, ls_ref,
                xe_hbm, n2_ref, wg_hbm, wu_hbm, wd_hbm,
                x2s_ref, xbuf, gsem, wg_s, wu_s, wd_s, st_gu, st_d, wsem, cnt, *, tm):
    j = pl.program_id(0)
    slot = j % 2
    d = x2s_ref.shape[1]
    rows_gu = wg_hbm.shape[1] // MOE_PARTS
    rows_d = wd_hbm.shape[1] // MOE_PARTS

    def chunk_dma(c, kind):
        load = c // MOE_CHUNKS
        part = c % MOE_PARTS
        e = le_ref[load]
        if kind == 2:
            return pltpu.make_async_copy(wd_hbm.at[e, pl.ds(part * rows_d, rows_d)], st_d.at[c % 2], wsem.at[c % 2])
        src = wg_hbm if kind == 0 else wu_hbm
        return pltpu.make_async_copy(src.at[e, pl.ds(part * rows_gu, rows_gu)], st_gu.at[c % 2], wsem.at[c % 2])

    def for_kind(c, fn):
        kind = (c % MOE_CHUNKS) // MOE_PARTS
        for k in range(3):
            @pl.when(kind == k)
            def _(k=k):
                fn(k)

    def issue_upto(limit):
        for _ in range(2):
            @pl.when(cnt[0] < limit)
            def _():
                c = cnt[0]
                for_kind(c, lambda k: chunk_dma(c, k).start())
                cnt[0] = c + 1

    def retire():
        c = cnt[1]
        s = ls_ref[c // MOE_CHUNKS]
        part = c % MOE_PARTS

        def finish(k):
            chunk_dma(c, k).wait()
            if k == 2:
                wd_s[s, pl.ds(part * rows_d, rows_d), :] = st_d[c % 2].astype(_BF16)
            else:
                dst = wg_s if k == 0 else wu_s
                dst[s, pl.ds(part * rows_gu, rows_gu), :] = st_gu[c % 2].astype(_BF16)

        for_kind(c, finish)
        cnt[1] = c + 1

    def pump(required, allowed, extra):
        n_iter = jnp.maximum(required - cnt[1], jnp.minimum(extra, allowed - cnt[1]))

        def body(_, carry):
            issue_upto(jnp.minimum(allowed, cnt[1] + 2))
            retire()
            return carry

        lax.fori_loop(0, jnp.maximum(n_iter, 0), body, 0)
        issue_upto(jnp.minimum(allowed, cnt[1] + 2))

    @pl.when(j == 0)
    def _():
        cnt[0] = 0
        cnt[1] = 0
        _row_gather_start(src_ref, 0, xe_hbm, xbuf, 0, gsem, tm, unrolled=False)

    @pl.when(valid_ref[j] == 1)
    def _():
        b = bkt_ref[j]
        pump(need_ref[b] * MOE_CHUNKS, allow_ref[b] * MOE_CHUNKS, MOE_PUMP)
        s_a = sa_ref[b]
        s_b = sb_ref[b]
        _row_gather_wait(xe_hbm, xbuf, slot, gsem, tm)
        _row_gather_start(src_ref, (j + 1) * tm, xe_hbm, xbuf, 1 - slot, gsem, tm, unrolled=True)
        xe = xbuf[slot]
        x1 = xe[:, 0:d]
        w_a = xe[:, d + 2:d + 3]
        w_b = xe[:, d + 3:d + 4]
        ht = _rms(x1, n2_ref[...]).astype(_BF16)

        def expert(s, wgt):
            hg = jnp.dot(ht, wg_s[s], preferred_element_type=_F32)
            hu = jnp.dot(ht, wu_s[s], preferred_element_type=_F32)
            return (hg * jax.nn.sigmoid(hg) * hu * wgt).astype(_BF16)

        y = jnp.dot(expert(s_a, w_a), wd_s[s_a], preferred_element_type=_F32)
        y = y + jnp.dot(expert(s_b, w_b), wd_s[s_b], preferred_element_type=_F32)
        x2s_ref[...] = x1 + y

    @pl.when(valid_ref[j] == 0)
    def _():
        x2s_ref[...] = jnp.zeros(x2s_ref.shape, _F32)

        @pl.when(valid_ref[jnp.maximum(j - 1, 0)] == 1)
        def _():
            _row_gather_wait(xe_hbm, xbuf, slot, gsem, tm)
            lax.fori_loop(0, cnt[0] - cnt[1], lambda _, carry: (retire(), carry)[1], 0)


def _moe(src, bkt, valid, xe, n2, wg, wu, wd, tm, n_groups, epg):
    n_tiles = valid.shape[0]
    d = wg.shape[1]
    f = wg.shape[2]
    kern = functools.partial(_moe_kernel, tm=tm)
    le, ls, need, allow, slot_a, slot_b = (jnp.asarray(v, jnp.int32) for v in _moe_weight_plan(n_groups, epg))

    grid_spec = pltpu.PrefetchScalarGridSpec(
        num_scalar_prefetch=9,
        grid=(n_tiles,),
        in_specs=[pl.BlockSpec(memory_space=pl.ANY),
                  pl.BlockSpec((1, d), lambda j, *_: (0, 0)),
                  pl.BlockSpec(memory_space=pl.ANY),
                  pl.BlockSpec(memory_space=pl.ANY),
                  pl.BlockSpec(memory_space=pl.ANY)],
        out_specs=pl.BlockSpec((tm, d), lambda j, *_: (j, 0)),
        scratch_shapes=[pltpu.VMEM((2, tm, d + LANES), _F32),
                        pltpu.SemaphoreType.DMA((2,)),
                        pltpu.VMEM((MOE_WEIGHT_SLOTS, d, f), _BF16),
                        pltpu.VMEM((MOE_WEIGHT_SLOTS, d, f), _BF16),
                        pltpu.VMEM((MOE_WEIGHT_SLOTS, f, d), _BF16),
                        pltpu.VMEM((2, d // MOE_PARTS, f), _F32),
                        pltpu.VMEM((2, f // MOE_PARTS, d), _F32),
                        pltpu.SemaphoreType.DMA((2,)),
                        pltpu.SMEM((2,), jnp.int32)],
    )
    return pl.pallas_call(
        kern,
        grid_spec=grid_spec,
        out_shape=jax.ShapeDtypeStruct((n_tiles * tm, d), _F32),
        compiler_params=_params(1),
        name="moe",
    )(src, bkt, valid, need, allow, slot_a, slot_b, le, ls, xe, n2, wg, wu, wd)


def _ple_kernel(pos_ref, x2s_hbm, p_ref, ng_ref, wg_ref, wp_ref, nf_ref, o_ref, xbuf, gsem, *, tm, n_steps):
    i = pl.program_id(0)
    slot = i % 2

    @pl.when(i == 0)
    def _():
        _row_gather_start(pos_ref, 0, x2s_hbm, xbuf, 0, gsem, tm, unrolled=False)

    _row_gather_wait(x2s_hbm, xbuf, slot, gsem, tm)
    nxt = jnp.where(i + 1 == n_steps, 0, i + 1)
    _row_gather_start(pos_ref, nxt * tm, x2s_hbm, xbuf, 1 - slot, gsem, tm, unrolled=True)
    x2 = xbuf[slot]
    g = jax.nn.sigmoid(jnp.dot(_rms(x2, ng_ref[...]).astype(_BF16), wg_ref[...],
                               preferred_element_type=_F32))
    e = jnp.dot(p_ref[...].astype(_BF16), wp_ref[...], preferred_element_type=_F32)
    o_ref[...] = _rms(x2 + g * e, nf_ref[...])

    @pl.when(i == n_steps - 1)
    def _():
        _row_gather_wait(x2s_hbm, xbuf, 1 - slot, gsem, tm)


def _ple(pos, x2s, p2d, ng, wg, wp, nf, tm, t):
    d = x2s.shape[1]
    pd = p2d.shape[1]
    n_steps = t // tm
    kern = functools.partial(_ple_kernel, tm=tm, n_steps=n_steps)

    def cspec(shape):
        nd = len(shape)
        return pl.BlockSpec(shape, lambda i, *_: (0,) * nd, pipeline_mode=pl.Buffered(1))

    grid_spec = pltpu.PrefetchScalarGridSpec(
        num_scalar_prefetch=1,
        grid=(n_steps,),
        in_specs=[pl.BlockSpec(memory_space=pl.ANY),
                  pl.BlockSpec((tm, pd), lambda i, *_: (i, 0)),
                  cspec(ng.shape), cspec(wg.shape), cspec(wp.shape), cspec(nf.shape)],
        out_specs=pl.BlockSpec((tm, d), lambda i, *_: (i, 0)),
        scratch_shapes=[pltpu.VMEM((2, tm, d), _F32),
                        pltpu.SemaphoreType.DMA((2,))],
    )
    return pl.pallas_call(
        kern,
        grid_spec=grid_spec,
        out_shape=jax.ShapeDtypeStruct((t, d), _F32),
        compiler_params=_params(1),
        name="ple",
    )(pos, x2s, p2d, ng, wg, wp, nf)


def _block_diag(w, per_block):
    h, hd, _ = w.shape
    nb = h // per_block
    w4 = w.reshape(nb, per_block, hd, hd)
    rows = [jnp.pad(w4[:, p], ((0, 0), (0, 0), (p * hd, (per_block - 1 - p) * hd))) for p in range(per_block)]
    return jnp.concatenate(rows, axis=1)


def _layer(x2d, p2d, bsz, seq, norm1_g, w_in, conv_w, conv_b, w_rg_a, b_rg_a, w_rg_x, b_rg_x, lru_lambda,
           w_pool, pool_scale, w_branch_a, w_branch_b, w_out, norm2_g, w_router_group, b_router_group,
           w_router_expert, b_router_expert, w_e_gate, w_e_up, w_e_down, norm_ple_g, w_ple_gate,
           w_ple_proj, out_norm_g):
    t, d = x2d.shape
    c = conv_b.shape[0]
    heads, hd, _ = w_rg_a.shape
    n_groups = w_router_group.shape[1]
    n_exp = w_router_expert.shape[1]
    epg = n_exp // n_groups
    assert epg == 4 and TOP_K == 2 and hd * (MXU_DIM // hd) == MXU_DIM
    assert w_pool.shape[0] == len(POOL_WINDOWS) and w_pool.shape[1] == MXU_DIM

    tm_in, tn_in, rc_in = min(2048, t), 512, min(256, t)
    tm_mix = min(256, seq)
    tm_cmb = min(256, t)
    tm_moe = min(128, t)
    tm_ple = min(256, t)

    row = lambda v: v.reshape(1, -1).astype(_F32)
    per_block = MXU_DIM // hd

    z = _inproj(x2d, row(norm1_g), w_in, tm_in, tn_in, rc_in)
    ya, yb = _mixer(z, bsz, seq, conv_w.reshape(CONV_WIDTH, c), row(conv_b),
                    _block_diag(w_rg_a, per_block).astype(_BF16), row(b_rg_a),
                    _block_diag(w_rg_x, per_block).astype(_BF16), row(b_rg_x),
                    row(lru_lambda), w_pool.astype(_BF16), row(pool_scale), tm_mix)

    n_rt = n_groups + n_exp
    wr = jnp.pad(jnp.concatenate([w_router_group, w_router_expert], axis=1),
                 ((0, 0), (0, LANES - n_rt))).astype(_BF16)
    br = jnp.pad(jnp.concatenate([b_router_group, b_router_expert]), (0, LANES - n_rt)).reshape(1, LANES)
    xe, counts = _combine(ya, yb, z, x2d, w_branch_a.astype(_BF16), w_branch_b.astype(_BF16),
                          w_out.astype(_BF16), row(norm2_g), wr, br, tm_cmb, n_groups, epg)

    n_buckets = n_groups * _N_PAIRS
    n_tiles = t // tm_moe + n_buckets
    cnt = counts[0, :n_buckets].astype(jnp.int32)
    nt = (cnt + tm_moe - 1) // tm_moe
    cum = jnp.cumsum(nt)
    total = cum[-1]
    bucket = xe[:, d].astype(jnp.int32)
    rank = xe[:, d + 1].astype(jnp.int32)
    pos = (cum - nt)[bucket] * tm_moe + rank
    tok = jnp.arange(t, dtype=jnp.int32)
    src = (jnp.arange(n_tiles * tm_moe, dtype=jnp.int32) % t).at[pos].set(tok)
    tile = jnp.minimum(jnp.arange(n_tiles, dtype=jnp.int32), total - 1)
    tb = jnp.sum((cum[None, :] <= tile[:, None]).astype(jnp.int32), axis=1)
    valid = (jnp.arange(n_tiles, dtype=jnp.int32) < total).astype(jnp.int32)

    x2s = _moe(src, tb, valid, xe, row(norm2_g), w_e_gate, w_e_up, w_e_down, tm_moe, n_groups, epg)
    return _ple(pos, x2s, p2d, row(norm_ple_g), w_ple_gate.astype(_BF16), w_ple_proj.astype(_BF16),
                row(out_norm_g), tm_ple, t)


def kernel(x, p, norm1_g, w_in, conv_w, conv_b, w_rg_a, b_rg_a, w_rg_x, b_rg_x, lru_lambda, w_pool, pool_scale, w_branch_a, w_branch_b, w_out, norm2_g, w_router_group, b_router_group, w_router_expert, b_router_expert, w_e_gate, w_e_up, w_e_down, norm_ple_g, w_ple_gate, w_ple_proj, final_norm_g):
    bsz, seq, d = x.shape
    depth = p.shape[0]
    assert depth == 1, "the final RMSNorm is fused into the last layer's embedding kernel"
    out = _layer(x.reshape(bsz * seq, d), p[0].reshape(bsz * seq, -1), bsz, seq,
                 norm1_g[0], w_in[0], conv_w[0], conv_b[0], w_rg_a[0], b_rg_a[0], w_rg_x[0], b_rg_x[0],
                 lru_lambda[0], w_pool[0], pool_scale[0], w_branch_a[0], w_branch_b[0], w_out[0],
                 norm2_g[0], w_router_group[0], b_router_group[0], w_router_expert[0],
                 b_router_expert[0], w_e_gate[0], w_e_up[0], w_e_down[0], norm_ple_g[0],
                 w_ple_gate[0], w_ple_proj[0], final_norm_g)
    return out.reshape(bsz, seq, d)
```

```python
import functools

import jax
import jax.numpy as jnp
from jax import lax
from jax.experimental import pallas as pl
from jax.experimental.pallas import tpu as pltpu

EPS = 1e-6
LRU_C = 8.0
CONV_WIDTH = 4
POOL_WINDOWS = (2, 4, 8, 16)
TOP_K = 2

LANES = 128
SUBLANES = 8
MXU_DIM = 256
VMEM_LIMIT_BYTES = 56 * 1024 * 1024

_BF16 = jnp.bfloat16
_F32 = jnp.float32

_PAIR_SLOT_A = (0, 0, 0, 1, 1, 3)
_PAIR_SLOT_B = (1, 2, 3, 3, 2, 2)
_N_PAIRS = len(_PAIR_SLOT_A)


def _rms(x, g):
    ms = jnp.mean(x * x, axis=-1, keepdims=True)
    return x * lax.rsqrt(ms + EPS) * g


def _const_spec(shape):
    nd = len(shape)
    return pl.BlockSpec(shape, lambda *_: (0,) * nd, pipeline_mode=pl.Buffered(1))


def _params(n_axes):
    return pltpu.CompilerParams(dimension_semantics=("arbitrary",) * n_axes,
                                vmem_limit_bytes=VMEM_LIMIT_BYTES)


def _inproj_kernel(x_hbm, g_ref, w_ref, z_ref, h_ref, xs_ref, sem, *, tm, rc):
    i = pl.program_id(0)
    n_chunks = tm // rc

    def chunk_copy(c):
        return pltpu.make_async_copy(x_hbm.at[pl.ds(i * tm + c * rc, rc)], xs_ref.at[c % 2], sem.at[c % 2])

    @pl.when(pl.program_id(1) == 0)
    def _():
        chunk_copy(0).start()
        for c in range(n_chunks):
            if c + 1 < n_chunks:
                chunk_copy(c + 1).start()
            chunk_copy(c).wait()
            h_ref[c * rc:(c + 1) * rc, :] = _rms(xs_ref[c % 2], g_ref[...]).astype(_BF16)

    z_ref[...] = jnp.dot(h_ref[...], w_ref[...].astype(_BF16), preferred_element_type=_F32)


def _inproj(x2d, g, w, tm, tn, rc):
    t, d = x2d.shape
    n = w.shape[1]
    kern = functools.partial(_inproj_kernel, tm=tm, rc=rc)
    return pl.pallas_call(
        kern,
        grid=(t // tm, n // tn),
        in_specs=[pl.BlockSpec(memory_space=pl.ANY),
                  pl.BlockSpec((1, d), lambda i, j: (0, 0)),
                  pl.BlockSpec((d, tn), lambda i, j: (0, j))],
        out_specs=pl.BlockSpec((tm, tn), lambda i, j: (i, j)),
        out_shape=jax.ShapeDtypeStruct((t, n), _F32),
        scratch_shapes=[pltpu.VMEM((tm, d), _BF16),
                        pltpu.VMEM((2, rc, d), _F32),
                        pltpu.SemaphoreType.DMA((2,))],
        compiler_params=_params(2),
        name="inproj",
    )(x2d, g, w)


def _scan_pitch(tm):
    p = -(-tm // SUBLANES)
    while p % SUBLANES != 4:
        p += 1
    return p


def _mixer_kernel(z_ref, cw_ref, cb_ref, wa_ref, ba_ref, wx_ref, bx_ref, lam_ref, wp_ref, ps_ref,
                  ya_ref, yb_ref,
                  er_ref, ep_ref, a_ref, b_ref, h_ref, car_ref, *, tm, pitch):
    s = pl.program_id(1)
    c = cb_ref.shape[1]
    n_slab = c // LANES
    hist_r = SUBLANES
    hist_p = 2 * SUBLANES

    @pl.when(s == 0)
    def _():
        er_ref[0:hist_r, :] = jnp.zeros((hist_r, c), _F32)
        ep_ref[0:hist_p, :] = jnp.zeros((hist_p, c), _F32)
        car_ref[...] = jnp.zeros(car_ref.shape, _F32)
        a_ref[:, tm:, :] = jnp.ones((n_slab, SUBLANES * pitch - tm, LANES), _F32)
        b_ref[:, tm:, :] = jnp.zeros((n_slab, SUBLANES * pitch - tm, LANES), _F32)

    @pl.when(s > 0)
    def _():
        er_ref[0:hist_r, :] = er_ref[tm:tm + hist_r, :]
        ep_ref[0:hist_p, :] = ep_ref[tm:tm + hist_p, :]

    er_ref[hist_r:hist_r + tm, :] = z_ref[:, 0:c]
    ep_ref[hist_p:hist_p + tm, :] = z_ref[:, 2 * c:3 * c]

    kvec = -LRU_C * jax.nn.softplus(-lam_ref[...])
    nblk = c // MXU_DIM
    for k in range(nblk):
        cs = slice(k * MXU_DIM, (k + 1) * MXU_DIM)
        xc = cb_ref[:, cs] + cw_ref[CONV_WIDTH - 1:CONV_WIDTH, cs] * er_ref[hist_r:hist_r + tm, cs]
        for j in range(1, CONV_WIDTH):
            xc = xc + cw_ref[CONV_WIDTH - 1 - j:CONV_WIDTH - j, cs] * er_ref[hist_r - j:hist_r - j + tm, cs]
        xcb = xc.astype(_BF16)
        r = jax.nn.sigmoid(jnp.dot(xcb, wa_ref[k], preferred_element_type=_F32) + ba_ref[:, cs])
        ig = jax.nn.sigmoid(jnp.dot(xcb, wx_ref[k], preferred_element_type=_F32) + bx_ref[:, cs])
        log_a = r * kvec[:, cs]
        a = jnp.exp(log_a)
        mult = jnp.sqrt(1.0 - a * a)
        bb = mult * ig * xc
        for q in range(MXU_DIM // LANES):
            slab = k * (MXU_DIM // LANES) + q
            a_ref[slab, 0:tm, :] = a[:, q * LANES:(q + 1) * LANES]
            b_ref[slab, 0:tm, :] = bb[:, q * LANES:(q + 1) * LANES]

    def sweep1(i, carry):
        hs, ps = carry
        nh, npd = [], []
        for sl in range(n_slab):
            av = a_ref[sl, pl.ds(i, SUBLANES, stride=pitch), :]
            bv = b_ref[sl, pl.ds(i, SUBLANES, stride=pitch), :]
            nh.append(av * hs[sl] + bv)
            npd.append(av * ps[sl])
        return tuple(nh), tuple(npd)

    zero = jnp.zeros((SUBLANES, LANES), _F32)
    one = jnp.ones((SUBLANES, LANES), _F32)
    h_end, a_end = lax.fori_loop(0, pitch, sweep1, ((zero,) * n_slab, (one,) * n_slab))

    row = lax.broadcasted_iota(jnp.int32, (SUBLANES, LANES), 0)
    h0 = []
    for sl in range(n_slab):
        hh, aa = h_end[sl], a_end[sl]
        d = 1
        while d < SUBLANES:
            hs_ = jnp.where(row >= d, pltpu.roll(hh, d, 0), 0.0)
            as_ = jnp.where(row >= d, pltpu.roll(aa, d, 0), 1.0)
            hh = aa * hs_ + hh
            aa = aa * as_
            d *= 2
        cin = car_ref[:, sl * LANES:(sl + 1) * LANES]
        full = hh + aa * cin
        h0.append(jnp.where(row >= 1, pltpu.roll(full, 1, 0), cin))
        car_ref[:, sl * LANES:(sl + 1) * LANES] = jnp.broadcast_to(full[SUBLANES - 1:SUBLANES, :],
                                                                  (SUBLANES, LANES))

    def sweep2(i, hs):
        nh = []
        for sl in range(n_slab):
            av = a_ref[sl, pl.ds(i, SUBLANES, stride=pitch), :]
            bv = b_ref[sl, pl.ds(i, SUBLANES, stride=pitch), :]
            hv = av * hs[sl] + bv
            h_ref[sl, pl.ds(i, SUBLANES, stride=pitch), :] = hv
            nh.append(hv)
        return tuple(nh)

    lax.fori_loop(0, pitch, sweep2, tuple(h0))

    for sl in range(n_slab):
        cs = slice(sl * LANES, (sl + 1) * LANES)
        g = z_ref[:, c + sl * LANES:c + (sl + 1) * LANES]
        ya_ref[:, cs] = (h_ref[sl, 0:tm, :] * jax.nn.gelu(g)).astype(_BF16)

    t_idx = (s * tm + lax.broadcasted_iota(jnp.int32, (tm, 1), 0) + 1).astype(_F32)
    n_grp = len(POOL_WINDOWS)
    gd = c // n_grp
    for gi, w in enumerate(POOL_WINDOWS):
        cs = slice(gi * gd, (gi + 1) * gd)
        e = ep_ref[:, cs]
        acc = e
        d = 1
        while d < w:
            acc = acc + pltpu.roll(acc, d, 0)
            d *= 2
        xt = e[hist_p:, :]
        cnt = jnp.minimum(t_idx, float(w))
        dd = acc[hist_p:, :] / cnt - xt
        yb = jnp.dot(dd.astype(_BF16), wp_ref[gi], preferred_element_type=_F32) * ps_ref[:, cs]
        yb_ref[:, cs] = yb.astype(_BF16)


def _mixer(z, bsz, seq, cw, cb, wa_bd, ba, wx_bd, bx, lam, wp, ps, tm):
    t = z.shape[0]
    c = cb.shape[1]
    ns = seq // tm
    pitch = _scan_pitch(tm)
    n_slab = c // LANES
    kern = functools.partial(_mixer_kernel, tm=tm, pitch=pitch)
    return pl.pallas_call(
        kern,
        grid=(bsz, ns),
        in_specs=[pl.BlockSpec((tm, 3 * c), lambda b, s: (b * ns + s, 0)),
                  _const_spec(cw.shape), _const_spec(cb.shape),
                  _const_spec(wa_bd.shape), _const_spec(ba.shape),
                  _const_spec(wx_bd.shape), _const_spec(bx.shape),
                  _const_spec(lam.shape), _const_spec(wp.shape), _const_spec(ps.shape)],
        out_specs=[pl.BlockSpec((tm, c), lambda b, s: (b * ns + s, 0)),
                   pl.BlockSpec((tm, c), lambda b, s: (b * ns + s, 0))],
        out_shape=[jax.ShapeDtypeStruct((t, c), _BF16), jax.ShapeDtypeStruct((t, c), _BF16)],
        scratch_shapes=[pltpu.VMEM((SUBLANES + tm, c), _F32),
                        pltpu.VMEM((2 * SUBLANES + tm, c), _F32),
                        pltpu.VMEM((n_slab, SUBLANES * pitch, LANES), _F32),
                        pltpu.VMEM((n_slab, SUBLANES * pitch, LANES), _F32),
                        pltpu.VMEM((n_slab, SUBLANES * pitch, LANES), _F32),
                        pltpu.VMEM((SUBLANES, c), _F32)],
        compiler_params=_params(2),
        name="mixer",
    )(z, cw, cb, wa_bd, ba, wx_bd, bx, lam, wp, ps)


def _combine_kernel(ya_ref, yb_ref, ga0_ref, ga1_ref, gb0_ref, gb1_ref, x_ref,
                    wa_ref, wb_ref, wo_ref, n2_ref, wr_ref, br_ref,
                    xe_ref, cnt_ref, cnt_s, *, tm, n_groups, epg):
    i = pl.program_id(0)
    d = x_ref.shape[1]
    half = d // 2

    @pl.when(i == 0)
    def _():
        cnt_s[...] = jnp.zeros(cnt_s.shape, _F32)

    pa = jnp.dot(ya_ref[...], wa_ref[...], preferred_element_type=_F32)
    pb = jnp.dot(yb_ref[...], wb_ref[...], preferred_element_type=_F32)
    u0 = jax.nn.sigmoid(ga0_ref[...]) * pa[:, :half] + jax.nn.sigmoid(gb0_ref[...]) * pb[:, :half]
    u1 = jax.nn.sigmoid(ga1_ref[...]) * pa[:, half:] + jax.nn.sigmoid(gb1_ref[...]) * pb[:, half:]
    u = jnp.concatenate([u0, u1], axis=1).astype(_BF16)
    x1 = x_ref[...] + jnp.dot(u, wo_ref[...], preferred_element_type=_F32)
    xe_ref[:, 0:d] = x1

    ht = _rms(x1, n2_ref[...]).astype(_BF16)
    logits = jnp.dot(ht, wr_ref[...], preferred_element_type=_F32) + br_ref[...]

    lane = lax.broadcasted_iota(jnp.int32, (tm, LANES), 1).astype(_F32)
    ninf = -jnp.inf
    big = float(LANES)

    def first_argmax(v):
        m = jnp.max(v, axis=-1, keepdims=True)
        return m, jnp.min(jnp.where(v == m, lane, big), axis=-1, keepdims=True)

    is_g = lane < float(n_groups)
    gmax, gidx = first_argmax(jnp.where(is_g, logits, ninf))
    g_w = 1.0 / jnp.sum(jnp.where(is_g, jnp.exp(logits - gmax), 0.0), axis=-1, keepdims=True)
    lo_lane = float(n_groups) + float(epg) * gidx
    in_grp = (lane >= lo_lane) & (lane < lo_lane + float(epg))
    le = jnp.where(in_grp, logits, ninf)
    m1, i1 = first_argmax(le)
    m2, i2 = first_argmax(jnp.where(lane == i1, ninf, le))
    e21 = jnp.exp(m2 - m1)
    w1 = g_w / (1.0 + e21)
    w2 = w1 * e21
    e1 = i1 - lo_lane
    e2 = i2 - lo_lane
    lo = jnp.minimum(e1, e2)
    hi = jnp.maximum(e1, e2)
    w_lo = jnp.where(e1 < e2, w1, w2)
    w_hi = jnp.where(e1 < e2, w2, w1)
    pair = jnp.where(lo == 0.0, hi - 1.0, jnp.where(lo == 1.0, 6.0 - hi, 5.0))
    swap = pair == 5.0
    w_a = jnp.where(swap, w_hi, w_lo)
    w_b = jnp.where(swap, w_lo, w_hi)
    bucket = float(_N_PAIRS) * gidx + pair

    onehot = lane == bucket
    oh_bf = jnp.where(onehot, 1.0, 0.0).astype(_BF16)
    rr = lax.broadcasted_iota(jnp.int32, (tm, tm), 0)
    cc = lax.broadcasted_iota(jnp.int32, (tm, tm), 1)
    tri = jnp.where(cc < rr, 1.0, 0.0).astype(_BF16)
    before = jnp.dot(tri, oh_bf, preferred_element_type=_F32) + cnt_s[...]
    rank = jnp.sum(jnp.where(onehot, before, 0.0), axis=-1, keepdims=True)
    cnt_s[...] = cnt_s[...] + jnp.sum(jnp.where(onehot, 1.0, 0.0), axis=0, keepdims=True)
    cnt_ref[...] = cnt_s[...]

    info = jnp.where(lane == 0.0, bucket,
                     jnp.where(lane == 1.0, rank,
                               jnp.where(lane == 2.0, w_a, jnp.where(lane == 3.0, w_b, 0.0))))
    xe_ref[:, d:d + LANES] = info


def _combine(ya, yb, z, x2d, wa, wb, wo, n2, wr, br, tm, n_groups, epg):
    t, d = x2d.shape
    c = ya.shape[1]
    half = d // 2
    off = (3 * c) // half
    kern = functools.partial(_combine_kernel, tm=tm, n_groups=n_groups, epg=epg)

    def zspec(j):
        return pl.BlockSpec((tm, half), lambda i, j=j: (i, off + j))

    return pl.pallas_call(
        kern,
        grid=(t // tm,),
        in_specs=[pl.BlockSpec((tm, c), lambda i: (i, 0)),
                  pl.BlockSpec((tm, c), lambda i: (i, 0)),
                  zspec(0), zspec(1), zspec(2), zspec(3),
                  pl.BlockSpec((tm, d), lambda i: (i, 0)),
                  _const_spec(wa.shape), _const_spec(wb.shape), _const_spec(wo.shape),
                  _const_spec(n2.shape), _const_spec(wr.shape), _const_spec(br.shape)],
        out_specs=[pl.BlockSpec((tm, d + LANES), lambda i: (i, 0)),
                   pl.BlockSpec((1, LANES), lambda i: (0, 0))],
        out_shape=[jax.ShapeDtypeStruct((t, d + LANES), _F32),
                   jax.ShapeDtypeStruct((1, LANES), _F32)],
        scratch_shapes=[pltpu.VMEM((1, LANES), _F32)],
        compiler_params=_params(1),
        name="combine",
    )(ya, yb, z, z, z, z, x2d, wa, wb, wo, n2, wr, br)


def _row_gather_start(idx_ref, base, src_hbm, buf, slot, sem, n_rows, unrolled, width=None):
    width = src_hbm.shape[1] if width is None else width

    def start(r):
        row = idx_ref[base + r]
        pltpu.make_async_copy(src_hbm.at[pl.ds(row, 1), pl.ds(0, width)],
                              buf.at[slot, pl.ds(r, 1), pl.ds(0, width)],
                              sem.at[slot]).start(priority=1)

    if unrolled:
        for r in range(n_rows):
            start(r)
    else:
        def body(r, _):
            start(r)
            return 0
        lax.fori_loop(0, n_rows, body, 0, unroll=8)


def _row_gather_wait(src_hbm, buf, slot, sem, n_rows, width=None):
    width = src_hbm.shape[1] if width is None else width
    pltpu.make_async_copy(src_hbm.at[pl.ds(0, n_rows), pl.ds(0, width)],
                          buf.at[slot, :, pl.ds(0, width)], sem.at[slot]).wait()


MOE_WEIGHT_SLOTS = 6
MOE_PARTS = 4
MOE_CHUNKS = 3 * MOE_PARTS
MOE_PUMP = 3
MOE_DEPTH = 4


def _moe_weight_plan(n_groups, epg):
    first_use = {e: min(p for p in range(_N_PAIRS) if e in (_PAIR_SLOT_A[p], _PAIR_SLOT_B[p])) for e in range(epg)}
    last_use = {e: max(p for p in range(_N_PAIRS) if e in (_PAIR_SLOT_A[p], _PAIR_SLOT_B[p])) for e in range(epg)}
    free_after = [-1] * MOE_WEIGHT_SLOTS
    loads, slot_of = [], {}
    for g in range(n_groups):
        for e in sorted(range(epg), key=lambda e: (first_use[e], e)):
            needed_by = g * _N_PAIRS + first_use[e]
            s = min(range(MOE_WEIGHT_SLOTS), key=lambda s: (free_after[s], s))
            assert free_after[s] < needed_by
            loads.append((g * epg + e, s, free_after[s], needed_by))
            free_after[s] = g * _N_PAIRS + last_use[e]
            slot_of[(g, e)] = s
    n_buckets = n_groups * _N_PAIRS
    need = [sum(1 for l in loads if l[3] <= b) for b in range(n_buckets)]
    allow = [sum(1 for l in loads if l[2] < b) for b in range(n_buckets)]
    slot_a = [slot_of[(b // _N_PAIRS, _PAIR_SLOT_A[b % _N_PAIRS])] for b in range(n_buckets)]
    slot_b = [slot_of[(b // _N_PAIRS, _PAIR_SLOT_B[b % _N_PAIRS])] for b in range(n_buckets)]
    return [l[0] for l in loads], [l[1] for l in loads], need, allow, slot_a, slot_b


def _moe_kernel(src_ref, bkt_ref, valid_ref, need_ref, allow_ref, sa_ref, sb_ref, le_ref, ls_ref,
                xe_hbm, n2_ref, wg_hbm, wu_hbm, wd_hbm,
                x2s_ref, xbuf, gsem, wg_s, wu_s, wd_s, st_gu, st_d, wsem, cnt, *, tm):
    j = pl.program_id(0)
    slot = j % 2
    d = x2s_ref.shape[1]
    rows_gu = wg_hbm.shape[1] // MOE_PARTS
    rows_d = wd_hbm.shape[1] // MOE_PARTS

    def chunk_dma(c, kind):
        load = c // MOE_CHUNKS
        part = c % MOE_PARTS
        e = le_ref[load]
        if kind == 2:
            return pltpu.make_async_copy(wd_hbm.at[e, pl.ds(part * rows_d, rows_d)], st_d.at[c % MOE_DEPTH],
                                         wsem.at[c % MOE_DEPTH])
        src = wg_hbm if kind == 0 else wu_hbm
        return pltpu.make_async_copy(src.at[e, pl.ds(part * rows_gu, rows_gu)], st_gu.at[c % MOE_DEPTH],
                                     wsem.at[c % MOE_DEPTH])

    def for_kind(c, fn):
        kind = (c % MOE_CHUNKS) // MOE_PARTS
        for k in range(3):
            @pl.when(kind == k)
            def _(k=k):
                fn(k)

    def issue_one(allowed):
        @pl.when(cnt[0] < jnp.minimum(allowed, cnt[1] + MOE_DEPTH))
        def _():
            c = cnt[0]
            for_kind(c, lambda k: chunk_dma(c, k).start())
            cnt[0] = c + 1

    def retire():
        c = cnt[1]
        s = ls_ref[c // MOE_CHUNKS]
        part = c % MOE_PARTS

        def finish(k):
            chunk_dma(c, k).wait()
            if k == 2:
                wd_s[s, pl.ds(part * rows_d, rows_d), :] = st_d[c % MOE_DEPTH].astype(_BF16)
            else:
                dst = wg_s if k == 0 else wu_s
                dst[s, pl.ds(part * rows_gu, rows_gu), :] = st_gu[c % MOE_DEPTH].astype(_BF16)

        for_kind(c, finish)
        cnt[1] = c + 1

    def pump(required, allowed, extra):
        n_iter = jnp.maximum(required - cnt[1], jnp.minimum(extra, allowed - cnt[1]))
        lax.fori_loop(0, MOE_DEPTH, lambda _, carry: (issue_one(allowed), carry)[1], 0)

        def body(_, carry):
            retire()
            issue_one(allowed)
            return carry

        lax.fori_loop(0, jnp.maximum(n_iter, 0), body, 0)

    @pl.when(j == 0)
    def _():
        cnt[0] = 0
        cnt[1] = 0
        _row_gather_start(src_ref, 0, xe_hbm, xbuf, 0, gsem, tm, unrolled=False)

    @pl.when(valid_ref[j] == 1)
    def _():
        b = bkt_ref[j]
        pump(need_ref[b] * MOE_CHUNKS, allow_ref[b] * MOE_CHUNKS, MOE_PUMP)
        s_a = sa_ref[b]
        s_b = sb_ref[b]
        _row_gather_wait(xe_hbm, xbuf, slot, gsem, tm)
        _row_gather_start(src_ref, (j + 1) * tm, xe_hbm, xbuf, 1 - slot, gsem, tm, unrolled=True)
        xe = xbuf[slot]
        x1 = xe[:, 0:d]
        w_a = xe[:, d + 2:d + 3]
        w_b = xe[:, d + 3:d + 4]
        ht = _rms(x1, n2_ref[...]).astype(_BF16)

        def expert(s, wgt):
            hg = jnp.dot(ht, wg_s[s], preferred_element_type=_F32)
            hu = jnp.dot(ht, wu_s[s], preferred_element_type=_F32)
            return (hg * jax.nn.sigmoid(hg) * hu * wgt).astype(_BF16)

        y = jnp.dot(expert(s_a, w_a), wd_s[s_a], preferred_element_type=_F32)
        y = y + jnp.dot(expert(s_b, w_b), wd_s[s_b], preferred_element_type=_F32)
        x2s_ref[...] = x1 + y

    @pl.when(valid_ref[j] == 0)
    def _():
        x2s_ref[...] = jnp.zeros(x2s_ref.shape, _F32)

        @pl.when(valid_ref[jnp.maximum(j - 1, 0)] == 1)
        def _():
            _row_gather_wait(xe_hbm, xbuf, slot, gsem, tm)
            lax.fori_loop(0, cnt[0] - cnt[1], lambda _, carry: (retire(), carry)[1], 0)


def _moe(src, bkt, valid, xe, n2, wg, wu, wd, tm, n_groups, epg):
    n_tiles = valid.shape[0]
    d = wg.shape[1]
    f = wg.shape[2]
    kern = functools.partial(_moe_kernel, tm=tm)
    le, ls, need, allow, slot_a, slot_b = (jnp.asarray(v, jnp.int32) for v in _moe_weight_plan(n_groups, epg))

    grid_spec = pltpu.PrefetchScalarGridSpec(
        num_scalar_prefetch=9,
        grid=(n_tiles,),
        in_specs=[pl.BlockSpec(memory_space=pl.ANY),
                  pl.BlockSpec((1, d), lambda j, *_: (0, 0)),
                  pl.BlockSpec(memory_space=pl.ANY),
                  pl.BlockSpec(memory_space=pl.ANY),
                  pl.BlockSpec(memory_space=pl.ANY)],
        out_specs=pl.BlockSpec((tm, d), lambda j, *_: (j, 0)),
        scratch_shapes=[pltpu.VMEM((2, tm, d + LANES), _F32),
                        pltpu.SemaphoreType.DMA((2,)),
                        pltpu.VMEM((MOE_WEIGHT_SLOTS, d, f), _BF16),
                        pltpu.VMEM((MOE_WEIGHT_SLOTS, d, f), _BF16),
                        pltpu.VMEM((MOE_WEIGHT_SLOTS, f, d), _BF16),
                        pltpu.VMEM((MOE_DEPTH, d // MOE_PARTS, f), _F32),
                        pltpu.VMEM((MOE_DEPTH, f // MOE_PARTS, d), _F32),
                        pltpu.SemaphoreType.DMA((MOE_DEPTH,)),
                        pltpu.SMEM((2,), jnp.int32)],
    )
    return pl.pallas_call(
        kern,
        grid_spec=grid_spec,
        out_shape=jax.ShapeDtypeStruct((n_tiles * tm, d), _F32),
        compiler_params=_params(1),
        name="moe",
    )(src, bkt, valid, need, allow, slot_a, slot_b, le, ls, xe, n2, wg, wu, wd)


def _ple_kernel(pos_ref, x2s_hbm, p_ref, ng_ref, wg_ref, wp_ref, nf_ref, o_ref, xbuf, gsem, *, tm, n_steps):
    i = pl.program_id(0)
    slot = i % 2

    @pl.when(i == 0)
    def _():
        _row_gather_start(pos_ref, 0, x2s_hbm, xbuf, 0, gsem, tm, unrolled=False)

    _row_gather_wait(x2s_hbm, xbuf, slot, gsem, tm)
    nxt = jnp.where(i + 1 == n_steps, 0, i + 1)
    _row_gather_start(pos_ref, nxt * tm, x2s_hbm, xbuf, 1 - slot, gsem, tm, unrolled=True)
    x2 = xbuf[slot]
    g = jax.nn.sigmoid(jnp.dot(_rms(x2, ng_ref[...]).astype(_BF16), wg_ref[...],
                               preferred_element_type=_F32))
    e = jnp.dot(p_ref[...].astype(_BF16), wp_ref[...], preferred_element_type=_F32)
    o_ref[...] = _rms(x2 + g * e, nf_ref[...])

    @pl.when(i == n_steps - 1)
    def _():
        _row_gather_wait(x2s_hbm, xbuf, 1 - slot, gsem, tm)


def _ple(pos, x2s, p2d, ng, wg, wp, nf, tm, t):
    d = x2s.shape[1]
    pd = p2d.shape[1]
    n_steps = t // tm
    kern = functools.partial(_ple_kernel, tm=tm, n_steps=n_steps)

    def cspec(shape):
        nd = len(shape)
        return pl.BlockSpec(shape, lambda i, *_: (0,) * nd, pipeline_mode=pl.Buffered(1))

    grid_spec = pltpu.PrefetchScalarGridSpec(
        num_scalar_prefetch=1,
        grid=(n_steps,),
        in_specs=[pl.BlockSpec(memory_space=pl.ANY),
                  pl.BlockSpec((tm, pd), lambda i, *_: (i, 0)),
                  cspec(ng.shape), cspec(wg.shape), cspec(wp.shape), cspec(nf.shape)],
        out_specs=pl.BlockSpec((tm, d), lambda i, *_: (i, 0)),
        scratch_shapes=[pltpu.VMEM((2, tm, d), _F32),
                        pltpu.SemaphoreType.DMA((2,))],
    )
    return pl.pallas_call(
        kern,
        grid_spec=grid_spec,
        out_shape=jax.ShapeDtypeStruct((t, d), _F32),
        compiler_params=_params(1),
        name="ple",
    )(pos, x2s, p2d, ng, wg, wp, nf)


def _block_diag(w, per_block):
    h, hd, _ = w.shape
    nb = h // per_block
    w4 = w.reshape(nb, per_block, hd, hd)
    rows = [jnp.pad(w4[:, p], ((0, 0), (0, 0), (p * hd, (per_block - 1 - p) * hd))) for p in range(per_block)]
    return jnp.concatenate(rows, axis=1)


def _layer(x2d, p2d, bsz, seq, norm1_g, w_in, conv_w, conv_b, w_rg_a, b_rg_a, w_rg_x, b_rg_x, lru_lambda,
           w_pool, pool_scale, w_branch_a, w_branch_b, w_out, norm2_g, w_router_group, b_router_group,
           w_router_expert, b_router_expert, w_e_gate, w_e_up, w_e_down, norm_ple_g, w_ple_gate,
           w_ple_proj, out_norm_g):
    t, d = x2d.shape
    c = conv_b.shape[0]
    heads, hd, _ = w_rg_a.shape
    n_groups = w_router_group.shape[1]
    n_exp = w_router_expert.shape[1]
    epg = n_exp // n_groups
    assert epg == 4 and TOP_K == 2 and hd * (MXU_DIM // hd) == MXU_DIM
    assert w_pool.shape[0] == len(POOL_WINDOWS) and w_pool.shape[1] == MXU_DIM

    tm_in, tn_in, rc_in = min(2048, t), 512, min(256, t)
    tm_mix = min(256, seq)
    tm_cmb = min(256, t)
    tm_moe = min(128, t)
    tm_ple = min(256, t)

    row = lambda v: v.reshape(1, -1).astype(_F32)
    per_block = MXU_DIM // hd

    z = _inproj(x2d, row(norm1_g), w_in, tm_in, tn_in, rc_in)
    ya, yb = _mixer(z, bsz, seq, conv_w.reshape(CONV_WIDTH, c), row(conv_b),
                    _block_diag(w_rg_a, per_block).astype(_BF16), row(b_rg_a),
                    _block_diag(w_rg_x, per_block).astype(_BF16), row(b_rg_x),
                    row(lru_lambda), w_pool.astype(_BF16), row(pool_scale), tm_mix)

    n_rt = n_groups + n_exp
    wr = jnp.pad(jnp.concatenate([w_router_group, w_router_expert], axis=1),
                 ((0, 0), (0, LANES - n_rt))).astype(_BF16)
    br = jnp.pad(jnp.concatenate([b_router_group, b_router_expert]), (0, LANES - n_rt)).reshape(1, LANES)
    xe, counts = _combine(ya, yb, z, x2d, w_branch_a.astype(_BF16), w_branch_b.astype(_BF16),
                          w_out.astype(_BF16), row(norm2_g), wr, br, tm_cmb, n_groups, epg)

    n_buckets = n_groups * _N_PAIRS
    n_tiles = t // tm_moe + n_buckets
    cnt = counts[0, :n_buckets].astype(jnp.int32)
    nt = (cnt + tm_moe - 1) // tm_moe
    cum = jnp.cumsum(nt)
    total = cum[-1]
    bucket = xe[:, d].astype(jnp.int32)
    rank = xe[:, d + 1].astype(jnp.int32)
    pos = (cum - nt)[bucket] * tm_moe + rank
    tok = jnp.arange(t, dtype=jnp.int32)
    src = (jnp.arange(n_tiles * tm_moe, dtype=jnp.int32) % t).at[pos].set(tok)
    tile = jnp.minimum(jnp.arange(n_tiles, dtype=jnp.int32), total - 1)
    tb = jnp.sum((cum[None, :] <= tile[:, None]).astype(jnp.int32), axis=1)
    valid = (jnp.arange(n_tiles, dtype=jnp.int32) < total).astype(jnp.int32)

    x2s = _moe(src, tb, valid, xe, row(norm2_g), w_e_gate, w_e_up, w_e_down, tm_moe, n_groups, epg)
    return _ple(pos, x2s, p2d, row(norm_ple_g), w_ple_gate.astype(_BF16), w_ple_proj.astype(_BF16),
                row(out_norm_g), tm_ple, t)


def kernel(x, p, norm1_g, w_in, conv_w, conv_b, w_rg_a, b_rg_a, w_rg_x, b_rg_x, lru_lambda, w_pool, pool_scale, w_branch_a, w_branch_b, w_out, norm2_g, w_router_group, b_router_group, w_router_expert, b_router_expert, w_e_gate, w_e_up, w_e_down, norm_ple_g, w_ple_gate, w_ple_proj, final_norm_g):
    bsz, seq, d = x.shape
    depth = p.shape[0]
    assert depth == 1, "the final RMSNorm is fused into the last layer's embedding kernel"
    out = _layer(x.reshape(bsz * seq, d), p[0].reshape(bsz * seq, -1), bsz, seq,
                 norm1_g[0], w_in[0], conv_w[0], conv_b[0], w_rg_a[0], b_rg_a[0], w_rg_x[0], b_rg_x[0],
                 lru_lambda[0], w_pool[0], pool_scale[0], w_branch_a[0], w_branch_b[0], w_out[0],
                 norm2_g[0], w_router_group[0], b_router_group[0], w_router_expert[0],
                 b_router_expert[0], w_e_gate[0], w_e_up[0], w_e_down[0], norm_ple_g[0],
                 w_ple_gate[0], w_ple_proj[0], final_norm_g)
    return out.reshape(bsz, seq, d)
```

```python
import functools

import jax
import jax.numpy as jnp
from jax import lax
from jax.experimental import pallas as pl
from jax.experimental.pallas import tpu as pltpu

EPS = 1e-6
LRU_C = 8.0
CONV_WIDTH = 4
POOL_WINDOWS = (2, 4, 8, 16)
TOP_K = 2

LANES = 128
SUBLANES = 8
MXU_DIM = 256
VMEM_LIMIT_BYTES = 56 * 1024 * 1024

_BF16 = jnp.bfloat16
_F32 = jnp.float32

_PAIR_SLOT_A = (0, 0, 0, 1, 1, 3)
_PAIR_SLOT_B = (1, 2, 3, 3, 2, 2)
_N_PAIRS = len(_PAIR_SLOT_A)


def _rms(x, g):
    ms = jnp.mean(x * x, axis=-1, keepdims=True)
    return x * lax.rsqrt(ms + EPS) * g


def _const_spec(shape):
    nd = len(shape)
    return pl.BlockSpec(shape, lambda *_: (0,) * nd, pipeline_mode=pl.Buffered(1))


def _params(n_axes):
    return pltpu.CompilerParams(dimension_semantics=("arbitrary",) * n_axes,
                                vmem_limit_bytes=VMEM_LIMIT_BYTES)


def _inproj_kernel(x_hbm, g_ref, w_ref, z_ref, h_ref, xs_ref, sem, *, tm, rc):
    i = pl.program_id(0)
    n_chunks = tm // rc

    def chunk_copy(c):
        return pltpu.make_async_copy(x_hbm.at[pl.ds(i * tm + c * rc, rc)], xs_ref.at[c % 2], sem.at[c % 2])

    @pl.when(pl.program_id(1) == 0)
    def _():
        chunk_copy(0).start()
        for c in range(n_chunks):
            if c + 1 < n_chunks:
                chunk_copy(c + 1).start()
            chunk_copy(c).wait()
            h_ref[c * rc:(c + 1) * rc, :] = _rms(xs_ref[c % 2], g_ref[...]).astype(_BF16)

    z_ref[...] = jnp.dot(h_ref[...], w_ref[...].astype(_BF16), preferred_element_type=_F32)


def _inproj(x2d, g, w, tm, tn, rc):
    t, d = x2d.shape
    n = w.shape[1]
    kern = functools.partial(_inproj_kernel, tm=tm, rc=rc)
    return pl.pallas_call(
        kern,
        grid=(t // tm, n // tn),
        in_specs=[pl.BlockSpec(memory_space=pl.ANY),
                  pl.BlockSpec((1, d), lambda i, j: (0, 0)),
                  pl.BlockSpec((d, tn), lambda i, j: (0, j))],
        out_specs=pl.BlockSpec((tm, tn), lambda i, j: (i, j)),
        out_shape=jax.ShapeDtypeStruct((t, n), _F32),
        scratch_shapes=[pltpu.VMEM((tm, d), _BF16),
                        pltpu.VMEM((2, rc, d), _F32),
                        pltpu.SemaphoreType.DMA((2,))],
        compiler_params=_params(2),
        name="inproj",
    )(x2d, g, w)


def _scan_pitch(tm):
    p = -(-tm // SUBLANES)
    while p % SUBLANES != 4:
        p += 1
    return p


def _mixer_kernel(z_ref, cw_ref, cb_ref, wa_ref, ba_ref, wx_ref, bx_ref, lam_ref, wp_ref, ps_ref,
                  ya_ref, yb_ref,
                  er_ref, ep_ref, a_ref, b_ref, h_ref, car_ref, *, tm, pitch):
    s = pl.program_id(1)
    c = cb_ref.shape[1]
    n_slab = c // LANES
    hist_r = SUBLANES
    hist_p = 2 * SUBLANES

    @pl.when(s == 0)
    def _():
        er_ref[0:hist_r, :] = jnp.zeros((hist_r, c), _F32)
        ep_ref[0:hist_p, :] = jnp.zeros((hist_p, c), _F32)
        car_ref[...] = jnp.zeros(car_ref.shape, _F32)
        a_ref[:, tm:, :] = jnp.ones((n_slab, SUBLANES * pitch - tm, LANES), _F32)
        b_ref[:, tm:, :] = jnp.zeros((n_slab, SUBLANES * pitch - tm, LANES), _F32)

    @pl.when(s > 0)
    def _():
        er_ref[0:hist_r, :] = er_ref[tm:tm + hist_r, :]
        ep_ref[0:hist_p, :] = ep_ref[tm:tm + hist_p, :]

    er_ref[hist_r:hist_r + tm, :] = z_ref[:, 0:c]
    ep_ref[hist_p:hist_p + tm, :] = z_ref[:, 2 * c:3 * c]

    kvec = -LRU_C * jax.nn.softplus(-lam_ref[...])
    nblk = c // MXU_DIM
    for k in range(nblk):
        cs = slice(k * MXU_DIM, (k + 1) * MXU_DIM)
        xc = cb_ref[:, cs] + cw_ref[CONV_WIDTH - 1:CONV_WIDTH, cs] * er_ref[hist_r:hist_r + tm, cs]
        for j in range(1, CONV_WIDTH):
            xc = xc + cw_ref[CONV_WIDTH - 1 - j:CONV_WIDTH - j, cs] * er_ref[hist_r - j:hist_r - j + tm, cs]
        xcb = xc.astype(_BF16)
        r = jax.nn.sigmoid(jnp.dot(xcb, wa_ref[k], preferred_element_type=_F32) + ba_ref[:, cs])
        ig = jax.nn.sigmoid(jnp.dot(xcb, wx_ref[k], preferred_element_type=_F32) + bx_ref[:, cs])
        log_a = r * kvec[:, cs]
        a = jnp.exp(log_a)
        mult = jnp.sqrt(1.0 - a * a)
        bb = mult * ig * xc
        for q in range(MXU_DIM // LANES):
            slab = k * (MXU_DIM // LANES) + q
            a_ref[slab, 0:tm, :] = a[:, q * LANES:(q + 1) * LANES]
            b_ref[slab, 0:tm, :] = bb[:, q * LANES:(q + 1) * LANES]

    def sweep1(i, carry):
        hs, ps = carry
        nh, npd = [], []
        for sl in range(n_slab):
            av = a_ref[sl, pl.ds(i, SUBLANES, stride=pitch), :]
            bv = b_ref[sl, pl.ds(i, SUBLANES, stride=pitch), :]
            nh.append(av * hs[sl] + bv)
            npd.append(av * ps[sl])
        return tuple(nh), tuple(npd)

    zero = jnp.zeros((SUBLANES, LANES), _F32)
    one = jnp.ones((SUBLANES, LANES), _F32)
    h_end, a_end = lax.fori_loop(0, pitch, sweep1, ((zero,) * n_slab, (one,) * n_slab))

    row = lax.broadcasted_iota(jnp.int32, (SUBLANES, LANES), 0)
    h0 = []
    for sl in range(n_slab):
        hh, aa = h_end[sl], a_end[sl]
        d = 1
        while d < SUBLANES:
            hs_ = jnp.where(row >= d, pltpu.roll(hh, d, 0), 0.0)
            as_ = jnp.where(row >= d, pltpu.roll(aa, d, 0), 1.0)
            hh = aa * hs_ + hh
            aa = aa * as_
            d *= 2
        cin = car_ref[:, sl * LANES:(sl + 1) * LANES]
        full = hh + aa * cin
        h0.append(jnp.where(row >= 1, pltpu.roll(full, 1, 0), cin))
        car_ref[:, sl * LANES:(sl + 1) * LANES] = jnp.broadcast_to(full[SUBLANES - 1:SUBLANES, :],
                                                                  (SUBLANES, LANES))

    def sweep2(i, hs):
        nh = []
        for sl in range(n_slab):
            av = a_ref[sl, pl.ds(i, SUBLANES, stride=pitch), :]
            bv = b_ref[sl, pl.ds(i, SUBLANES, stride=pitch), :]
            hv = av * hs[sl] + bv
            h_ref[sl, pl.ds(i, SUBLANES, stride=pitch), :] = hv
            nh.append(hv)
        return tuple(nh)

    lax.fori_loop(0, pitch, sweep2, tuple(h0))

    for sl in range(n_slab):
        cs = slice(sl * LANES, (sl + 1) * LANES)
        g = z_ref[:, c + sl * LANES:c + (sl + 1) * LANES]
        ya_ref[:, cs] = (h_ref[sl, 0:tm, :] * jax.nn.gelu(g)).astype(_BF16)

    t_idx = (s * tm + lax.broadcasted_iota(jnp.int32, (tm, 1), 0) + 1).astype(_F32)
    n_grp = len(POOL_WINDOWS)
    gd = c // n_grp
    for gi, w in enumerate(POOL_WINDOWS):
        cs = slice(gi * gd, (gi + 1) * gd)
        e = ep_ref[:, cs]
        acc = e
        d = 1
        while d < w:
            acc = acc + pltpu.roll(acc, d, 0)
            d *= 2
        xt = e[hist_p:, :]
        cnt = jnp.minimum(t_idx, float(w))
        dd = acc[hist_p:, :] / cnt - xt
        yb = jnp.dot(dd.astype(_BF16), wp_ref[gi], preferred_element_type=_F32) * ps_ref[:, cs]
        yb_ref[:, cs] = yb.astype(_BF16)


def _mixer(z, bsz, seq, cw, cb, wa_bd, ba, wx_bd, bx, lam, wp, ps, tm):
    t = z.shape[0]
    c = cb.shape[1]
    ns = seq // tm
    pitch = _scan_pitch(tm)
    n_slab = c // LANES
    kern = functools.partial(_mixer_kernel, tm=tm, pitch=pitch)
    return pl.pallas_call(
        kern,
        grid=(bsz, ns),
        in_specs=[pl.BlockSpec((tm, 3 * c), lambda b, s: (b * ns + s, 0)),
                  _const_spec(cw.shape), _const_spec(cb.shape),
                  _const_spec(wa_bd.shape), _const_spec(ba.shape),
                  _const_spec(wx_bd.shape), _const_spec(bx.shape),
                  _const_spec(lam.shape), _const_spec(wp.shape), _const_spec(ps.shape)],
        out_specs=[pl.BlockSpec((tm, c), lambda b, s: (b * ns + s, 0)),
                   pl.BlockSpec((tm, c), lambda b, s: (b * ns + s, 0))],
        out_shape=[jax.ShapeDtypeStruct((t, c), _BF16), jax.ShapeDtypeStruct((t, c), _BF16)],
        scratch_shapes=[pltpu.VMEM((SUBLANES + tm, c), _F32),
                        pltpu.VMEM((2 * SUBLANES + tm, c), _F32),
                        pltpu.VMEM((n_slab, SUBLANES * pitch, LANES), _F32),
                        pltpu.VMEM((n_slab, SUBLANES * pitch, LANES), _F32),
                        pltpu.VMEM((n_slab, SUBLANES * pitch, LANES), _F32),
                        pltpu.VMEM((SUBLANES, c), _F32)],
        compiler_params=_params(2),
        name="mixer",
    )(z, cw, cb, wa_bd, ba, wx_bd, bx, lam, wp, ps)


def _combine_kernel(ya_ref, yb_ref, ga0_ref, ga1_ref, gb0_ref, gb1_ref, x_ref,
                    wa_ref, wb_ref, wo_ref, n2_ref, wr_ref, br_ref,
                    xe_ref, cnt_ref, meta_ref, cnt_s, *, tm, n_groups, epg):
    i = pl.program_id(0)
    d = x_ref.shape[1]
    half = d // 2

    @pl.when(i == 0)
    def _():
        cnt_s[...] = jnp.zeros(cnt_s.shape, _F32)

    pa = jnp.dot(ya_ref[...], wa_ref[...], preferred_element_type=_F32)
    pb = jnp.dot(yb_ref[...], wb_ref[...], preferred_element_type=_F32)
    u0 = jax.nn.sigmoid(ga0_ref[...]) * pa[:, :half] + jax.nn.sigmoid(gb0_ref[...]) * pb[:, :half]
    u1 = jax.nn.sigmoid(ga1_ref[...]) * pa[:, half:] + jax.nn.sigmoid(gb1_ref[...]) * pb[:, half:]
    u = jnp.concatenate([u0, u1], axis=1).astype(_BF16)
    x1 = x_ref[...] + jnp.dot(u, wo_ref[...], preferred_element_type=_F32)
    xe_ref[:, 0:d] = x1

    ht = _rms(x1, n2_ref[...]).astype(_BF16)
    logits = jnp.dot(ht, wr_ref[...], preferred_element_type=_F32) + br_ref[...]

    lane = lax.broadcasted_iota(jnp.int32, (tm, LANES), 1).astype(_F32)
    ninf = -jnp.inf
    big = float(LANES)

    def first_argmax(v):
        m = jnp.max(v, axis=-1, keepdims=True)
        return m, jnp.min(jnp.where(v == m, lane, big), axis=-1, keepdims=True)

    is_g = lane < float(n_groups)
    gmax, gidx = first_argmax(jnp.where(is_g, logits, ninf))
    g_w = 1.0 / jnp.sum(jnp.where(is_g, jnp.exp(logits - gmax), 0.0), axis=-1, keepdims=True)
    lo_lane = float(n_groups) + float(epg) * gidx
    in_grp = (lane >= lo_lane) & (lane < lo_lane + float(epg))
    le = jnp.where(in_grp, logits, ninf)
    m1, i1 = first_argmax(le)
    m2, i2 = first_argmax(jnp.where(lane == i1, ninf, le))
    e21 = jnp.exp(m2 - m1)
    w1 = g_w / (1.0 + e21)
    w2 = w1 * e21
    e1 = i1 - lo_lane
    e2 = i2 - lo_lane
    lo = jnp.minimum(e1, e2)
    hi = jnp.maximum(e1, e2)
    w_lo = jnp.where(e1 < e2, w1, w2)
    w_hi = jnp.where(e1 < e2, w2, w1)
    pair = jnp.where(lo == 0.0, hi - 1.0, jnp.where(lo == 1.0, 6.0 - hi, 5.0))
    swap = pair == 5.0
    w_a = jnp.where(swap, w_hi, w_lo)
    w_b = jnp.where(swap, w_lo, w_hi)
    bucket = float(_N_PAIRS) * gidx + pair

    onehot = lane == bucket
    oh_bf = jnp.where(onehot, 1.0, 0.0).astype(_BF16)
    rr = lax.broadcasted_iota(jnp.int32, (tm, tm), 0)
    cc = lax.broadcasted_iota(jnp.int32, (tm, tm), 1)
    tri = jnp.where(cc < rr, 1.0, 0.0).astype(_BF16)
    before = jnp.dot(tri, oh_bf, preferred_element_type=_F32) + cnt_s[...]
    rank = jnp.sum(jnp.where(onehot, before, 0.0), axis=-1, keepdims=True)
    cnt_s[...] = cnt_s[...] + jnp.sum(jnp.where(onehot, 1.0, 0.0), axis=0, keepdims=True)
    cnt_ref[...] = cnt_s[...]

    info = jnp.where(lane == 0.0, bucket,
                     jnp.where(lane == 1.0, rank,
                               jnp.where(lane == 2.0, w_a, jnp.where(lane == 3.0, w_b, 0.0))))
    xe_ref[:, d:d + LANES] = info
    meta_ref[...] = jnp.transpose(info)[0:SUBLANES, :].astype(jnp.int32)


def _combine(ya, yb, z, x2d, wa, wb, wo, n2, wr, br, tm, n_groups, epg):
    t, d = x2d.shape
    c = ya.shape[1]
    half = d // 2
    off = (3 * c) // half
    kern = functools.partial(_combine_kernel, tm=tm, n_groups=n_groups, epg=epg)

    def zspec(j):
        return pl.BlockSpec((tm, half), lambda i, j=j: (i, off + j))

    return pl.pallas_call(
        kern,
        grid=(t // tm,),
        in_specs=[pl.BlockSpec((tm, c), lambda i: (i, 0)),
                  pl.BlockSpec((tm, c), lambda i: (i, 0)),
                  zspec(0), zspec(1), zspec(2), zspec(3),
                  pl.BlockSpec((tm, d), lambda i: (i, 0)),
                  _const_spec(wa.shape), _const_spec(wb.shape), _const_spec(wo.shape),
                  _const_spec(n2.shape), _const_spec(wr.shape), _const_spec(br.shape)],
        out_specs=[pl.BlockSpec((tm, d + LANES), lambda i: (i, 0)),
                   pl.BlockSpec((1, LANES), lambda i: (0, 0)),
                   pl.BlockSpec((SUBLANES, tm), lambda i: (0, i))],
        out_shape=[jax.ShapeDtypeStruct((t, d + LANES), _F32),
                   jax.ShapeDtypeStruct((1, LANES), _F32),
                   jax.ShapeDtypeStruct((SUBLANES, t), jnp.int32)],
        scratch_shapes=[pltpu.VMEM((1, LANES), _F32)],
        compiler_params=_params(1),
        name="combine",
    )(ya, yb, z, z, z, z, x2d, wa, wb, wo, n2, wr, br)


def _bucket_starts(cnt_ref, start_ref, n_buckets, tm, on_tile=None):
    def bucket_body(b, tile_idx):
        start_ref[b] = tile_idx * tm
        nt = lax.div(cnt_ref[b] + (tm - 1), tm)
        if on_tile is not None:
            lax.fori_loop(0, nt, lambda k, carry: (on_tile(tile_idx + k, b), carry)[1], 0)
        return tile_idx + nt

    return lax.fori_loop(0, n_buckets, bucket_body, 0)


def _row_gather_start(row_of, src_hbm, buf, slot, sem, n_rows, unrolled, width=None):
    width = src_hbm.shape[1] if width is None else width

    def start(r):
        row = row_of(r)
        pltpu.make_async_copy(src_hbm.at[pl.ds(row, 1), pl.ds(0, width)],
                              buf.at[slot, pl.ds(r, 1), pl.ds(0, width)],
                              sem.at[slot]).start(priority=1)

    if unrolled:
        for r in range(n_rows):
            start(r)
    else:
        def body(r, _):
            start(r)
            return 0
        lax.fori_loop(0, n_rows, body, 0, unroll=8)


def _row_gather_wait(src_hbm, buf, slot, sem, n_rows, width=None):
    width = src_hbm.shape[1] if width is None else width
    pltpu.make_async_copy(src_hbm.at[pl.ds(0, n_rows), pl.ds(0, width)],
                          buf.at[slot, :, pl.ds(0, width)], sem.at[slot]).wait()


MOE_WEIGHT_SLOTS = 6
MOE_PARTS = 4
MOE_CHUNKS = 3 * MOE_PARTS
MOE_PUMP = 3
MOE_DEPTH = 4


def _moe_weight_plan(n_groups, epg):
    first_use = {e: min(p for p in range(_N_PAIRS) if e in (_PAIR_SLOT_A[p], _PAIR_SLOT_B[p])) for e in range(epg)}
    last_use = {e: max(p for p in range(_N_PAIRS) if e in (_PAIR_SLOT_A[p], _PAIR_SLOT_B[p])) for e in range(epg)}
    free_after = [-1] * MOE_WEIGHT_SLOTS
    loads, slot_of = [], {}
    for g in range(n_groups):
        for e in sorted(range(epg), key=lambda e: (first_use[e], e)):
            needed_by = g * _N_PAIRS + first_use[e]
            s = min(range(MOE_WEIGHT_SLOTS), key=lambda s: (free_after[s], s))
            assert free_after[s] < needed_by
            loads.append((g * epg + e, s, free_after[s], needed_by))
            free_after[s] = g * _N_PAIRS + last_use[e]
            slot_of[(g, e)] = s
    n_buckets = n_groups * _N_PAIRS
    need = [sum(1 for l in loads if l[3] <= b) for b in range(n_buckets)]
    allow = [sum(1 for l in loads if l[2] < b) for b in range(n_buckets)]
    slot_a = [slot_of[(b // _N_PAIRS, _PAIR_SLOT_A[b % _N_PAIRS])] for b in range(n_buckets)]
    slot_b = [slot_of[(b // _N_PAIRS, _PAIR_SLOT_B[b % _N_PAIRS])] for b in range(n_buckets)]
    return [l[0] for l in loads], [l[1] for l in loads], need, allow, slot_a, slot_b


def _moe_kernel(tokb_ref, tokr_ref, cntb_ref, need_ref, allow_ref, sa_ref, sb_ref, le_ref, ls_ref,
                xe_hbm, n2_ref, wg_hbm, wu_hbm, wd_hbm,
                x2s_ref, xbuf, gsem, wg_s, wu_s, wd_s, st_gu, st_d, wsem, cnt, src_s, tbk_s, start_s, tot_s,
                *, tm):
    j = pl.program_id(0)
    slot = j % 2
    d = x2s_ref.shape[1]
    t_rows = xe_hbm.shape[0]
    n_slots = src_s.shape[0]
    n_buckets = start_s.shape[0]

    def src_row(tile):
        return lambda r: src_s[tile * tm + r]

    @pl.when(j == 0)
    def _():
        def set_tile(tile, b):
            tbk_s[tile] = b

        tot_s[0] = _bucket_starts(cntb_ref, start_s, n_buckets, tm, on_tile=set_tile)
        for base in range(0, n_slots, t_rows):
            def fill(q, carry, base=base):
                src_s[base + q] = q
                return carry
            lax.fori_loop(0, min(t_rows, n_slots - base), fill, 0, unroll=8)

        def place(tok, carry):
            src_s[start_s[tokb_ref[tok]] + tokr_ref[tok]] = tok
            return carry
        lax.fori_loop(0, t_rows, place, 0, unroll=8)

    n_valid = tot_s[0]
    rows_gu = wg_hbm.shape[1] // MOE_PARTS
    rows_d = wd_hbm.shape[1] // MOE_PARTS

    def chunk_dma(c, kind):
        load = c // MOE_CHUNKS
        part = c % MOE_PARTS
        e = le_ref[load]
        if kind == 2:
            return pltpu.make_async_copy(wd_hbm.at[e, pl.ds(part * rows_d, rows_d)], st_d.at[c % MOE_DEPTH],
                                         wsem.at[c % MOE_DEPTH])
        src = wg_hbm if kind == 0 else wu_hbm
        return pltpu.make_async_copy(src.at[e, pl.ds(part * rows_gu, rows_gu)], st_gu.at[c % MOE_DEPTH],
                                     wsem.at[c % MOE_DEPTH])

    def for_kind(c, fn):
        kind = (c % MOE_CHUNKS) // MOE_PARTS
        for k in range(3):
            @pl.when(kind == k)
            def _(k=k):
                fn(k)

    def issue_one(allowed):
        @pl.when(cnt[0] < jnp.minimum(allowed, cnt[1] + MOE_DEPTH))
        def _():
            c = cnt[0]
            for_kind(c, lambda k: chunk_dma(c, k).start())
            cnt[0] = c + 1

    def retire():
        c = cnt[1]
        s = ls_ref[c // MOE_CHUNKS]
        part = c % MOE_PARTS

        def finish(k):
            chunk_dma(c, k).wait()
            if k == 2:
                wd_s[s, pl.ds(part * rows_d, rows_d), :] = st_d[c % MOE_DEPTH].astype(_BF16)
            else:
                dst = wg_s if k == 0 else wu_s
                dst[s, pl.ds(part * rows_gu, rows_gu), :] = st_gu[c % MOE_DEPTH].astype(_BF16)

        for_kind(c, finish)
        cnt[1] = c + 1

    def pump(required, allowed, extra):
        n_iter = jnp.maximum(required - cnt[1], jnp.minimum(extra, allowed - cnt[1]))
        lax.fori_loop(0, MOE_DEPTH, lambda _, carry: (issue_one(allowed), carry)[1], 0)

        def body(_, carry):
            retire()
            issue_one(allowed)
            return carry

        lax.fori_loop(0, jnp.maximum(n_iter, 0), body, 0)

    @pl.when(j == 0)
    def _():
        cnt[0] = 0
        cnt[1] = 0
        _row_gather_start(src_row(0), xe_hbm, xbuf, 0, gsem, tm, unrolled=False)

    @pl.when(j < n_valid)
    def _():
        b = tbk_s[j]
        pump(need_ref[b] * MOE_CHUNKS, allow_ref[b] * MOE_CHUNKS, MOE_PUMP)
        s_a = sa_ref[b]
        s_b = sb_ref[b]
        _row_gather_wait(xe_hbm, xbuf, slot, gsem, tm)
        _row_gather_start(src_row(j + 1), xe_hbm, xbuf, 1 - slot, gsem, tm, unrolled=True)
        xe = xbuf[slot]
        x1 = xe[:, 0:d]
        w_a = xe[:, d + 2:d + 3]
        w_b = xe[:, d + 3:d + 4]
        ht = _rms(x1, n2_ref[...]).astype(_BF16)

        def expert(s, wgt):
            hg = jnp.dot(ht, wg_s[s], preferred_element_type=_F32)
            hu = jnp.dot(ht, wu_s[s], preferred_element_type=_F32)
            return (hg * jax.nn.sigmoid(hg) * hu * wgt).astype(_BF16)

        y = jnp.dot(expert(s_a, w_a), wd_s[s_a], preferred_element_type=_F32)
        y = y + jnp.dot(expert(s_b, w_b), wd_s[s_b], preferred_element_type=_F32)
        x2s_ref[...] = x1 + y

    @pl.when(j >= n_valid)
    def _():
        x2s_ref[...] = jnp.zeros(x2s_ref.shape, _F32)

        @pl.when(j == n_valid)
        def _():
            _row_gather_wait(xe_hbm, xbuf, slot, gsem, tm)
            lax.fori_loop(0, cnt[0] - cnt[1], lambda _, carry: (retire(), carry)[1], 0)


def _moe(tok_bucket, tok_rank, bucket_cnt, xe, n2, wg, wu, wd, tm, n_groups, epg):
    n_buckets = bucket_cnt.shape[0]
    n_tiles = xe.shape[0] // tm + n_buckets
    d = wg.shape[1]
    f = wg.shape[2]
    kern = functools.partial(_moe_kernel, tm=tm)
    le, ls, need, allow, slot_a, slot_b = (jnp.asarray(v, jnp.int32) for v in _moe_weight_plan(n_groups, epg))

    grid_spec = pltpu.PrefetchScalarGridSpec(
        num_scalar_prefetch=9,
        grid=(n_tiles,),
        in_specs=[pl.BlockSpec(memory_space=pl.ANY),
                  pl.BlockSpec((1, d), lambda j, *_: (0, 0)),
                  pl.BlockSpec(memory_space=pl.ANY),
                  pl.BlockSpec(memory_space=pl.ANY),
                  pl.BlockSpec(memory_space=pl.ANY)],
        out_specs=pl.BlockSpec((tm, d), lambda j, *_: (j, 0)),
        scratch_shapes=[pltpu.VMEM((2, tm, d + LANES), _F32),
                        pltpu.SemaphoreType.DMA((2,)),
                        pltpu.VMEM((MOE_WEIGHT_SLOTS, d, f), _BF16),
                        pltpu.VMEM((MOE_WEIGHT_SLOTS, d, f), _BF16),
                        pltpu.VMEM((MOE_WEIGHT_SLOTS, f, d), _BF16),
                        pltpu.VMEM((MOE_DEPTH, d // MOE_PARTS, f), _F32),
                        pltpu.VMEM((MOE_DEPTH, f // MOE_PARTS, d), _F32),
                        pltpu.SemaphoreType.DMA((MOE_DEPTH,)),
                        pltpu.SMEM((2,), jnp.int32),
                        pltpu.SMEM((n_tiles * tm,), jnp.int32),
                        pltpu.SMEM((n_tiles,), jnp.int32),
                        pltpu.SMEM((n_buckets,), jnp.int32),
                        pltpu.SMEM((1,), jnp.int32)],
    )
    return pl.pallas_call(
        kern,
        grid_spec=grid_spec,
        out_shape=jax.ShapeDtypeStruct((n_tiles * tm, d), _F32),
        compiler_params=_params(1),
        name="moe",
    )(tok_bucket, tok_rank, bucket_cnt, need, allow, slot_a, slot_b, le, ls, xe, n2, wg, wu, wd)


def _ple_kernel(tokb_ref, tokr_ref, cntb_ref, x2s_hbm, p_ref, ng_ref, wg_ref, wp_ref, nf_ref, o_ref,
                xbuf, gsem, start_s, *, tm, n_steps, tm_sorted):
    i = pl.program_id(0)
    slot = i % 2

    def sorted_row(tile):
        return lambda r: start_s[tokb_ref[tile * tm + r]] + tokr_ref[tile * tm + r]

    @pl.when(i == 0)
    def _():
        _bucket_starts(cntb_ref, start_s, start_s.shape[0], tm_sorted)
        _row_gather_start(sorted_row(0), x2s_hbm, xbuf, 0, gsem, tm, unrolled=False)

    _row_gather_wait(x2s_hbm, xbuf, slot, gsem, tm)
    nxt = jnp.where(i + 1 == n_steps, 0, i + 1)
    _row_gather_start(sorted_row(nxt), x2s_hbm, xbuf, 1 - slot, gsem, tm, unrolled=True)
    x2 = xbuf[slot]
    g = jax.nn.sigmoid(jnp.dot(_rms(x2, ng_ref[...]).astype(_BF16), wg_ref[...],
                               preferred_element_type=_F32))
    e = jnp.dot(p_ref[...].astype(_BF16), wp_ref[...], preferred_element_type=_F32)
    o_ref[...] = _rms(x2 + g * e, nf_ref[...])

    @pl.when(i == n_steps - 1)
    def _():
        _row_gather_wait(x2s_hbm, xbuf, 1 - slot, gsem, tm)


def _ple(tok_bucket, tok_rank, bucket_cnt, x2s, p2d, ng, wg, wp, nf, tm, t, tm_sorted):
    d = x2s.shape[1]
    pd = p2d.shape[1]
    n_steps = t // tm
    kern = functools.partial(_ple_kernel, tm=tm, n_steps=n_steps, tm_sorted=tm_sorted)

    def cspec(shape):
        nd = len(shape)
        return pl.BlockSpec(shape, lambda i, *_: (0,) * nd, pipeline_mode=pl.Buffered(1))

    grid_spec = pltpu.PrefetchScalarGridSpec(
        num_scalar_prefetch=3,
        grid=(n_steps,),
        in_specs=[pl.BlockSpec(memory_space=pl.ANY),
                  pl.BlockSpec((tm, pd), lambda i, *_: (i, 0)),
                  cspec(ng.shape), cspec(wg.shape), cspec(wp.shape), cspec(nf.shape)],
        out_specs=pl.BlockSpec((tm, d), lambda i, *_: (i, 0)),
        scratch_shapes=[pltpu.VMEM((2, tm, d), _F32),
                        pltpu.SemaphoreType.DMA((2,)),
                        pltpu.SMEM((bucket_cnt.shape[0],), jnp.int32)],
    )
    return pl.pallas_call(
        kern,
        grid_spec=grid_spec,
        out_shape=jax.ShapeDtypeStruct((t, d), _F32),
        compiler_params=_params(1),
        name="ple",
    )(tok_bucket, tok_rank, bucket_cnt, x2s, p2d, ng, wg, wp, nf)


def _block_diag(w, per_block):
    h, hd, _ = w.shape
    nb = h // per_block
    w4 = w.reshape(nb, per_block, hd, hd)
    rows = [jnp.pad(w4[:, p], ((0, 0), (0, 0), (p * hd, (per_block - 1 - p) * hd))) for p in range(per_block)]
    return jnp.concatenate(rows, axis=1)


def _layer(x2d, p2d, bsz, seq, norm1_g, w_in, conv_w, conv_b, w_rg_a, b_rg_a, w_rg_x, b_rg_x, lru_lambda,
           w_pool, pool_scale, w_branch_a, w_branch_b, w_out, norm2_g, w_router_group, b_router_group,
           w_router_expert, b_router_expert, w_e_gate, w_e_up, w_e_down, norm_ple_g, w_ple_gate,
           w_ple_proj, out_norm_g):
    t, d = x2d.shape
    c = conv_b.shape[0]
    heads, hd, _ = w_rg_a.shape
    n_groups = w_router_group.shape[1]
    n_exp = w_router_expert.shape[1]
    epg = n_exp // n_groups
    assert epg == 4 and TOP_K == 2 and hd * (MXU_DIM // hd) == MXU_DIM
    assert w_pool.shape[0] == len(POOL_WINDOWS) and w_pool.shape[1] == MXU_DIM

    tm_in, tn_in, rc_in = min(2048, t), 512, min(256, t)
    tm_mix = min(256, seq)
    tm_cmb = min(256, t)
    tm_moe = min(128, t)
    tm_ple = min(256, t)

    row = lambda v: v.reshape(1, -1).astype(_F32)
    per_block = MXU_DIM // hd

    z = _inproj(x2d, row(norm1_g), w_in, tm_in, tn_in, rc_in)
    ya, yb = _mixer(z, bsz, seq, conv_w.reshape(CONV_WIDTH, c), row(conv_b),
                    _block_diag(w_rg_a, per_block).astype(_BF16), row(b_rg_a),
                    _block_diag(w_rg_x, per_block).astype(_BF16), row(b_rg_x),
                    row(lru_lambda), w_pool.astype(_BF16), row(pool_scale), tm_mix)

    n_rt = n_groups + n_exp
    wr = jnp.pad(jnp.concatenate([w_router_group, w_router_expert], axis=1),
                 ((0, 0), (0, LANES - n_rt))).astype(_BF16)
    br = jnp.pad(jnp.concatenate([b_router_group, b_router_expert]), (0, LANES - n_rt)).reshape(1, LANES)
    xe, counts, meta = _combine(ya, yb, z, x2d, w_branch_a.astype(_BF16), w_branch_b.astype(_BF16),
                                w_out.astype(_BF16), row(norm2_g), wr, br, tm_cmb, n_groups, epg)

    n_buckets = n_groups * _N_PAIRS
    tok_bucket, tok_rank = meta[0], meta[1]
    bucket_cnt = counts[0, :n_buckets].astype(jnp.int32)

    x2s = _moe(tok_bucket, tok_rank, bucket_cnt, xe, row(norm2_g), w_e_gate, w_e_up, w_e_down,
               tm_moe, n_groups, epg)
    return _ple(tok_bucket, tok_rank, bucket_cnt, x2s, p2d, row(norm_ple_g), w_ple_gate.astype(_BF16),
                w_ple_proj.astype(_BF16), row(out_norm_g), tm_ple, t, tm_moe)


def kernel(x, p, norm1_g, w_in, conv_w, conv_b, w_rg_a, b_rg_a, w_rg_x, b_rg_x, lru_lambda, w_pool, pool_scale, w_branch_a, w_branch_b, w_out, norm2_g, w_router_group, b_router_group, w_router_expert, b_router_expert, w_e_gate, w_e_up, w_e_down, norm_ple_g, w_ple_gate, w_ple_proj, final_norm_g):
    bsz, seq, d = x.shape
    depth = p.shape[0]
    assert depth == 1, "the final RMSNorm is fused into the last layer's embedding kernel"
    out = _layer(x.reshape(bsz * seq, d), p[0].reshape(bsz * seq, -1), bsz, seq,
                 norm1_g[0], w_in[0], conv_w[0], conv_b[0], w_rg_a[0], b_rg_a[0], w_rg_x[0], b_rg_x[0],
                 lru_lambda[0], w_pool[0], pool_scale[0], w_branch_a[0], w_branch_b[0], w_out[0],
                 norm2_g[0], w_router_group[0], b_router_group[0], w_router_expert[0],
                 b_router_expert[0], w_e_gate[0], w_e_up[0], w_e_down[0], norm_ple_g[0],
                 w_ple_gate[0], w_ple_proj[0], final_norm_g)
    return out.reshape(bsz, seq, d)
```

```python
import functools

import jax
import jax.numpy as jnp
from jax import lax
from jax.experimental import pallas as pl
from jax.experimental.pallas import tpu as pltpu

EPS = 1e-6
LRU_C = 8.0
CONV_WIDTH = 4
POOL_WINDOWS = (2, 4, 8, 16)
TOP_K = 2

LANES = 128
SUBLANES = 8
MXU_DIM = 256
VMEM_LIMIT_BYTES = 56 * 1024 * 1024

_BF16 = jnp.bfloat16
_F32 = jnp.float32

_PAIR_SLOT_A = (0, 0, 0, 1, 1, 3)
_PAIR_SLOT_B = (1, 2, 3, 3, 2, 2)
_N_PAIRS = len(_PAIR_SLOT_A)


def _rms(x, g):
    ms = jnp.mean(x * x, axis=-1, keepdims=True)
    return x * lax.rsqrt(ms + EPS) * g


def _const_spec(shape):
    nd = len(shape)
    return pl.BlockSpec(shape, lambda *_: (0,) * nd, pipeline_mode=pl.Buffered(1))


def _params(n_axes):
    return pltpu.CompilerParams(dimension_semantics=("arbitrary",) * n_axes,
                                vmem_limit_bytes=VMEM_LIMIT_BYTES)


def _inproj_kernel(x_hbm, g_ref, w_ref, z_ref, h_ref, xs_ref, sem, *, tm, rc):
    i = pl.program_id(0)
    j = pl.program_id(1)
    n_chunks = tm // rc
    more_tiles = i + 1 < pl.num_programs(0)

    def chunk_copy(tile, c):
        return pltpu.make_async_copy(x_hbm.at[pl.ds(pl.multiple_of(tile * tm + c * rc, rc), rc)],
                                     xs_ref.at[c % 2], sem.at[c % 2])

    def normalise(tile, c):
        h_ref[tile % 2, pl.ds(pl.multiple_of(c * rc, rc), rc), :] = _rms(xs_ref[c % 2], g_ref[...]).astype(_BF16)

    @pl.when(jnp.logical_and(i == 0, j == 0))
    def _():
        chunk_copy(0, 0).start()
        for c in range(n_chunks):
            if c + 1 < n_chunks:
                chunk_copy(0, c + 1).start()
            chunk_copy(0, c).wait()
            normalise(0, c)

    @pl.when(jnp.logical_and(more_tiles, jnp.logical_and(j >= 1, j <= n_chunks)))
    def _():
        chunk_copy(i + 1, j - 1).wait()
        normalise(i + 1, j - 1)

    @pl.when(jnp.logical_and(more_tiles, j < n_chunks))
    def _():
        chunk_copy(i + 1, j).start()

    z_ref[...] = jnp.dot(h_ref[i % 2], w_ref[...].astype(_BF16), preferred_element_type=_F32)


def _inproj(x2d, g, w, tm, tn, rc):
    t, d = x2d.shape
    n = w.shape[1]
    kern = functools.partial(_inproj_kernel, tm=tm, rc=rc)
    assert n // tn > tm // rc, "a row tile's chunks are prepared during the column steps of the previous tile"
    return pl.pallas_call(
        kern,
        grid=(t // tm, n // tn),
        in_specs=[pl.BlockSpec(memory_space=pl.ANY),
                  pl.BlockSpec((1, d), lambda i, j: (0, 0)),
                  pl.BlockSpec((d, tn), lambda i, j: (0, j))],
        out_specs=pl.BlockSpec((tm, tn), lambda i, j: (i, j)),
        out_shape=jax.ShapeDtypeStruct((t, n), _F32),
        scratch_shapes=[pltpu.VMEM((2, tm, d), _BF16),
                        pltpu.VMEM((2, rc, d), _F32),
                        pltpu.SemaphoreType.DMA((2,))],
        compiler_params=_params(2),
        name="inproj",
    )(x2d, g, w)


def _scan_pitch(tm):
    p = -(-tm // SUBLANES)
    while p % SUBLANES != 4:
        p += 1
    return p


def _mixer_kernel(z_ref, cw_ref, cb_ref, wa_ref, ba_ref, wx_ref, bx_ref, lam_ref, wp_ref, ps_ref,
                  ya_ref, yb_ref,
                  er_ref, ep_ref, a_ref, b_ref, h_ref, car_ref, *, tm, pitch):
    s = pl.program_id(1)
    c = cb_ref.shape[1]
    n_slab = c // LANES
    hist_r = SUBLANES
    hist_p = 2 * SUBLANES

    @pl.when(s == 0)
    def _():
        er_ref[0:hist_r, :] = jnp.zeros((hist_r, c), _F32)
        ep_ref[0:hist_p, :] = jnp.zeros((hist_p, c), _F32)
        car_ref[...] = jnp.zeros(car_ref.shape, _F32)
        a_ref[:, tm:, :] = jnp.ones((n_slab, SUBLANES * pitch - tm, LANES), _F32)
        b_ref[:, tm:, :] = jnp.zeros((n_slab, SUBLANES * pitch - tm, LANES), _F32)

    @pl.when(s > 0)
    def _():
        er_ref[0:hist_r, :] = er_ref[tm:tm + hist_r, :]
        ep_ref[0:hist_p, :] = ep_ref[tm:tm + hist_p, :]

    er_ref[hist_r:hist_r + tm, :] = z_ref[:, 0:c]
    ep_ref[hist_p:hist_p + tm, :] = z_ref[:, 2 * c:3 * c]

    kvec = -LRU_C * jax.nn.softplus(-lam_ref[...])
    nblk = c // MXU_DIM
    for k in range(nblk):
        cs = slice(k * MXU_DIM, (k + 1) * MXU_DIM)
        xc = cb_ref[:, cs] + cw_ref[CONV_WIDTH - 1:CONV_WIDTH, cs] * er_ref[hist_r:hist_r + tm, cs]
        for j in range(1, CONV_WIDTH):
            xc = xc + cw_ref[CONV_WIDTH - 1 - j:CONV_WIDTH - j, cs] * er_ref[hist_r - j:hist_r - j + tm, cs]
        xcb = xc.astype(_BF16)
        r = jax.nn.sigmoid(jnp.dot(xcb, wa_ref[k], preferred_element_type=_F32) + ba_ref[:, cs])
        ig = jax.nn.sigmoid(jnp.dot(xcb, wx_ref[k], preferred_element_type=_F32) + bx_ref[:, cs])
        log_a = r * kvec[:, cs]
        a = jnp.exp(log_a)
        mult = jnp.sqrt(1.0 - a * a)
        bb = mult * ig * xc
        for q in range(MXU_DIM // LANES):
            slab = k * (MXU_DIM // LANES) + q
            a_ref[slab, 0:tm, :] = a[:, q * LANES:(q + 1) * LANES]
            b_ref[slab, 0:tm, :] = bb[:, q * LANES:(q + 1) * LANES]

    def sweep1(i, carry):
        hs, ps = carry
        nh, npd = [], []
        for sl in range(n_slab):
            av = a_ref[sl, pl.ds(i, SUBLANES, stride=pitch), :]
            bv = b_ref[sl, pl.ds(i, SUBLANES, stride=pitch), :]
            nh.append(av * hs[sl] + bv)
            npd.append(av * ps[sl])
        return tuple(nh), tuple(npd)

    zero = jnp.zeros((SUBLANES, LANES), _F32)
    one = jnp.ones((SUBLANES, LANES), _F32)
    h_end, a_end = lax.fori_loop(0, pitch, sweep1, ((zero,) * n_slab, (one,) * n_slab))

    row = lax.broadcasted_iota(jnp.int32, (SUBLANES, LANES), 0)
    h0 = []
    for sl in range(n_slab):
        hh, aa = h_end[sl], a_end[sl]
        d = 1
        while d < SUBLANES:
            hs_ = jnp.where(row >= d, pltpu.roll(hh, d, 0), 0.0)
            as_ = jnp.where(row >= d, pltpu.roll(aa, d, 0), 1.0)
            hh = aa * hs_ + hh
            aa = aa * as_
            d *= 2
        cin = car_ref[:, sl * LANES:(sl + 1) * LANES]
        full = hh + aa * cin
        h0.append(jnp.where(row >= 1, pltpu.roll(full, 1, 0), cin))
        car_ref[:, sl * LANES:(sl + 1) * LANES] = jnp.broadcast_to(full[SUBLANES - 1:SUBLANES, :],
                                                                  (SUBLANES, LANES))

    def sweep2(i, hs):
        nh = []
        for sl in range(n_slab):
            av = a_ref[sl, pl.ds(i, SUBLANES, stride=pitch), :]
            bv = b_ref[sl, pl.ds(i, SUBLANES, stride=pitch), :]
            hv = av * hs[sl] + bv
            h_ref[sl, pl.ds(i, SUBLANES, stride=pitch), :] = hv
            nh.append(hv)
        return tuple(nh)

    lax.fori_loop(0, pitch, sweep2, tuple(h0))

    for sl in range(n_slab):
        cs = slice(sl * LANES, (sl + 1) * LANES)
        g = z_ref[:, c + sl * LANES:c + (sl + 1) * LANES]
        ya_ref[:, cs] = (h_ref[sl, 0:tm, :] * jax.nn.gelu(g)).astype(_BF16)

    t_idx = (s * tm + lax.broadcasted_iota(jnp.int32, (tm, 1), 0) + 1).astype(_F32)
    n_grp = len(POOL_WINDOWS)
    gd = c // n_grp
    for gi, w in enumerate(POOL_WINDOWS):
        cs = slice(gi * gd, (gi + 1) * gd)
        e = ep_ref[:, cs]
        acc = e
        d = 1
        while d < w:
            acc = acc + pltpu.roll(acc, d, 0)
            d *= 2
        xt = e[hist_p:, :]
        cnt = jnp.minimum(t_idx, float(w))
        dd = acc[hist_p:, :] / cnt - xt
        yb = jnp.dot(dd.astype(_BF16), wp_ref[gi], preferred_element_type=_F32) * ps_ref[:, cs]
        yb_ref[:, cs] = yb.astype(_BF16)


def _mixer(z, bsz, seq, cw, cb, wa_bd, ba, wx_bd, bx, lam, wp, ps, tm):
    t = z.shape[0]
    c = cb.shape[1]
    ns = seq // tm
    pitch = _scan_pitch(tm)
    n_slab = c // LANES
    kern = functools.partial(_mixer_kernel, tm=tm, pitch=pitch)
    return pl.pallas_call(
        kern,
        grid=(bsz, ns),
        in_specs=[pl.BlockSpec((tm, 3 * c), lambda b, s: (b * ns + s, 0)),
                  _const_spec(cw.shape), _const_spec(cb.shape),
                  _const_spec(wa_bd.shape), _const_spec(ba.shape),
                  _const_spec(wx_bd.shape), _const_spec(bx.shape),
                  _const_spec(lam.shape), _const_spec(wp.shape), _const_spec(ps.shape)],
        out_specs=[pl.BlockSpec((tm, c), lambda b, s: (b * ns + s, 0)),
                   pl.BlockSpec((tm, c), lambda b, s: (b * ns + s, 0))],
        out_shape=[jax.ShapeDtypeStruct((t, c), _BF16), jax.ShapeDtypeStruct((t, c), _BF16)],
        scratch_shapes=[pltpu.VMEM((SUBLANES + tm, c), _F32),
                        pltpu.VMEM((2 * SUBLANES + tm, c), _F32),
                        pltpu.VMEM((n_slab, SUBLANES * pitch, LANES), _F32),
                        pltpu.VMEM((n_slab, SUBLANES * pitch, LANES), _F32),
                        pltpu.VMEM((n_slab, SUBLANES * pitch, LANES), _F32),
                        pltpu.VMEM((SUBLANES, c), _F32)],
        compiler_params=_params(2),
        name="mixer",
    )(z, cw, cb, wa_bd, ba, wx_bd, bx, lam, wp, ps)


def _combine_kernel(ya_ref, yb_ref, ga0_ref, ga1_ref, gb0_ref, gb1_ref, x_ref,
                    wa_ref, wb_ref, wo_ref, n2_ref, wr_ref, br_ref,
                    xe_ref, cnt_ref, meta_ref, cnt_s, *, tm, n_groups, epg):
    i = pl.program_id(0)
    d = x_ref.shape[1]
    half = d // 2

    @pl.when(i == 0)
    def _():
        cnt_s[...] = jnp.zeros(cnt_s.shape, _F32)

    pa = jnp.dot(ya_ref[...], wa_ref[...], preferred_element_type=_F32)
    pb = jnp.dot(yb_ref[...], wb_ref[...], preferred_element_type=_F32)
    u0 = jax.nn.sigmoid(ga0_ref[...]) * pa[:, :half] + jax.nn.sigmoid(gb0_ref[...]) * pb[:, :half]
    u1 = jax.nn.sigmoid(ga1_ref[...]) * pa[:, half:] + jax.nn.sigmoid(gb1_ref[...]) * pb[:, half:]
    u = jnp.concatenate([u0, u1], axis=1).astype(_BF16)
    x1 = x_ref[...] + jnp.dot(u, wo_ref[...], preferred_element_type=_F32)
    xe_ref[:, 0:d] = x1

    ht = _rms(x1, n2_ref[...]).astype(_BF16)
    logits = jnp.dot(ht, wr_ref[...], preferred_element_type=_F32) + br_ref[...]

    lane = lax.broadcasted_iota(jnp.int32, (tm, LANES), 1).astype(_F32)
    ninf = -jnp.inf
    big = float(LANES)

    def first_argmax(v):
        m = jnp.max(v, axis=-1, keepdims=True)
        return m, jnp.min(jnp.where(v == m, lane, big), axis=-1, keepdims=True)

    is_g = lane < float(n_groups)
    gmax, gidx = first_argmax(jnp.where(is_g, logits, ninf))
    g_w = 1.0 / jnp.sum(jnp.where(is_g, jnp.exp(logits - gmax), 0.0), axis=-1, keepdims=True)
    lo_lane = float(n_groups) + float(epg) * gidx
    in_grp = (lane >= lo_lane) & (lane < lo_lane + float(epg))
    le = jnp.where(in_grp, logits, ninf)
    m1, i1 = first_argmax(le)
    m2, i2 = first_argmax(jnp.where(lane == i1, ninf, le))
    e21 = jnp.exp(m2 - m1)
    w1 = g_w / (1.0 + e21)
    w2 = w1 * e21
    e1 = i1 - lo_lane
    e2 = i2 - lo_lane
    lo = jnp.minimum(e1, e2)
    hi = jnp.maximum(e1, e2)
    w_lo = jnp.where(e1 < e2, w1, w2)
    w_hi = jnp.where(e1 < e2, w2, w1)
    pair = jnp.where(lo == 0.0, hi - 1.0, jnp.where(lo == 1.0, 6.0 - hi, 5.0))
    swap = pair == 5.0
    w_a = jnp.where(swap, w_hi, w_lo)
    w_b = jnp.where(swap, w_lo, w_hi)
    bucket = float(_N_PAIRS) * gidx + pair

    onehot = lane == bucket
    oh_bf = jnp.where(onehot, 1.0, 0.0).astype(_BF16)
    rr = lax.broadcasted_iota(jnp.int32, (tm, tm), 0)
    cc = lax.broadcasted_iota(jnp.int32, (tm, tm), 1)
    tri = jnp.where(cc < rr, 1.0, 0.0).astype(_BF16)
    before = jnp.dot(tri, oh_bf, preferred_element_type=_F32) + cnt_s[...]
    rank = jnp.sum(jnp.where(onehot, before, 0.0), axis=-1, keepdims=True)
    cnt_s[...] = cnt_s[...] + jnp.sum(jnp.where(onehot, 1.0, 0.0), axis=0, keepdims=True)
    cnt_ref[...] = cnt_s[...]

    info = jnp.where(lane == 0.0, bucket,
                     jnp.where(lane == 1.0, rank,
                               jnp.where(lane == 2.0, w_a, jnp.where(lane == 3.0, w_b, 0.0))))
    xe_ref[:, d:d + LANES] = info
    meta_ref[...] = jnp.transpose(info)[0:SUBLANES, :].astype(jnp.int32)


def _combine(ya, yb, z, x2d, wa, wb, wo, n2, wr, br, tm, n_groups, epg):
    t, d = x2d.shape
    c = ya.shape[1]
    half = d // 2
    off = (3 * c) // half
    kern = functools.partial(_combine_kernel, tm=tm, n_groups=n_groups, epg=epg)

    def zspec(j):
        return pl.BlockSpec((tm, half), lambda i, j=j: (i, off + j))

    return pl.pallas_call(
        kern,
        grid=(t // tm,),
        in_specs=[pl.BlockSpec((tm, c), lambda i: (i, 0)),
                  pl.BlockSpec((tm, c), lambda i: (i, 0)),
                  zspec(0), zspec(1), zspec(2), zspec(3),
                  pl.BlockSpec((tm, d), lambda i: (i, 0)),
                  _const_spec(wa.shape), _const_spec(wb.shape), _const_spec(wo.shape),
                  _const_spec(n2.shape), _const_spec(wr.shape), _const_spec(br.shape)],
        out_specs=[pl.BlockSpec((tm, d + LANES), lambda i: (i, 0)),
                   pl.BlockSpec((1, LANES), lambda i: (0, 0)),
                   pl.BlockSpec((SUBLANES, tm), lambda i: (0, i))],
        out_shape=[jax.ShapeDtypeStruct((t, d + LANES), _F32),
                   jax.ShapeDtypeStruct((1, LANES), _F32),
                   jax.ShapeDtypeStruct((SUBLANES, t), jnp.int32)],
        scratch_shapes=[pltpu.VMEM((1, LANES), _F32)],
        compiler_params=_params(1),
        name="combine",
    )(ya, yb, z, z, z, z, x2d, wa, wb, wo, n2, wr, br)


def _bucket_starts(cnt_ref, start_ref, n_buckets, tm, on_tile=None):
    def bucket_body(b, tile_idx):
        start_ref[b] = tile_idx * tm
        nt = lax.div(cnt_ref[b] + (tm - 1), tm)
        if on_tile is not None:
            lax.fori_loop(0, nt, lambda k, carry: (on_tile(tile_idx + k, b), carry)[1], 0)
        return tile_idx + nt

    return lax.fori_loop(0, n_buckets, bucket_body, 0)


def _row_gather_start(row_of, src_hbm, buf, slot, sem, n_rows, unrolled, width=None):
    width = src_hbm.shape[1] if width is None else width

    def start(r):
        row = row_of(r)
        pltpu.make_async_copy(src_hbm.at[pl.ds(row, 1), pl.ds(0, width)],
                              buf.at[slot, pl.ds(r, 1), pl.ds(0, width)],
                              sem.at[slot]).start(priority=1)

    if unrolled:
        for r in range(n_rows):
            start(r)
    else:
        def body(r, _):
            start(r)
            return 0
        lax.fori_loop(0, n_rows, body, 0, unroll=8)


def _row_gather_wait(src_hbm, buf, slot, sem, n_rows, width=None):
    width = src_hbm.shape[1] if width is None else width
    pltpu.make_async_copy(src_hbm.at[pl.ds(0, n_rows), pl.ds(0, width)],
                          buf.at[slot, :, pl.ds(0, width)], sem.at[slot]).wait()


MOE_WEIGHT_SLOTS = 6
MOE_PARTS = 4
MOE_CHUNKS = 3 * MOE_PARTS
MOE_PUMP = 3
MOE_DEPTH = 4


def _moe_weight_plan(n_groups, epg):
    first_use = {e: min(p for p in range(_N_PAIRS) if e in (_PAIR_SLOT_A[p], _PAIR_SLOT_B[p])) for e in range(epg)}
    last_use = {e: max(p for p in range(_N_PAIRS) if e in (_PAIR_SLOT_A[p], _PAIR_SLOT_B[p])) for e in range(epg)}
    free_after = [-1] * MOE_WEIGHT_SLOTS
    loads, slot_of = [], {}
    for g in range(n_groups):
        for e in sorted(range(epg), key=lambda e: (first_use[e], e)):
            needed_by = g * _N_PAIRS + first_use[e]
            s = min(range(MOE_WEIGHT_SLOTS), key=lambda s: (free_after[s], s))
            assert free_after[s] < needed_by
            loads.append((g * epg + e, s, free_after[s], needed_by))
            free_after[s] = g * _N_PAIRS + last_use[e]
            slot_of[(g, e)] = s
    n_buckets = n_groups * _N_PAIRS
    need = [sum(1 for l in loads if l[3] <= b) for b in range(n_buckets)]
    allow = [sum(1 for l in loads if l[2] < b) for b in range(n_buckets)]
    slot_a = [slot_of[(b // _N_PAIRS, _PAIR_SLOT_A[b % _N_PAIRS])] for b in range(n_buckets)]
    slot_b = [slot_of[(b // _N_PAIRS, _PAIR_SLOT_B[b % _N_PAIRS])] for b in range(n_buckets)]
    return [l[0] for l in loads], [l[1] for l in loads], need, allow, slot_a, slot_b


def _moe_kernel(tokb_ref, tokr_ref, cntb_ref, need_ref, allow_ref, sa_ref, sb_ref, le_ref, ls_ref,
                xe_hbm, n2_ref, wg_hbm, wu_hbm, wd_hbm,
                x2s_ref, xbuf, gsem, wg_s, wu_s, wd_s, st_gu, st_d, wsem, cnt, src_s, tbk_s, start_s, tot_s,
                *, tm):
    j = pl.program_id(0)
    slot = j % 2
    d = x2s_ref.shape[1]
    t_rows = xe_hbm.shape[0]
    n_slots = src_s.shape[0]
    n_buckets = start_s.shape[0]

    def src_row(tile):
        return lambda r: src_s[tile * tm + r]

    @pl.when(j == 0)
    def _():
        def set_tile(tile, b):
            tbk_s[tile] = b

        total = _bucket_starts(cntb_ref, start_s, n_buckets, tm, on_tile=set_tile)
        tot_s[0] = total

        def fill(lo, hi):
            def body(q, carry):
                src_s[q] = lax.rem(q, t_rows)
                return carry
            lax.fori_loop(lo, hi, body, 0)

        def fill_bucket(b, carry):
            nxt = jnp.where(b + 1 < n_buckets, start_s[jnp.minimum(b + 1, n_buckets - 1)], total * tm)
            fill(start_s[b] + cntb_ref[b], nxt)
            return carry
        lax.fori_loop(0, n_buckets, fill_bucket, 0)
        fill(total * tm, (total + 1) * tm)

        def place(tok, carry):
            src_s[start_s[tokb_ref[tok]] + tokr_ref[tok]] = tok
            return carry
        lax.fori_loop(0, t_rows, place, 0, unroll=16)

    n_valid = tot_s[0]
    rows_gu = wg_hbm.shape[1] // MOE_PARTS
    rows_d = wd_hbm.shape[1] // MOE_PARTS

    def chunk_dma(c, kind):
        load = c // MOE_CHUNKS
        part = c % MOE_PARTS
        e = le_ref[load]
        if kind == 2:
            return pltpu.make_async_copy(wd_hbm.at[e, pl.ds(part * rows_d, rows_d)], st_d.at[c % MOE_DEPTH],
                                         wsem.at[c % MOE_DEPTH])
        src = wg_hbm if kind == 0 else wu_hbm
        return pltpu.make_async_copy(src.at[e, pl.ds(part * rows_gu, rows_gu)], st_gu.at[c % MOE_DEPTH],
                                     wsem.at[c % MOE_DEPTH])

    def for_kind(c, fn):
        kind = (c % MOE_CHUNKS) // MOE_PARTS
        for k in range(3):
            @pl.when(kind == k)
            def _(k=k):
                fn(k)

    def issue_one(allowed):
        @pl.when(cnt[0] < jnp.minimum(allowed, cnt[1] + MOE_DEPTH))
        def _():
            c = cnt[0]
            for_kind(c, lambda k: chunk_dma(c, k).start())
            cnt[0] = c + 1

    def retire():
        c = cnt[1]
        s = ls_ref[c // MOE_CHUNKS]
        part = c % MOE_PARTS

        def finish(k):
            chunk_dma(c, k).wait()
            if k == 2:
                wd_s[s, pl.ds(part * rows_d, rows_d), :] = st_d[c % MOE_DEPTH].astype(_BF16)
            else:
                dst = wg_s if k == 0 else wu_s
                dst[s, pl.ds(part * rows_gu, rows_gu), :] = st_gu[c % MOE_DEPTH].astype(_BF16)

        for_kind(c, finish)
        cnt[1] = c + 1

    def pump(required, allowed, extra):
        n_iter = jnp.maximum(required - cnt[1], jnp.minimum(extra, allowed - cnt[1]))
        lax.fori_loop(0, MOE_DEPTH, lambda _, carry: (issue_one(allowed), carry)[1], 0)

        def body(_, carry):
            retire()
            issue_one(allowed)
            return carry

        lax.fori_loop(0, jnp.maximum(n_iter, 0), body, 0)

    @pl.when(j == 0)
    def _():
        cnt[0] = 0
        cnt[1] = 0
        _row_gather_start(src_row(0), xe_hbm, xbuf, 0, gsem, tm, unrolled=False)

    @pl.when(j < n_valid)
    def _():
        b = tbk_s[j]
        pump(need_ref[b] * MOE_CHUNKS, allow_ref[b] * MOE_CHUNKS, MOE_PUMP)
        s_a = sa_ref[b]
        s_b = sb_ref[b]
        _row_gather_wait(xe_hbm, xbuf, slot, gsem, tm)
        _row_gather_start(src_row(j + 1), xe_hbm, xbuf, 1 - slot, gsem, tm, unrolled=True)
        xe = xbuf[slot]
        x1 = xe[:, 0:d]
        w_a = xe[:, d + 2:d + 3]
        w_b = xe[:, d + 3:d + 4]
        ht = _rms(x1, n2_ref[...]).astype(_BF16)

        def expert(s, wgt):
            hg = jnp.dot(ht, wg_s[s], preferred_element_type=_F32)
            hu = jnp.dot(ht, wu_s[s], preferred_element_type=_F32)
            return (hg * jax.nn.sigmoid(hg) * hu * wgt).astype(_BF16)

        y = jnp.dot(expert(s_a, w_a), wd_s[s_a], preferred_element_type=_F32)
        y = y + jnp.dot(expert(s_b, w_b), wd_s[s_b], preferred_element_type=_F32)
        x2s_ref[...] = x1 + y

    @pl.when(j >= n_valid)
    def _():
        x2s_ref[...] = jnp.zeros(x2s_ref.shape, _F32)

        @pl.when(j == n_valid)
        def _():
            _row_gather_wait(xe_hbm, xbuf, slot, gsem, tm)
            lax.fori_loop(0, cnt[0] - cnt[1], lambda _, carry: (retire(), carry)[1], 0)


def _moe(tok_bucket, tok_rank, bucket_cnt, xe, n2, wg, wu, wd, tm, n_groups, epg):
    n_buckets = bucket_cnt.shape[0]
    n_tiles = xe.shape[0] // tm + n_buckets
    d = wg.shape[1]
    f = wg.shape[2]
    kern = functools.partial(_moe_kernel, tm=tm)
    le, ls, need, allow, slot_a, slot_b = (jnp.asarray(v, jnp.int32) for v in _moe_weight_plan(n_groups, epg))

    grid_spec = pltpu.PrefetchScalarGridSpec(
        num_scalar_prefetch=9,
        grid=(n_tiles,),
        in_specs=[pl.BlockSpec(memory_space=pl.ANY),
                  pl.BlockSpec((1, d), lambda j, *_: (0, 0)),
                  pl.BlockSpec(memory_space=pl.ANY),
                  pl.BlockSpec(memory_space=pl.ANY),
                  pl.BlockSpec(memory_space=pl.ANY)],
        out_specs=pl.BlockSpec((tm, d), lambda j, *_: (j, 0)),
        scratch_shapes=[pltpu.VMEM((2, tm, d + LANES), _F32),
                        pltpu.SemaphoreType.DMA((2,)),
                        pltpu.VMEM((MOE_WEIGHT_SLOTS, d, f), _BF16),
                        pltpu.VMEM((MOE_WEIGHT_SLOTS, d, f), _BF16),
                        pltpu.VMEM((MOE_WEIGHT_SLOTS, f, d), _BF16),
                        pltpu.VMEM((MOE_DEPTH, d // MOE_PARTS, f), _F32),
                        pltpu.VMEM((MOE_DEPTH, f // MOE_PARTS, d), _F32),
                        pltpu.SemaphoreType.DMA((MOE_DEPTH,)),
                        pltpu.SMEM((2,), jnp.int32),
                        pltpu.SMEM((n_tiles * tm,), jnp.int32),
                        pltpu.SMEM((n_tiles,), jnp.int32),
                        pltpu.SMEM((n_buckets,), jnp.int32),
                        pltpu.SMEM((1,), jnp.int32)],
    )
    return pl.pallas_call(
        kern,
        grid_spec=grid_spec,
        out_shape=jax.ShapeDtypeStruct((n_tiles * tm, d), _F32),
        compiler_params=_params(1),
        name="moe",
    )(tok_bucket, tok_rank, bucket_cnt, need, allow, slot_a, slot_b, le, ls, xe, n2, wg, wu, wd)


def _ple_kernel(tokb_ref, tokr_ref, cntb_ref, x2s_hbm, p_ref, ng_ref, wg_ref, wp_ref, nf_ref, o_ref,
                xbuf, gsem, start_s, *, tm, n_steps, tm_sorted):
    i = pl.program_id(0)
    slot = i % 2

    def sorted_row(tile):
        return lambda r: start_s[tokb_ref[tile * tm + r]] + tokr_ref[tile * tm + r]

    @pl.when(i == 0)
    def _():
        _bucket_starts(cntb_ref, start_s, start_s.shape[0], tm_sorted)
        _row_gather_start(sorted_row(0), x2s_hbm, xbuf, 0, gsem, tm, unrolled=False)

    _row_gather_wait(x2s_hbm, xbuf, slot, gsem, tm)
    nxt = jnp.where(i + 1 == n_steps, 0, i + 1)
    _row_gather_start(sorted_row(nxt), x2s_hbm, xbuf, 1 - slot, gsem, tm, unrolled=True)
    x2 = xbuf[slot]
    g = jax.nn.sigmoid(jnp.dot(_rms(x2, ng_ref[...]).astype(_BF16), wg_ref[...],
                               preferred_element_type=_F32))
    e = jnp.dot(p_ref[...].astype(_BF16), wp_ref[...], preferred_element_type=_F32)
    o_ref[...] = _rms(x2 + g * e, nf_ref[...])

    @pl.when(i == n_steps - 1)
    def _():
        _row_gather_wait(x2s_hbm, xbuf, 1 - slot, gsem, tm)


def _ple(tok_bucket, tok_rank, bucket_cnt, x2s, p2d, ng, wg, wp, nf, tm, t, tm_sorted):
    d = x2s.shape[1]
    pd = p2d.shape[1]
    n_steps = t // tm
    kern = functools.partial(_ple_kernel, tm=tm, n_steps=n_steps, tm_sorted=tm_sorted)

    def cspec(shape):
        nd = len(shape)
        return pl.BlockSpec(shape, lambda i, *_: (0,) * nd, pipeline_mode=pl.Buffered(1))

    grid_spec = pltpu.PrefetchScalarGridSpec(
        num_scalar_prefetch=3,
        grid=(n_steps,),
        in_specs=[pl.BlockSpec(memory_space=pl.ANY),
                  pl.BlockSpec((tm, pd), lambda i, *_: (i, 0)),
                  cspec(ng.shape), cspec(wg.shape), cspec(wp.shape), cspec(nf.shape)],
        out_specs=pl.BlockSpec((tm, d), lambda i, *_: (i, 0)),
        scratch_shapes=[pltpu.VMEM((2, tm, d), _F32),
                        pltpu.SemaphoreType.DMA((2,)),
                        pltpu.SMEM((bucket_cnt.shape[0],), jnp.int32)],
    )
    return pl.pallas_call(
        kern,
        grid_spec=grid_spec,
        out_shape=jax.ShapeDtypeStruct((t, d), _F32),
        compiler_params=_params(1),
        name="ple",
    )(tok_bucket, tok_rank, bucket_cnt, x2s, p2d, ng, wg, wp, nf)


def _block_diag(w, per_block):
    h, hd, _ = w.shape
    nb = h // per_block
    w4 = w.reshape(nb, per_block, hd, hd)
    rows = [jnp.pad(w4[:, p], ((0, 0), (0, 0), (p * hd, (per_block - 1 - p) * hd))) for p in range(per_block)]
    return jnp.concatenate(rows, axis=1)


def _layer(x2d, p2d, bsz, seq, norm1_g, w_in, conv_w, conv_b, w_rg_a, b_rg_a, w_rg_x, b_rg_x, lru_lambda,
           w_pool, pool_scale, w_branch_a, w_branch_b, w_out, norm2_g, w_router_group, b_router_group,
           w_router_expert, b_router_expert, w_e_gate, w_e_up, w_e_down, norm_ple_g, w_ple_gate,
           w_ple_proj, out_norm_g):
    t, d = x2d.shape
    c = conv_b.shape[0]
    heads, hd, _ = w_rg_a.shape
    n_groups = w_router_group.shape[1]
    n_exp = w_router_expert.shape[1]
    epg = n_exp // n_groups
    assert epg == 4 and TOP_K == 2 and hd * (MXU_DIM // hd) == MXU_DIM
    assert w_pool.shape[0] == len(POOL_WINDOWS) and w_pool.shape[1] == MXU_DIM

    tm_in, tn_in, rc_in = min(2048, t), 512, min(256, t)
    tm_mix = min(256, seq)
    tm_cmb = min(256, t)
    tm_moe = min(128, t)
    tm_ple = min(256, t)

    row = lambda v: v.reshape(1, -1).astype(_F32)
    per_block = MXU_DIM // hd

    z = _inproj(x2d, row(norm1_g), w_in, tm_in, tn_in, rc_in)
    ya, yb = _mixer(z, bsz, seq, conv_w.reshape(CONV_WIDTH, c), row(conv_b),
                    _block_diag(w_rg_a, per_block).astype(_BF16), row(b_rg_a),
                    _block_diag(w_rg_x, per_block).astype(_BF16), row(b_rg_x),
                    row(lru_lambda), w_pool.astype(_BF16), row(pool_scale), tm_mix)

    n_rt = n_groups + n_exp
    wr = jnp.pad(jnp.concatenate([w_router_group, w_router_expert], axis=1),
                 ((0, 0), (0, LANES - n_rt))).astype(_BF16)
    br = jnp.pad(jnp.concatenate([b_router_group, b_router_expert]), (0, LANES - n_rt)).reshape(1, LANES)
    xe, counts, meta = _combine(ya, yb, z, x2d, w_branch_a.astype(_BF16), w_branch_b.astype(_BF16),
                                w_out.astype(_BF16), row(norm2_g), wr, br, tm_cmb, n_groups, epg)

    n_buckets = n_groups * _N_PAIRS
    tok_bucket, tok_rank = meta[0], meta[1]
    bucket_cnt = counts[0, :n_buckets].astype(jnp.int32)

    x2s = _moe(tok_bucket, tok_rank, bucket_cnt, xe, row(norm2_g), w_e_gate, w_e_up, w_e_down,
               tm_moe, n_groups, epg)
    return _ple(tok_bucket, tok_rank, bucket_cnt, x2s, p2d, row(norm_ple_g), w_ple_gate.astype(_BF16),
                w_ple_proj.astype(_BF16), row(out_norm_g), tm_ple, t, tm_moe)


def kernel(x, p, norm1_g, w_in, conv_w, conv_b, w_rg_a, b_rg_a, w_rg_x, b_rg_x, lru_lambda, w_pool, pool_scale, w_branch_a, w_branch_b, w_out, norm2_g, w_router_group, b_router_group, w_router_expert, b_router_expert, w_e_gate, w_e_up, w_e_down, norm_ple_g, w_ple_gate, w_ple_proj, final_norm_g):
    bsz, seq, d = x.shape
    depth = p.shape[0]
    assert depth == 1, "the final RMSNorm is fused into the last layer's embedding kernel"
    out = _layer(x.reshape(bsz * seq, d), p[0].reshape(bsz * seq, -1), bsz, seq,
                 norm1_g[0], w_in[0], conv_w[0], conv_b[0], w_rg_a[0], b_rg_a[0], w_rg_x[0], b_rg_x[0],
                 lru_lambda[0], w_pool[0], pool_scale[0], w_branch_a[0], w_branch_b[0], w_out[0],
                 norm2_g[0], w_router_group[0], b_router_group[0], w_router_expert[0],
                 b_router_expert[0], w_e_gate[0], w_e_up[0], w_e_down[0], norm_ple_g[0],
                 w_ple_gate[0], w_ple_proj[0], final_norm_g)
    return out.reshape(bsz, seq, d)
```

```python
import functools

import jax
import jax.numpy as jnp
from jax import lax
from jax.experimental import pallas as pl
from jax.experimental.pallas import tpu as pltpu

EPS = 1e-6
LRU_C = 8.0
CONV_WIDTH = 4
POOL_WINDOWS = (2, 4, 8, 16)
TOP_K = 2

LANES = 128
SUBLANES = 8
MXU_DIM = 256
VMEM_LIMIT_BYTES = 56 * 1024 * 1024

_BF16 = jnp.bfloat16
_F32 = jnp.float32

_PAIR_SLOT_A = (0, 0, 0, 1, 1, 3)
_PAIR_SLOT_B = (1, 2, 3, 3, 2, 2)
_N_PAIRS = len(_PAIR_SLOT_A)


def _rms(x, g):
    ms = jnp.mean(x * x, axis=-1, keepdims=True)
    return x * lax.rsqrt(ms + EPS) * g


def _const_spec(shape):
    nd = len(shape)
    return pl.BlockSpec(shape, lambda *_: (0,) * nd, pipeline_mode=pl.Buffered(1))


def _params(n_axes):
    return pltpu.CompilerParams(dimension_semantics=("arbitrary",) * n_axes,
                                vmem_limit_bytes=VMEM_LIMIT_BYTES)


def _inproj_kernel(x_hbm, g_ref, w_ref, z_ref, h_ref, xs_ref, sem, *, tm, rc):
    i = pl.program_id(0)
    j = pl.program_id(1)
    n_chunks = tm // rc
    more_tiles = i + 1 < pl.num_programs(0)

    def chunk_copy(tile, c):
        return pltpu.make_async_copy(x_hbm.at[pl.ds(pl.multiple_of(tile * tm + c * rc, rc), rc)],
                                     xs_ref.at[c % 2], sem.at[c % 2])

    def normalise(tile, c):
        h_ref[tile % 2, pl.ds(pl.multiple_of(c * rc, rc), rc), :] = _rms(xs_ref[c % 2], g_ref[...]).astype(_BF16)

    @pl.when(jnp.logical_and(i == 0, j == 0))
    def _():
        chunk_copy(0, 0).start()
        for c in range(n_chunks):
            if c + 1 < n_chunks:
                chunk_copy(0, c + 1).start()
            chunk_copy(0, c).wait()
            normalise(0, c)

    @pl.when(jnp.logical_and(more_tiles, jnp.logical_and(j >= 1, j <= n_chunks)))
    def _():
        chunk_copy(i + 1, j - 1).wait()
        normalise(i + 1, j - 1)

    @pl.when(jnp.logical_and(more_tiles, j < n_chunks))
    def _():
        chunk_copy(i + 1, j).start()

    z_ref[...] = jnp.dot(h_ref[i % 2], w_ref[...].astype(_BF16), preferred_element_type=_F32)


def _inproj(x2d, g, w, tm, tn, rc):
    t, d = x2d.shape
    n = w.shape[1]
    kern = functools.partial(_inproj_kernel, tm=tm, rc=rc)
    assert n // tn > tm // rc, "a row tile's chunks are prepared during the column steps of the previous tile"
    return pl.pallas_call(
        kern,
        grid=(t // tm, n // tn),
        in_specs=[pl.BlockSpec(memory_space=pl.ANY),
                  pl.BlockSpec((1, d), lambda i, j: (0, 0)),
                  pl.BlockSpec((d, tn), lambda i, j: (0, j))],
        out_specs=pl.BlockSpec((tm, tn), lambda i, j: (i, j)),
        out_shape=jax.ShapeDtypeStruct((t, n), _F32),
        scratch_shapes=[pltpu.VMEM((2, tm, d), _BF16),
                        pltpu.VMEM((2, rc, d), _F32),
                        pltpu.SemaphoreType.DMA((2,))],
        compiler_params=_params(2),
        name="inproj",
    )(x2d, g, w)


def _scan_pitch(tm):
    p = -(-tm // SUBLANES)
    while p % SUBLANES != 4:
        p += 1
    return p


def _mixer_init(er_ref, ep_ref, a_ref, b_ref, car_ref, tm):
    er_ref[...] = jnp.zeros(er_ref.shape, _F32)
    ep_ref[...] = jnp.zeros(ep_ref.shape, _F32)
    car_ref[...] = jnp.zeros(car_ref.shape, _F32)
    a_ref[:, tm:, :] = jnp.ones((a_ref.shape[0], a_ref.shape[1] - tm, LANES), _F32)
    b_ref[:, tm:, :] = jnp.zeros((b_ref.shape[0], b_ref.shape[1] - tm, LANES), _F32)


def _mixer_tile(z_ref, cw_ref, cb_ref, wa_ref, ba_ref, wx_ref, bx_ref, lam_ref, wp_ref, ps_ref,
                ya_ref, yb_ref, er_ref, ep_ref, a_ref, b_ref, h_ref, car_ref, *, first, t0, tm, pitch):
    c = cb_ref.shape[1]
    n_slab = c // LANES
    hist_r = SUBLANES
    hist_p = 2 * SUBLANES

    er_ref[0:hist_r, :] = jnp.where(first, 0.0, er_ref[tm:tm + hist_r, :])
    ep_ref[0:hist_p, :] = jnp.where(first, 0.0, ep_ref[tm:tm + hist_p, :])
    er_ref[hist_r:hist_r + tm, :] = z_ref[:, 0:c]
    ep_ref[hist_p:hist_p + tm, :] = z_ref[:, 2 * c:3 * c]

    kvec = -LRU_C * jax.nn.softplus(-lam_ref[...])
    nblk = c // MXU_DIM
    for k in range(nblk):
        cs = slice(k * MXU_DIM, (k + 1) * MXU_DIM)
        xc = cb_ref[:, cs] + cw_ref[CONV_WIDTH - 1:CONV_WIDTH, cs] * er_ref[hist_r:hist_r + tm, cs]
        for j in range(1, CONV_WIDTH):
            xc = xc + cw_ref[CONV_WIDTH - 1 - j:CONV_WIDTH - j, cs] * er_ref[hist_r - j:hist_r - j + tm, cs]
        xcb = xc.astype(_BF16)
        r = jax.nn.sigmoid(jnp.dot(xcb, wa_ref[k], preferred_element_type=_F32) + ba_ref[:, cs])
        ig = jax.nn.sigmoid(jnp.dot(xcb, wx_ref[k], preferred_element_type=_F32) + bx_ref[:, cs])
        log_a = r * kvec[:, cs]
        a = jnp.exp(log_a)
        mult = jnp.sqrt(1.0 - a * a)
        bb = mult * ig * xc
        for q in range(MXU_DIM // LANES):
            slab = k * (MXU_DIM // LANES) + q
            a_ref[slab, 0:tm, :] = a[:, q * LANES:(q + 1) * LANES]
            b_ref[slab, 0:tm, :] = bb[:, q * LANES:(q + 1) * LANES]
        yield

    def seg(i):
        return pl.ds(i, SUBLANES, stride=pitch)

    row = lax.broadcasted_iota(jnp.int32, (SUBLANES, LANES), 0)
    for sl in range(n_slab):
        hh = jnp.zeros((SUBLANES, LANES), _F32)
        aa = jnp.ones((SUBLANES, LANES), _F32)
        for i in range(pitch):
            av = a_ref[sl, seg(i), :]
            hh = av * hh + b_ref[sl, seg(i), :]
            aa = av * aa
        d = 1
        while d < SUBLANES:
            hs_ = jnp.where(row >= d, pltpu.roll(hh, d, 0), 0.0)
            as_ = jnp.where(row >= d, pltpu.roll(aa, d, 0), 1.0)
            hh = aa * hs_ + hh
            aa = aa * as_
            d *= 2
        cs = slice(sl * LANES, (sl + 1) * LANES)
        cin = jnp.where(first, 0.0, car_ref[:, cs])
        full = hh + aa * cin
        hv = jnp.where(row >= 1, pltpu.roll(full, 1, 0), cin)
        car_ref[:, cs] = jnp.broadcast_to(full[SUBLANES - 1:SUBLANES, :], (SUBLANES, LANES))
        for i in range(pitch):
            hv = a_ref[sl, seg(i), :] * hv + b_ref[sl, seg(i), :]
            h_ref[sl, seg(i), :] = hv
        g = z_ref[:, c + sl * LANES:c + (sl + 1) * LANES]
        ya_ref[:, cs] = (h_ref[sl, 0:tm, :] * jax.nn.gelu(g)).astype(_BF16)
        yield

    t_idx = (t0 + lax.broadcasted_iota(jnp.int32, (tm, 1), 0) + 1).astype(_F32)
    n_grp = len(POOL_WINDOWS)
    gd = c // n_grp
    for gi, w in enumerate(POOL_WINDOWS):
        cs = slice(gi * gd, (gi + 1) * gd)
        e = ep_ref[:, cs]
        acc = e
        d = 1
        while d < w:
            acc = acc + pltpu.roll(acc, d, 0)
            d *= 2
        xt = e[hist_p:, :]
        cnt = jnp.minimum(t_idx, float(w))
        dd = acc[hist_p:, :] / cnt - xt
        yb = jnp.dot(dd.astype(_BF16), wp_ref[gi], preferred_element_type=_F32) * ps_ref[:, cs]
        yb_ref[:, cs] = yb.astype(_BF16)
        yield


def _combine_tile(ya_ref, yb_ref, ga0_ref, ga1_ref, gb0_ref, gb1_ref, x_ref,
                  wa_ref, wb_ref, wo_ref, n2_ref, wr_ref, br_ref,
                  xe_ref, cnt_ref, meta_ref, cnt_s, *, live, tm, n_groups, epg):
    d = x_ref.shape[1]
    half = d // 2
    piece = MXU_DIM * 2
    ya = ya_ref[...]
    yb = yb_ref[...]
    us = []
    for q in range(d // piece):
        cs = slice(q * piece, (q + 1) * piece)
        ga_ref, gb_ref = (ga0_ref, gb0_ref) if q * piece < half else (ga1_ref, gb1_ref)
        gs = slice((q * piece) % half, (q * piece) % half + piece)
        pa = jnp.dot(ya, wa_ref[:, cs], preferred_element_type=_F32)
        yield
        pb = jnp.dot(yb, wb_ref[:, cs], preferred_element_type=_F32)
        us.append((jax.nn.sigmoid(ga_ref[:, gs]) * pa + jax.nn.sigmoid(gb_ref[:, gs]) * pb).astype(_BF16))
        yield
    u = jnp.concatenate(us, axis=1)
    for q in range(d // piece):
        cs = slice(q * piece, (q + 1) * piece)
        xe_ref[:, cs] = x_ref[:, cs] + jnp.dot(u, wo_ref[:, cs], preferred_element_type=_F32)
        yield
    x1 = xe_ref[:, 0:d]

    ht = _rms(x1, n2_ref[...]).astype(_BF16)
    logits = jnp.dot(ht, wr_ref[...], preferred_element_type=_F32) + br_ref[...]

    lane = lax.broadcasted_iota(jnp.int32, (tm, LANES), 1).astype(_F32)
    ninf = -jnp.inf
    big = float(LANES)

    def first_argmax(v):
        m = jnp.max(v, axis=-1, keepdims=True)
        return m, jnp.min(jnp.where(v == m, lane, big), axis=-1, keepdims=True)

    is_g = lane < float(n_groups)
    gmax, gidx = first_argmax(jnp.where(is_g, logits, ninf))
    g_w = 1.0 / jnp.sum(jnp.where(is_g, jnp.exp(logits - gmax), 0.0), axis=-1, keepdims=True)
    lo_lane = float(n_groups) + float(epg) * gidx
    in_grp = (lane >= lo_lane) & (lane < lo_lane + float(epg))
    le = jnp.where(in_grp, logits, ninf)
    m1, i1 = first_argmax(le)
    m2, i2 = first_argmax(jnp.where(lane == i1, ninf, le))
    e21 = jnp.exp(m2 - m1)
    w1 = g_w / (1.0 + e21)
    w2 = w1 * e21
    e1 = i1 - lo_lane
    e2 = i2 - lo_lane
    lo = jnp.minimum(e1, e2)
    hi = jnp.maximum(e1, e2)
    w_lo = jnp.where(e1 < e2, w1, w2)
    w_hi = jnp.where(e1 < e2, w2, w1)
    pair = jnp.where(lo == 0.0, hi - 1.0, jnp.where(lo == 1.0, 6.0 - hi, 5.0))
    swap = pair == 5.0
    w_a = jnp.where(swap, w_hi, w_lo)
    w_b = jnp.where(swap, w_lo, w_hi)
    bucket = float(_N_PAIRS) * gidx + pair

    onehot = lane == bucket
    oh_bf = jnp.where(onehot, 1.0, 0.0).astype(_BF16)
    rr = lax.broadcasted_iota(jnp.int32, (tm, tm), 0)
    cc = lax.broadcasted_iota(jnp.int32, (tm, tm), 1)
    tri = jnp.where(cc < rr, 1.0, 0.0).astype(_BF16)
    before = jnp.dot(tri, oh_bf, preferred_element_type=_F32) + cnt_s[...]
    rank = jnp.sum(jnp.where(onehot, before, 0.0), axis=-1, keepdims=True)
    cnt_s[...] = cnt_s[...] + jnp.where(live, jnp.sum(jnp.where(onehot, 1.0, 0.0), axis=0, keepdims=True), 0.0)
    cnt_ref[...] = cnt_s[...]

    info = jnp.where(lane == 0.0, bucket,
                     jnp.where(lane == 1.0, rank,
                               jnp.where(lane == 2.0, w_a, jnp.where(lane == 3.0, w_b, 0.0))))
    xe_ref[:, d:d + LANES] = info
    meta_ref[...] = jnp.transpose(info)[0:SUBLANES, :].astype(jnp.int32)


def _mix_combine_kernel(*refs, tm, pitch, ns, n_tiles, n_groups, epg):
    mix_in, cmb_in = refs[0:10], refs[10:21]
    xe_ref, cnt_ref, meta_ref = refs[21:24]
    ynew_ref, yold_ref, er_ref, ep_ref, a_ref, b_ref, h_ref, car_ref, cnt_s = refs[24:]
    s = pl.program_id(0)

    @pl.when(s == 0)
    def _():
        _mixer_init(er_ref, ep_ref, a_ref, b_ref, car_ref, tm)
        cnt_s[...] = jnp.zeros(cnt_s.shape, _F32)
        yold_ref[...] = jnp.zeros(yold_ref.shape, _BF16)

    seq_pos = lax.rem(jnp.minimum(s, n_tiles - 1), ns)
    mixer = _mixer_tile(*mix_in, ynew_ref.at[0], ynew_ref.at[1], er_ref, ep_ref, a_ref, b_ref, h_ref,
                        car_ref, first=seq_pos == 0, t0=seq_pos * tm, tm=tm, pitch=pitch)
    combine = _combine_tile(yold_ref.at[0], yold_ref.at[1], *cmb_in, xe_ref, cnt_ref, meta_ref,
                            cnt_s, live=s >= 1, tm=tm, n_groups=n_groups, epg=epg)
    stages = [mixer, combine]
    while stages:
        for stage in list(stages):
            if next(stage, stages) is stages:
                stages.remove(stage)
    yold_ref[...] = ynew_ref[...]


def _mix_combine(z, x2d, seq, mixer_w, wa, wb, wo, n2, wr, br, tm, n_groups, epg):
    t, d = x2d.shape
    c = wa.shape[0]
    half = d // 2
    n_tiles = t // tm
    ns = seq // tm
    pitch = _scan_pitch(tm)
    n_slab = c // LANES
    off = (3 * c) // half
    kern = functools.partial(_mix_combine_kernel, tm=tm, pitch=pitch, ns=ns, n_tiles=n_tiles,
                             n_groups=n_groups, epg=epg)

    def cur(s):
        return jnp.minimum(s, n_tiles - 1)

    def prev(s):
        return jnp.maximum(s - 1, 0)

    def zspec(j):
        return pl.BlockSpec((tm, half), lambda s, j=j: (prev(s), off + j))

    return pl.pallas_call(
        kern,
        grid=(n_tiles + 1,),
        in_specs=[pl.BlockSpec((tm, 3 * c), lambda s: (cur(s), 0))]
                 + [_const_spec(w.shape) for w in mixer_w]
                 + [zspec(0), zspec(1), zspec(2), zspec(3),
                    pl.BlockSpec((tm, d), lambda s: (prev(s), 0)),
                    _const_spec(wa.shape), _const_spec(wb.shape), _const_spec(wo.shape),
                    _const_spec(n2.shape), _const_spec(wr.shape), _const_spec(br.shape)],
        out_specs=[pl.BlockSpec((tm, d + LANES), lambda s: (prev(s), 0)),
                   pl.BlockSpec((1, LANES), lambda s: (0, 0)),
                   pl.BlockSpec((SUBLANES, tm), lambda s: (0, prev(s)))],
        out_shape=[jax.ShapeDtypeStruct((t, d + LANES), _F32),
                   jax.ShapeDtypeStruct((1, LANES), _F32),
                   jax.ShapeDtypeStruct((SUBLANES, t), jnp.int32)],
        scratch_shapes=[pltpu.VMEM((2, tm, c), _BF16),
                        pltpu.VMEM((2, tm, c), _BF16),
                        pltpu.VMEM((SUBLANES + tm, c), _F32),
                        pltpu.VMEM((2 * SUBLANES + tm, c), _F32),
                        pltpu.VMEM((n_slab, SUBLANES * pitch, LANES), _F32),
                        pltpu.VMEM((n_slab, SUBLANES * pitch, LANES), _F32),
                        pltpu.VMEM((n_slab, SUBLANES * pitch, LANES), _F32),
                        pltpu.VMEM((SUBLANES, c), _F32),
                        pltpu.VMEM((1, LANES), _F32)],
        compiler_params=_params(1),
        name="mix_combine",
    )(z, *mixer_w, z, z, z, z, x2d, wa, wb, wo, n2, wr, br)


def _bucket_starts(cnt_ref, start_ref, n_buckets, tm, on_tile=None):
    def bucket_body(b, tile_idx):
        start_ref[b] = tile_idx * tm
        nt = lax.div(cnt_ref[b] + (tm - 1), tm)
        if on_tile is not None:
            lax.fori_loop(0, nt, lambda k, carry: (on_tile(tile_idx + k, b), carry)[1], 0)
        return tile_idx + nt

    return lax.fori_loop(0, n_buckets, bucket_body, 0)


def _row_gather_start(row_of, src_hbm, buf, slot, sem, n_rows, unrolled, width=None):
    width = src_hbm.shape[1] if width is None else width

    def start(r):
        row = row_of(r)
        pltpu.make_async_copy(src_hbm.at[pl.ds(row, 1), pl.ds(0, width)],
                              buf.at[slot, pl.ds(r, 1), pl.ds(0, width)],
                              sem.at[slot]).start(priority=1)

    if unrolled:
        for r in range(n_rows):
            start(r)
    else:
        def body(r, _):
            start(r)
            return 0
        lax.fori_loop(0, n_rows, body, 0, unroll=8)


def _row_gather_wait(src_hbm, buf, slot, sem, n_rows, width=None):
    width = src_hbm.shape[1] if width is None else width
    pltpu.make_async_copy(src_hbm.at[pl.ds(0, n_rows), pl.ds(0, width)],
                          buf.at[slot, :, pl.ds(0, width)], sem.at[slot]).wait()


MOE_WEIGHT_SLOTS = 6
MOE_PARTS = 4
MOE_CHUNKS = 3 * MOE_PARTS
MOE_PUMP = 3
MOE_DEPTH = 4


def _moe_weight_plan(n_groups, epg):
    first_use = {e: min(p for p in range(_N_PAIRS) if e in (_PAIR_SLOT_A[p], _PAIR_SLOT_B[p])) for e in range(epg)}
    last_use = {e: max(p for p in range(_N_PAIRS) if e in (_PAIR_SLOT_A[p], _PAIR_SLOT_B[p])) for e in range(epg)}
    free_after = [-1] * MOE_WEIGHT_SLOTS
    loads, slot_of = [], {}
    for g in range(n_groups):
        for e in sorted(range(epg), key=lambda e: (first_use[e], e)):
            needed_by = g * _N_PAIRS + first_use[e]
            s = min(range(MOE_WEIGHT_SLOTS), key=lambda s: (free_after[s], s))
            assert free_after[s] < needed_by
            loads.append((g * epg + e, s, free_after[s], needed_by))
            free_after[s] = g * _N_PAIRS + last_use[e]
            slot_of[(g, e)] = s
    n_buckets = n_groups * _N_PAIRS
    need = [sum(1 for l in loads if l[3] <= b) for b in range(n_buckets)]
    allow = [sum(1 for l in loads if l[2] < b) for b in range(n_buckets)]
    slot_a = [slot_of[(b // _N_PAIRS, _PAIR_SLOT_A[b % _N_PAIRS])] for b in range(n_buckets)]
    slot_b = [slot_of[(b // _N_PAIRS, _PAIR_SLOT_B[b % _N_PAIRS])] for b in range(n_buckets)]
    return [l[0] for l in loads], [l[1] for l in loads], need, allow, slot_a, slot_b


def _moe_kernel(tokb_ref, tokr_ref, cntb_ref, need_ref, allow_ref, sa_ref, sb_ref, le_ref, ls_ref,
                xe_hbm, n2_ref, wg_hbm, wu_hbm, wd_hbm,
                x2s_ref, xbuf, gsem, wg_s, wu_s, wd_s, st_gu, st_d, wsem, cnt, src_s, tbk_s, start_s, tot_s,
                *, tm):
    j = pl.program_id(0)
    slot = j % 2
    d = x2s_ref.shape[1]
    t_rows = xe_hbm.shape[0]
    n_slots = src_s.shape[0]
    n_buckets = start_s.shape[0]

    def src_row(tile):
        return lambda r: src_s[tile * tm + r]

    @pl.when(j == 0)
    def _():
        def set_tile(tile, b):
            tbk_s[tile] = b

        total = _bucket_starts(cntb_ref, start_s, n_buckets, tm, on_tile=set_tile)
        tot_s[0] = total

        def fill(lo, hi):
            def body(q, carry):
                src_s[q] = lax.rem(q, t_rows)
                return carry
            lax.fori_loop(lo, hi, body, 0)

        def fill_bucket(b, carry):
            nxt = jnp.where(b + 1 < n_buckets, start_s[jnp.minimum(b + 1, n_buckets - 1)], total * tm)
            fill(start_s[b] + cntb_ref[b], nxt)
            return carry
        lax.fori_loop(0, n_buckets, fill_bucket, 0)
        fill(total * tm, (total + 1) * tm)

        def place(tok, carry):
            src_s[start_s[tokb_ref[tok]] + tokr_ref[tok]] = tok
            return carry
        lax.fori_loop(0, t_rows, place, 0, unroll=16)

    n_valid = tot_s[0]
    rows_gu = wg_hbm.shape[1] // MOE_PARTS
    rows_d = wd_hbm.shape[1] // MOE_PARTS

    def chunk_dma(c, kind):
        load = c // MOE_CHUNKS
        part = c % MOE_PARTS
        e = le_ref[load]
        if kind == 2:
            return pltpu.make_async_copy(wd_hbm.at[e, pl.ds(part * rows_d, rows_d)], st_d.at[c % MOE_DEPTH],
                                         wsem.at[c % MOE_DEPTH])
        src = wg_hbm if kind == 0 else wu_hbm
        return pltpu.make_async_copy(src.at[e, pl.ds(part * rows_gu, rows_gu)], st_gu.at[c % MOE_DEPTH],
                                     wsem.at[c % MOE_DEPTH])

    def for_kind(c, fn):
        kind = (c % MOE_CHUNKS) // MOE_PARTS
        for k in range(3):
            @pl.when(kind == k)
            def _(k=k):
                fn(k)

    def issue_one(allowed):
        @pl.when(cnt[0] < jnp.minimum(allowed, cnt[1] + MOE_DEPTH))
        def _():
            c = cnt[0]
            for_kind(c, lambda k: chunk_dma(c, k).start())
            cnt[0] = c + 1

    def retire():
        c = cnt[1]
        s = ls_ref[c // MOE_CHUNKS]
        part = c % MOE_PARTS

        def finish(k):
            chunk_dma(c, k).wait()
            if k == 2:
                wd_s[s, pl.ds(part * rows_d, rows_d), :] = st_d[c % MOE_DEPTH].astype(_BF16)
            else:
                dst = wg_s if k == 0 else wu_s
                dst[s, pl.ds(part * rows_gu, rows_gu), :] = st_gu[c % MOE_DEPTH].astype(_BF16)

        for_kind(c, finish)
        cnt[1] = c + 1

    def pump(required, allowed, extra):
        n_iter = jnp.maximum(required - cnt[1], jnp.minimum(extra, allowed - cnt[1]))
        lax.fori_loop(0, MOE_DEPTH, lambda _, carry: (issue_one(allowed), carry)[1], 0)

        def body(_, carry):
            retire()
            issue_one(allowed)
            return carry

        lax.fori_loop(0, jnp.maximum(n_iter, 0), body, 0)

    @pl.when(j == 0)
    def _():
        cnt[0] = 0
        cnt[1] = 0
        _row_gather_start(src_row(0), xe_hbm, xbuf, 0, gsem, tm, unrolled=False)

    @pl.when(j < n_valid)
    def _():
        b = tbk_s[j]
        pump(need_ref[b] * MOE_CHUNKS, allow_ref[b] * MOE_CHUNKS, MOE_PUMP)
        s_a = sa_ref[b]
        s_b = sb_ref[b]
        _row_gather_wait(xe_hbm, xbuf, slot, gsem, tm)
        _row_gather_start(src_row(j + 1), xe_hbm, xbuf, 1 - slot, gsem, tm, unrolled=True)
        xe = xbuf[slot]
        x1 = xe[:, 0:d]
        w_a = xe[:, d + 2:d + 3]
        w_b = xe[:, d + 3:d + 4]
        ht = _rms(x1, n2_ref[...]).astype(_BF16)

        def expert(s, wgt):
            hg = jnp.dot(ht, wg_s[s], preferred_element_type=_F32)
            hu = jnp.dot(ht, wu_s[s], preferred_element_type=_F32)
            return (hg * jax.nn.sigmoid(hg) * hu * wgt).astype(_BF16)

        y = jnp.dot(expert(s_a, w_a), wd_s[s_a], preferred_element_type=_F32)
        y = y + jnp.dot(expert(s_b, w_b), wd_s[s_b], preferred_element_type=_F32)
        x2s_ref[...] = x1 + y

    @pl.when(j >= n_valid)
    def _():
        x2s_ref[...] = jnp.zeros(x2s_ref.shape, _F32)

        @pl.when(j == n_valid)
        def _():
            _row_gather_wait(xe_hbm, xbuf, slot, gsem, tm)
            lax.fori_loop(0, cnt[0] - cnt[1], lambda _, carry: (retire(), carry)[1], 0)


def _moe(tok_bucket, tok_rank, bucket_cnt, xe, n2, wg, wu, wd, tm, n_groups, epg):
    n_buckets = bucket_cnt.shape[0]
    n_tiles = xe.shape[0] // tm + n_buckets
    d = wg.shape[1]
    f = wg.shape[2]
    kern = functools.partial(_moe_kernel, tm=tm)
    le, ls, need, allow, slot_a, slot_b = (jnp.asarray(v, jnp.int32) for v in _moe_weight_plan(n_groups, epg))

    grid_spec = pltpu.PrefetchScalarGridSpec(
        num_scalar_prefetch=9,
        grid=(n_tiles,),
        in_specs=[pl.BlockSpec(memory_space=pl.ANY),
                  pl.BlockSpec((1, d), lambda j, *_: (0, 0)),
                  pl.BlockSpec(memory_space=pl.ANY),
                  pl.BlockSpec(memory_space=pl.ANY),
                  pl.BlockSpec(memory_space=pl.ANY)],
        out_specs=pl.BlockSpec((tm, d), lambda j, *_: (j, 0)),
        scratch_shapes=[pltpu.VMEM((2, tm, d + LANES), _F32),
                        pltpu.SemaphoreType.DMA((2,)),
                        pltpu.VMEM((MOE_WEIGHT_SLOTS, d, f), _BF16),
                        pltpu.VMEM((MOE_WEIGHT_SLOTS, d, f), _BF16),
                        pltpu.VMEM((MOE_WEIGHT_SLOTS, f, d), _BF16),
                        pltpu.VMEM((MOE_DEPTH, d // MOE_PARTS, f), _F32),
                        pltpu.VMEM((MOE_DEPTH, f // MOE_PARTS, d), _F32),
                        pltpu.SemaphoreType.DMA((MOE_DEPTH,)),
                        pltpu.SMEM((2,), jnp.int32),
                        pltpu.SMEM((n_tiles * tm,), jnp.int32),
                        pltpu.SMEM((n_tiles,), jnp.int32),
                        pltpu.SMEM((n_buckets,), jnp.int32),
                        pltpu.SMEM((1,), jnp.int32)],
    )
    return pl.pallas_call(
        kern,
        grid_spec=grid_spec,
        out_shape=jax.ShapeDtypeStruct((n_tiles * tm, d), _F32),
        compiler_params=_params(1),
        name="moe",
    )(tok_bucket, tok_rank, bucket_cnt, need, allow, slot_a, slot_b, le, ls, xe, n2, wg, wu, wd)


def _ple_kernel(tokb_ref, tokr_ref, cntb_ref, x2s_hbm, p_ref, ng_ref, wg_ref, wp_ref, nf_ref, o_ref,
                xbuf, gsem, start_s, *, tm, n_steps, tm_sorted):
    i = pl.program_id(0)
    slot = i % 2

    def sorted_row(tile):
        return lambda r: start_s[tokb_ref[tile * tm + r]] + tokr_ref[tile * tm + r]

    @pl.when(i == 0)
    def _():
        _bucket_starts(cntb_ref, start_s, start_s.shape[0], tm_sorted)
        _row_gather_start(sorted_row(0), x2s_hbm, xbuf, 0, gsem, tm, unrolled=False)

    _row_gather_wait(x2s_hbm, xbuf, slot, gsem, tm)
    nxt = jnp.where(i + 1 == n_steps, 0, i + 1)
    _row_gather_start(sorted_row(nxt), x2s_hbm, xbuf, 1 - slot, gsem, tm, unrolled=True)
    x2 = xbuf[slot]
    g = jax.nn.sigmoid(jnp.dot(_rms(x2, ng_ref[...]).astype(_BF16), wg_ref[...],
                               preferred_element_type=_F32))
    e = jnp.dot(p_ref[...].astype(_BF16), wp_ref[...], preferred_element_type=_F32)
    o_ref[...] = _rms(x2 + g * e, nf_ref[...])

    @pl.when(i == n_steps - 1)
    def _():
        _row_gather_wait(x2s_hbm, xbuf, 1 - slot, gsem, tm)


def _ple(tok_bucket, tok_rank, bucket_cnt, x2s, p2d, ng, wg, wp, nf, tm, t, tm_sorted):
    d = x2s.shape[1]
    pd = p2d.shape[1]
    n_steps = t // tm
    kern = functools.partial(_ple_kernel, tm=tm, n_steps=n_steps, tm_sorted=tm_sorted)

    def cspec(shape):
        nd = len(shape)
        return pl.BlockSpec(shape, lambda i, *_: (0,) * nd, pipeline_mode=pl.Buffered(1))

    grid_spec = pltpu.PrefetchScalarGridSpec(
        num_scalar_prefetch=3,
        grid=(n_steps,),
        in_specs=[pl.BlockSpec(memory_space=pl.ANY),
                  pl.BlockSpec((tm, pd), lambda i, *_: (i, 0)),
                  cspec(ng.shape), cspec(wg.shape), cspec(wp.shape), cspec(nf.shape)],
        out_specs=pl.BlockSpec((tm, d), lambda i, *_: (i, 0)),
        scratch_shapes=[pltpu.VMEM((2, tm, d), _F32),
                        pltpu.SemaphoreType.DMA((2,)),
                        pltpu.SMEM((bucket_cnt.shape[0],), jnp.int32)],
    )
    return pl.pallas_call(
        kern,
        grid_spec=grid_spec,
        out_shape=jax.ShapeDtypeStruct((t, d), _F32),
        compiler_params=_params(1),
        name="ple",
    )(tok_bucket, tok_rank, bucket_cnt, x2s, p2d, ng, wg, wp, nf)


def _block_diag(w, per_block):
    h, hd, _ = w.shape
    nb = h // per_block
    w4 = w.reshape(nb, per_block, hd, hd)
    rows = [jnp.pad(w4[:, p], ((0, 0), (0, 0), (p * hd, (per_block - 1 - p) * hd))) for p in range(per_block)]
    return jnp.concatenate(rows, axis=1)


def _layer(x2d, p2d, bsz, seq, norm1_g, w_in, conv_w, conv_b, w_rg_a, b_rg_a, w_rg_x, b_rg_x, lru_lambda,
           w_pool, pool_scale, w_branch_a, w_branch_b, w_out, norm2_g, w_router_group, b_router_group,
           w_router_expert, b_router_expert, w_e_gate, w_e_up, w_e_down, norm_ple_g, w_ple_gate,
           w_ple_proj, out_norm_g):
    t, d = x2d.shape
    c = conv_b.shape[0]
    heads, hd, _ = w_rg_a.shape
    n_groups = w_router_group.shape[1]
    n_exp = w_router_expert.shape[1]
    epg = n_exp // n_groups
    assert epg == 4 and TOP_K == 2 and hd * (MXU_DIM // hd) == MXU_DIM
    assert w_pool.shape[0] == len(POOL_WINDOWS) and w_pool.shape[1] == MXU_DIM

    tm_in, tn_in, rc_in = min(2048, t), 512, min(256, t)
    tm_mix = min(256, seq)
    tm_cmb = min(256, t)
    tm_moe = min(128, t)
    tm_ple = min(256, t)

    row = lambda v: v.reshape(1, -1).astype(_F32)
    per_block = MXU_DIM // hd

    z = _inproj(x2d, row(norm1_g), w_in, tm_in, tn_in, rc_in)
    mixer_w = (conv_w.reshape(CONV_WIDTH, c), row(conv_b),
               _block_diag(w_rg_a, per_block).astype(_BF16), row(b_rg_a),
               _block_diag(w_rg_x, per_block).astype(_BF16), row(b_rg_x),
               row(lru_lambda), w_pool.astype(_BF16), row(pool_scale))

    n_rt = n_groups + n_exp
    wr = jnp.pad(jnp.concatenate([w_router_group, w_router_expert], axis=1),
                 ((0, 0), (0, LANES - n_rt))).astype(_BF16)
    br = jnp.pad(jnp.concatenate([b_router_group, b_router_expert]), (0, LANES - n_rt)).reshape(1, LANES)
    xe, counts, meta = _mix_combine(z, x2d, seq, mixer_w, w_branch_a.astype(_BF16), w_branch_b.astype(_BF16),
                                    w_out.astype(_BF16), row(norm2_g), wr, br, tm_cmb, n_groups, epg)

    n_buckets = n_groups * _N_PAIRS
    tok_bucket, tok_rank = meta[0], meta[1]
    bucket_cnt = counts[0, :n_buckets].astype(jnp.int32)

    x2s = _moe(tok_bucket, tok_rank, bucket_cnt, xe, row(norm2_g), w_e_gate, w_e_up, w_e_down,
               tm_moe, n_groups, epg)
    return _ple(tok_bucket, tok_rank, bucket_cnt, x2s, p2d, row(norm_ple_g), w_ple_gate.astype(_BF16),
                w_ple_proj.astype(_BF16), row(out_norm_g), tm_ple, t, tm_moe)


def kernel(x, p, norm1_g, w_in, conv_w, conv_b, w_rg_a, b_rg_a, w_rg_x, b_rg_x, lru_lambda, w_pool, pool_scale, w_branch_a, w_branch_b, w_out, norm2_g, w_router_group, b_router_group, w_router_expert, b_router_expert, w_e_gate, w_e_up, w_e_down, norm_ple_g, w_ple_gate, w_ple_proj, final_norm_g):
    bsz, seq, d = x.shape
    depth = p.shape[0]
    assert depth == 1, "the final RMSNorm is fused into the last layer's embedding kernel"
    out = _layer(x.reshape(bsz * seq, d), p[0].reshape(bsz * seq, -1), bsz, seq,
                 norm1_g[0], w_in[0], conv_w[0], conv_b[0], w_rg_a[0], b_rg_a[0], w_rg_x[0], b_rg_x[0],
                 lru_lambda[0], w_pool[0], pool_scale[0], w_branch_a[0], w_branch_b[0], w_out[0],
                 norm2_g[0], w_router_group[0], b_router_group[0], w_router_expert[0],
                 b_router_expert[0], w_e_gate[0], w_e_up[0], w_e_down[0], norm_ple_g[0],
                 w_ple_gate[0], w_ple_proj[0], final_norm_g)
    return out.reshape(bsz, seq, d)
```

```python
import functools

import jax
import jax.numpy as jnp
from jax import lax
from jax.experimental import pallas as pl
from jax.experimental.pallas import tpu as pltpu

EPS = 1e-6
LRU_C = 8.0
CONV_WIDTH = 4
POOL_WINDOWS = (2, 4, 8, 16)
TOP_K = 2
SQRT_GUARD = 1e-30

LANES = 128
SUBLANES = 8
MXU_DIM = 256
VMEM_LIMIT_BYTES = 56 * 1024 * 1024

_BF16 = jnp.bfloat16
_F32 = jnp.float32

_PAIR_SLOT_A = (0, 0, 0, 1, 1, 3)
_PAIR_SLOT_B = (1, 2, 3, 3, 2, 2)
_N_PAIRS = len(_PAIR_SLOT_A)


def _sigmoid(x):
    return 0.5 * jnp.tanh(0.5 * x) + 0.5


def _rms(x, g):
    ms = jnp.mean(x * x, axis=-1, keepdims=True)
    return x * lax.rsqrt(ms + EPS) * g


def _const_spec(shape):
    nd = len(shape)
    return pl.BlockSpec(shape, lambda *_: (0,) * nd, pipeline_mode=pl.Buffered(1))


def _params(n_axes, flags=None):
    return pltpu.CompilerParams(dimension_semantics=("arbitrary",) * n_axes,
                                vmem_limit_bytes=VMEM_LIMIT_BYTES, flags=flags)


def _inproj_kernel(x_hbm, g_ref, w_ref, z_ref, h_ref, xs_ref, sem, *, tm, rc):
    i = pl.program_id(0)
    j = pl.program_id(1)
    n_chunks = tm // rc
    more_tiles = i + 1 < pl.num_programs(0)

    def chunk_copy(tile, c):
        return pltpu.make_async_copy(x_hbm.at[pl.ds(pl.multiple_of(tile * tm + c * rc, rc), rc)],
                                     xs_ref.at[c % 2], sem.at[c % 2])

    def normalise(tile, c):
        h_ref[tile % 2, pl.ds(pl.multiple_of(c * rc, rc), rc), :] = _rms(xs_ref[c % 2], g_ref[...]).astype(_BF16)

    @pl.when(jnp.logical_and(i == 0, j == 0))
    def _():
        chunk_copy(0, 0).start()
        for c in range(n_chunks):
            if c + 1 < n_chunks:
                chunk_copy(0, c + 1).start()
            chunk_copy(0, c).wait()
            normalise(0, c)

    @pl.when(jnp.logical_and(more_tiles, jnp.logical_and(j >= 1, j <= n_chunks)))
    def _():
        chunk_copy(i + 1, j - 1).wait()
        normalise(i + 1, j - 1)

    @pl.when(jnp.logical_and(more_tiles, j < n_chunks))
    def _():
        chunk_copy(i + 1, j).start()

    z_ref[...] = jnp.dot(h_ref[i % 2], w_ref[...].astype(_BF16), preferred_element_type=_F32)


def _inproj(x2d, g, w, tm, tn, rc):
    t, d = x2d.shape
    n = w.shape[1]
    kern = functools.partial(_inproj_kernel, tm=tm, rc=rc)
    assert n // tn > tm // rc, "a row tile's chunks are prepared during the column steps of the previous tile"
    return pl.pallas_call(
        kern,
        grid=(t // tm, n // tn),
        in_specs=[pl.BlockSpec(memory_space=pl.ANY),
                  pl.BlockSpec((1, d), lambda i, j: (0, 0)),
                  pl.BlockSpec((d, tn), lambda i, j: (0, j))],
        out_specs=pl.BlockSpec((tm, tn), lambda i, j: (i, j)),
        out_shape=jax.ShapeDtypeStruct((t, n), _F32),
        scratch_shapes=[pltpu.VMEM((2, tm, d), _BF16),
                        pltpu.VMEM((2, rc, d), _F32),
                        pltpu.SemaphoreType.DMA((2,))],
        compiler_params=_params(2),
        name="inproj",
    )(x2d, g, w)


def _scan_pitch(tm):
    p = -(-tm // SUBLANES)
    while p % SUBLANES != 4:
        p += 1
    return p


def _mixer_init(er_ref, ep_ref, a_ref, b_ref, car_ref, tm):
    er_ref[...] = jnp.zeros(er_ref.shape, _F32)
    ep_ref[...] = jnp.zeros(ep_ref.shape, _F32)
    car_ref[...] = jnp.zeros(car_ref.shape, _F32)
    a_ref[:, tm:, :] = jnp.ones((a_ref.shape[0], a_ref.shape[1] - tm, LANES), _F32)
    b_ref[:, tm:, :] = jnp.zeros((b_ref.shape[0], b_ref.shape[1] - tm, LANES), _F32)


def _mixer_tile(z_ref, cw_ref, cb_ref, wa_ref, ba_ref, wx_ref, bx_ref, lam_ref, wp_ref, ps_ref,
                ya_ref, yb_ref, er_ref, ep_ref, a_ref, b_ref, h_ref, car_ref, *, first, t0, tm, pitch):
    c = cb_ref.shape[1]
    n_slab = c // LANES
    hist_r = SUBLANES
    hist_p = 2 * SUBLANES

    er_ref[0:hist_r, :] = jnp.where(first, 0.0, er_ref[tm:tm + hist_r, :])
    ep_ref[0:hist_p, :] = jnp.where(first, 0.0, ep_ref[tm:tm + hist_p, :])
    er_ref[hist_r:hist_r + tm, :] = z_ref[:, 0:c]
    ep_ref[hist_p:hist_p + tm, :] = z_ref[:, 2 * c:3 * c]

    kvec = -LRU_C * jax.nn.softplus(-lam_ref[...])
    nblk = c // MXU_DIM
    for k in range(nblk):
        cs = slice(k * MXU_DIM, (k + 1) * MXU_DIM)
        xc = cb_ref[:, cs] + cw_ref[CONV_WIDTH - 1:CONV_WIDTH, cs] * er_ref[hist_r:hist_r + tm, cs]
        for j in range(1, CONV_WIDTH):
            xc = xc + cw_ref[CONV_WIDTH - 1 - j:CONV_WIDTH - j, cs] * er_ref[hist_r - j:hist_r - j + tm, cs]
        xcb = xc.astype(_BF16)
        r = _sigmoid(jnp.dot(xcb, wa_ref[k], preferred_element_type=_F32) + ba_ref[:, cs])
        ig = _sigmoid(jnp.dot(xcb, wx_ref[k], preferred_element_type=_F32) + bx_ref[:, cs])
        log_a = r * kvec[:, cs]
        a = jnp.exp(log_a)
        v = 1.0 - a * a
        mult = v * lax.rsqrt(jnp.maximum(v, SQRT_GUARD))
        bb = mult * ig * xc
        for q in range(MXU_DIM // LANES):
            slab = k * (MXU_DIM // LANES) + q
            a_ref[slab, 0:tm, :] = a[:, q * LANES:(q + 1) * LANES]
            b_ref[slab, 0:tm, :] = bb[:, q * LANES:(q + 1) * LANES]
        yield

    def seg(i):
        return pl.ds(i, SUBLANES, stride=pitch)

    row = lax.broadcasted_iota(jnp.int32, (SUBLANES, LANES), 0)
    for sl in range(n_slab):
        hh = jnp.zeros((SUBLANES, LANES), _F32)
        aa = jnp.ones((SUBLANES, LANES), _F32)
        for i in range(pitch):
            av = a_ref[sl, seg(i), :]
            hh = av * hh + b_ref[sl, seg(i), :]
            aa = av * aa
        d = 1
        while d < SUBLANES:
            hs_ = jnp.where(row >= d, pltpu.roll(hh, d, 0), 0.0)
            as_ = jnp.where(row >= d, pltpu.roll(aa, d, 0), 1.0)
            hh = aa * hs_ + hh
            aa = aa * as_
            d *= 2
        cs = slice(sl * LANES, (sl + 1) * LANES)
        cin = jnp.where(first, 0.0, car_ref[:, cs])
        full = hh + aa * cin
        hv = jnp.where(row >= 1, pltpu.roll(full, 1, 0), cin)
        car_ref[:, cs] = jnp.broadcast_to(full[SUBLANES - 1:SUBLANES, :], (SUBLANES, LANES))
        for i in range(pitch):
            hv = a_ref[sl, seg(i), :] * hv + b_ref[sl, seg(i), :]
            h_ref[sl, seg(i), :] = hv
        g = z_ref[:, c + sl * LANES:c + (sl + 1) * LANES]
        ya_ref[:, cs] = (h_ref[sl, 0:tm, :] * jax.nn.gelu(g)).astype(_BF16)
        yield

    t_idx = (t0 + lax.broadcasted_iota(jnp.int32, (tm, 1), 0) + 1).astype(_F32)
    n_grp = len(POOL_WINDOWS)
    gd = c // n_grp
    for gi, w in enumerate(POOL_WINDOWS):
        cs = slice(gi * gd, (gi + 1) * gd)
        e = ep_ref[:, cs]
        acc = e
        d = 1
        while d < w:
            acc = acc + pltpu.roll(acc, d, 0)
            d *= 2
        xt = e[hist_p:, :]
        cnt = jnp.minimum(t_idx, float(w))
        dd = acc[hist_p:, :] / cnt - xt
        yb = jnp.dot(dd.astype(_BF16), wp_ref[gi], preferred_element_type=_F32) * ps_ref[:, cs]
        yb_ref[:, cs] = yb.astype(_BF16)
        yield


def _combine_tile(ya_ref, yb_ref, ga0_ref, ga1_ref, gb0_ref, gb1_ref, x_ref,
                  wa_ref, wb_ref, wo_ref, n2_ref, wr_ref, br_ref,
                  xe_ref, cnt_ref, meta_ref, cnt_s, *, live, tm, n_groups, epg):
    d = x_ref.shape[1]
    half = d // 2
    piece = MXU_DIM * 2
    ya = ya_ref[...]
    yb = yb_ref[...]
    us = []
    for q in range(d // piece):
        cs = slice(q * piece, (q + 1) * piece)
        ga_ref, gb_ref = (ga0_ref, gb0_ref) if q * piece < half else (ga1_ref, gb1_ref)
        gs = slice((q * piece) % half, (q * piece) % half + piece)
        pa = jnp.dot(ya, wa_ref[:, cs], preferred_element_type=_F32)
        yield
        pb = jnp.dot(yb, wb_ref[:, cs], preferred_element_type=_F32)
        yield
        us.append((_sigmoid(ga_ref[:, gs]) * pa + _sigmoid(gb_ref[:, gs]) * pb).astype(_BF16))
    u = jnp.concatenate(us, axis=1)
    for q in range(d // MXU_DIM):
        cs = slice(q * MXU_DIM, (q + 1) * MXU_DIM)
        xe_ref[:, cs] = x_ref[:, cs] + jnp.dot(u, wo_ref[:, cs], preferred_element_type=_F32)
        yield
    x1 = xe_ref[:, 0:d]

    ht = _rms(x1, n2_ref[...]).astype(_BF16)
    logits = jnp.dot(ht, wr_ref[...], preferred_element_type=_F32) + br_ref[...]

    lane = lax.broadcasted_iota(jnp.int32, (tm, LANES), 1).astype(_F32)
    ninf = -jnp.inf
    big = float(LANES)

    def first_argmax(v):
        m = jnp.max(v, axis=-1, keepdims=True)
        return m, jnp.min(jnp.where(v == m, lane, big), axis=-1, keepdims=True)

    is_g = lane < float(n_groups)
    gmax, gidx = first_argmax(jnp.where(is_g, logits, ninf))
    g_w = 1.0 / jnp.sum(jnp.where(is_g, jnp.exp(logits - gmax), 0.0), axis=-1, keepdims=True)
    lo_lane = float(n_groups) + float(epg) * gidx
    in_grp = (lane >= lo_lane) & (lane < lo_lane + float(epg))
    le = jnp.where(in_grp, logits, ninf)
    m1, i1 = first_argmax(le)
    m2, i2 = first_argmax(jnp.where(lane == i1, ninf, le))
    e21 = jnp.exp(m2 - m1)
    w1 = g_w / (1.0 + e21)
    w2 = w1 * e21
    e1 = i1 - lo_lane
    e2 = i2 - lo_lane
    lo = jnp.minimum(e1, e2)
    hi = jnp.maximum(e1, e2)
    w_lo = jnp.where(e1 < e2, w1, w2)
    w_hi = jnp.where(e1 < e2, w2, w1)
    pair = jnp.where(lo == 0.0, hi - 1.0, jnp.where(lo == 1.0, 6.0 - hi, 5.0))
    swap = pair == 5.0
    w_a = jnp.where(swap, w_hi, w_lo)
    w_b = jnp.where(swap, w_lo, w_hi)
    bucket = float(_N_PAIRS) * gidx + pair

    onehot = lane == bucket
    oh_bf = jnp.where(onehot, 1.0, 0.0).astype(_BF16)
    rr = lax.broadcasted_iota(jnp.int32, (tm, tm), 0)
    cc = lax.broadcasted_iota(jnp.int32, (tm, tm), 1)
    tri = jnp.where(cc < rr, 1.0, 0.0).astype(_BF16)
    before = jnp.dot(tri, oh_bf, preferred_element_type=_F32) + cnt_s[...]
    rank = jnp.sum(jnp.where(onehot, before, 0.0), axis=-1, keepdims=True)
    cnt_s[...] = cnt_s[...] + jnp.where(live, jnp.sum(jnp.where(onehot, 1.0, 0.0), axis=0, keepdims=True), 0.0)
    cnt_ref[...] = cnt_s[...]

    info = jnp.where(lane == 0.0, bucket,
                     jnp.where(lane == 1.0, rank,
                               jnp.where(lane == 2.0, w_a, jnp.where(lane == 3.0, w_b, 0.0))))
    xe_ref[:, d:d + LANES] = info
    meta_ref[...] = jnp.transpose(info)[0:SUBLANES, :].astype(jnp.int32)


def _mix_combine_kernel(*refs, tm, pitch, ns, n_tiles, n_groups, epg):
    mix_in, cmb_in = refs[0:10], refs[10:21]
    xe_ref, cnt_ref, meta_ref = refs[21:24]
    ynew_ref, yold_ref, er_ref, ep_ref, a_ref, b_ref, h_ref, car_ref, cnt_s = refs[24:]
    s = pl.program_id(0)

    @pl.when(s == 0)
    def _():
        _mixer_init(er_ref, ep_ref, a_ref, b_ref, car_ref, tm)
        cnt_s[...] = jnp.zeros(cnt_s.shape, _F32)
        yold_ref[...] = jnp.zeros(yold_ref.shape, _BF16)

    seq_pos = lax.rem(jnp.minimum(s, n_tiles - 1), ns)
    mixer = _mixer_tile(*mix_in, ynew_ref.at[0], ynew_ref.at[1], er_ref, ep_ref, a_ref, b_ref, h_ref,
                        car_ref, first=seq_pos == 0, t0=seq_pos * tm, tm=tm, pitch=pitch)
    combine = _combine_tile(yold_ref.at[0], yold_ref.at[1], *cmb_in, xe_ref, cnt_ref, meta_ref,
                            cnt_s, live=s >= 1, tm=tm, n_groups=n_groups, epg=epg)
    for _ in range(4):
        next(combine)
    stages = [mixer, combine]
    while stages:
        for stage in list(stages):
            if next(stage, stages) is stages:
                stages.remove(stage)
    yold_ref[...] = ynew_ref[...]


def _mix_combine(z, x2d, seq, mixer_w, wa, wb, wo, n2, wr, br, tm, n_groups, epg):
    t, d = x2d.shape
    c = wa.shape[0]
    half = d // 2
    n_tiles = t // tm
    ns = seq // tm
    pitch = _scan_pitch(tm)
    n_slab = c // LANES
    off = (3 * c) // half
    kern = functools.partial(_mix_combine_kernel, tm=tm, pitch=pitch, ns=ns, n_tiles=n_tiles,
                             n_groups=n_groups, epg=epg)

    def cur(s):
        return jnp.minimum(s, n_tiles - 1)

    def prev(s):
        return jnp.maximum(s - 1, 0)

    def zspec(j):
        return pl.BlockSpec((tm, half), lambda s, j=j: (prev(s), off + j))

    return pl.pallas_call(
        kern,
        grid=(n_tiles + 1,),
        in_specs=[pl.BlockSpec((tm, 3 * c), lambda s: (cur(s), 0))]
                 + [_const_spec(w.shape) for w in mixer_w]
                 + [zspec(0), zspec(1), zspec(2), zspec(3),
                    pl.BlockSpec((tm, d), lambda s: (prev(s), 0)),
                    _const_spec(wa.shape), _const_spec(wb.shape), _const_spec(wo.shape),
                    _const_spec(n2.shape), _const_spec(wr.shape), _const_spec(br.shape)],
        out_specs=[pl.BlockSpec((tm, d + LANES), lambda s: (prev(s), 0)),
                   pl.BlockSpec((1, LANES), lambda s: (0, 0)),
                   pl.BlockSpec((SUBLANES, tm), lambda s: (0, prev(s)))],
        out_shape=[jax.ShapeDtypeStruct((t, d + LANES), _F32),
                   jax.ShapeDtypeStruct((1, LANES), _F32),
                   jax.ShapeDtypeStruct((SUBLANES, t), jnp.int32)],
        scratch_shapes=[pltpu.VMEM((2, tm, c), _BF16),
                        pltpu.VMEM((2, tm, c), _BF16),
                        pltpu.VMEM((SUBLANES + tm, c), _F32),
                        pltpu.VMEM((2 * SUBLANES + tm, c), _F32),
                        pltpu.VMEM((n_slab, SUBLANES * pitch, LANES), _F32),
                        pltpu.VMEM((n_slab, SUBLANES * pitch, LANES), _F32),
                        pltpu.VMEM((n_slab, SUBLANES * pitch, LANES), _F32),
                        pltpu.VMEM((SUBLANES, c), _F32),
                        pltpu.VMEM((1, LANES), _F32)],
        compiler_params=_params(1),
        name="mix_combine",
    )(z, *mixer_w, z, z, z, z, x2d, wa, wb, wo, n2, wr, br)


def _bucket_starts(cnt_ref, start_ref, n_buckets, tm, on_tile=None):
    def bucket_body(b, tile_idx):
        start_ref[b] = tile_idx * tm
        nt = lax.div(cnt_ref[b] + (tm - 1), tm)
        if on_tile is not None:
            lax.fori_loop(0, nt, lambda k, carry: (on_tile(tile_idx + k, b), carry)[1], 0)
        return tile_idx + nt

    return lax.fori_loop(0, n_buckets, bucket_body, 0)


def _row_gather_start(row_of, src_hbm, buf, slot, sem, n_rows, unrolled):
    def start(r, priority):
        row = row_of(r)
        pltpu.make_async_copy(src_hbm.at[pl.ds(row, 1)], buf.at[slot, pl.ds(r, 1)],
                              sem.at[slot]).start(priority=priority)

    if unrolled:
        for r in range(n_rows):
            start(r, r % 2)
    else:
        def body(r, _):
            start(2 * r, 0)
            start(2 * r + 1, 1)
            return 0
        lax.fori_loop(0, n_rows // 2, body, 0, unroll=4)


def _row_gather_wait(src_hbm, buf, slot, sem, n_rows):
    pltpu.make_async_copy(src_hbm.at[pl.ds(0, n_rows)], buf.at[slot], sem.at[slot]).wait()


MOE_WEIGHT_SLOTS = 6
MOE_PARTS = 4
MOE_CHUNKS = 3 * MOE_PARTS
MOE_PUMP = 3
MOE_DEPTH = 4


def _moe_weight_plan(n_groups, epg):
    first_use = {e: min(p for p in range(_N_PAIRS) if e in (_PAIR_SLOT_A[p], _PAIR_SLOT_B[p])) for e in range(epg)}
    last_use = {e: max(p for p in range(_N_PAIRS) if e in (_PAIR_SLOT_A[p], _PAIR_SLOT_B[p])) for e in range(epg)}
    free_after = [-1] * MOE_WEIGHT_SLOTS
    loads, slot_of = [], {}
    for g in range(n_groups):
        for e in sorted(range(epg), key=lambda e: (first_use[e], e)):
            needed_by = g * _N_PAIRS + first_use[e]
            s = min(range(MOE_WEIGHT_SLOTS), key=lambda s: (free_after[s], s))
            assert free_after[s] < needed_by
            loads.append((g * epg + e, s, free_after[s], needed_by))
            free_after[s] = g * _N_PAIRS + last_use[e]
            slot_of[(g, e)] = s
    n_buckets = n_groups * _N_PAIRS
    need = [sum(1 for l in loads if l[3] <= b) for b in range(n_buckets)]
    allow = [sum(1 for l in loads if l[2] < b) for b in range(n_buckets)]
    slot_a = [slot_of[(b // _N_PAIRS, _PAIR_SLOT_A[b % _N_PAIRS])] for b in range(n_buckets)]
    slot_b = [slot_of[(b // _N_PAIRS, _PAIR_SLOT_B[b % _N_PAIRS])] for b in range(n_buckets)]
    return [l[0] for l in loads], [l[1] for l in loads], need, allow, slot_a, slot_b


def _moe_kernel(tokb_ref, tokr_ref, cntb_ref, need_ref, allow_ref, sa_ref, sb_ref, le_ref, ls_ref,
                xe_hbm, n2_ref, wg_hbm, wu_hbm, wd_hbm,
                x2s_ref, xbuf, gsem, wg_s, wu_s, wd_s, st_gu, st_d, wsem, cnt, src_s, tbk_s, start_s, tot_s,
                *, tm):
    j = pl.program_id(0)
    slot = j % 2
    d = x2s_ref.shape[1]
    t_rows = xe_hbm.shape[0]
    n_slots = src_s.shape[0]
    n_buckets = start_s.shape[0]

    def src_row(tile):
        return lambda r: src_s[tile * tm + r]

    @pl.when(j == 0)
    def _():
        def set_tile(tile, b):
            tbk_s[tile] = b

        total = _bucket_starts(cntb_ref, start_s, n_buckets, tm, on_tile=set_tile)
        tot_s[0] = total

        def fill(lo, hi):
            def body(q, carry):
                src_s[q] = jnp.minimum(jnp.where(q >= t_rows, q - t_rows, q), t_rows - 1)
                return carry
            lax.fori_loop(lo, hi, body, 0)

        def fill_bucket(b, carry):
            nxt = jnp.where(b + 1 < n_buckets, start_s[jnp.minimum(b + 1, n_buckets - 1)], total * tm)
            fill(start_s[b] + cntb_ref[b], nxt)
            return carry
        lax.fori_loop(0, n_buckets, fill_bucket, 0)
        fill(total * tm, (total + 1) * tm)

        def place(tok, carry):
            src_s[start_s[tokb_ref[tok]] + tokr_ref[tok]] = tok
            return carry
        lax.fori_loop(0, t_rows, place, 0, unroll=16)

    n_valid = tot_s[0]
    rows_gu = wg_hbm.shape[1] // MOE_PARTS
    rows_d = wd_hbm.shape[1] // MOE_PARTS

    def chunk_dma(c, kind):
        load = c // MOE_CHUNKS
        part = c % MOE_PARTS
        e = le_ref[load]
        if kind == 2:
            return pltpu.make_async_copy(wd_hbm.at[e, pl.ds(part * rows_d, rows_d)], st_d.at[c % MOE_DEPTH],
                                         wsem.at[c % MOE_DEPTH])
        src = wg_hbm if kind == 0 else wu_hbm
        return pltpu.make_async_copy(src.at[e, pl.ds(part * rows_gu, rows_gu)], st_gu.at[c % MOE_DEPTH],
                                     wsem.at[c % MOE_DEPTH])

    def for_kind(c, fn):
        kind = (c % MOE_CHUNKS) // MOE_PARTS
        for k in range(3):
            @pl.when(kind == k)
            def _(k=k):
                fn(k)

    def issue_one(allowed):
        @pl.when(cnt[0] < jnp.minimum(allowed, cnt[1] + MOE_DEPTH))
        def _():
            c = cnt[0]
            for_kind(c, lambda k: chunk_dma(c, k).start())
            cnt[0] = c + 1

    def retire():
        c = cnt[1]
        s = ls_ref[c // MOE_CHUNKS]
        part = c % MOE_PARTS

        def finish(k):
            chunk_dma(c, k).wait()
            if k == 2:
                wd_s[s, pl.ds(part * rows_d, rows_d), :] = st_d[c % MOE_DEPTH].astype(_BF16)
            else:
                dst = wg_s if k == 0 else wu_s
                dst[s, pl.ds(part * rows_gu, rows_gu), :] = st_gu[c % MOE_DEPTH].astype(_BF16)

        for_kind(c, finish)
        cnt[1] = c + 1

    def pump(required, allowed, extra):
        n_iter = jnp.maximum(required - cnt[1], jnp.minimum(extra, allowed - cnt[1]))
        lax.fori_loop(0, MOE_DEPTH, lambda _, carry: (issue_one(allowed), carry)[1], 0)

        def body(_, carry):
            retire()
            issue_one(allowed)
            return carry

        lax.fori_loop(0, jnp.maximum(n_iter, 0), body, 0)

    @pl.when(j == 0)
    def _():
        cnt[0] = 0
        cnt[1] = 0
        _row_gather_start(src_row(0), xe_hbm, xbuf, 0, gsem, tm, unrolled=False)

    @pl.when(j < n_valid)
    def _():
        b = tbk_s[j]
        pump(need_ref[b] * MOE_CHUNKS, allow_ref[b] * MOE_CHUNKS, MOE_PUMP)
        s_a = sa_ref[b]
        s_b = sb_ref[b]
        _row_gather_wait(xe_hbm, xbuf, slot, gsem, tm)
        _row_gather_start(src_row(j + 1), xe_hbm, xbuf, 1 - slot, gsem, tm, unrolled=True)
        xe = xbuf[slot]
        x1 = xe[:, 0:d]
        w_a = xe[:, d + 2:d + 3]
        w_b = xe[:, d + 3:d + 4]
        ht = _rms(x1, n2_ref[...]).astype(_BF16)

        def expert(s, wgt):
            hg = jnp.dot(ht, wg_s[s], preferred_element_type=_F32)
            hu = jnp.dot(ht, wu_s[s], preferred_element_type=_F32)
            return (hg * _sigmoid(hg) * hu * wgt).astype(_BF16)

        y = jnp.dot(expert(s_a, w_a), wd_s[s_a], preferred_element_type=_F32)
        y = y + jnp.dot(expert(s_b, w_b), wd_s[s_b], preferred_element_type=_F32)
        x2s_ref[...] = x1 + y

    @pl.when(j >= n_valid)
    def _():
        x2s_ref[...] = jnp.zeros(x2s_ref.shape, _F32)

        @pl.when(j == n_valid)
        def _():
            _row_gather_wait(xe_hbm, xbuf, slot, gsem, tm)
            lax.fori_loop(0, cnt[0] - cnt[1], lambda _, carry: (retire(), carry)[1], 0)


def _moe(tok_bucket, tok_rank, bucket_cnt, xe, n2, wg, wu, wd, tm, n_groups, epg):
    n_buckets = bucket_cnt.shape[0]
    n_tiles = xe.shape[0] // tm + n_buckets
    d = wg.shape[1]
    f = wg.shape[2]
    kern = functools.partial(_moe_kernel, tm=tm)
    le, ls, need, allow, slot_a, slot_b = (jnp.asarray(v, jnp.int32) for v in _moe_weight_plan(n_groups, epg))

    grid_spec = pltpu.PrefetchScalarGridSpec(
        num_scalar_prefetch=9,
        grid=(n_tiles,),
        in_specs=[pl.BlockSpec(memory_space=pl.ANY),
                  pl.BlockSpec((1, d), lambda j, *_: (0, 0)),
                  pl.BlockSpec(memory_space=pl.ANY),
                  pl.BlockSpec(memory_space=pl.ANY),
                  pl.BlockSpec(memory_space=pl.ANY)],
        out_specs=pl.BlockSpec((tm, d), lambda j, *_: (j, 0)),
        scratch_shapes=[pltpu.VMEM((2, tm, d + LANES), _F32),
                        pltpu.SemaphoreType.DMA((2,)),
                        pltpu.VMEM((MOE_WEIGHT_SLOTS, d, f), _BF16),
                        pltpu.VMEM((MOE_WEIGHT_SLOTS, d, f), _BF16),
                        pltpu.VMEM((MOE_WEIGHT_SLOTS, f, d), _BF16),
                        pltpu.VMEM((MOE_DEPTH, d // MOE_PARTS, f), _F32),
                        pltpu.VMEM((MOE_DEPTH, f // MOE_PARTS, d), _F32),
                        pltpu.SemaphoreType.DMA((MOE_DEPTH,)),
                        pltpu.SMEM((2,), jnp.int32),
                        pltpu.SMEM((n_tiles * tm,), jnp.int32),
                        pltpu.SMEM((n_tiles,), jnp.int32),
                        pltpu.SMEM((n_buckets,), jnp.int32),
                        pltpu.SMEM((1,), jnp.int32)],
    )
    return pl.pallas_call(
        kern,
        grid_spec=grid_spec,
        out_shape=jax.ShapeDtypeStruct((n_tiles * tm, d), _F32),
        compiler_params=_params(1),
        name="moe",
    )(tok_bucket, tok_rank, bucket_cnt, need, allow, slot_a, slot_b, le, ls, xe, n2, wg, wu, wd)


def _ple_kernel(tokb_ref, tokr_ref, cntb_ref, x2s_hbm, p_ref, ng_ref, wg_ref, wp_ref, nf_ref, o_ref,
                xbuf, gsem, start_s, *, tm, n_steps, tm_sorted):
    i = pl.program_id(0)
    slot = i % 2

    def sorted_row(tile):
        return lambda r: start_s[tokb_ref[tile * tm + r]] + tokr_ref[tile * tm + r]

    @pl.when(i == 0)
    def _():
        _bucket_starts(cntb_ref, start_s, start_s.shape[0], tm_sorted)
        _row_gather_start(sorted_row(0), x2s_hbm, xbuf, 0, gsem, tm, unrolled=False)

    _row_gather_wait(x2s_hbm, xbuf, slot, gsem, tm)
    nxt = jnp.where(i + 1 == n_steps, 0, i + 1)
    _row_gather_start(sorted_row(nxt), x2s_hbm, xbuf, 1 - slot, gsem, tm, unrolled=True)
    x2 = xbuf[slot]
    g = _sigmoid(jnp.dot(_rms(x2, ng_ref[...]).astype(_BF16), wg_ref[...],
                               preferred_element_type=_F32))
    e = jnp.dot(p_ref[...].astype(_BF16), wp_ref[...], preferred_element_type=_F32)
    o_ref[...] = _rms(x2 + g * e, nf_ref[...])

    @pl.when(i == n_steps - 1)
    def _():
        _row_gather_wait(x2s_hbm, xbuf, 1 - slot, gsem, tm)


def _ple(tok_bucket, tok_rank, bucket_cnt, x2s, p2d, ng, wg, wp, nf, tm, t, tm_sorted):
    d = x2s.shape[1]
    pd = p2d.shape[1]
    n_steps = t // tm
    kern = functools.partial(_ple_kernel, tm=tm, n_steps=n_steps, tm_sorted=tm_sorted)

    def cspec(shape):
        nd = len(shape)
        return pl.BlockSpec(shape, lambda i, *_: (0,) * nd, pipeline_mode=pl.Buffered(1))

    grid_spec = pltpu.PrefetchScalarGridSpec(
        num_scalar_prefetch=3,
        grid=(n_steps,),
        in_specs=[pl.BlockSpec(memory_space=pl.ANY),
                  pl.BlockSpec((tm, pd), lambda i, *_: (i, 0)),
                  cspec(ng.shape), cspec(wg.shape), cspec(wp.shape), cspec(nf.shape)],
        out_specs=pl.BlockSpec((tm, d), lambda i, *_: (i, 0)),
        scratch_shapes=[pltpu.VMEM((2, tm, d), _F32),
                        pltpu.SemaphoreType.DMA((2,)),
                        pltpu.SMEM((bucket_cnt.shape[0],), jnp.int32)],
    )
    return pl.pallas_call(
        kern,
        grid_spec=grid_spec,
        out_shape=jax.ShapeDtypeStruct((t, d), _F32),
        compiler_params=_params(1),
        name="ple",
    )(tok_bucket, tok_rank, bucket_cnt, x2s, p2d, ng, wg, wp, nf)


def _block_diag(w, per_block):
    h, hd, _ = w.shape
    nb = h // per_block
    w4 = w.reshape(nb, per_block, hd, hd)
    rows = [jnp.pad(w4[:, p], ((0, 0), (0, 0), (p * hd, (per_block - 1 - p) * hd))) for p in range(per_block)]
    return jnp.concatenate(rows, axis=1)


def _layer(x2d, p2d, bsz, seq, norm1_g, w_in, conv_w, conv_b, w_rg_a, b_rg_a, w_rg_x, b_rg_x, lru_lambda,
           w_pool, pool_scale, w_branch_a, w_branch_b, w_out, norm2_g, w_router_group, b_router_group,
           w_router_expert, b_router_expert, w_e_gate, w_e_up, w_e_down, norm_ple_g, w_ple_gate,
           w_ple_proj, out_norm_g):
    t, d = x2d.shape
    c = conv_b.shape[0]
    heads, hd, _ = w_rg_a.shape
    n_groups = w_router_group.shape[1]
    n_exp = w_router_expert.shape[1]
    epg = n_exp // n_groups
    assert epg == 4 and TOP_K == 2 and hd * (MXU_DIM // hd) == MXU_DIM
    assert w_pool.shape[0] == len(POOL_WINDOWS) and w_pool.shape[1] == MXU_DIM

    tm_in, tn_in, rc_in = min(2048, t), 512, min(256, t)
    tm_mix = min(256, seq)
    tm_cmb = min(256, t)
    tm_moe = min(128, t)
    tm_ple = min(512, t)

    row = lambda v: v.reshape(1, -1).astype(_F32)
    per_block = MXU_DIM // hd

    z = _inproj(x2d, row(norm1_g), w_in, tm_in, tn_in, rc_in)
    mixer_w = (conv_w.reshape(CONV_WIDTH, c), row(conv_b),
               _block_diag(w_rg_a, per_block).astype(_BF16), row(b_rg_a),
               _block_diag(w_rg_x, per_block).astype(_BF16), row(b_rg_x),
               row(lru_lambda), w_pool.astype(_BF16), row(pool_scale))

    n_rt = n_groups + n_exp
    wr = jnp.pad(jnp.concatenate([w_router_group, w_router_expert], axis=1),
                 ((0, 0), (0, LANES - n_rt))).astype(_BF16)
    br = jnp.pad(jnp.concatenate([b_router_group, b_router_expert]), (0, LANES - n_rt)).reshape(1, LANES)
    xe, counts, meta = _mix_combine(z, x2d, seq, mixer_w, w_branch_a.astype(_BF16), w_branch_b.astype(_BF16),
                                    w_out.astype(_BF16), row(norm2_g), wr, br, tm_cmb, n_groups, epg)

    n_buckets = n_groups * _N_PAIRS
    tok_bucket, tok_rank = meta[0], meta[1]
    bucket_cnt = counts[0, :n_buckets].astype(jnp.int32)

    x2s = _moe(tok_bucket, tok_rank, bucket_cnt, xe, row(norm2_g), w_e_gate, w_e_up, w_e_down,
               tm_moe, n_groups, epg)
    return _ple(tok_bucket, tok_rank, bucket_cnt, x2s, p2d, row(norm_ple_g), w_ple_gate.astype(_BF16),
                w_ple_proj.astype(_BF16), row(out_norm_g), tm_ple, t, tm_moe)


def kernel(x, p, norm1_g, w_in, conv_w, conv_b, w_rg_a, b_rg_a, w_rg_x, b_rg_x, lru_lambda, w_pool, pool_scale, w_branch_a, w_branch_b, w_out, norm2_g, w_router_group, b_router_group, w_router_expert, b_router_expert, w_e_gate, w_e_up, w_e_down, norm_ple_g, w_ple_gate, w_ple_proj, final_norm_g):
    bsz, seq, d = x.shape
    depth = p.shape[0]
    assert depth == 1, "the final RMSNorm is fused into the last layer's embedding kernel"
    out = _layer(x.reshape(bsz * seq, d), p[0].reshape(bsz * seq, -1), bsz, seq,
                 norm1_g[0], w_in[0], conv_w[0], conv_b[0], w_rg_a[0], b_rg_a[0], w_rg_x[0], b_rg_x[0],
                 lru_lambda[0], w_pool[0], pool_scale[0], w_branch_a[0], w_branch_b[0], w_out[0],
                 norm2_g[0], w_router_group[0], b_router_group[0], w_router_expert[0],
                 b_router_expert[0], w_e_gate[0], w_e_up[0], w_e_down[0], norm_ple_g[0],
                 w_ple_gate[0], w_ple_proj[0], final_norm_g)
    return out.reshape(bsz, seq, d)
```

```python
import functools

import jax
import jax.numpy as jnp
from jax import lax
from jax.experimental import pallas as pl
from jax.experimental.pallas import tpu as pltpu

EPS = 1e-6
LRU_C = 8.0
CONV_WIDTH = 4
POOL_WINDOWS = (2, 4, 8, 16)
TOP_K = 2
SQRT_GUARD = 1e-30

LANES = 128
SUBLANES = 8
MXU_DIM = 256
VMEM_LIMIT_BYTES = 56 * 1024 * 1024

_BF16 = jnp.bfloat16
_F32 = jnp.float32

_PAIR_SLOT_A = (0, 0, 0, 1, 1, 3)
_PAIR_SLOT_B = (1, 2, 3, 3, 2, 2)
_N_PAIRS = len(_PAIR_SLOT_A)


def _sigmoid(x):
    return 0.5 * jnp.tanh(0.5 * x) + 0.5


def _rms(x, g):
    ms = jnp.mean(x * x, axis=-1, keepdims=True)
    return x * lax.rsqrt(ms + EPS) * g


def _const_spec(shape):
    nd = len(shape)
    return pl.BlockSpec(shape, lambda *_: (0,) * nd, pipeline_mode=pl.Buffered(1))


def _params(n_axes, flags=None):
    return pltpu.CompilerParams(dimension_semantics=("arbitrary",) * n_axes,
                                vmem_limit_bytes=VMEM_LIMIT_BYTES, flags=flags)


def _inproj_kernel(x_hbm, g_ref, w_ref, z_ref, h_ref, xs_ref, sem, *, tm, rc):
    i = pl.program_id(0)
    j = pl.program_id(1)
    n_chunks = tm // rc
    more_tiles = i + 1 < pl.num_programs(0)

    def chunk_copy(tile, c):
        return pltpu.make_async_copy(x_hbm.at[pl.ds(pl.multiple_of(tile * tm + c * rc, rc), rc)],
                                     xs_ref.at[c % 2], sem.at[c % 2])

    def normalise(tile, c):
        h_ref[tile % 2, pl.ds(pl.multiple_of(c * rc, rc), rc), :] = _rms(xs_ref[c % 2], g_ref[...]).astype(_BF16)

    @pl.when(jnp.logical_and(i == 0, j == 0))
    def _():
        chunk_copy(0, 0).start()
        for c in range(n_chunks):
            if c + 1 < n_chunks:
                chunk_copy(0, c + 1).start()
            chunk_copy(0, c).wait()
            normalise(0, c)

    @pl.when(jnp.logical_and(more_tiles, jnp.logical_and(j >= 1, j <= n_chunks)))
    def _():
        chunk_copy(i + 1, j - 1).wait()
        normalise(i + 1, j - 1)

    @pl.when(jnp.logical_and(more_tiles, j < n_chunks))
    def _():
        chunk_copy(i + 1, j).start()

    z_ref[...] = jnp.dot(h_ref[i % 2], w_ref[...].astype(_BF16), preferred_element_type=_F32)


def _inproj(x2d, g, w, tm, tn, rc):
    t, d = x2d.shape
    n = w.shape[1]
    kern = functools.partial(_inproj_kernel, tm=tm, rc=rc)
    assert n // tn > tm // rc, "a row tile's chunks are prepared during the column steps of the previous tile"
    return pl.pallas_call(
        kern,
        grid=(t // tm, n // tn),
        in_specs=[pl.BlockSpec(memory_space=pl.ANY),
                  pl.BlockSpec((1, d), lambda i, j: (0, 0)),
                  pl.BlockSpec((d, tn), lambda i, j: (0, j))],
        out_specs=pl.BlockSpec((tm, tn), lambda i, j: (i, j)),
        out_shape=jax.ShapeDtypeStruct((t, n), _F32),
        scratch_shapes=[pltpu.VMEM((2, tm, d), _BF16),
                        pltpu.VMEM((2, rc, d), _F32),
                        pltpu.SemaphoreType.DMA((2,))],
        compiler_params=_params(2),
        name="inproj",
    )(x2d, g, w)


def _scan_pitch(tm):
    p = -(-tm // SUBLANES)
    while p % SUBLANES != 4:
        p += 1
    return p


def _mixer_init(er_ref, ep_ref, a_ref, b_ref, car_ref, tm):
    er_ref[...] = jnp.zeros(er_ref.shape, _F32)
    ep_ref[...] = jnp.zeros(ep_ref.shape, _F32)
    car_ref[...] = jnp.zeros(car_ref.shape, _F32)
    a_ref[:, tm:, :] = jnp.ones((a_ref.shape[0], a_ref.shape[1] - tm, LANES), _F32)
    b_ref[:, tm:, :] = jnp.zeros((b_ref.shape[0], b_ref.shape[1] - tm, LANES), _F32)


def _mixer_tile(z_ref, cw_ref, cb_ref, wa_ref, ba_ref, wx_ref, bx_ref, lam_ref, wp_ref, ps_ref,
                ya_ref, yb_ref, er_ref, ep_ref, a_ref, b_ref, h_ref, car_ref, *, first, t0, tm, pitch):
    c = cb_ref.shape[1]
    n_slab = c // LANES
    hist_r = SUBLANES
    hist_p = 2 * SUBLANES

    er_ref[0:hist_r, :] = jnp.where(first, 0.0, er_ref[tm:tm + hist_r, :])
    ep_ref[0:hist_p, :] = jnp.where(first, 0.0, ep_ref[tm:tm + hist_p, :])
    er_ref[hist_r:hist_r + tm, :] = z_ref[:, 0:c]
    ep_ref[hist_p:hist_p + tm, :] = z_ref[:, 2 * c:3 * c]

    kvec = -LRU_C * jax.nn.softplus(-lam_ref[...])
    nblk = c // MXU_DIM
    for k in range(nblk):
        cs = slice(k * MXU_DIM, (k + 1) * MXU_DIM)
        xc = cb_ref[:, cs] + cw_ref[CONV_WIDTH - 1:CONV_WIDTH, cs] * er_ref[hist_r:hist_r + tm, cs]
        for j in range(1, CONV_WIDTH):
            xc = xc + cw_ref[CONV_WIDTH - 1 - j:CONV_WIDTH - j, cs] * er_ref[hist_r - j:hist_r - j + tm, cs]
        xcb = xc.astype(_BF16)
        r = _sigmoid(jnp.dot(xcb, wa_ref[k], preferred_element_type=_F32) + ba_ref[:, cs])
        ig = _sigmoid(jnp.dot(xcb, wx_ref[k], preferred_element_type=_F32) + bx_ref[:, cs])
        log_a = r * kvec[:, cs]
        a = jnp.exp(log_a)
        v = 1.0 - a * a
        mult = v * lax.rsqrt(jnp.maximum(v, SQRT_GUARD))
        bb = mult * ig * xc
        for q in range(MXU_DIM // LANES):
            slab = k * (MXU_DIM // LANES) + q
            a_ref[slab, 0:tm, :] = a[:, q * LANES:(q + 1) * LANES]
            b_ref[slab, 0:tm, :] = bb[:, q * LANES:(q + 1) * LANES]
        yield

    def seg(i):
        return pl.ds(i, SUBLANES, stride=pitch)

    row = lax.broadcasted_iota(jnp.int32, (SUBLANES, LANES), 0)
    for sl in range(n_slab):
        hh = jnp.zeros((SUBLANES, LANES), _F32)
        aa = jnp.ones((SUBLANES, LANES), _F32)
        for i in range(pitch):
            av = a_ref[sl, seg(i), :]
            hh = av * hh + b_ref[sl, seg(i), :]
            aa = av * aa
        d = 1
        while d < SUBLANES:
            hs_ = jnp.where(row >= d, pltpu.roll(hh, d, 0), 0.0)
            as_ = jnp.where(row >= d, pltpu.roll(aa, d, 0), 1.0)
            hh = aa * hs_ + hh
            aa = aa * as_
            d *= 2
        cs = slice(sl * LANES, (sl + 1) * LANES)
        cin = jnp.where(first, 0.0, car_ref[:, cs])
        full = hh + aa * cin
        hv = jnp.where(row >= 1, pltpu.roll(full, 1, 0), cin)
        car_ref[:, cs] = jnp.broadcast_to(full[SUBLANES - 1:SUBLANES, :], (SUBLANES, LANES))
        for i in range(pitch):
            hv = a_ref[sl, seg(i), :] * hv + b_ref[sl, seg(i), :]
            h_ref[sl, seg(i), :] = hv
        g = z_ref[:, c + sl * LANES:c + (sl + 1) * LANES]
        ya_ref[:, cs] = (h_ref[sl, 0:tm, :] * jax.nn.gelu(g)).astype(_BF16)
        yield

    t_idx = (t0 + lax.broadcasted_iota(jnp.int32, (tm, 1), 0) + 1).astype(_F32)
    n_grp = len(POOL_WINDOWS)
    gd = c // n_grp
    for gi, w in enumerate(POOL_WINDOWS):
        cs = slice(gi * gd, (gi + 1) * gd)
        e = ep_ref[:, cs]
        acc = e
        d = 1
        while d < w:
            acc = acc + pltpu.roll(acc, d, 0)
            d *= 2
        xt = e[hist_p:, :]
        cnt = jnp.minimum(t_idx, float(w))
        dd = acc[hist_p:, :] / cnt - xt
        yb = jnp.dot(dd.astype(_BF16), wp_ref[gi], preferred_element_type=_F32) * ps_ref[:, cs]
        yb_ref[:, cs] = yb.astype(_BF16)
        yield


def _combine_tile(ya_ref, yb_ref, ga0_ref, ga1_ref, gb0_ref, gb1_ref, x_ref,
                  wa_ref, wb_ref, wo_ref, n2_ref, wr_ref, br_ref,
                  xe_ref, cnt_ref, meta_ref, cnt_s, *, live, tm, n_groups, epg):
    d = x_ref.shape[1]
    half = d // 2
    piece = MXU_DIM * 2
    ya = ya_ref[...]
    yb = yb_ref[...]
    us = []
    for q in range(d // piece):
        cs = slice(q * piece, (q + 1) * piece)
        ga_ref, gb_ref = (ga0_ref, gb0_ref) if q * piece < half else (ga1_ref, gb1_ref)
        gs = slice((q * piece) % half, (q * piece) % half + piece)
        pa = jnp.dot(ya, wa_ref[:, cs], preferred_element_type=_F32)
        yield
        pb = jnp.dot(yb, wb_ref[:, cs], preferred_element_type=_F32)
        yield
        us.append((_sigmoid(ga_ref[:, gs]) * pa + _sigmoid(gb_ref[:, gs]) * pb).astype(_BF16))
    u = jnp.concatenate(us, axis=1)
    for q in range(d // MXU_DIM):
        cs = slice(q * MXU_DIM, (q + 1) * MXU_DIM)
        xe_ref[:, cs] = x_ref[:, cs] + jnp.dot(u, wo_ref[:, cs], preferred_element_type=_F32)
        yield
    x1 = xe_ref[:, 0:d]

    ht = _rms(x1, n2_ref[...]).astype(_BF16)
    logits = jnp.dot(ht, wr_ref[...], preferred_element_type=_F32) + br_ref[...]

    lane = lax.broadcasted_iota(jnp.int32, (tm, LANES), 1).astype(_F32)
    ninf = -jnp.inf
    big = float(LANES)

    def first_argmax(v):
        m = jnp.max(v, axis=-1, keepdims=True)
        return m, jnp.min(jnp.where(v == m, lane, big), axis=-1, keepdims=True)

    is_g = lane < float(n_groups)
    gmax, gidx = first_argmax(jnp.where(is_g, logits, ninf))
    g_w = 1.0 / jnp.sum(jnp.where(is_g, jnp.exp(logits - gmax), 0.0), axis=-1, keepdims=True)
    lo_lane = float(n_groups) + float(epg) * gidx
    in_grp = (lane >= lo_lane) & (lane < lo_lane + float(epg))
    le = jnp.where(in_grp, logits, ninf)
    m1, i1 = first_argmax(le)
    m2, i2 = first_argmax(jnp.where(lane == i1, ninf, le))
    e21 = jnp.exp(m2 - m1)
    w1 = g_w / (1.0 + e21)
    w2 = w1 * e21
    e1 = i1 - lo_lane
    e2 = i2 - lo_lane
    lo = jnp.minimum(e1, e2)
    hi = jnp.maximum(e1, e2)
    w_lo = jnp.where(e1 < e2, w1, w2)
    w_hi = jnp.where(e1 < e2, w2, w1)
    pair = jnp.where(lo == 0.0, hi - 1.0, jnp.where(lo == 1.0, 6.0 - hi, 5.0))
    swap = pair == 5.0
    w_a = jnp.where(swap, w_hi, w_lo)
    w_b = jnp.where(swap, w_lo, w_hi)
    bucket = float(_N_PAIRS) * gidx + pair

    onehot = lane == bucket
    oh_bf = jnp.where(onehot, 1.0, 0.0).astype(_BF16)
    rr = lax.broadcasted_iota(jnp.int32, (tm, tm), 0)
    cc = lax.broadcasted_iota(jnp.int32, (tm, tm), 1)
    tri = jnp.where(cc < rr, 1.0, 0.0).astype(_BF16)
    before = jnp.dot(tri, oh_bf, preferred_element_type=_F32) + cnt_s[...]
    rank = jnp.sum(jnp.where(onehot, before, 0.0), axis=-1, keepdims=True)
    cnt_s[...] = cnt_s[...] + jnp.where(live, jnp.sum(jnp.where(onehot, 1.0, 0.0), axis=0, keepdims=True), 0.0)
    cnt_ref[...] = cnt_s[...]

    info = jnp.where(lane == 0.0, bucket,
                     jnp.where(lane == 1.0, rank,
                               jnp.where(lane == 2.0, w_a, jnp.where(lane == 3.0, w_b, 0.0))))
    xe_ref[:, d:d + LANES] = info
    meta_ref[...] = jnp.transpose(info)[0:SUBLANES, :].astype(jnp.int32)


def _mix_combine_kernel(*refs, tm, pitch, ns, n_tiles, n_groups, epg):
    mix_in, cmb_in = refs[0:10], refs[10:21]
    xe_ref, cnt_ref, meta_ref = refs[21:24]
    ynew_ref, yold_ref, er_ref, ep_ref, a_ref, b_ref, h_ref, car_ref, cnt_s = refs[24:]
    s = pl.program_id(0)

    @pl.when(s == 0)
    def _():
        _mixer_init(er_ref, ep_ref, a_ref, b_ref, car_ref, tm)
        cnt_s[...] = jnp.zeros(cnt_s.shape, _F32)
        yold_ref[...] = jnp.zeros(yold_ref.shape, _BF16)

    seq_pos = lax.rem(jnp.minimum(s, n_tiles - 1), ns)
    mixer = _mixer_tile(*mix_in, ynew_ref.at[0], ynew_ref.at[1], er_ref, ep_ref, a_ref, b_ref, h_ref,
                        car_ref, first=seq_pos == 0, t0=seq_pos * tm, tm=tm, pitch=pitch)
    combine = _combine_tile(yold_ref.at[0], yold_ref.at[1], *cmb_in, xe_ref, cnt_ref, meta_ref,
                            cnt_s, live=s >= 1, tm=tm, n_groups=n_groups, epg=epg)
    stages = [mixer, combine]
    while stages:
        for stage in list(stages):
            if next(stage, stages) is stages:
                stages.remove(stage)
    yold_ref[...] = ynew_ref[...]


def _mix_combine(z, x2d, seq, mixer_w, wa, wb, wo, n2, wr, br, tm, n_groups, epg):
    t, d = x2d.shape
    c = wa.shape[0]
    half = d // 2
    n_tiles = t // tm
    ns = seq // tm
    pitch = _scan_pitch(tm)
    n_slab = c // LANES
    off = (3 * c) // half
    kern = functools.partial(_mix_combine_kernel, tm=tm, pitch=pitch, ns=ns, n_tiles=n_tiles,
                             n_groups=n_groups, epg=epg)

    def cur(s):
        return jnp.minimum(s, n_tiles - 1)

    def prev(s):
        return jnp.maximum(s - 1, 0)

    def zspec(j):
        return pl.BlockSpec((tm, half), lambda s, j=j: (prev(s), off + j))

    return pl.pallas_call(
        kern,
        grid=(n_tiles + 1,),
        in_specs=[pl.BlockSpec((tm, 3 * c), lambda s: (cur(s), 0))]
                 + [_const_spec(w.shape) for w in mixer_w]
                 + [zspec(0), zspec(1), zspec(2), zspec(3),
                    pl.BlockSpec((tm, d), lambda s: (prev(s), 0)),
                    _const_spec(wa.shape), _const_spec(wb.shape), _const_spec(wo.shape),
                    _const_spec(n2.shape), _const_spec(wr.shape), _const_spec(br.shape)],
        out_specs=[pl.BlockSpec((tm, d + LANES), lambda s: (prev(s), 0)),
                   pl.BlockSpec((1, LANES), lambda s: (0, 0)),
                   pl.BlockSpec((SUBLANES, tm), lambda s: (0, prev(s)))],
        out_shape=[jax.ShapeDtypeStruct((t, d + LANES), _F32),
                   jax.ShapeDtypeStruct((1, LANES), _F32),
                   jax.ShapeDtypeStruct((SUBLANES, t), jnp.int32)],
        scratch_shapes=[pltpu.VMEM((2, tm, c), _BF16),
                        pltpu.VMEM((2, tm, c), _BF16),
                        pltpu.VMEM((SUBLANES + tm, c), _F32),
                        pltpu.VMEM((2 * SUBLANES + tm, c), _F32),
                        pltpu.VMEM((n_slab, SUBLANES * pitch, LANES), _F32),
                        pltpu.VMEM((n_slab, SUBLANES * pitch, LANES), _F32),
                        pltpu.VMEM((n_slab, SUBLANES * pitch, LANES), _F32),
                        pltpu.VMEM((SUBLANES, c), _F32),
                        pltpu.VMEM((1, LANES), _F32)],
        compiler_params=_params(1),
        name="mix_combine",
    )(z, *mixer_w, z, z, z, z, x2d, wa, wb, wo, n2, wr, br)


def _bucket_starts(cnt_ref, start_ref, n_buckets, tm, on_tile=None):
    def bucket_body(b, tile_idx):
        start_ref[b] = tile_idx * tm
        nt = lax.div(cnt_ref[b] + (tm - 1), tm)
        if on_tile is not None:
            lax.fori_loop(0, nt, lambda k, carry: (on_tile(tile_idx + k, b), carry)[1], 0)
        return tile_idx + nt

    return lax.fori_loop(0, n_buckets, bucket_body, 0)


def _row_gather_start(row_of, src_hbm, buf, slot, sem, n_rows, unrolled):
    def start(r, priority):
        row = row_of(r)
        pltpu.make_async_copy(src_hbm.at[pl.ds(row, 1)], buf.at[slot, pl.ds(r, 1)],
                              sem.at[slot]).start(priority=priority)

    if unrolled:
        for r in range(n_rows):
            start(r, r % 2)
    else:
        def body(r, _):
            start(2 * r, 0)
            start(2 * r + 1, 1)
            return 0
        lax.fori_loop(0, n_rows // 2, body, 0, unroll=4)


def _row_gather_wait(src_hbm, buf, slot, sem, n_rows):
    pltpu.make_async_copy(src_hbm.at[pl.ds(0, n_rows)], buf.at[slot], sem.at[slot]).wait()


MOE_WEIGHT_SLOTS = 6
MOE_PARTS = 4
MOE_CHUNKS = 3 * MOE_PARTS
MOE_PUMP = 3
MOE_DEPTH = 4
MOE_GATHER_AHEAD = 2


def _moe_weight_plan(n_groups, epg):
    first_use = {e: min(p for p in range(_N_PAIRS) if e in (_PAIR_SLOT_A[p], _PAIR_SLOT_B[p])) for e in range(epg)}
    last_use = {e: max(p for p in range(_N_PAIRS) if e in (_PAIR_SLOT_A[p], _PAIR_SLOT_B[p])) for e in range(epg)}
    free_after = [-1] * MOE_WEIGHT_SLOTS
    loads, slot_of = [], {}
    for g in range(n_groups):
        for e in sorted(range(epg), key=lambda e: (first_use[e], e)):
            needed_by = g * _N_PAIRS + first_use[e]
            s = min(range(MOE_WEIGHT_SLOTS), key=lambda s: (free_after[s], s))
            assert free_after[s] < needed_by
            loads.append((g * epg + e, s, free_after[s], needed_by))
            free_after[s] = g * _N_PAIRS + last_use[e]
            slot_of[(g, e)] = s
    n_buckets = n_groups * _N_PAIRS
    need = [sum(1 for l in loads if l[3] <= b) for b in range(n_buckets)]
    allow = [sum(1 for l in loads if l[2] < b) for b in range(n_buckets)]
    slot_a = [slot_of[(b // _N_PAIRS, _PAIR_SLOT_A[b % _N_PAIRS])] for b in range(n_buckets)]
    slot_b = [slot_of[(b // _N_PAIRS, _PAIR_SLOT_B[b % _N_PAIRS])] for b in range(n_buckets)]
    return [l[0] for l in loads], [l[1] for l in loads], need, allow, slot_a, slot_b


def _moe_kernel(tokb_ref, tokr_ref, cntb_ref, need_ref, allow_ref, sa_ref, sb_ref, le_ref, ls_ref,
                xe_hbm, n2_ref, wg_hbm, wu_hbm, wd_hbm,
                x2s_ref, xbuf, gsem, wg_s, wu_s, wd_s, st_gu, st_d, wsem, cnt, src_s, tbk_s, start_s, tot_s,
                *, tm):
    j = pl.program_id(0)
    slot = lax.rem(j, MOE_GATHER_AHEAD + 1)
    d = x2s_ref.shape[1]
    t_rows = xe_hbm.shape[0]
    n_buckets = start_s.shape[0]

    def src_row(tile):
        return lambda r: src_s[tile * tm + r]

    @pl.when(j == 0)
    def _():
        def set_tile(tile, b):
            tbk_s[tile] = b

        total = _bucket_starts(cntb_ref, start_s, n_buckets, tm, on_tile=set_tile)
        tot_s[0] = total

        def fill(lo, hi):
            def body(q, carry):
                src_s[q] = jnp.minimum(jnp.where(q >= t_rows, q - t_rows, q), t_rows - 1)
                return carry
            lax.fori_loop(lo, hi, body, 0)

        def fill_bucket(b, carry):
            nxt = jnp.where(b + 1 < n_buckets, start_s[jnp.minimum(b + 1, n_buckets - 1)], total * tm)
            fill(start_s[b] + cntb_ref[b], nxt)
            return carry
        lax.fori_loop(0, n_buckets, fill_bucket, 0)
        fill(total * tm, (total + MOE_GATHER_AHEAD) * tm)

        def place(tok, carry):
            src_s[start_s[tokb_ref[tok]] + tokr_ref[tok]] = tok
            return carry
        lax.fori_loop(0, t_rows, place, 0, unroll=16)

    n_valid = tot_s[0]
    rows_gu = wg_hbm.shape[1] // MOE_PARTS
    rows_d = wd_hbm.shape[1] // MOE_PARTS

    def chunk_dma(c, kind):
        load = c // MOE_CHUNKS
        part = c % MOE_PARTS
        e = le_ref[load]
        if kind == 2:
            return pltpu.make_async_copy(wd_hbm.at[e, pl.ds(part * rows_d, rows_d)], st_d.at[c % MOE_DEPTH],
                                         wsem.at[c % MOE_DEPTH])
        src = wg_hbm if kind == 0 else wu_hbm
        return pltpu.make_async_copy(src.at[e, pl.ds(part * rows_gu, rows_gu)], st_gu.at[c % MOE_DEPTH],
                                     wsem.at[c % MOE_DEPTH])

    def for_kind(c, fn):
        kind = (c % MOE_CHUNKS) // MOE_PARTS
        for k in range(3):
            @pl.when(kind == k)
            def _(k=k):
                fn(k)

    def issue_one(allowed):
        @pl.when(cnt[0] < jnp.minimum(allowed, cnt[1] + MOE_DEPTH))
        def _():
            c = cnt[0]
            for_kind(c, lambda k: chunk_dma(c, k).start())
            cnt[0] = c + 1

    def retire():
        c = cnt[1]
        s = ls_ref[c // MOE_CHUNKS]
        part = c % MOE_PARTS

        def finish(k):
            chunk_dma(c, k).wait()
            if k == 2:
                wd_s[s, pl.ds(part * rows_d, rows_d), :] = st_d[c % MOE_DEPTH].astype(_BF16)
            else:
                dst = wg_s if k == 0 else wu_s
                dst[s, pl.ds(part * rows_gu, rows_gu), :] = st_gu[c % MOE_DEPTH].astype(_BF16)

        for_kind(c, finish)
        cnt[1] = c + 1

    def pump(required, allowed, extra):
        n_iter = jnp.maximum(required - cnt[1], jnp.minimum(extra, allowed - cnt[1]))
        lax.fori_loop(0, MOE_DEPTH, lambda _, carry: (issue_one(allowed), carry)[1], 0)

        def body(_, carry):
            retire()
            issue_one(allowed)
            return carry

        lax.fori_loop(0, jnp.maximum(n_iter, 0), body, 0)

    @pl.when(j == 0)
    def _():
        cnt[0] = 0
        cnt[1] = 0
        for tile in range(MOE_GATHER_AHEAD):
            _row_gather_start(src_row(tile), xe_hbm, xbuf, tile, gsem, tm, unrolled=False)

    @pl.when(j < n_valid)
    def _():
        b = tbk_s[j]
        pump(need_ref[b] * MOE_CHUNKS, allow_ref[b] * MOE_CHUNKS, MOE_PUMP)
        s_a = sa_ref[b]
        s_b = sb_ref[b]
        _row_gather_wait(xe_hbm, xbuf, slot, gsem, tm)
        _row_gather_start(src_row(j + MOE_GATHER_AHEAD), xe_hbm, xbuf,
                          lax.rem(j + MOE_GATHER_AHEAD, MOE_GATHER_AHEAD + 1), gsem, tm, unrolled=True)
        xe = xbuf[slot]
        x1 = xe[:, 0:d]
        w_a = xe[:, d + 2:d + 3]
        w_b = xe[:, d + 3:d + 4]
        ht = _rms(x1, n2_ref[...]).astype(_BF16)

        def expert(s, wgt):
            hg = jnp.dot(ht, wg_s[s], preferred_element_type=_F32)
            hu = jnp.dot(ht, wu_s[s], preferred_element_type=_F32)
            return (hg * _sigmoid(hg) * hu * wgt).astype(_BF16)

        y = jnp.dot(expert(s_a, w_a), wd_s[s_a], preferred_element_type=_F32)
        y = y + jnp.dot(expert(s_b, w_b), wd_s[s_b], preferred_element_type=_F32)
        x2s_ref[...] = x1 + y

    @pl.when(j >= n_valid)
    def _():
        x2s_ref[...] = jnp.zeros(x2s_ref.shape, _F32)

        @pl.when(j == n_valid)
        def _():
            for ahead in range(MOE_GATHER_AHEAD):
                _row_gather_wait(xe_hbm, xbuf, lax.rem(j + ahead, MOE_GATHER_AHEAD + 1), gsem, tm)
            lax.fori_loop(0, cnt[0] - cnt[1], lambda _, carry: (retire(), carry)[1], 0)


def _moe(tok_bucket, tok_rank, bucket_cnt, xe, n2, wg, wu, wd, tm, n_groups, epg):
    n_buckets = bucket_cnt.shape[0]
    n_tiles = xe.shape[0] // tm + n_buckets - 1 + MOE_GATHER_AHEAD
    d = wg.shape[1]
    f = wg.shape[2]
    kern = functools.partial(_moe_kernel, tm=tm)
    le, ls, need, allow, slot_a, slot_b = (jnp.asarray(v, jnp.int32) for v in _moe_weight_plan(n_groups, epg))

    grid_spec = pltpu.PrefetchScalarGridSpec(
        num_scalar_prefetch=9,
        grid=(n_tiles,),
        in_specs=[pl.BlockSpec(memory_space=pl.ANY),
                  pl.BlockSpec((1, d), lambda j, *_: (0, 0)),
                  pl.BlockSpec(memory_space=pl.ANY),
                  pl.BlockSpec(memory_space=pl.ANY),
                  pl.BlockSpec(memory_space=pl.ANY)],
        out_specs=pl.BlockSpec((tm, d), lambda j, *_: (j, 0)),
        scratch_shapes=[pltpu.VMEM((MOE_GATHER_AHEAD + 1, tm, d + LANES), _F32),
                        pltpu.SemaphoreType.DMA((MOE_GATHER_AHEAD + 1,)),
                        pltpu.VMEM((MOE_WEIGHT_SLOTS, d, f), _BF16),
                        pltpu.VMEM((MOE_WEIGHT_SLOTS, d, f), _BF16),
                        pltpu.VMEM((MOE_WEIGHT_SLOTS, f, d), _BF16),
                        pltpu.VMEM((MOE_DEPTH, d // MOE_PARTS, f), _F32),
                        pltpu.VMEM((MOE_DEPTH, f // MOE_PARTS, d), _F32),
                        pltpu.SemaphoreType.DMA((MOE_DEPTH,)),
                        pltpu.SMEM((2,), jnp.int32),
                        pltpu.SMEM((n_tiles * tm,), jnp.int32),
                        pltpu.SMEM((n_tiles,), jnp.int32),
                        pltpu.SMEM((n_buckets,), jnp.int32),
                        pltpu.SMEM((1,), jnp.int32)],
    )
    return pl.pallas_call(
        kern,
        grid_spec=grid_spec,
        out_shape=jax.ShapeDtypeStruct((n_tiles * tm, d), _F32),
        compiler_params=_params(1),
        name="moe",
    )(tok_bucket, tok_rank, bucket_cnt, need, allow, slot_a, slot_b, le, ls, xe, n2, wg, wu, wd)


def _ple_kernel(tokb_ref, tokr_ref, cntb_ref, x2s_hbm, p_ref, ng_ref, wg_ref, wp_ref, nf_ref, o_ref,
                xbuf, gsem, start_s, *, tm, n_steps, tm_sorted):
    i = pl.program_id(0)
    slot = i % 2

    def sorted_row(tile):
        return lambda r: start_s[tokb_ref[tile * tm + r]] + tokr_ref[tile * tm + r]

    @pl.when(i == 0)
    def _():
        _bucket_starts(cntb_ref, start_s, start_s.shape[0], tm_sorted)
        _row_gather_start(sorted_row(0), x2s_hbm, xbuf, 0, gsem, tm, unrolled=False)

    _row_gather_wait(x2s_hbm, xbuf, slot, gsem, tm)
    nxt = jnp.where(i + 1 == n_steps, 0, i + 1)
    _row_gather_start(sorted_row(nxt), x2s_hbm, xbuf, 1 - slot, gsem, tm, unrolled=True)
    x2 = xbuf[slot]
    g = _sigmoid(jnp.dot(_rms(x2, ng_ref[...]).astype(_BF16), wg_ref[...],
                               preferred_element_type=_F32))
    e = jnp.dot(p_ref[...].astype(_BF16), wp_ref[...], preferred_element_type=_F32)
    o_ref[...] = _rms(x2 + g * e, nf_ref[...])

    @pl.when(i == n_steps - 1)
    def _():
        _row_gather_wait(x2s_hbm, xbuf, 1 - slot, gsem, tm)


def _ple(tok_bucket, tok_rank, bucket_cnt, x2s, p2d, ng, wg, wp, nf, tm, t, tm_sorted):
    d = x2s.shape[1]
    pd = p2d.shape[1]
    n_steps = t // tm
    kern = functools.partial(_ple_kernel, tm=tm, n_steps=n_steps, tm_sorted=tm_sorted)

    def cspec(shape):
        nd = len(shape)
        return pl.BlockSpec(shape, lambda i, *_: (0,) * nd, pipeline_mode=pl.Buffered(1))

    grid_spec = pltpu.PrefetchScalarGridSpec(
        num_scalar_prefetch=3,
        grid=(n_steps,),
        in_specs=[pl.BlockSpec(memory_space=pl.ANY),
                  pl.BlockSpec((tm, pd), lambda i, *_: (i, 0)),
                  cspec(ng.shape), cspec(wg.shape), cspec(wp.shape), cspec(nf.shape)],
        out_specs=pl.BlockSpec((tm, d), lambda i, *_: (i, 0)),
        scratch_shapes=[pltpu.VMEM((2, tm, d), _F32),
                        pltpu.SemaphoreType.DMA((2,)),
                        pltpu.SMEM((bucket_cnt.shape[0],), jnp.int32)],
    )
    return pl.pallas_call(
        kern,
        grid_spec=grid_spec,
        out_shape=jax.ShapeDtypeStruct((t, d), _F32),
        compiler_params=_params(1),
        name="ple",
    )(tok_bucket, tok_rank, bucket_cnt, x2s, p2d, ng, wg, wp, nf)


def _block_diag(w, per_block):
    h, hd, _ = w.shape
    nb = h // per_block
    w4 = w.reshape(nb, per_block, hd, hd)
    rows = [jnp.pad(w4[:, p], ((0, 0), (0, 0), (p * hd, (per_block - 1 - p) * hd))) for p in range(per_block)]
    return jnp.concatenate(rows, axis=1)


def _layer(x2d, p2d, bsz, seq, norm1_g, w_in, conv_w, conv_b, w_rg_a, b_rg_a, w_rg_x, b_rg_x, lru_lambda,
           w_pool, pool_scale, w_branch_a, w_branch_b, w_out, norm2_g, w_router_group, b_router_group,
           w_router_expert, b_router_expert, w_e_gate, w_e_up, w_e_down, norm_ple_g, w_ple_gate,
           w_ple_proj, out_norm_g):
    t, d = x2d.shape
    c = conv_b.shape[0]
    heads, hd, _ = w_rg_a.shape
    n_groups = w_router_group.shape[1]
    n_exp = w_router_expert.shape[1]
    epg = n_exp // n_groups
    assert epg == 4 and TOP_K == 2 and hd * (MXU_DIM // hd) == MXU_DIM
    assert w_pool.shape[0] == len(POOL_WINDOWS) and w_pool.shape[1] == MXU_DIM

    tm_in, tn_in, rc_in = min(2048, t), 512, min(256, t)
    tm_mix = min(256, seq)
    tm_cmb = min(256, t)
    tm_moe = min(128, t)
    tm_ple = min(512, t)

    row = lambda v: v.reshape(1, -1).astype(_F32)
    per_block = MXU_DIM // hd

    z = _inproj(x2d, row(norm1_g), w_in, tm_in, tn_in, rc_in)
    mixer_w = (conv_w.reshape(CONV_WIDTH, c), row(conv_b),
               _block_diag(w_rg_a, per_block).astype(_BF16), row(b_rg_a),
               _block_diag(w_rg_x, per_block).astype(_BF16), row(b_rg_x),
               row(lru_lambda), w_pool.astype(_BF16), row(pool_scale))

    n_rt = n_groups + n_exp
    wr = jnp.pad(jnp.concatenate([w_router_group, w_router_expert], axis=1),
                 ((0, 0), (0, LANES - n_rt))).astype(_BF16)
    br = jnp.pad(jnp.concatenate([b_router_group, b_router_expert]), (0, LANES - n_rt)).reshape(1, LANES)
    xe, counts, meta = _mix_combine(z, x2d, seq, mixer_w, w_branch_a.astype(_BF16), w_branch_b.astype(_BF16),
                                    w_out.astype(_BF16), row(norm2_g), wr, br, tm_cmb, n_groups, epg)

    n_buckets = n_groups * _N_PAIRS
    tok_bucket, tok_rank = meta[0], meta[1]
    bucket_cnt = counts[0, :n_buckets].astype(jnp.int32)

    x2s = _moe(tok_bucket, tok_rank, bucket_cnt, xe, row(norm2_g), w_e_gate, w_e_up, w_e_down,
               tm_moe, n_groups, epg)
    return _ple(tok_bucket, tok_rank, bucket_cnt, x2s, p2d, row(norm_ple_g), w_ple_gate.astype(_BF16),
                w_ple_proj.astype(_BF16), row(out_norm_g), tm_ple, t, tm_moe)


def kernel(x, p, norm1_g, w_in, conv_w, conv_b, w_rg_a, b_rg_a, w_rg_x, b_rg_x, lru_lambda, w_pool, pool_scale, w_branch_a, w_branch_b, w_out, norm2_g, w_router_group, b_router_group, w_router_expert, b_router_expert, w_e_gate, w_e_up, w_e_down, norm_ple_g, w_ple_gate, w_ple_proj, final_norm_g):
    bsz, seq, d = x.shape
    depth = p.shape[0]
    assert depth == 1, "the final RMSNorm is fused into the last layer's embedding kernel"
    out = _layer(x.reshape(bsz * seq, d), p[0].reshape(bsz * seq, -1), bsz, seq,
                 norm1_g[0], w_in[0], conv_w[0], conv_b[0], w_rg_a[0], b_rg_a[0], w_rg_x[0], b_rg_x[0],
                 lru_lambda[0], w_pool[0], pool_scale[0], w_branch_a[0], w_branch_b[0], w_out[0],
                 norm2_g[0], w_router_group[0], b_router_group[0], w_router_expert[0],
                 b_router_expert[0], w_e_gate[0], w_e_up[0], w_e_down[0], norm_ple_g[0],
                 w_ple_gate[0], w_ple_proj[0], final_norm_g)
    return out.reshape(bsz, seq, d)
```

```python
import functools

import jax
import jax.numpy as jnp
from jax import lax
from jax.experimental import pallas as pl
from jax.experimental.pallas import tpu as pltpu

EPS = 1e-6
LRU_C = 8.0
CONV_WIDTH = 4
POOL_WINDOWS = (2, 4, 8, 16)
TOP_K = 2
SQRT_GUARD = 1e-30

LANES = 128
SUBLANES = 8
MXU_DIM = 256
VMEM_LIMIT_BYTES = 56 * 1024 * 1024

_BF16 = jnp.bfloat16
_F32 = jnp.float32

_PAIR_SLOT_A = (0, 0, 0, 1, 1, 3)
_PAIR_SLOT_B = (1, 2, 3, 3, 2, 2)
_N_PAIRS = len(_PAIR_SLOT_A)


def _sigmoid(x):
    return 0.5 * jnp.tanh(0.5 * x) + 0.5


def _rms(x, g):
    ms = jnp.mean(x * x, axis=-1, keepdims=True)
    return x * lax.rsqrt(ms + EPS) * g


def _const_spec(shape):
    nd = len(shape)
    return pl.BlockSpec(shape, lambda *_: (0,) * nd, pipeline_mode=pl.Buffered(1))


def _params(n_axes, flags=None):
    return pltpu.CompilerParams(dimension_semantics=("arbitrary",) * n_axes,
                                vmem_limit_bytes=VMEM_LIMIT_BYTES, flags=flags)


def _inproj_kernel(x_hbm, g_ref, w_ref, z_ref, h_ref, xs_ref, sem, *, tm, rc):
    i = pl.program_id(0)
    j = pl.program_id(1)
    n_chunks = tm // rc
    more_tiles = i + 1 < pl.num_programs(0)

    def chunk_copy(tile, c):
        return pltpu.make_async_copy(x_hbm.at[pl.ds(pl.multiple_of(tile * tm + c * rc, rc), rc)],
                                     xs_ref.at[c % 2], sem.at[c % 2])

    def normalise(tile, c):
        h_ref[tile % 2, pl.ds(pl.multiple_of(c * rc, rc), rc), :] = _rms(xs_ref[c % 2], g_ref[...]).astype(_BF16)

    @pl.when(jnp.logical_and(i == 0, j == 0))
    def _():
        chunk_copy(0, 0).start()
        for c in range(n_chunks):
            if c + 1 < n_chunks:
                chunk_copy(0, c + 1).start()
            chunk_copy(0, c).wait()
            normalise(0, c)

    @pl.when(jnp.logical_and(more_tiles, jnp.logical_and(j >= 1, j <= n_chunks)))
    def _():
        chunk_copy(i + 1, j - 1).wait()
        normalise(i + 1, j - 1)

    @pl.when(jnp.logical_and(more_tiles, j < n_chunks))
    def _():
        chunk_copy(i + 1, j).start()

    z_ref[...] = jnp.dot(h_ref[i % 2], w_ref[...].astype(_BF16), preferred_element_type=_F32)


def _inproj(x2d, g, w, tm, tn, rc):
    t, d = x2d.shape
    n = w.shape[1]
    kern = functools.partial(_inproj_kernel, tm=tm, rc=rc)
    assert n // tn > tm // rc, "a row tile's chunks are prepared during the column steps of the previous tile"
    return pl.pallas_call(
        kern,
        grid=(t // tm, n // tn),
        in_specs=[pl.BlockSpec(memory_space=pl.ANY),
                  pl.BlockSpec((1, d), lambda i, j: (0, 0)),
                  pl.BlockSpec((d, tn), lambda i, j: (0, j))],
        out_specs=pl.BlockSpec((tm, tn), lambda i, j: (i, j)),
        out_shape=jax.ShapeDtypeStruct((t, n), _F32),
        scratch_shapes=[pltpu.VMEM((2, tm, d), _BF16),
                        pltpu.VMEM((2, rc, d), _F32),
                        pltpu.SemaphoreType.DMA((2,))],
        compiler_params=_params(2),
        name="inproj",
    )(x2d, g, w)


def _scan_pitch(tm):
    p = -(-tm // SUBLANES)
    while p % SUBLANES != 4:
        p += 1
    return p


def _mixer_init(er_ref, ep_ref, a_ref, b_ref, car_ref, tm):
    er_ref[...] = jnp.zeros(er_ref.shape, _F32)
    ep_ref[...] = jnp.zeros(ep_ref.shape, _F32)
    car_ref[...] = jnp.zeros(car_ref.shape, _F32)
    a_ref[:, tm:, :] = jnp.ones((a_ref.shape[0], a_ref.shape[1] - tm, LANES), _F32)
    b_ref[:, tm:, :] = jnp.zeros((b_ref.shape[0], b_ref.shape[1] - tm, LANES), _F32)


def _mixer_tile(z_ref, cw_ref, cb_ref, wa_ref, ba_ref, wx_ref, bx_ref, lam_ref, wp_ref, ps_ref,
                ya_ref, yb_ref, er_ref, ep_ref, a_ref, b_ref, h_ref, car_ref, *, first, t0, tm, pitch):
    c = cb_ref.shape[1]
    n_slab = c // LANES
    hist_r = SUBLANES
    hist_p = 2 * SUBLANES

    er_ref[0:hist_r, :] = jnp.where(first, 0.0, er_ref[tm:tm + hist_r, :])
    ep_ref[0:hist_p, :] = jnp.where(first, 0.0, ep_ref[tm:tm + hist_p, :])
    er_ref[hist_r:hist_r + tm, :] = z_ref[:, 0:c]
    ep_ref[hist_p:hist_p + tm, :] = z_ref[:, 2 * c:3 * c]

    kvec = -LRU_C * jax.nn.softplus(-lam_ref[...])
    nblk = c // MXU_DIM
    for k in range(nblk):
        cs = slice(k * MXU_DIM, (k + 1) * MXU_DIM)
        xc = cb_ref[:, cs] + cw_ref[CONV_WIDTH - 1:CONV_WIDTH, cs] * er_ref[hist_r:hist_r + tm, cs]
        for j in range(1, CONV_WIDTH):
            xc = xc + cw_ref[CONV_WIDTH - 1 - j:CONV_WIDTH - j, cs] * er_ref[hist_r - j:hist_r - j + tm, cs]
        xcb = xc.astype(_BF16)
        r = _sigmoid(jnp.dot(xcb, wa_ref[k], preferred_element_type=_F32) + ba_ref[:, cs])
        ig = _sigmoid(jnp.dot(xcb, wx_ref[k], preferred_element_type=_F32) + bx_ref[:, cs])
        log_a = r * kvec[:, cs]
        a = jnp.exp(log_a)
        v = 1.0 - a * a
        mult = v * lax.rsqrt(jnp.maximum(v, SQRT_GUARD))
        bb = mult * ig * xc
        for q in range(MXU_DIM // LANES):
            slab = k * (MXU_DIM // LANES) + q
            a_ref[slab, 0:tm, :] = a[:, q * LANES:(q + 1) * LANES]
            b_ref[slab, 0:tm, :] = bb[:, q * LANES:(q + 1) * LANES]
        yield

    def seg(i):
        return pl.ds(i, SUBLANES, stride=pitch)

    row = lax.broadcasted_iota(jnp.int32, (SUBLANES, LANES), 0)
    for sl in range(n_slab):
        hh = jnp.zeros((SUBLANES, LANES), _F32)
        aa = jnp.ones((SUBLANES, LANES), _F32)
        for i in range(pitch):
            av = a_ref[sl, seg(i), :]
            hh = av * hh + b_ref[sl, seg(i), :]
            aa = av * aa
        d = 1
        while d < SUBLANES:
            hs_ = jnp.where(row >= d, pltpu.roll(hh, d, 0), 0.0)
            as_ = jnp.where(row >= d, pltpu.roll(aa, d, 0), 1.0)
            hh = aa * hs_ + hh
            aa = aa * as_
            d *= 2
        cs = slice(sl * LANES, (sl + 1) * LANES)
        cin = jnp.where(first, 0.0, car_ref[:, cs])
        full = hh + aa * cin
        hv = jnp.where(row >= 1, pltpu.roll(full, 1, 0), cin)
        car_ref[:, cs] = jnp.broadcast_to(full[SUBLANES - 1:SUBLANES, :], (SUBLANES, LANES))
        for i in range(pitch):
            hv = a_ref[sl, seg(i), :] * hv + b_ref[sl, seg(i), :]
            h_ref[sl, seg(i), :] = hv
        g = z_ref[:, c + sl * LANES:c + (sl + 1) * LANES]
        ya_ref[:, cs] = (h_ref[sl, 0:tm, :] * jax.nn.gelu(g)).astype(_BF16)
        yield

    t_idx = (t0 + lax.broadcasted_iota(jnp.int32, (tm, 1), 0) + 1).astype(_F32)
    n_grp = len(POOL_WINDOWS)
    gd = c // n_grp
    for gi, w in enumerate(POOL_WINDOWS):
        cs = slice(gi * gd, (gi + 1) * gd)
        e = ep_ref[:, cs]
        acc = e
        d = 1
        while d < w:
            acc = acc + pltpu.roll(acc, d, 0)
            d *= 2
        xt = e[hist_p:, :]
        cnt = jnp.minimum(t_idx, float(w))
        dd = acc[hist_p:, :] / cnt - xt
        yb = jnp.dot(dd.astype(_BF16), wp_ref[gi], preferred_element_type=_F32) * ps_ref[:, cs]
        yb_ref[:, cs] = yb.astype(_BF16)
        yield


def _combine_tile(ya_ref, yb_ref, ga0_ref, ga1_ref, gb0_ref, gb1_ref, x_ref,
                  wa_ref, wb_ref, wo_ref, n2_ref, wr_ref, br_ref,
                  xe_ref, cnt_ref, meta_ref, cnt_s, *, live, tm, n_groups, epg):
    d = x_ref.shape[1]
    half = d // 2
    piece = MXU_DIM * 2
    ya = ya_ref[...]
    yb = yb_ref[...]
    us = []
    for q in range(d // piece):
        cs = slice(q * piece, (q + 1) * piece)
        ga_ref, gb_ref = (ga0_ref, gb0_ref) if q * piece < half else (ga1_ref, gb1_ref)
        gs = slice((q * piece) % half, (q * piece) % half + piece)
        pa = jnp.dot(ya, wa_ref[:, cs], preferred_element_type=_F32)
        yield
        pb = jnp.dot(yb, wb_ref[:, cs], preferred_element_type=_F32)
        yield
        us.append((_sigmoid(ga_ref[:, gs]) * pa + _sigmoid(gb_ref[:, gs]) * pb).astype(_BF16))
    u = jnp.concatenate(us, axis=1)
    for q in range(d // MXU_DIM):
        cs = slice(q * MXU_DIM, (q + 1) * MXU_DIM)
        xe_ref[:, cs] = x_ref[:, cs] + jnp.dot(u, wo_ref[:, cs], preferred_element_type=_F32)
        yield
    x1 = xe_ref[:, 0:d]

    ht = _rms(x1, n2_ref[...]).astype(_BF16)
    logits = jnp.dot(ht, wr_ref[...], preferred_element_type=_F32) + br_ref[...]

    lane = lax.broadcasted_iota(jnp.int32, (tm, LANES), 1).astype(_F32)
    ninf = -jnp.inf
    big = float(LANES)

    def first_argmax(v):
        m = jnp.max(v, axis=-1, keepdims=True)
        return m, jnp.min(jnp.where(v == m, lane, big), axis=-1, keepdims=True)

    is_g = lane < float(n_groups)
    gmax, gidx = first_argmax(jnp.where(is_g, logits, ninf))
    g_w = 1.0 / jnp.sum(jnp.where(is_g, jnp.exp(logits - gmax), 0.0), axis=-1, keepdims=True)
    lo_lane = float(n_groups) + float(epg) * gidx
    in_grp = (lane >= lo_lane) & (lane < lo_lane + float(epg))
    le = jnp.where(in_grp, logits, ninf)
    m1, i1 = first_argmax(le)
    m2, i2 = first_argmax(jnp.where(lane == i1, ninf, le))
    e21 = jnp.exp(m2 - m1)
    w1 = g_w / (1.0 + e21)
    w2 = w1 * e21
    e1 = i1 - lo_lane
    e2 = i2 - lo_lane
    lo = jnp.minimum(e1, e2)
    hi = jnp.maximum(e1, e2)
    w_lo = jnp.where(e1 < e2, w1, w2)
    w_hi = jnp.where(e1 < e2, w2, w1)
    pair = jnp.where(lo == 0.0, hi - 1.0, jnp.where(lo == 1.0, 6.0 - hi, 5.0))
    swap = pair == 5.0
    w_a = jnp.where(swap, w_hi, w_lo)
    w_b = jnp.where(swap, w_lo, w_hi)
    bucket = float(_N_PAIRS) * gidx + pair

    onehot = lane == bucket
    oh_bf = jnp.where(onehot, 1.0, 0.0).astype(_BF16)
    rr = lax.broadcasted_iota(jnp.int32, (tm, tm), 0)
    cc = lax.broadcasted_iota(jnp.int32, (tm, tm), 1)
    tri = jnp.where(cc < rr, 1.0, 0.0).astype(_BF16)
    before = jnp.dot(tri, oh_bf, preferred_element_type=_F32) + cnt_s[...]
    rank = jnp.sum(jnp.where(onehot, before, 0.0), axis=-1, keepdims=True)
    cnt_s[...] = cnt_s[...] + jnp.where(live, jnp.sum(jnp.where(onehot, 1.0, 0.0), axis=0, keepdims=True), 0.0)
    cnt_ref[...] = cnt_s[...]

    info = jnp.where(lane == 0.0, bucket,
                     jnp.where(lane == 1.0, rank,
                               jnp.where(lane == 2.0, w_a, jnp.where(lane == 3.0, w_b, 0.0))))
    xe_ref[:, d:d + LANES] = info
    meta_ref[...] = jnp.transpose(info)[0:SUBLANES, :].astype(jnp.int32)


def _mix_combine_kernel(*refs, tm, pitch, ns, n_tiles, n_groups, epg):
    mix_in, cmb_in = refs[0:10], refs[10:21]
    xe_ref, cnt_ref, meta_ref = refs[21:24]
    ynew_ref, yold_ref, er_ref, ep_ref, a_ref, b_ref, h_ref, car_ref, cnt_s = refs[24:]
    s = pl.program_id(0)

    @pl.when(s == 0)
    def _():
        _mixer_init(er_ref, ep_ref, a_ref, b_ref, car_ref, tm)
        cnt_s[...] = jnp.zeros(cnt_s.shape, _F32)
        yold_ref[...] = jnp.zeros(yold_ref.shape, _BF16)

    seq_pos = lax.rem(jnp.minimum(s, n_tiles - 1), ns)
    mixer = _mixer_tile(*mix_in, ynew_ref.at[0], ynew_ref.at[1], er_ref, ep_ref, a_ref, b_ref, h_ref,
                        car_ref, first=seq_pos == 0, t0=seq_pos * tm, tm=tm, pitch=pitch)
    combine = _combine_tile(yold_ref.at[0], yold_ref.at[1], *cmb_in, xe_ref, cnt_ref, meta_ref,
                            cnt_s, live=s >= 1, tm=tm, n_groups=n_groups, epg=epg)
    stages = [mixer, combine]
    while stages:
        for stage in list(stages):
            if next(stage, stages) is stages:
                stages.remove(stage)
    yold_ref[...] = ynew_ref[...]


def _mix_combine(z, x2d, seq, mixer_w, wa, wb, wo, n2, wr, br, tm, n_groups, epg):
    t, d = x2d.shape
    c = wa.shape[0]
    half = d // 2
    n_tiles = t // tm
    ns = seq // tm
    pitch = _scan_pitch(tm)
    n_slab = c // LANES
    off = (3 * c) // half
    kern = functools.partial(_mix_combine_kernel, tm=tm, pitch=pitch, ns=ns, n_tiles=n_tiles,
                             n_groups=n_groups, epg=epg)

    def cur(s):
        return jnp.minimum(s, n_tiles - 1)

    def prev(s):
        return jnp.maximum(s - 1, 0)

    def zspec(j):
        return pl.BlockSpec((tm, half), lambda s, j=j: (prev(s), off + j))

    return pl.pallas_call(
        kern,
        grid=(n_tiles + 1,),
        in_specs=[pl.BlockSpec((tm, 3 * c), lambda s: (cur(s), 0))]
                 + [_const_spec(w.shape) for w in mixer_w]
                 + [zspec(0), zspec(1), zspec(2), zspec(3),
                    pl.BlockSpec((tm, d), lambda s: (prev(s), 0)),
                    _const_spec(wa.shape), _const_spec(wb.shape), _const_spec(wo.shape),
                    _const_spec(n2.shape), _const_spec(wr.shape), _const_spec(br.shape)],
        out_specs=[pl.BlockSpec((tm, d + LANES), lambda s: (prev(s), 0)),
                   pl.BlockSpec((1, LANES), lambda s: (0, 0)),
                   pl.BlockSpec((SUBLANES, tm), lambda s: (0, prev(s)))],
        out_shape=[jax.ShapeDtypeStruct((t, d + LANES), _F32),
                   jax.ShapeDtypeStruct((1, LANES), _F32),
                   jax.ShapeDtypeStruct((SUBLANES, t), jnp.int32)],
        scratch_shapes=[pltpu.VMEM((2, tm, c), _BF16),
                        pltpu.VMEM((2, tm, c), _BF16),
                        pltpu.VMEM((SUBLANES + tm, c), _F32),
                        pltpu.VMEM((2 * SUBLANES + tm, c), _F32),
                        pltpu.VMEM((n_slab, SUBLANES * pitch, LANES), _F32),
                        pltpu.VMEM((n_slab, SUBLANES * pitch, LANES), _F32),
                        pltpu.VMEM((n_slab, SUBLANES * pitch, LANES), _F32),
                        pltpu.VMEM((SUBLANES, c), _F32),
                        pltpu.VMEM((1, LANES), _F32)],
        compiler_params=_params(1),
        name="mix_combine",
    )(z, *mixer_w, z, z, z, z, x2d, wa, wb, wo, n2, wr, br)


def _bucket_starts(cnt_ref, start_ref, n_buckets, tm, on_tile=None):
    def bucket_body(b, tile_idx):
        start_ref[b] = tile_idx * tm
        nt = lax.div(cnt_ref[b] + (tm - 1), tm)
        if on_tile is not None:
            lax.fori_loop(0, nt, lambda k, carry: (on_tile(tile_idx + k, b), carry)[1], 0)
        return tile_idx + nt

    return lax.fori_loop(0, n_buckets, bucket_body, 0)


def _row_gather_start(row_of, src_hbm, buf, slot, sem, n_rows, unrolled):
    def start(r, priority):
        row = row_of(r)
        pltpu.make_async_copy(src_hbm.at[pl.ds(row, 1)], buf.at[slot, pl.ds(r, 1)],
                              sem.at[slot]).start(priority=priority)

    if unrolled:
        for r in range(n_rows):
            start(r, r % 2)
    else:
        def body(r, _):
            start(2 * r, 0)
            start(2 * r + 1, 1)
            return 0
        lax.fori_loop(0, n_rows // 2, body, 0, unroll=4)


def _row_gather_wait(src_hbm, buf, slot, sem, n_rows):
    pltpu.make_async_copy(src_hbm.at[pl.ds(0, n_rows)], buf.at[slot], sem.at[slot]).wait()


MOE_WEIGHT_SLOTS = 6
MOE_PARTS = 4
MOE_CHUNKS = 3 * MOE_PARTS
MOE_PUMP = 3
MOE_DEPTH = 4
MOE_GATHER_AHEAD = 1
MOE_PLACE_SEGMENT = 512


def _moe_weight_plan(n_groups, epg):
    first_use = {e: min(p for p in range(_N_PAIRS) if e in (_PAIR_SLOT_A[p], _PAIR_SLOT_B[p])) for e in range(epg)}
    last_use = {e: max(p for p in range(_N_PAIRS) if e in (_PAIR_SLOT_A[p], _PAIR_SLOT_B[p])) for e in range(epg)}
    free_after = [-1] * MOE_WEIGHT_SLOTS
    loads, slot_of = [], {}
    for g in range(n_groups):
        for e in sorted(range(epg), key=lambda e: (first_use[e], e)):
            needed_by = g * _N_PAIRS + first_use[e]
            s = min(range(MOE_WEIGHT_SLOTS), key=lambda s: (free_after[s], s))
            assert free_after[s] < needed_by
            loads.append((g * epg + e, s, free_after[s], needed_by))
            free_after[s] = g * _N_PAIRS + last_use[e]
            slot_of[(g, e)] = s
    n_buckets = n_groups * _N_PAIRS
    need = [sum(1 for l in loads if l[3] <= b) for b in range(n_buckets)]
    allow = [sum(1 for l in loads if l[2] < b) for b in range(n_buckets)]
    slot_a = [slot_of[(b // _N_PAIRS, _PAIR_SLOT_A[b % _N_PAIRS])] for b in range(n_buckets)]
    slot_b = [slot_of[(b // _N_PAIRS, _PAIR_SLOT_B[b % _N_PAIRS])] for b in range(n_buckets)]
    return [l[0] for l in loads], [l[1] for l in loads], need, allow, slot_a, slot_b


def _moe_kernel(tokb_ref, tokr_ref, cntb_ref, need_ref, allow_ref, sa_ref, sb_ref, le_ref, ls_ref,
                xe_hbm, n2_ref, wg_hbm, wu_hbm, wd_hbm,
                x2s_ref, xbuf, gsem, wg_s, wu_s, wd_s, st_gu, st_d, wsem, cnt, src_s, tbk_s, start_s, tot_s,
                *, tm):
    j = pl.program_id(0)
    slot = lax.rem(j, MOE_GATHER_AHEAD + 1)
    d = x2s_ref.shape[1]
    t_rows = xe_hbm.shape[0]
    n_buckets = start_s.shape[0]

    def src_row(tile):
        return lambda r: src_s[tile * tm + r]

    rows_gu = wg_hbm.shape[1] // MOE_PARTS
    rows_d = wd_hbm.shape[1] // MOE_PARTS

    def chunk_dma(c, kind):
        load = c // MOE_CHUNKS
        part = c % MOE_PARTS
        e = le_ref[load]
        if kind == 2:
            return pltpu.make_async_copy(wd_hbm.at[e, pl.ds(part * rows_d, rows_d)], st_d.at[c % MOE_DEPTH],
                                         wsem.at[c % MOE_DEPTH])
        src = wg_hbm if kind == 0 else wu_hbm
        return pltpu.make_async_copy(src.at[e, pl.ds(part * rows_gu, rows_gu)], st_gu.at[c % MOE_DEPTH],
                                     wsem.at[c % MOE_DEPTH])

    def for_kind(c, fn):
        kind = (c % MOE_CHUNKS) // MOE_PARTS
        for k in range(3):
            @pl.when(kind == k)
            def _(k=k):
                fn(k)

    def issue_one(allowed):
        @pl.when(cnt[0] < jnp.minimum(allowed, cnt[1] + MOE_DEPTH))
        def _():
            c = cnt[0]
            for_kind(c, lambda k: chunk_dma(c, k).start())
            cnt[0] = c + 1

    def retire():
        c = cnt[1]
        s = ls_ref[c // MOE_CHUNKS]
        part = c % MOE_PARTS

        def finish(k):
            chunk_dma(c, k).wait()
            if k == 2:
                wd_s[s, pl.ds(part * rows_d, rows_d), :] = st_d[c % MOE_DEPTH].astype(_BF16)
            else:
                dst = wg_s if k == 0 else wu_s
                dst[s, pl.ds(part * rows_gu, rows_gu), :] = st_gu[c % MOE_DEPTH].astype(_BF16)

        for_kind(c, finish)
        cnt[1] = c + 1

    def pump(required, allowed, extra):
        n_iter = jnp.maximum(required - cnt[1], jnp.minimum(extra, allowed - cnt[1]))
        lax.fori_loop(0, MOE_DEPTH, lambda _, carry: (issue_one(allowed), carry)[1], 0)

        def body(_, carry):
            retire()
            issue_one(allowed)
            return carry

        lax.fori_loop(0, jnp.maximum(n_iter, 0), body, 0)

    @pl.when(j == 0)
    def _():
        cnt[0] = 0
        cnt[1] = 0

        def set_tile(tile, b):
            tbk_s[tile] = b

        total = _bucket_starts(cntb_ref, start_s, n_buckets, tm, on_tile=set_tile)
        tot_s[0] = total

        def fill(lo, hi):
            def body(q, carry):
                src_s[q] = jnp.minimum(jnp.where(q >= t_rows, q - t_rows, q), t_rows - 1)
                return carry
            lax.fori_loop(lo, hi, body, 0)

        def fill_bucket(b, carry):
            nxt = jnp.where(b + 1 < n_buckets, start_s[jnp.minimum(b + 1, n_buckets - 1)], total * tm)
            fill(start_s[b] + cntb_ref[b], nxt)
            return carry
        lax.fori_loop(0, n_buckets, fill_bucket, 0)
        fill(total * tm, (total + MOE_GATHER_AHEAD) * tm)

        def place(tok, carry):
            src_s[start_s[tokb_ref[tok]] + tokr_ref[tok]] = tok
            return carry

        first_allowed = allow_ref[tbk_s[0]] * MOE_CHUNKS
        seg = min(MOE_PLACE_SEGMENT, t_rows)

        def place_segment(sg, carry):
            lax.fori_loop(0, seg, lambda i, c: place(sg * seg + i, c), 0, unroll=16)
            pump(0, first_allowed, 2)
            return carry
        lax.fori_loop(0, t_rows // seg, place_segment, 0)

        for tile in range(MOE_GATHER_AHEAD):
            _row_gather_start(src_row(tile), xe_hbm, xbuf, tile, gsem, tm, unrolled=False)

    n_valid = tot_s[0]

    @pl.when(j < n_valid)
    def _():
        b = tbk_s[j]
        pump(need_ref[b] * MOE_CHUNKS, allow_ref[b] * MOE_CHUNKS, MOE_PUMP)
        s_a = sa_ref[b]
        s_b = sb_ref[b]

        def tile_body(cur):
            _row_gather_wait(xe_hbm, xbuf, cur, gsem, tm)
            _row_gather_start(src_row(j + MOE_GATHER_AHEAD), xe_hbm, xbuf,
                              (cur + MOE_GATHER_AHEAD) % (MOE_GATHER_AHEAD + 1), gsem, tm, unrolled=True)
            xe = xbuf[cur]
            x1 = xe[:, 0:d]
            w_a = xe[:, d + 2:d + 3]
            w_b = xe[:, d + 3:d + 4]
            ht = _rms(x1, n2_ref[...]).astype(_BF16)

            def expert(s, wgt):
                hg = jnp.dot(ht, wg_s[s], preferred_element_type=_F32)
                hu = jnp.dot(ht, wu_s[s], preferred_element_type=_F32)
                return (hg * _sigmoid(hg) * hu * wgt).astype(_BF16)

            y = jnp.dot(expert(s_a, w_a), wd_s[s_a], preferred_element_type=_F32)
            y = y + jnp.dot(expert(s_b, w_b), wd_s[s_b], preferred_element_type=_F32)
            x2s_ref[...] = x1 + y

        for cur in range(MOE_GATHER_AHEAD + 1):
            pl.when(slot == cur)(functools.partial(tile_body, cur))

    @pl.when(j >= n_valid)
    def _():
        x2s_ref[...] = jnp.zeros(x2s_ref.shape, _F32)

        @pl.when(j == n_valid)
        def _():
            for ahead in range(MOE_GATHER_AHEAD):
                _row_gather_wait(xe_hbm, xbuf, lax.rem(j + ahead, MOE_GATHER_AHEAD + 1), gsem, tm)
            lax.fori_loop(0, cnt[0] - cnt[1], lambda _, carry: (retire(), carry)[1], 0)


def _moe(tok_bucket, tok_rank, bucket_cnt, xe, n2, wg, wu, wd, tm, n_groups, epg):
    n_buckets = bucket_cnt.shape[0]
    n_tiles = xe.shape[0] // tm + n_buckets - 1 + MOE_GATHER_AHEAD
    d = wg.shape[1]
    f = wg.shape[2]
    kern = functools.partial(_moe_kernel, tm=tm)
    le, ls, need, allow, slot_a, slot_b = (jnp.asarray(v, jnp.int32) for v in _moe_weight_plan(n_groups, epg))

    grid_spec = pltpu.PrefetchScalarGridSpec(
        num_scalar_prefetch=9,
        grid=(n_tiles,),
        in_specs=[pl.BlockSpec(memory_space=pl.ANY),
                  pl.BlockSpec((1, d), lambda j, *_: (0, 0)),
                  pl.BlockSpec(memory_space=pl.ANY),
                  pl.BlockSpec(memory_space=pl.ANY),
                  pl.BlockSpec(memory_space=pl.ANY)],
        out_specs=pl.BlockSpec((tm, d), lambda j, *_: (j, 0)),
        scratch_shapes=[pltpu.VMEM((MOE_GATHER_AHEAD + 1, tm, d + LANES), _F32),
                        pltpu.SemaphoreType.DMA((MOE_GATHER_AHEAD + 1,)),
                        pltpu.VMEM((MOE_WEIGHT_SLOTS, d, f), _BF16),
                        pltpu.VMEM((MOE_WEIGHT_SLOTS, d, f), _BF16),
                        pltpu.VMEM((MOE_WEIGHT_SLOTS, f, d), _BF16),
                        pltpu.VMEM((MOE_DEPTH, d // MOE_PARTS, f), _F32),
                        pltpu.VMEM((MOE_DEPTH, f // MOE_PARTS, d), _F32),
                        pltpu.SemaphoreType.DMA((MOE_DEPTH,)),
                        pltpu.SMEM((2,), jnp.int32),
                        pltpu.SMEM((n_tiles * tm,), jnp.int32),
                        pltpu.SMEM((n_tiles,), jnp.int32),
                        pltpu.SMEM((n_buckets,), jnp.int32),
                        pltpu.SMEM((1,), jnp.int32)],
    )
    return pl.pallas_call(
        kern,
        grid_spec=grid_spec,
        out_shape=jax.ShapeDtypeStruct((n_tiles * tm, d), _F32),
        compiler_params=_params(1),
        name="moe",
    )(tok_bucket, tok_rank, bucket_cnt, need, allow, slot_a, slot_b, le, ls, xe, n2, wg, wu, wd)


def _ple_kernel(tokb_ref, tokr_ref, cntb_ref, x2s_hbm, p_ref, ng_ref, wg_ref, wp_ref, nf_ref, o_ref,
                xbuf, gsem, start_s, *, tm, n_steps, tm_sorted):
    i = pl.program_id(0)
    slot = i % 2

    def sorted_row(tile):
        return lambda r: start_s[tokb_ref[tile * tm + r]] + tokr_ref[tile * tm + r]

    @pl.when(i == 0)
    def _():
        _bucket_starts(cntb_ref, start_s, start_s.shape[0], tm_sorted)
        _row_gather_start(sorted_row(0), x2s_hbm, xbuf, 0, gsem, tm, unrolled=False)

    nxt = jnp.where(i + 1 == n_steps, 0, i + 1)

    def step(cur):
        _row_gather_wait(x2s_hbm, xbuf, cur, gsem, tm)
        _row_gather_start(sorted_row(nxt), x2s_hbm, xbuf, 1 - cur, gsem, tm, unrolled=True)
        x2 = xbuf[cur]
        g = _sigmoid(jnp.dot(_rms(x2, ng_ref[...]).astype(_BF16), wg_ref[...], preferred_element_type=_F32))
        e = jnp.dot(p_ref[...].astype(_BF16), wp_ref[...], preferred_element_type=_F32)
        o_ref[...] = _rms(x2 + g * e, nf_ref[...])

        @pl.when(i == n_steps - 1)
        def _():
            _row_gather_wait(x2s_hbm, xbuf, 1 - cur, gsem, tm)

    for parity in range(2):
        pl.when(slot == parity)(functools.partial(step, parity))


def _ple(tok_bucket, tok_rank, bucket_cnt, x2s, p2d, ng, wg, wp, nf, tm, t, tm_sorted):
    d = x2s.shape[1]
    pd = p2d.shape[1]
    n_steps = t // tm
    kern = functools.partial(_ple_kernel, tm=tm, n_steps=n_steps, tm_sorted=tm_sorted)

    def cspec(shape):
        nd = len(shape)
        return pl.BlockSpec(shape, lambda i, *_: (0,) * nd, pipeline_mode=pl.Buffered(1))

    grid_spec = pltpu.PrefetchScalarGridSpec(
        num_scalar_prefetch=3,
        grid=(n_steps,),
        in_specs=[pl.BlockSpec(memory_space=pl.ANY),
                  pl.BlockSpec((tm, pd), lambda i, *_: (i, 0)),
                  cspec(ng.shape), cspec(wg.shape), cspec(wp.shape), cspec(nf.shape)],
        out_specs=pl.BlockSpec((tm, d), lambda i, *_: (i, 0)),
        scratch_shapes=[pltpu.VMEM((2, tm, d), _F32),
                        pltpu.SemaphoreType.DMA((2,)),
                        pltpu.SMEM((bucket_cnt.shape[0],), jnp.int32)],
    )
    return pl.pallas_call(
        kern,
        grid_spec=grid_spec,
        out_shape=jax.ShapeDtypeStruct((t, d), _F32),
        compiler_params=_params(1),
        name="ple",
    )(tok_bucket, tok_rank, bucket_cnt, x2s, p2d, ng, wg, wp, nf)


def _block_diag(w, per_block):
    h, hd, _ = w.shape
    nb = h // per_block
    w4 = w.reshape(nb, per_block, hd, hd)
    rows = [jnp.pad(w4[:, p], ((0, 0), (0, 0), (p * hd, (per_block - 1 - p) * hd))) for p in range(per_block)]
    return jnp.concatenate(rows, axis=1)


def _layer(x2d, p2d, bsz, seq, norm1_g, w_in, conv_w, conv_b, w_rg_a, b_rg_a, w_rg_x, b_rg_x, lru_lambda,
           w_pool, pool_scale, w_branch_a, w_branch_b, w_out, norm2_g, w_router_group, b_router_group,
           w_router_expert, b_router_expert, w_e_gate, w_e_up, w_e_down, norm_ple_g, w_ple_gate,
           w_ple_proj, out_norm_g):
    t, d = x2d.shape
    c = conv_b.shape[0]
    heads, hd, _ = w_rg_a.shape
    n_groups = w_router_group.shape[1]
    n_exp = w_router_expert.shape[1]
    epg = n_exp // n_groups
    assert epg == 4 and TOP_K == 2 and hd * (MXU_DIM // hd) == MXU_DIM
    assert w_pool.shape[0] == len(POOL_WINDOWS) and w_pool.shape[1] == MXU_DIM

    tm_in, tn_in, rc_in = min(2048, t), 512, min(256, t)
    tm_mix = min(256, seq)
    tm_cmb = min(256, t)
    tm_moe = min(128, t)
    tm_ple = min(512, t)

    row = lambda v: v.reshape(1, -1).astype(_F32)
    per_block = MXU_DIM // hd

    z = _inproj(x2d, row(norm1_g), w_in, tm_in, tn_in, rc_in)
    mixer_w = (conv_w.reshape(CONV_WIDTH, c), row(conv_b),
               _block_diag(w_rg_a, per_block).astype(_BF16), row(b_rg_a),
               _block_diag(w_rg_x, per_block).astype(_BF16), row(b_rg_x),
               row(lru_lambda), w_pool.astype(_BF16), row(pool_scale))

    n_rt = n_groups + n_exp
    wr = jnp.pad(jnp.concatenate([w_router_group, w_router_expert], axis=1),
                 ((0, 0), (0, LANES - n_rt))).astype(_BF16)
    br = jnp.pad(jnp.concatenate([b_router_group, b_router_expert]), (0, LANES - n_rt)).reshape(1, LANES)
    xe, counts, meta = _mix_combine(z, x2d, seq, mixer_w, w_branch_a.astype(_BF16), w_branch_b.astype(_BF16),
                                    w_out.astype(_BF16), row(norm2_g), wr, br, tm_cmb, n_groups, epg)

    n_buckets = n_groups * _N_PAIRS
    tok_bucket, tok_rank = meta[0], meta[1]
    bucket_cnt = counts[0, :n_buckets].astype(jnp.int32)

    x2s = _moe(tok_bucket, tok_rank, bucket_cnt, xe, row(norm2_g), w_e_gate, w_e_up, w_e_down,
               tm_moe, n_groups, epg)
    return _ple(tok_bucket, tok_rank, bucket_cnt, x2s, p2d, row(norm_ple_g), w_ple_gate.astype(_BF16),
                w_ple_proj.astype(_BF16), row(out_norm_g), tm_ple, t, tm_moe)


def kernel(x, p, norm1_g, w_in, conv_w, conv_b, w_rg_a, b_rg_a, w_rg_x, b_rg_x, lru_lambda, w_pool, pool_scale, w_branch_a, w_branch_b, w_out, norm2_g, w_router_group, b_router_group, w_router_expert, b_router_expert, w_e_gate, w_e_up, w_e_down, norm_ple_g, w_ple_gate, w_ple_proj, final_norm_g):
    bsz, seq, d = x.shape
    depth = p.shape[0]
    assert depth == 1, "the final RMSNorm is fused into the last layer's embedding kernel"
    out = _layer(x.reshape(bsz * seq, d), p[0].reshape(bsz * seq, -1), bsz, seq,
                 norm1_g[0], w_in[0], conv_w[0], conv_b[0], w_rg_a[0], b_rg_a[0], w_rg_x[0], b_rg_x[0],
                 lru_lambda[0], w_pool[0], pool_scale[0], w_branch_a[0], w_branch_b[0], w_out[0],
                 norm2_g[0], w_router_group[0], b_router_group[0], w_router_expert[0],
                 b_router_expert[0], w_e_gate[0], w_e_up[0], w_e_down[0], norm_ple_g[0],
                 w_ple_gate[0], w_ple_proj[0], final_norm_g)
    return out.reshape(bsz, seq, d)
```

```python
import functools

import jax
import jax.numpy as jnp
from jax import lax
from jax.experimental import pallas as pl
from jax.experimental.pallas import tpu as pltpu

EPS = 1e-6
LRU_C = 8.0
CONV_WIDTH = 4
POOL_WINDOWS = (2, 4, 8, 16)
TOP_K = 2
SQRT_GUARD = 1e-30

LANES = 128
SUBLANES = 8
MXU_DIM = 256
VMEM_LIMIT_BYTES = 56 * 1024 * 1024

_BF16 = jnp.bfloat16
_F32 = jnp.float32

_PAIR_SLOT_A = (0, 0, 0, 1, 1, 3)
_PAIR_SLOT_B = (1, 2, 3, 3, 2, 2)
_N_PAIRS = len(_PAIR_SLOT_A)


def _sigmoid(x):
    return 0.5 * jnp.tanh(0.5 * x) + 0.5


def _rms(x, g):
    ms = jnp.mean(x * x, axis=-1, keepdims=True)
    return x * lax.rsqrt(ms + EPS) * g


def _const_spec(shape):
    nd = len(shape)
    return pl.BlockSpec(shape, lambda *_: (0,) * nd, pipeline_mode=pl.Buffered(1))


def _params(n_axes, flags=None):
    return pltpu.CompilerParams(dimension_semantics=("arbitrary",) * n_axes,
                                vmem_limit_bytes=VMEM_LIMIT_BYTES, flags=flags)


def _inproj_kernel(x_hbm, g_ref, w_ref, z_ref, h_ref, xs_ref, sem, *, tm, rc):
    i = pl.program_id(0)
    j = pl.program_id(1)
    n_chunks = tm // rc
    more_tiles = i + 1 < pl.num_programs(0)

    def chunk_copy(tile, c):
        return pltpu.make_async_copy(x_hbm.at[pl.ds(pl.multiple_of(tile * tm + c * rc, rc), rc)],
                                     xs_ref.at[c % 2], sem.at[c % 2])

    def normalise(tile, c):
        h_ref[tile % 2, pl.ds(pl.multiple_of(c * rc, rc), rc), :] = _rms(xs_ref[c % 2], g_ref[...]).astype(_BF16)

    @pl.when(jnp.logical_and(i == 0, j == 0))
    def _():
        chunk_copy(0, 0).start()
        for c in range(n_chunks):
            if c + 1 < n_chunks:
                chunk_copy(0, c + 1).start()
            chunk_copy(0, c).wait()
            normalise(0, c)

    @pl.when(jnp.logical_and(more_tiles, jnp.logical_and(j >= 1, j <= n_chunks)))
    def _():
        chunk_copy(i + 1, j - 1).wait()
        normalise(i + 1, j - 1)

    @pl.when(jnp.logical_and(more_tiles, j < n_chunks))
    def _():
        chunk_copy(i + 1, j).start()

    z_ref[...] = jnp.dot(h_ref[i % 2], w_ref[...].astype(_BF16), preferred_element_type=_F32)


def _inproj(x2d, g, w, tm, tn, rc):
    t, d = x2d.shape
    n = w.shape[1]
    kern = functools.partial(_inproj_kernel, tm=tm, rc=rc)
    assert n // tn > tm // rc, "a row tile's chunks are prepared during the column steps of the previous tile"
    return pl.pallas_call(
        kern,
        grid=(t // tm, n // tn),
        in_specs=[pl.BlockSpec(memory_space=pl.ANY),
                  pl.BlockSpec((1, d), lambda i, j: (0, 0)),
                  pl.BlockSpec((d, tn), lambda i, j: (0, j))],
        out_specs=pl.BlockSpec((tm, tn), lambda i, j: (i, j)),
        out_shape=jax.ShapeDtypeStruct((t, n), _F32),
        scratch_shapes=[pltpu.VMEM((2, tm, d), _BF16),
                        pltpu.VMEM((2, rc, d), _F32),
                        pltpu.SemaphoreType.DMA((2,))],
        compiler_params=_params(2),
        name="inproj",
    )(x2d, g, w)


def _scan_pitch(tm):
    p = -(-tm // SUBLANES)
    while p % SUBLANES != 4:
        p += 1
    return p


def _mixer_init(er_ref, ep_ref, a_ref, b_ref, car_ref, tm):
    er_ref[...] = jnp.zeros(er_ref.shape, _F32)
    ep_ref[...] = jnp.zeros(ep_ref.shape, _F32)
    car_ref[...] = jnp.zeros(car_ref.shape, _F32)
    a_ref[:, tm:, :] = jnp.ones((a_ref.shape[0], a_ref.shape[1] - tm, LANES), _F32)
    b_ref[:, tm:, :] = jnp.zeros((b_ref.shape[0], b_ref.shape[1] - tm, LANES), _F32)


def _mixer_tile(z_ref, cw_ref, cb_ref, wa_ref, ba_ref, wx_ref, bx_ref, lam_ref, wp_ref, ps_ref,
                ya_ref, yb_ref, er_ref, ep_ref, a_ref, b_ref, h_ref, car_ref, *, first, t0, tm, pitch):
    c = cb_ref.shape[1]
    n_slab = c // LANES
    hist_r = SUBLANES
    hist_p = 2 * SUBLANES

    er_ref[0:hist_r, :] = jnp.where(first, 0.0, er_ref[tm:tm + hist_r, :])
    ep_ref[0:hist_p, :] = jnp.where(first, 0.0, ep_ref[tm:tm + hist_p, :])
    er_ref[hist_r:hist_r + tm, :] = z_ref[:, 0:c]
    ep_ref[hist_p:hist_p + tm, :] = z_ref[:, 2 * c:3 * c]

    kvec = -LRU_C * jax.nn.softplus(-lam_ref[...])
    nblk = c // MXU_DIM
    for k in range(nblk):
        cs = slice(k * MXU_DIM, (k + 1) * MXU_DIM)
        xc = cb_ref[:, cs] + cw_ref[CONV_WIDTH - 1:CONV_WIDTH, cs] * er_ref[hist_r:hist_r + tm, cs]
        for j in range(1, CONV_WIDTH):
            xc = xc + cw_ref[CONV_WIDTH - 1 - j:CONV_WIDTH - j, cs] * er_ref[hist_r - j:hist_r - j + tm, cs]
        xcb = xc.astype(_BF16)
        r = _sigmoid(jnp.dot(xcb, wa_ref[k], preferred_element_type=_F32) + ba_ref[:, cs])
        ig = _sigmoid(jnp.dot(xcb, wx_ref[k], preferred_element_type=_F32) + bx_ref[:, cs])
        log_a = r * kvec[:, cs]
        a = jnp.exp(log_a)
        v = 1.0 - a * a
        mult = v * lax.rsqrt(jnp.maximum(v, SQRT_GUARD))
        bb = mult * ig * xc
        for q in range(MXU_DIM // LANES):
            slab = k * (MXU_DIM // LANES) + q
            a_ref[slab, 0:tm, :] = a[:, q * LANES:(q + 1) * LANES]
            b_ref[slab, 0:tm, :] = bb[:, q * LANES:(q + 1) * LANES]
        yield

    def seg(i):
        return pl.ds(i, SUBLANES, stride=pitch)

    row = lax.broadcasted_iota(jnp.int32, (SUBLANES, LANES), 0)
    for sl in range(n_slab):
        hh = jnp.zeros((SUBLANES, LANES), _F32)
        aa = jnp.ones((SUBLANES, LANES), _F32)
        for i in range(pitch):
            av = a_ref[sl, seg(i), :]
            hh = av * hh + b_ref[sl, seg(i), :]
            aa = av * aa
        d = 1
        while d < SUBLANES:
            hs_ = jnp.where(row >= d, pltpu.roll(hh, d, 0), 0.0)
            as_ = jnp.where(row >= d, pltpu.roll(aa, d, 0), 1.0)
            hh = aa * hs_ + hh
            aa = aa * as_
            d *= 2
        cs = slice(sl * LANES, (sl + 1) * LANES)
        cin = jnp.where(first, 0.0, car_ref[:, cs])
        full = hh + aa * cin
        hv = jnp.where(row >= 1, pltpu.roll(full, 1, 0), cin)
        car_ref[:, cs] = jnp.broadcast_to(full[SUBLANES - 1:SUBLANES, :], (SUBLANES, LANES))
        for i in range(pitch):
            hv = a_ref[sl, seg(i), :] * hv + b_ref[sl, seg(i), :]
            h_ref[sl, seg(i), :] = hv
        g = z_ref[:, c + sl * LANES:c + (sl + 1) * LANES]
        ya_ref[:, cs] = (h_ref[sl, 0:tm, :] * jax.nn.gelu(g)).astype(_BF16)
        yield

    t_idx = (t0 + lax.broadcasted_iota(jnp.int32, (tm, 1), 0) + 1).astype(_F32)
    n_grp = len(POOL_WINDOWS)
    gd = c // n_grp
    for gi, w in enumerate(POOL_WINDOWS):
        cs = slice(gi * gd, (gi + 1) * gd)
        e = ep_ref[:, cs]
        acc = e
        d = 1
        while d < w:
            acc = acc + pltpu.roll(acc, d, 0)
            d *= 2
        xt = e[hist_p:, :]
        cnt = jnp.minimum(t_idx, float(w))
        dd = acc[hist_p:, :] / cnt - xt
        yb = jnp.dot(dd.astype(_BF16), wp_ref[gi], preferred_element_type=_F32) * ps_ref[:, cs]
        yb_ref[:, cs] = yb.astype(_BF16)
        yield


def _combine_tile(ya_ref, yb_ref, ga0_ref, ga1_ref, gb0_ref, gb1_ref, x_ref,
                  wa_ref, wb_ref, wo_ref, n2_ref, wr_ref, br_ref,
                  xe_ref, cnt_ref, meta_ref, cnt_s, *, live, tm, n_groups, epg):
    d = x_ref.shape[1]
    half = d // 2
    piece = MXU_DIM * 2
    ya = ya_ref[...]
    yb = yb_ref[...]
    us = []
    for q in range(d // piece):
        cs = slice(q * piece, (q + 1) * piece)
        ga_ref, gb_ref = (ga0_ref, gb0_ref) if q * piece < half else (ga1_ref, gb1_ref)
        gs = slice((q * piece) % half, (q * piece) % half + piece)
        pa = jnp.dot(ya, wa_ref[:, cs], preferred_element_type=_F32)
        yield
        pb = jnp.dot(yb, wb_ref[:, cs], preferred_element_type=_F32)
        yield
        ta = jnp.tanh(0.5 * ga_ref[:, gs])
        tb = jnp.tanh(0.5 * gb_ref[:, gs])
        us.append((0.5 * ((pa + pb) + (ta * pa + tb * pb))).astype(_BF16))
    u = jnp.concatenate(us, axis=1)
    for q in range(d // MXU_DIM):
        cs = slice(q * MXU_DIM, (q + 1) * MXU_DIM)
        xe_ref[:, cs] = x_ref[:, cs] + jnp.dot(u, wo_ref[:, cs], preferred_element_type=_F32)
        yield
    x1 = xe_ref[:, 0:d]

    ht = _rms(x1, n2_ref[...]).astype(_BF16)
    logits = jnp.dot(ht, wr_ref[...], preferred_element_type=_F32) + br_ref[...]

    lane = lax.broadcasted_iota(jnp.int32, (tm, LANES), 1).astype(_F32)
    ninf = -jnp.inf
    big = float(LANES)

    def first_argmax(v):
        m = jnp.max(v, axis=-1, keepdims=True)
        return m, jnp.min(jnp.where(v == m, lane, big), axis=-1, keepdims=True)

    is_g = lane < float(n_groups)
    gmax, gidx = first_argmax(jnp.where(is_g, logits, ninf))
    g_w = 1.0 / jnp.sum(jnp.where(is_g, jnp.exp(logits - gmax), 0.0), axis=-1, keepdims=True)
    lo_lane = float(n_groups) + float(epg) * gidx
    in_grp = (lane >= lo_lane) & (lane < lo_lane + float(epg))
    le = jnp.where(in_grp, logits, ninf)
    m1, i1 = first_argmax(le)
    m2, i2 = first_argmax(jnp.where(lane == i1, ninf, le))
    e21 = jnp.exp(m2 - m1)
    w1 = g_w / (1.0 + e21)
    w2 = w1 * e21
    e1 = i1 - lo_lane
    e2 = i2 - lo_lane
    lo = jnp.minimum(e1, e2)
    hi = jnp.maximum(e1, e2)
    w_lo = jnp.where(e1 < e2, w1, w2)
    w_hi = jnp.where(e1 < e2, w2, w1)
    pair = jnp.where(lo == 0.0, hi - 1.0, jnp.where(lo == 1.0, 6.0 - hi, 5.0))
    swap = pair == 5.0
    w_a = jnp.where(swap, w_hi, w_lo)
    w_b = jnp.where(swap, w_lo, w_hi)
    bucket = float(_N_PAIRS) * gidx + pair

    onehot = lane == bucket
    oh_bf = jnp.where(onehot, 1.0, 0.0).astype(_BF16)
    rr = lax.broadcasted_iota(jnp.int32, (tm, tm), 0)
    cc = lax.broadcasted_iota(jnp.int32, (tm, tm), 1)
    tri = jnp.where(cc < rr, 1.0, 0.0).astype(_BF16)
    before = jnp.dot(tri, oh_bf, preferred_element_type=_F32) + cnt_s[...]
    rank = jnp.sum(jnp.where(onehot, before, 0.0), axis=-1, keepdims=True)
    cnt_s[...] = cnt_s[...] + jnp.where(live, jnp.sum(jnp.where(onehot, 1.0, 0.0), axis=0, keepdims=True), 0.0)
    cnt_ref[...] = cnt_s[...]

    info = jnp.where(lane == 0.0, bucket,
                     jnp.where(lane == 1.0, rank,
                               jnp.where(lane == 2.0, w_a, jnp.where(lane == 3.0, w_b, 0.0))))
    xe_ref[:, d:d + LANES] = info
    meta_ref[...] = jnp.transpose(info)[0:SUBLANES, :].astype(jnp.int32)


def _mix_combine_kernel(*refs, tm, pitch, ns, n_tiles, n_groups, epg):
    mix_in, cmb_in = refs[0:10], refs[10:21]
    xe_ref, cnt_ref, meta_ref = refs[21:24]
    ynew_ref, yold_ref, er_ref, ep_ref, a_ref, b_ref, h_ref, car_ref, cnt_s = refs[24:]
    s = pl.program_id(0)

    @pl.when(s == 0)
    def _():
        _mixer_init(er_ref, ep_ref, a_ref, b_ref, car_ref, tm)
        cnt_s[...] = jnp.zeros(cnt_s.shape, _F32)
        yold_ref[...] = jnp.zeros(yold_ref.shape, _BF16)

    seq_pos = lax.rem(jnp.minimum(s, n_tiles - 1), ns)
    mixer = _mixer_tile(*mix_in, ynew_ref.at[0], ynew_ref.at[1], er_ref, ep_ref, a_ref, b_ref, h_ref,
                        car_ref, first=seq_pos == 0, t0=seq_pos * tm, tm=tm, pitch=pitch)
    combine = _combine_tile(yold_ref.at[0], yold_ref.at[1], *cmb_in, xe_ref, cnt_ref, meta_ref,
                            cnt_s, live=s >= 1, tm=tm, n_groups=n_groups, epg=epg)
    stages = [mixer, combine]
    while stages:
        for stage in list(stages):
            if next(stage, stages) is stages:
                stages.remove(stage)
    yold_ref[...] = ynew_ref[...]


def _mix_combine(z, x2d, seq, mixer_w, wa, wb, wo, n2, wr, br, tm, n_groups, epg):
    t, d = x2d.shape
    c = wa.shape[0]
    half = d // 2
    n_tiles = t // tm
    ns = seq // tm
    pitch = _scan_pitch(tm)
    n_slab = c // LANES
    off = (3 * c) // half
    kern = functools.partial(_mix_combine_kernel, tm=tm, pitch=pitch, ns=ns, n_tiles=n_tiles,
                             n_groups=n_groups, epg=epg)

    def cur(s):
        return jnp.minimum(s, n_tiles - 1)

    def prev(s):
        return jnp.maximum(s - 1, 0)

    def zspec(j):
        return pl.BlockSpec((tm, half), lambda s, j=j: (prev(s), off + j))

    return pl.pallas_call(
        kern,
        grid=(n_tiles + 1,),
        in_specs=[pl.BlockSpec((tm, 3 * c), lambda s: (cur(s), 0))]
                 + [_const_spec(w.shape) for w in mixer_w]
                 + [zspec(0), zspec(1), zspec(2), zspec(3),
                    pl.BlockSpec((tm, d), lambda s: (prev(s), 0)),
                    _const_spec(wa.shape), _const_spec(wb.shape), _const_spec(wo.shape),
                    _const_spec(n2.shape), _const_spec(wr.shape), _const_spec(br.shape)],
        out_specs=[pl.BlockSpec((tm, d + LANES), lambda s: (prev(s), 0)),
                   pl.BlockSpec((1, LANES), lambda s: (0, 0)),
                   pl.BlockSpec((SUBLANES, tm), lambda s: (0, prev(s)))],
        out_shape=[jax.ShapeDtypeStruct((t, d + LANES), _F32),
                   jax.ShapeDtypeStruct((1, LANES), _F32),
                   jax.ShapeDtypeStruct((SUBLANES, t), jnp.int32)],
        scratch_shapes=[pltpu.VMEM((2, tm, c), _BF16),
                        pltpu.VMEM((2, tm, c), _BF16),
                        pltpu.VMEM((SUBLANES + tm, c), _F32),
                        pltpu.VMEM((2 * SUBLANES + tm, c), _F32),
                        pltpu.VMEM((n_slab, SUBLANES * pitch, LANES), _F32),
                        pltpu.VMEM((n_slab, SUBLANES * pitch, LANES), _F32),
                        pltpu.VMEM((n_slab, SUBLANES * pitch, LANES), _F32),
                        pltpu.VMEM((SUBLANES, c), _F32),
                        pltpu.VMEM((1, LANES), _F32)],
        compiler_params=_params(1),
        name="mix_combine",
    )(z, *mixer_w, z, z, z, z, x2d, wa, wb, wo, n2, wr, br)


def _bucket_starts(cnt_ref, start_ref, n_buckets, tm, on_tile=None):
    def bucket_body(b, tile_idx):
        start_ref[b] = tile_idx * tm
        nt = lax.div(cnt_ref[b] + (tm - 1), tm)
        if on_tile is not None:
            lax.fori_loop(0, nt, lambda k, carry: (on_tile(tile_idx + k, b), carry)[1], 0)
        return tile_idx + nt

    return lax.fori_loop(0, n_buckets, bucket_body, 0)


def _row_gather_start(row_of, src_hbm, buf, slot, sem, n_rows, unrolled):
    def start(r, priority):
        row = row_of(r)
        pltpu.make_async_copy(src_hbm.at[pl.ds(row, 1)], buf.at[slot, pl.ds(r, 1)],
                              sem.at[slot]).start(priority=priority)

    if unrolled:
        for r in range(n_rows):
            start(r, r % 2)
    else:
        def body(r, _):
            start(2 * r, 0)
            start(2 * r + 1, 1)
            return 0
        lax.fori_loop(0, n_rows // 2, body, 0, unroll=4)


def _row_gather_wait(src_hbm, buf, slot, sem, n_rows):
    pltpu.make_async_copy(src_hbm.at[pl.ds(0, n_rows)], buf.at[slot], sem.at[slot]).wait()


MOE_WEIGHT_SLOTS = 6
MOE_PARTS = 4
MOE_CHUNKS = 3 * MOE_PARTS
MOE_PUMP = 3
MOE_DEPTH = 4
MOE_GATHER_AHEAD = 1
MOE_PLACE_SEGMENT = 512


def _moe_weight_plan(n_groups, epg):
    first_use = {e: min(p for p in range(_N_PAIRS) if e in (_PAIR_SLOT_A[p], _PAIR_SLOT_B[p])) for e in range(epg)}
    last_use = {e: max(p for p in range(_N_PAIRS) if e in (_PAIR_SLOT_A[p], _PAIR_SLOT_B[p])) for e in range(epg)}
    free_after = [-1] * MOE_WEIGHT_SLOTS
    loads, slot_of = [], {}
    for g in range(n_groups):
        for e in sorted(range(epg), key=lambda e: (first_use[e], e)):
            needed_by = g * _N_PAIRS + first_use[e]
            s = min(range(MOE_WEIGHT_SLOTS), key=lambda s: (free_after[s], s))
            assert free_after[s] < needed_by
            loads.append((g * epg + e, s, free_after[s], needed_by))
            free_after[s] = g * _N_PAIRS + last_use[e]
            slot_of[(g, e)] = s
    n_buckets = n_groups * _N_PAIRS
    need = [sum(1 for l in loads if l[3] <= b) for b in range(n_buckets)]
    allow = [sum(1 for l in loads if l[2] < b) for b in range(n_buckets)]
    slot_a = [slot_of[(b // _N_PAIRS, _PAIR_SLOT_A[b % _N_PAIRS])] for b in range(n_buckets)]
    slot_b = [slot_of[(b // _N_PAIRS, _PAIR_SLOT_B[b % _N_PAIRS])] for b in range(n_buckets)]
    return [l[0] for l in loads], [l[1] for l in loads], need, allow, slot_a, slot_b


def _moe_kernel(tokb_ref, tokr_ref, cntb_ref, need_ref, allow_ref, sa_ref, sb_ref, le_ref, ls_ref,
                xe_hbm, n2_ref, wg_hbm, wu_hbm, wd_hbm,
                x2s_ref, xbuf, gsem, wg_s, wu_s, wd_s, st_gu, st_d, wsem, cnt, src_s, tbk_s, start_s, tot_s,
                *, tm):
    j = pl.program_id(0)
    slot = lax.rem(j, MOE_GATHER_AHEAD + 1)
    d = x2s_ref.shape[1]
    t_rows = xe_hbm.shape[0]
    n_buckets = start_s.shape[0]

    def src_row(tile):
        return lambda r: src_s[tile * tm + r]

    rows_gu = wg_hbm.shape[1] // MOE_PARTS
    rows_d = wd_hbm.shape[1] // MOE_PARTS

    def chunk_dma(c, kind):
        load = c // MOE_CHUNKS
        part = c % MOE_PARTS
        e = le_ref[load]
        if kind == 2:
            return pltpu.make_async_copy(wd_hbm.at[e, pl.ds(part * rows_d, rows_d)], st_d.at[c % MOE_DEPTH],
                                         wsem.at[c % MOE_DEPTH])
        src = wg_hbm if kind == 0 else wu_hbm
        return pltpu.make_async_copy(src.at[e, pl.ds(part * rows_gu, rows_gu)], st_gu.at[c % MOE_DEPTH],
                                     wsem.at[c % MOE_DEPTH])

    def for_kind(c, fn):
        kind = (c % MOE_CHUNKS) // MOE_PARTS
        for k in range(3):
            @pl.when(kind == k)
            def _(k=k):
                fn(k)

    def issue_one(allowed):
        @pl.when(cnt[0] < jnp.minimum(allowed, cnt[1] + MOE_DEPTH))
        def _():
            c = cnt[0]
            for_kind(c, lambda k: chunk_dma(c, k).start())
            cnt[0] = c + 1

    def retire():
        c = cnt[1]
        s = ls_ref[c // MOE_CHUNKS]
        part = c % MOE_PARTS

        def finish(k):
            chunk_dma(c, k).wait()
            if k == 2:
                wd_s[s, pl.ds(part * rows_d, rows_d), :] = st_d[c % MOE_DEPTH].astype(_BF16)
            else:
                dst = wg_s if k == 0 else wu_s
                dst[s, pl.ds(part * rows_gu, rows_gu), :] = st_gu[c % MOE_DEPTH].astype(_BF16)

        for_kind(c, finish)
        cnt[1] = c + 1

    def pump(required, allowed, extra):
        n_iter = jnp.maximum(required - cnt[1], jnp.minimum(extra, allowed - cnt[1]))
        n_fill = jnp.minimum(allowed, cnt[1] + MOE_DEPTH) - cnt[0]
        lax.fori_loop(0, jnp.maximum(n_fill, 0), lambda _, carry: (issue_one(allowed), carry)[1], 0)

        def body(_, carry):
            retire()
            issue_one(allowed)
            return carry

        lax.fori_loop(0, jnp.maximum(n_iter, 0), body, 0)

    @pl.when(j == 0)
    def _():
        cnt[0] = 0
        cnt[1] = 0

        def set_tile(tile, b):
            tbk_s[tile] = b

        total = _bucket_starts(cntb_ref, start_s, n_buckets, tm, on_tile=set_tile)
        tot_s[0] = total

        def fill(lo, hi):
            def body(q, carry):
                src_s[q] = jnp.minimum(jnp.where(q >= t_rows, q - t_rows, q), t_rows - 1)
                return carry
            lax.fori_loop(lo, hi, body, 0)

        def fill_bucket(b, carry):
            nxt = jnp.where(b + 1 < n_buckets, start_s[jnp.minimum(b + 1, n_buckets - 1)], total * tm)
            fill(start_s[b] + cntb_ref[b], nxt)
            return carry
        lax.fori_loop(0, n_buckets, fill_bucket, 0)
        fill(total * tm, (total + MOE_GATHER_AHEAD) * tm)

        def place(tok, carry):
            src_s[start_s[tokb_ref[tok]] + tokr_ref[tok]] = tok
            return carry

        first_allowed = allow_ref[tbk_s[0]] * MOE_CHUNKS
        seg = min(MOE_PLACE_SEGMENT, t_rows)

        def place_segment(sg, carry):
            lax.fori_loop(0, seg, lambda i, c: place(sg * seg + i, c), 0, unroll=16)
            pump(0, first_allowed, 2)
            return carry
        lax.fori_loop(0, t_rows // seg, place_segment, 0)

        for tile in range(MOE_GATHER_AHEAD):
            _row_gather_start(src_row(tile), xe_hbm, xbuf, tile, gsem, tm, unrolled=False)

    n_valid = tot_s[0]

    @pl.when(j < n_valid)
    def _():
        b = tbk_s[j]
        pump(need_ref[b] * MOE_CHUNKS, allow_ref[b] * MOE_CHUNKS, MOE_PUMP)
        s_a = sa_ref[b]
        s_b = sb_ref[b]

        def tile_body(cur):
            _row_gather_wait(xe_hbm, xbuf, cur, gsem, tm)
            _row_gather_start(src_row(j + MOE_GATHER_AHEAD), xe_hbm, xbuf,
                              (cur + MOE_GATHER_AHEAD) % (MOE_GATHER_AHEAD + 1), gsem, tm, unrolled=True)
            xe = xbuf[cur]
            x1 = xe[:, 0:d]
            w_a = xe[:, d + 2:d + 3]
            w_b = xe[:, d + 3:d + 4]
            ht = _rms(x1, n2_ref[...]).astype(_BF16)

            def expert(s, wgt):
                hg = jnp.dot(ht, wg_s[s], preferred_element_type=_F32)
                hu = jnp.dot(ht, wu_s[s], preferred_element_type=_F32)
                half = 0.5 * hg
                return ((half + half * jnp.tanh(half)) * hu * wgt).astype(_BF16)

            y = jnp.dot(expert(s_a, w_a), wd_s[s_a], preferred_element_type=_F32)
            y = y + jnp.dot(expert(s_b, w_b), wd_s[s_b], preferred_element_type=_F32)
            x2s_ref[...] = x1 + y

        for cur in range(MOE_GATHER_AHEAD + 1):
            pl.when(slot == cur)(functools.partial(tile_body, cur))

    @pl.when(j >= n_valid)
    def _():
        x2s_ref[...] = jnp.zeros(x2s_ref.shape, _F32)

        @pl.when(j == n_valid)
        def _():
            for ahead in range(MOE_GATHER_AHEAD):
                _row_gather_wait(xe_hbm, xbuf, lax.rem(j + ahead, MOE_GATHER_AHEAD + 1), gsem, tm)
            lax.fori_loop(0, cnt[0] - cnt[1], lambda _, carry: (retire(), carry)[1], 0)


def _moe(tok_bucket, tok_rank, bucket_cnt, xe, n2, wg, wu, wd, tm, n_groups, epg):
    n_buckets = bucket_cnt.shape[0]
    n_tiles = xe.shape[0] // tm + n_buckets - 1 + MOE_GATHER_AHEAD
    d = wg.shape[1]
    f = wg.shape[2]
    kern = functools.partial(_moe_kernel, tm=tm)
    le, ls, need, allow, slot_a, slot_b = (jnp.asarray(v, jnp.int32) for v in _moe_weight_plan(n_groups, epg))

    grid_spec = pltpu.PrefetchScalarGridSpec(
        num_scalar_prefetch=9,
        grid=(n_tiles,),
        in_specs=[pl.BlockSpec(memory_space=pl.ANY),
                  pl.BlockSpec((1, d), lambda j, *_: (0, 0)),
                  pl.BlockSpec(memory_space=pl.ANY),
                  pl.BlockSpec(memory_space=pl.ANY),
                  pl.BlockSpec(memory_space=pl.ANY)],
        out_specs=pl.BlockSpec((tm, d), lambda j, *_: (j, 0)),
        scratch_shapes=[pltpu.VMEM((MOE_GATHER_AHEAD + 1, tm, d + LANES), _F32),
                        pltpu.SemaphoreType.DMA((MOE_GATHER_AHEAD + 1,)),
                        pltpu.VMEM((MOE_WEIGHT_SLOTS, d, f), _BF16),
                        pltpu.VMEM((MOE_WEIGHT_SLOTS, d, f), _BF16),
                        pltpu.VMEM((MOE_WEIGHT_SLOTS, f, d), _BF16),
                        pltpu.VMEM((MOE_DEPTH, d // MOE_PARTS, f), _F32),
                        pltpu.VMEM((MOE_DEPTH, f // MOE_PARTS, d), _F32),
                        pltpu.SemaphoreType.DMA((MOE_DEPTH,)),
                        pltpu.SMEM((2,), jnp.int32),
                        pltpu.SMEM((n_tiles * tm,), jnp.int32),
                        pltpu.SMEM((n_tiles,), jnp.int32),
                        pltpu.SMEM((n_buckets,), jnp.int32),
                        pltpu.SMEM((1,), jnp.int32)],
    )
    return pl.pallas_call(
        kern,
        grid_spec=grid_spec,
        out_shape=jax.ShapeDtypeStruct((n_tiles * tm, d), _F32),
        compiler_params=_params(1),
        name="moe",
    )(tok_bucket, tok_rank, bucket_cnt, need, allow, slot_a, slot_b, le, ls, xe, n2, wg, wu, wd)


def _ple_kernel(tokb_ref, tokr_ref, cntb_ref, x2s_hbm, p_ref, ng_ref, wg_ref, wp_ref, nf_ref, o_ref,
                xbuf, gsem, start_s, *, tm, n_steps, tm_sorted):
    i = pl.program_id(0)
    slot = i % 2

    def sorted_row(tile):
        return lambda r: start_s[tokb_ref[tile * tm + r]] + tokr_ref[tile * tm + r]

    @pl.when(i == 0)
    def _():
        _bucket_starts(cntb_ref, start_s, start_s.shape[0], tm_sorted)
        _row_gather_start(sorted_row(0), x2s_hbm, xbuf, 0, gsem, tm, unrolled=False)

    nxt = jnp.where(i + 1 == n_steps, 0, i + 1)

    def step(cur):
        _row_gather_wait(x2s_hbm, xbuf, cur, gsem, tm)
        _row_gather_start(sorted_row(nxt), x2s_hbm, xbuf, 1 - cur, gsem, tm, unrolled=True)
        x2 = xbuf[cur]
        g = _sigmoid(jnp.dot(_rms(x2, ng_ref[...]).astype(_BF16), wg_ref[...], preferred_element_type=_F32))
        e = jnp.dot(p_ref[...].astype(_BF16), wp_ref[...], preferred_element_type=_F32)
        o_ref[...] = _rms(x2 + g * e, nf_ref[...])

        @pl.when(i == n_steps - 1)
        def _():
            _row_gather_wait(x2s_hbm, xbuf, 1 - cur, gsem, tm)

    for parity in range(2):
        pl.when(slot == parity)(functools.partial(step, parity))


def _ple(tok_bucket, tok_rank, bucket_cnt, x2s, p2d, ng, wg, wp, nf, tm, t, tm_sorted):
    d = x2s.shape[1]
    pd = p2d.shape[1]
    n_steps = t // tm
    kern = functools.partial(_ple_kernel, tm=tm, n_steps=n_steps, tm_sorted=tm_sorted)

    def cspec(shape):
        nd = len(shape)
        return pl.BlockSpec(shape, lambda i, *_: (0,) * nd, pipeline_mode=pl.Buffered(1))

    grid_spec = pltpu.PrefetchScalarGridSpec(
        num_scalar_prefetch=3,
        grid=(n_steps,),
        in_specs=[pl.BlockSpec(memory_space=pl.ANY),
                  pl.BlockSpec((tm, pd), lambda i, *_: (i, 0)),
                  cspec(ng.shape), cspec(wg.shape), cspec(wp.shape), cspec(nf.shape)],
        out_specs=pl.BlockSpec((tm, d), lambda i, *_: (i, 0)),
        scratch_shapes=[pltpu.VMEM((2, tm, d), _F32),
                        pltpu.SemaphoreType.DMA((2,)),
                        pltpu.SMEM((bucket_cnt.shape[0],), jnp.int32)],
    )
    return pl.pallas_call(
        kern,
        grid_spec=grid_spec,
        out_shape=jax.ShapeDtypeStruct((t, d), _F32),
        compiler_params=_params(1),
        name="ple",
    )(tok_bucket, tok_rank, bucket_cnt, x2s, p2d, ng, wg, wp, nf)


def _block_diag(w, per_block):
    h, hd, _ = w.shape
    nb = h // per_block
    w4 = w.reshape(nb, per_block, hd, hd)
    rows = [jnp.pad(w4[:, p], ((0, 0), (0, 0), (p * hd, (per_block - 1 - p) * hd))) for p in range(per_block)]
    return jnp.concatenate(rows, axis=1)


def _layer(x2d, p2d, bsz, seq, norm1_g, w_in, conv_w, conv_b, w_rg_a, b_rg_a, w_rg_x, b_rg_x, lru_lambda,
           w_pool, pool_scale, w_branch_a, w_branch_b, w_out, norm2_g, w_router_group, b_router_group,
           w_router_expert, b_router_expert, w_e_gate, w_e_up, w_e_down, norm_ple_g, w_ple_gate,
           w_ple_proj, out_norm_g):
    t, d = x2d.shape
    c = conv_b.shape[0]
    heads, hd, _ = w_rg_a.shape
    n_groups = w_router_group.shape[1]
    n_exp = w_router_expert.shape[1]
    epg = n_exp // n_groups
    assert epg == 4 and TOP_K == 2 and hd * (MXU_DIM // hd) == MXU_DIM
    assert w_pool.shape[0] == len(POOL_WINDOWS) and w_pool.shape[1] == MXU_DIM

    tm_in, tn_in, rc_in = min(2048, t), 512, min(256, t)
    tm_mix = min(256, seq)
    tm_cmb = min(256, t)
    tm_moe = min(128, t)
    tm_ple = min(512, t)

    row = lambda v: v.reshape(1, -1).astype(_F32)
    per_block = MXU_DIM // hd

    z = _inproj(x2d, row(norm1_g), w_in, tm_in, tn_in, rc_in)
    mixer_w = (conv_w.reshape(CONV_WIDTH, c), row(conv_b),
               _block_diag(w_rg_a, per_block).astype(_BF16), row(b_rg_a),
               _block_diag(w_rg_x, per_block).astype(_BF16), row(b_rg_x),
               row(lru_lambda), w_pool.astype(_BF16), row(pool_scale))

    n_rt = n_groups + n_exp
    wr = jnp.pad(jnp.concatenate([w_router_group, w_router_expert], axis=1),
                 ((0, 0), (0, LANES - n_rt))).astype(_BF16)
    br = jnp.pad(jnp.concatenate([b_router_group, b_router_expert]), (0, LANES - n_rt)).reshape(1, LANES)
    xe, counts, meta = _mix_combine(z, x2d, seq, mixer_w, w_branch_a.astype(_BF16), w_branch_b.astype(_BF16),
                                    w_out.astype(_BF16), row(norm2_g), wr, br, tm_cmb, n_groups, epg)

    n_buckets = n_groups * _N_PAIRS
    tok_bucket, tok_rank = meta[0], meta[1]
    bucket_cnt = counts[0, :n_buckets].astype(jnp.int32)

    x2s = _moe(tok_bucket, tok_rank, bucket_cnt, xe, row(norm2_g), w_e_gate, w_e_up, w_e_down,
               tm_moe, n_groups, epg)
    return _ple(tok_bucket, tok_rank, bucket_cnt, x2s, p2d, row(norm_ple_g), w_ple_gate.astype(_BF16),
                w_ple_proj.astype(_BF16), row(out_norm_g), tm_ple, t, tm_moe)


def kernel(x, p, norm1_g, w_in, conv_w, conv_b, w_rg_a, b_rg_a, w_rg_x, b_rg_x, lru_lambda, w_pool, pool_scale, w_branch_a, w_branch_b, w_out, norm2_g, w_router_group, b_router_group, w_router_expert, b_router_expert, w_e_gate, w_e_up, w_e_down, norm_ple_g, w_ple_gate, w_ple_proj, final_norm_g):
    bsz, seq, d = x.shape
    depth = p.shape[0]
    assert depth == 1, "the final RMSNorm is fused into the last layer's embedding kernel"
    out = _layer(x.reshape(bsz * seq, d), p[0].reshape(bsz * seq, -1), bsz, seq,
                 norm1_g[0], w_in[0], conv_w[0], conv_b[0], w_rg_a[0], b_rg_a[0], w_rg_x[0], b_rg_x[0],
                 lru_lambda[0], w_pool[0], pool_scale[0], w_branch_a[0], w_branch_b[0], w_out[0],
                 norm2_g[0], w_router_group[0], b_router_group[0], w_router_expert[0],
                 b_router_expert[0], w_e_gate[0], w_e_up[0], w_e_down[0], norm_ple_g[0],
                 w_ple_gate[0], w_ple_proj[0], final_norm_g)
    return out.reshape(bsz, seq, d)
```

```python
import functools

import jax
import jax.numpy as jnp
from jax import lax
from jax.experimental import pallas as pl
from jax.experimental.pallas import tpu as pltpu

EPS = 1e-6
LRU_C = 8.0
CONV_WIDTH = 4
POOL_WINDOWS = (2, 4, 8, 16)
TOP_K = 2
SQRT_GUARD = 1e-30
WEIGHT_STAGE_ROWS = 128

LANES = 128
SUBLANES = 8
MXU_DIM = 256
VMEM_LIMIT_BYTES = 56 * 1024 * 1024

_BF16 = jnp.bfloat16
_F32 = jnp.float32

_PAIR_SLOT_A = (0, 0, 0, 1, 1, 3)
_PAIR_SLOT_B = (1, 2, 3, 3, 2, 2)
_N_PAIRS = len(_PAIR_SLOT_A)


def _sigmoid(x):
    return 0.5 * jnp.tanh(0.5 * x) + 0.5


def _rms(x, g):
    ms = jnp.mean(x * x, axis=-1, keepdims=True)
    return x * lax.rsqrt(ms + EPS) * g


def _load_cast_weight(w_hbm, w_s, stage, sem):
    rows = stage.shape[1]
    n_chunks = w_hbm.shape[0] // rows

    def chunk_copy(c):
        return pltpu.make_async_copy(w_hbm.at[pl.ds(c * rows, rows)], stage.at[c % 2], sem.at[c % 2])

    chunk_copy(0).start()
    for c in range(n_chunks):
        if c + 1 < n_chunks:
            chunk_copy(c + 1).start()
        chunk_copy(c).wait()
        w_s[c * rows:(c + 1) * rows, :] = stage[c % 2].astype(_BF16)


def _const_spec(shape):
    nd = len(shape)
    return pl.BlockSpec(shape, lambda *_: (0,) * nd, pipeline_mode=pl.Buffered(1))


def _params(n_axes, flags=None):
    return pltpu.CompilerParams(dimension_semantics=("arbitrary",) * n_axes,
                                vmem_limit_bytes=VMEM_LIMIT_BYTES, flags=flags)


def _inproj_kernel(x_hbm, g_ref, w_ref, z_ref, h_ref, xs_ref, sem, *, tm, rc):
    i = pl.program_id(0)
    j = pl.program_id(1)
    n_chunks = tm // rc
    more_tiles = i + 1 < pl.num_programs(0)

    def chunk_copy(tile, c):
        return pltpu.make_async_copy(x_hbm.at[pl.ds(pl.multiple_of(tile * tm + c * rc, rc), rc)],
                                     xs_ref.at[c % 2], sem.at[c % 2])

    def normalise(tile, c):
        h_ref[tile % 2, pl.ds(pl.multiple_of(c * rc, rc), rc), :] = _rms(xs_ref[c % 2], g_ref[...]).astype(_BF16)

    @pl.when(jnp.logical_and(i == 0, j == 0))
    def _():
        chunk_copy(0, 0).start()
        for c in range(n_chunks):
            if c + 1 < n_chunks:
                chunk_copy(0, c + 1).start()
            chunk_copy(0, c).wait()
            normalise(0, c)

    @pl.when(jnp.logical_and(more_tiles, jnp.logical_and(j >= 1, j <= n_chunks)))
    def _():
        chunk_copy(i + 1, j - 1).wait()
        normalise(i + 1, j - 1)

    @pl.when(jnp.logical_and(more_tiles, j < n_chunks))
    def _():
        chunk_copy(i + 1, j).start()

    z_ref[...] = jnp.dot(h_ref[i % 2], w_ref[...].astype(_BF16), preferred_element_type=_F32)


def _inproj(x2d, g, w, tm, tn, rc):
    t, d = x2d.shape
    n = w.shape[1]
    kern = functools.partial(_inproj_kernel, tm=tm, rc=rc)
    assert n // tn > tm // rc, "a row tile's chunks are prepared during the column steps of the previous tile"
    return pl.pallas_call(
        kern,
        grid=(t // tm, n // tn),
        in_specs=[pl.BlockSpec(memory_space=pl.ANY),
                  pl.BlockSpec((1, d), lambda i, j: (0, 0)),
                  pl.BlockSpec((d, tn), lambda i, j: (0, j))],
        out_specs=pl.BlockSpec((tm, tn), lambda i, j: (i, j)),
        out_shape=jax.ShapeDtypeStruct((t, n), _F32),
        scratch_shapes=[pltpu.VMEM((2, tm, d), _BF16),
                        pltpu.VMEM((2, rc, d), _F32),
                        pltpu.SemaphoreType.DMA((2,))],
        compiler_params=_params(2),
        name="inproj",
    )(x2d, g, w)


def _scan_pitch(tm):
    p = -(-tm // SUBLANES)
    while p % SUBLANES != 4:
        p += 1
    return p


def _mixer_init(er_ref, ep_ref, a_ref, b_ref, car_ref, tm):
    er_ref[...] = jnp.zeros(er_ref.shape, _F32)
    ep_ref[...] = jnp.zeros(ep_ref.shape, _F32)
    car_ref[...] = jnp.zeros(car_ref.shape, _F32)
    a_ref[:, tm:, :] = jnp.ones((a_ref.shape[0], a_ref.shape[1] - tm, LANES), _F32)
    b_ref[:, tm:, :] = jnp.zeros((b_ref.shape[0], b_ref.shape[1] - tm, LANES), _F32)


def _mixer_tile(z_ref, cw_ref, cb_ref, wa_ref, ba_ref, wx_ref, bx_ref, lam_ref, wp_ref, ps_ref,
                ya_ref, yb_ref, er_ref, ep_ref, a_ref, b_ref, h_ref, car_ref, *, first, t0, tm, pitch):
    c = cb_ref.shape[1]
    n_slab = c // LANES
    hist_r = SUBLANES
    hist_p = 2 * SUBLANES

    er_ref[0:hist_r, :] = jnp.where(first, 0.0, er_ref[tm:tm + hist_r, :])
    ep_ref[0:hist_p, :] = jnp.where(first, 0.0, ep_ref[tm:tm + hist_p, :])
    er_ref[hist_r:hist_r + tm, :] = z_ref[:, 0:c]
    ep_ref[hist_p:hist_p + tm, :] = z_ref[:, 2 * c:3 * c]

    kvec = -LRU_C * jax.nn.softplus(-lam_ref[...])
    nblk = c // MXU_DIM
    for k in range(nblk):
        cs = slice(k * MXU_DIM, (k + 1) * MXU_DIM)
        xc = cb_ref[:, cs] + cw_ref[CONV_WIDTH - 1:CONV_WIDTH, cs] * er_ref[hist_r:hist_r + tm, cs]
        for j in range(1, CONV_WIDTH):
            xc = xc + cw_ref[CONV_WIDTH - 1 - j:CONV_WIDTH - j, cs] * er_ref[hist_r - j:hist_r - j + tm, cs]
        xcb = xc.astype(_BF16)
        r = _sigmoid(jnp.dot(xcb, wa_ref[k], preferred_element_type=_F32) + ba_ref[:, cs])
        ig = _sigmoid(jnp.dot(xcb, wx_ref[k], preferred_element_type=_F32) + bx_ref[:, cs])
        log_a = r * kvec[:, cs]
        a = jnp.exp(log_a)
        v = 1.0 - a * a
        mult = v * lax.rsqrt(jnp.maximum(v, SQRT_GUARD))
        bb = mult * ig * xc
        for q in range(MXU_DIM // LANES):
            slab = k * (MXU_DIM // LANES) + q
            a_ref[slab, 0:tm, :] = a[:, q * LANES:(q + 1) * LANES]
            b_ref[slab, 0:tm, :] = bb[:, q * LANES:(q + 1) * LANES]
        yield

    def seg(i):
        return pl.ds(i, SUBLANES, stride=pitch)

    row = lax.broadcasted_iota(jnp.int32, (SUBLANES, LANES), 0)
    for sl in range(n_slab):
        hh = jnp.zeros((SUBLANES, LANES), _F32)
        aa = jnp.ones((SUBLANES, LANES), _F32)
        for i in range(pitch):
            av = a_ref[sl, seg(i), :]
            hh = av * hh + b_ref[sl, seg(i), :]
            aa = av * aa
        d = 1
        while d < SUBLANES:
            hs_ = jnp.where(row >= d, pltpu.roll(hh, d, 0), 0.0)
            as_ = jnp.where(row >= d, pltpu.roll(aa, d, 0), 1.0)
            hh = aa * hs_ + hh
            aa = aa * as_
            d *= 2
        cs = slice(sl * LANES, (sl + 1) * LANES)
        cin = jnp.where(first, 0.0, car_ref[:, cs])
        full = hh + aa * cin
        hv = jnp.where(row >= 1, pltpu.roll(full, 1, 0), cin)
        car_ref[:, cs] = jnp.broadcast_to(full[SUBLANES - 1:SUBLANES, :], (SUBLANES, LANES))
        for i in range(pitch):
            hv = a_ref[sl, seg(i), :] * hv + b_ref[sl, seg(i), :]
            h_ref[sl, seg(i), :] = hv
        g = z_ref[:, c + sl * LANES:c + (sl + 1) * LANES]
        ya_ref[:, cs] = (h_ref[sl, 0:tm, :] * jax.nn.gelu(g)).astype(_BF16)
        yield

    t_idx = (t0 + lax.broadcasted_iota(jnp.int32, (tm, 1), 0) + 1).astype(_F32)
    n_grp = len(POOL_WINDOWS)
    gd = c // n_grp
    for gi, w in enumerate(POOL_WINDOWS):
        cs = slice(gi * gd, (gi + 1) * gd)
        e = ep_ref[:, cs]
        acc = e
        d = 1
        while d < w:
            acc = acc + pltpu.roll(acc, d, 0)
            d *= 2
        xt = e[hist_p:, :]
        cnt = jnp.minimum(t_idx, float(w))
        dd = acc[hist_p:, :] / cnt - xt
        yb = jnp.dot(dd.astype(_BF16), wp_ref[gi], preferred_element_type=_F32) * ps_ref[:, cs]
        yb_ref[:, cs] = yb.astype(_BF16)
        yield


def _combine_tile(ya_ref, yb_ref, ga0_ref, ga1_ref, gb0_ref, gb1_ref, x_ref,
                  wa_ref, wb_ref, wo_ref, n2_ref, wr_ref, br_ref,
                  xe_ref, cnt_ref, meta_ref, cnt_s, *, live, tm, n_groups, epg):
    d = x_ref.shape[1]
    half = d // 2
    piece = MXU_DIM * 2
    ya = ya_ref[...]
    yb = yb_ref[...]
    us = []
    for q in range(d // piece):
        cs = slice(q * piece, (q + 1) * piece)
        ga_ref, gb_ref = (ga0_ref, gb0_ref) if q * piece < half else (ga1_ref, gb1_ref)
        gs = slice((q * piece) % half, (q * piece) % half + piece)
        pa = jnp.dot(ya, wa_ref[:, cs], preferred_element_type=_F32)
        yield
        pb = jnp.dot(yb, wb_ref[:, cs], preferred_element_type=_F32)
        yield
        ta = jnp.tanh(0.5 * ga_ref[:, gs])
        tb = jnp.tanh(0.5 * gb_ref[:, gs])
        us.append((0.5 * ((pa + pb) + (ta * pa + tb * pb))).astype(_BF16))
    u = jnp.concatenate(us, axis=1)
    for q in range(d // MXU_DIM):
        cs = slice(q * MXU_DIM, (q + 1) * MXU_DIM)
        xe_ref[:, cs] = x_ref[:, cs] + jnp.dot(u, wo_ref[:, cs], preferred_element_type=_F32)
        yield
    x1 = xe_ref[:, 0:d]

    ht = _rms(x1, n2_ref[...]).astype(_BF16)
    logits = jnp.dot(ht, wr_ref[...], preferred_element_type=_F32) + br_ref[...]

    lane = lax.broadcasted_iota(jnp.int32, (tm, LANES), 1).astype(_F32)
    ninf = -jnp.inf
    big = float(LANES)

    def first_argmax(v):
        m = jnp.max(v, axis=-1, keepdims=True)
        return m, jnp.min(jnp.where(v == m, lane, big), axis=-1, keepdims=True)

    is_g = lane < float(n_groups)
    gmax, gidx = first_argmax(jnp.where(is_g, logits, ninf))
    g_w = 1.0 / jnp.sum(jnp.where(is_g, jnp.exp(logits - gmax), 0.0), axis=-1, keepdims=True)
    lo_lane = float(n_groups) + float(epg) * gidx
    in_grp = (lane >= lo_lane) & (lane < lo_lane + float(epg))
    le = jnp.where(in_grp, logits, ninf)
    m1, i1 = first_argmax(le)
    m2, i2 = first_argmax(jnp.where(lane == i1, ninf, le))
    e21 = jnp.exp(m2 - m1)
    w1 = g_w / (1.0 + e21)
    w2 = w1 * e21
    e1 = i1 - lo_lane
    e2 = i2 - lo_lane
    lo = jnp.minimum(e1, e2)
    hi = jnp.maximum(e1, e2)
    w_lo = jnp.where(e1 < e2, w1, w2)
    w_hi = jnp.where(e1 < e2, w2, w1)
    pair = jnp.where(lo == 0.0, hi - 1.0, jnp.where(lo == 1.0, 6.0 - hi, 5.0))
    swap = pair == 5.0
    w_a = jnp.where(swap, w_hi, w_lo)
    w_b = jnp.where(swap, w_lo, w_hi)
    bucket = float(_N_PAIRS) * gidx + pair

    onehot = lane == bucket
    oh_bf = jnp.where(onehot, 1.0, 0.0).astype(_BF16)
    rr = lax.broadcasted_iota(jnp.int32, (tm, tm), 0)
    cc = lax.broadcasted_iota(jnp.int32, (tm, tm), 1)
    tri = jnp.where(cc < rr, 1.0, 0.0).astype(_BF16)
    before = jnp.dot(tri, oh_bf, preferred_element_type=_F32) + cnt_s[...]
    rank = jnp.sum(jnp.where(onehot, before, 0.0), axis=-1, keepdims=True)
    cnt_s[...] = cnt_s[...] + jnp.where(live, jnp.sum(jnp.where(onehot, 1.0, 0.0), axis=0, keepdims=True), 0.0)
    cnt_ref[...] = cnt_s[...]

    info = jnp.where(lane == 0.0, bucket,
                     jnp.where(lane == 1.0, rank,
                               jnp.where(lane == 2.0, w_a, jnp.where(lane == 3.0, w_b, 0.0))))
    xe_ref[:, d:d + LANES] = info
    meta_ref[...] = jnp.transpose(info)[0:SUBLANES, :].astype(jnp.int32)


def _mix_combine_kernel(*refs, tm, pitch, ns, n_tiles, n_groups, epg):
    mix_in, cmb_in = refs[0:10], refs[10:21]
    xe_ref, cnt_ref, meta_ref = refs[21:24]
    (ynew_ref, yold_ref, er_ref, ep_ref, a_ref, b_ref, h_ref, car_ref, cnt_s,
     wa_s, wb_s, wo_s, wstage, wsem) = refs[24:]
    s = pl.program_id(0)
    dense_hbm = cmb_in[5:8]
    cmb_in = cmb_in[0:5] + (wa_s, wb_s, wo_s) + cmb_in[8:]

    @pl.when(s == 0)
    def _():
        for w_hbm, w_s in zip(dense_hbm, (wa_s, wb_s, wo_s)):
            _load_cast_weight(w_hbm, w_s, wstage, wsem)
        _mixer_init(er_ref, ep_ref, a_ref, b_ref, car_ref, tm)
        cnt_s[...] = jnp.zeros(cnt_s.shape, _F32)
        yold_ref[...] = jnp.zeros(yold_ref.shape, _BF16)

    seq_pos = lax.rem(jnp.minimum(s, n_tiles - 1), ns)
    mixer = _mixer_tile(*mix_in, ynew_ref.at[0], ynew_ref.at[1], er_ref, ep_ref, a_ref, b_ref, h_ref,
                        car_ref, first=seq_pos == 0, t0=seq_pos * tm, tm=tm, pitch=pitch)
    combine = _combine_tile(yold_ref.at[0], yold_ref.at[1], *cmb_in, xe_ref, cnt_ref, meta_ref,
                            cnt_s, live=s >= 1, tm=tm, n_groups=n_groups, epg=epg)
    stages = [mixer, combine]
    while stages:
        for stage in list(stages):
            if next(stage, stages) is stages:
                stages.remove(stage)
    yold_ref[...] = ynew_ref[...]


def _mix_combine(z, x2d, seq, mixer_w, wa, wb, wo, n2, wr, br, tm, n_groups, epg):
    t, d = x2d.shape
    c = wa.shape[0]
    half = d // 2
    n_tiles = t // tm
    ns = seq // tm
    pitch = _scan_pitch(tm)
    n_slab = c // LANES
    off = (3 * c) // half
    kern = functools.partial(_mix_combine_kernel, tm=tm, pitch=pitch, ns=ns, n_tiles=n_tiles,
                             n_groups=n_groups, epg=epg)

    def cur(s):
        return jnp.minimum(s, n_tiles - 1)

    def prev(s):
        return jnp.maximum(s - 1, 0)

    def zspec(j):
        return pl.BlockSpec((tm, half), lambda s, j=j: (prev(s), off + j))

    return pl.pallas_call(
        kern,
        grid=(n_tiles + 1,),
        in_specs=[pl.BlockSpec((tm, 3 * c), lambda s: (cur(s), 0))]
                 + [_const_spec(w.shape) for w in mixer_w]
                 + [zspec(0), zspec(1), zspec(2), zspec(3),
                    pl.BlockSpec((tm, d), lambda s: (prev(s), 0)),
                    pl.BlockSpec(memory_space=pl.ANY), pl.BlockSpec(memory_space=pl.ANY),
                    pl.BlockSpec(memory_space=pl.ANY),
                    _const_spec(n2.shape), _const_spec(wr.shape), _const_spec(br.shape)],
        out_specs=[pl.BlockSpec((tm, d + LANES), lambda s: (prev(s), 0)),
                   pl.BlockSpec((1, LANES), lambda s: (0, 0)),
                   pl.BlockSpec((SUBLANES, tm), lambda s: (0, prev(s)))],
        out_shape=[jax.ShapeDtypeStruct((t, d + LANES), _F32),
                   jax.ShapeDtypeStruct((1, LANES), _F32),
                   jax.ShapeDtypeStruct((SUBLANES, t), jnp.int32)],
        scratch_shapes=[pltpu.VMEM((2, tm, c), _BF16),
                        pltpu.VMEM((2, tm, c), _BF16),
                        pltpu.VMEM((SUBLANES + tm, c), _F32),
                        pltpu.VMEM((2 * SUBLANES + tm, c), _F32),
                        pltpu.VMEM((n_slab, SUBLANES * pitch, LANES), _F32),
                        pltpu.VMEM((n_slab, SUBLANES * pitch, LANES), _F32),
                        pltpu.VMEM((n_slab, SUBLANES * pitch, LANES), _F32),
                        pltpu.VMEM((SUBLANES, c), _F32),
                        pltpu.VMEM((1, LANES), _F32),
                        pltpu.VMEM(wa.shape, _BF16), pltpu.VMEM(wb.shape, _BF16), pltpu.VMEM(wo.shape, _BF16),
                        pltpu.VMEM((2, WEIGHT_STAGE_ROWS, d), _F32),
                        pltpu.SemaphoreType.DMA((2,))],
        compiler_params=_params(1),
        name="mix_combine",
    )(z, *mixer_w, z, z, z, z, x2d, wa, wb, wo, n2, wr, br)


def _bucket_starts(cnt_ref, start_ref, n_buckets, tm, on_tile=None):
    def bucket_body(b, tile_idx):
        start_ref[b] = tile_idx * tm
        nt = lax.div(cnt_ref[b] + (tm - 1), tm)
        if on_tile is not None:
            lax.fori_loop(0, nt, lambda k, carry: (on_tile(tile_idx + k, b), carry)[1], 0)
        return tile_idx + nt

    return lax.fori_loop(0, n_buckets, bucket_body, 0)


def _row_gather_start(row_of, src_hbm, buf, slot, sem, n_rows, unrolled):
    def start(r, priority):
        row = row_of(r)
        pltpu.make_async_copy(src_hbm.at[pl.ds(row, 1)], buf.at[slot, pl.ds(r, 1)],
                              sem.at[slot]).start(priority=priority)

    if unrolled:
        for r in range(n_rows):
            start(r, r % 2)
    else:
        def body(r, _):
            start(2 * r, 0)
            start(2 * r + 1, 1)
            return 0
        lax.fori_loop(0, n_rows // 2, body, 0, unroll=4)


def _row_gather_wait(src_hbm, buf, slot, sem, n_rows):
    pltpu.make_async_copy(src_hbm.at[pl.ds(0, n_rows)], buf.at[slot], sem.at[slot]).wait()


MOE_WEIGHT_SLOTS = 6
MOE_PARTS = 4
MOE_CHUNKS = 3 * MOE_PARTS
MOE_PUMP = 3
MOE_DEPTH = 4
MOE_GATHER_AHEAD = 1
MOE_PLACE_SEGMENT = 512


def _moe_weight_plan(n_groups, epg):
    first_use = {e: min(p for p in range(_N_PAIRS) if e in (_PAIR_SLOT_A[p], _PAIR_SLOT_B[p])) for e in range(epg)}
    last_use = {e: max(p for p in range(_N_PAIRS) if e in (_PAIR_SLOT_A[p], _PAIR_SLOT_B[p])) for e in range(epg)}
    free_after = [-1] * MOE_WEIGHT_SLOTS
    loads, slot_of = [], {}
    for g in range(n_groups):
        for e in sorted(range(epg), key=lambda e: (first_use[e], e)):
            needed_by = g * _N_PAIRS + first_use[e]
            s = min(range(MOE_WEIGHT_SLOTS), key=lambda s: (free_after[s], s))
            assert free_after[s] < needed_by
            loads.append((g * epg + e, s, free_after[s], needed_by))
            free_after[s] = g * _N_PAIRS + last_use[e]
            slot_of[(g, e)] = s
    n_buckets = n_groups * _N_PAIRS
    need = [sum(1 for l in loads if l[3] <= b) for b in range(n_buckets)]
    allow = [sum(1 for l in loads if l[2] < b) for b in range(n_buckets)]
    slot_a = [slot_of[(b // _N_PAIRS, _PAIR_SLOT_A[b % _N_PAIRS])] for b in range(n_buckets)]
    slot_b = [slot_of[(b // _N_PAIRS, _PAIR_SLOT_B[b % _N_PAIRS])] for b in range(n_buckets)]
    return [l[0] for l in loads], [l[1] for l in loads], need, allow, slot_a, slot_b


def _moe_kernel(tokb_ref, tokr_ref, cntb_ref, need_ref, allow_ref, sa_ref, sb_ref, le_ref, ls_ref,
                xe_hbm, n2_ref, wg_hbm, wu_hbm, wd_hbm,
                x2s_ref, xbuf, gsem, wg_s, wu_s, wd_s, st_gu, st_d, wsem, cnt, src_s, tbk_s, start_s, tot_s,
                *, tm):
    j = pl.program_id(0)
    slot = lax.rem(j, MOE_GATHER_AHEAD + 1)
    d = x2s_ref.shape[1]
    t_rows = xe_hbm.shape[0]
    n_buckets = start_s.shape[0]

    def src_row(tile):
        return lambda r: src_s[tile * tm + r]

    rows_gu = wg_hbm.shape[1] // MOE_PARTS
    rows_d = wd_hbm.shape[1] // MOE_PARTS

    def chunk_dma(c, kind):
        load = c // MOE_CHUNKS
        part = c % MOE_PARTS
        e = le_ref[load]
        if kind == 2:
            return pltpu.make_async_copy(wd_hbm.at[e, pl.ds(part * rows_d, rows_d)], st_d.at[c % MOE_DEPTH],
                                         wsem.at[c % MOE_DEPTH])
        src = wg_hbm if kind == 0 else wu_hbm
        return pltpu.make_async_copy(src.at[e, pl.ds(part * rows_gu, rows_gu)], st_gu.at[c % MOE_DEPTH],
                                     wsem.at[c % MOE_DEPTH])

    def for_kind(c, fn):
        kind = (c % MOE_CHUNKS) // MOE_PARTS
        for k in range(3):
            @pl.when(kind == k)
            def _(k=k):
                fn(k)

    def issue_one(allowed):
        @pl.when(cnt[0] < jnp.minimum(allowed, cnt[1] + MOE_DEPTH))
        def _():
            c = cnt[0]
            for_kind(c, lambda k: chunk_dma(c, k).start())
            cnt[0] = c + 1

    def retire():
        c = cnt[1]
        s = ls_ref[c // MOE_CHUNKS]
        part = c % MOE_PARTS

        def finish(k):
            chunk_dma(c, k).wait()
            if k == 2:
                wd_s[s, pl.ds(part * rows_d, rows_d), :] = st_d[c % MOE_DEPTH].astype(_BF16)
            else:
                dst = wg_s if k == 0 else wu_s
                dst[s, pl.ds(part * rows_gu, rows_gu), :] = st_gu[c % MOE_DEPTH].astype(_BF16)

        for_kind(c, finish)
        cnt[1] = c + 1

    def pump(required, allowed, extra):
        n_iter = jnp.maximum(required - cnt[1], jnp.minimum(extra, allowed - cnt[1]))
        n_fill = jnp.minimum(allowed, cnt[1] + MOE_DEPTH) - cnt[0]
        lax.fori_loop(0, jnp.maximum(n_fill, 0), lambda _, carry: (issue_one(allowed), carry)[1], 0)

        def body(_, carry):
            retire()
            issue_one(allowed)
            return carry

        lax.fori_loop(0, jnp.maximum(n_iter, 0), body, 0)

    @pl.when(j == 0)
    def _():
        cnt[0] = 0
        cnt[1] = 0

        def set_tile(tile, b):
            tbk_s[tile] = b

        total = _bucket_starts(cntb_ref, start_s, n_buckets, tm, on_tile=set_tile)
        tot_s[0] = total

        def fill(lo, hi):
            def body(q, carry):
                src_s[q] = jnp.minimum(jnp.where(q >= t_rows, q - t_rows, q), t_rows - 1)
                return carry
            lax.fori_loop(lo, hi, body, 0)

        def fill_bucket(b, carry):
            nxt = jnp.where(b + 1 < n_buckets, start_s[jnp.minimum(b + 1, n_buckets - 1)], total * tm)
            fill(start_s[b] + cntb_ref[b], nxt)
            return carry
        lax.fori_loop(0, n_buckets, fill_bucket, 0)
        fill(total * tm, (total + MOE_GATHER_AHEAD) * tm)

        def place(tok, carry):
            src_s[start_s[tokb_ref[tok]] + tokr_ref[tok]] = tok
            return carry

        first_allowed = allow_ref[tbk_s[0]] * MOE_CHUNKS
        seg = min(MOE_PLACE_SEGMENT, t_rows)

        def place_segment(sg, carry):
            lax.fori_loop(0, seg, lambda i, c: place(sg * seg + i, c), 0, unroll=16)
            pump(0, first_allowed, 2)
            return carry
        lax.fori_loop(0, t_rows // seg, place_segment, 0)

        for tile in range(MOE_GATHER_AHEAD):
            _row_gather_start(src_row(tile), xe_hbm, xbuf, tile, gsem, tm, unrolled=False)

    n_valid = tot_s[0]

    @pl.when(j < n_valid)
    def _():
        b = tbk_s[j]
        pump(need_ref[b] * MOE_CHUNKS, allow_ref[b] * MOE_CHUNKS, MOE_PUMP)
        s_a = sa_ref[b]
        s_b = sb_ref[b]

        def tile_body(cur):
            _row_gather_wait(xe_hbm, xbuf, cur, gsem, tm)
            _row_gather_start(src_row(j + MOE_GATHER_AHEAD), xe_hbm, xbuf,
                              (cur + MOE_GATHER_AHEAD) % (MOE_GATHER_AHEAD + 1), gsem, tm, unrolled=True)
            xe = xbuf[cur]
            x1 = xe[:, 0:d]
            w_a = xe[:, d + 2:d + 3]
            w_b = xe[:, d + 3:d + 4]
            ht = _rms(x1, n2_ref[...]).astype(_BF16)

            def expert(s, wgt):
                hg = jnp.dot(ht, wg_s[s], preferred_element_type=_F32)
                hu = jnp.dot(ht, wu_s[s], preferred_element_type=_F32)
                half = 0.5 * hg
                return ((half + half * jnp.tanh(half)) * hu * wgt).astype(_BF16)

            y = jnp.dot(expert(s_a, w_a), wd_s[s_a], preferred_element_type=_F32)
            y = y + jnp.dot(expert(s_b, w_b), wd_s[s_b], preferred_element_type=_F32)
            x2s_ref[...] = x1 + y

        for cur in range(MOE_GATHER_AHEAD + 1):
            pl.when(slot == cur)(functools.partial(tile_body, cur))

    @pl.when(j >= n_valid)
    def _():
        x2s_ref[...] = jnp.zeros(x2s_ref.shape, _F32)

        @pl.when(j == n_valid)
        def _():
            for ahead in range(MOE_GATHER_AHEAD):
                _row_gather_wait(xe_hbm, xbuf, lax.rem(j + ahead, MOE_GATHER_AHEAD + 1), gsem, tm)
            lax.fori_loop(0, cnt[0] - cnt[1], lambda _, carry: (retire(), carry)[1], 0)


def _moe(tok_bucket, tok_rank, bucket_cnt, xe, n2, wg, wu, wd, tm, n_groups, epg):
    n_buckets = bucket_cnt.shape[0]
    n_tiles = xe.shape[0] // tm + n_buckets - 1 + MOE_GATHER_AHEAD
    d = wg.shape[1]
    f = wg.shape[2]
    kern = functools.partial(_moe_kernel, tm=tm)
    le, ls, need, allow, slot_a, slot_b = (jnp.asarray(v, jnp.int32) for v in _moe_weight_plan(n_groups, epg))

    grid_spec = pltpu.PrefetchScalarGridSpec(
        num_scalar_prefetch=9,
        grid=(n_tiles,),
        in_specs=[pl.BlockSpec(memory_space=pl.ANY),
                  pl.BlockSpec((1, d), lambda j, *_: (0, 0)),
                  pl.BlockSpec(memory_space=pl.ANY),
                  pl.BlockSpec(memory_space=pl.ANY),
                  pl.BlockSpec(memory_space=pl.ANY)],
        out_specs=pl.BlockSpec((tm, d), lambda j, *_: (j, 0)),
        scratch_shapes=[pltpu.VMEM((MOE_GATHER_AHEAD + 1, tm, d + LANES), _F32),
                        pltpu.SemaphoreType.DMA((MOE_GATHER_AHEAD + 1,)),
                        pltpu.VMEM((MOE_WEIGHT_SLOTS, d, f), _BF16),
                        pltpu.VMEM((MOE_WEIGHT_SLOTS, d, f), _BF16),
                        pltpu.VMEM((MOE_WEIGHT_SLOTS, f, d), _BF16),
                        pltpu.VMEM((MOE_DEPTH, d // MOE_PARTS, f), _F32),
                        pltpu.VMEM((MOE_DEPTH, f // MOE_PARTS, d), _F32),
                        pltpu.SemaphoreType.DMA((MOE_DEPTH,)),
                        pltpu.SMEM((2,), jnp.int32),
                        pltpu.SMEM((n_tiles * tm,), jnp.int32),
                        pltpu.SMEM((n_tiles,), jnp.int32),
                        pltpu.SMEM((n_buckets,), jnp.int32),
                        pltpu.SMEM((1,), jnp.int32)],
    )
    return pl.pallas_call(
        kern,
        grid_spec=grid_spec,
        out_shape=jax.ShapeDtypeStruct((n_tiles * tm, d), _F32),
        compiler_params=_params(1),
        name="moe",
    )(tok_bucket, tok_rank, bucket_cnt, need, allow, slot_a, slot_b, le, ls, xe, n2, wg, wu, wd)


def _ple_kernel(tokb_ref, tokr_ref, cntb_ref, x2s_hbm, p_ref, ng_ref, wg_hbm, wp_ref, nf_ref, o_ref,
                xbuf, gsem, start_s, wg_ref, wstage, wsem, *, tm, n_steps, tm_sorted):
    i = pl.program_id(0)
    slot = i % 2

    def sorted_row(tile):
        return lambda r: start_s[tokb_ref[tile * tm + r]] + tokr_ref[tile * tm + r]

    @pl.when(i == 0)
    def _():
        _bucket_starts(cntb_ref, start_s, start_s.shape[0], tm_sorted)
        _row_gather_start(sorted_row(0), x2s_hbm, xbuf, 0, gsem, tm, unrolled=False)
        _load_cast_weight(wg_hbm, wg_ref, wstage, wsem)

    nxt = jnp.where(i + 1 == n_steps, 0, i + 1)

    def step(cur):
        _row_gather_wait(x2s_hbm, xbuf, cur, gsem, tm)
        _row_gather_start(sorted_row(nxt), x2s_hbm, xbuf, 1 - cur, gsem, tm, unrolled=True)
        x2 = xbuf[cur]
        g = _sigmoid(jnp.dot(_rms(x2, ng_ref[...]).astype(_BF16), wg_ref[...], preferred_element_type=_F32))
        e = jnp.dot(p_ref[...].astype(_BF16), wp_ref[...], preferred_element_type=_F32)
        o_ref[...] = _rms(x2 + g * e, nf_ref[...])

        @pl.when(i == n_steps - 1)
        def _():
            _row_gather_wait(x2s_hbm, xbuf, 1 - cur, gsem, tm)

    for parity in range(2):
        pl.when(slot == parity)(functools.partial(step, parity))


def _ple(tok_bucket, tok_rank, bucket_cnt, x2s, p2d, ng, wg, wp, nf, tm, t, tm_sorted):
    d = x2s.shape[1]
    pd = p2d.shape[1]
    n_steps = t // tm
    kern = functools.partial(_ple_kernel, tm=tm, n_steps=n_steps, tm_sorted=tm_sorted)

    def cspec(shape):
        nd = len(shape)
        return pl.BlockSpec(shape, lambda i, *_: (0,) * nd, pipeline_mode=pl.Buffered(1))

    grid_spec = pltpu.PrefetchScalarGridSpec(
        num_scalar_prefetch=3,
        grid=(n_steps,),
        in_specs=[pl.BlockSpec(memory_space=pl.ANY),
                  pl.BlockSpec((tm, pd), lambda i, *_: (i, 0)),
                  cspec(ng.shape), pl.BlockSpec(memory_space=pl.ANY), cspec(wp.shape), cspec(nf.shape)],
        out_specs=pl.BlockSpec((tm, d), lambda i, *_: (i, 0)),
        scratch_shapes=[pltpu.VMEM((2, tm, d), _F32),
                        pltpu.SemaphoreType.DMA((2,)),
                        pltpu.SMEM((bucket_cnt.shape[0],), jnp.int32),
                        pltpu.VMEM(wg.shape, _BF16),
                        pltpu.VMEM((2, WEIGHT_STAGE_ROWS, wg.shape[1]), _F32),
                        pltpu.SemaphoreType.DMA((2,))],
    )
    return pl.pallas_call(
        kern,
        grid_spec=grid_spec,
        out_shape=jax.ShapeDtypeStruct((t, d), _F32),
        compiler_params=_params(1),
        name="ple",
    )(tok_bucket, tok_rank, bucket_cnt, x2s, p2d, ng, wg, wp, nf)


def _block_diag(w, per_block):
    h, hd, _ = w.shape
    nb = h // per_block
    w4 = w.reshape(nb, per_block, hd, hd)
    rows = [jnp.pad(w4[:, p], ((0, 0), (0, 0), (p * hd, (per_block - 1 - p) * hd))) for p in range(per_block)]
    return jnp.concatenate(rows, axis=1)


def _layer(x2d, p2d, bsz, seq, norm1_g, w_in, conv_w, conv_b, w_rg_a, b_rg_a, w_rg_x, b_rg_x, lru_lambda,
           w_pool, pool_scale, w_branch_a, w_branch_b, w_out, norm2_g, w_router_group, b_router_group,
           w_router_expert, b_router_expert, w_e_gate, w_e_up, w_e_down, norm_ple_g, w_ple_gate,
           w_ple_proj, out_norm_g):
    t, d = x2d.shape
    c = conv_b.shape[0]
    heads, hd, _ = w_rg_a.shape
    n_groups = w_router_group.shape[1]
    n_exp = w_router_expert.shape[1]
    epg = n_exp // n_groups
    assert epg == 4 and TOP_K == 2 and hd * (MXU_DIM // hd) == MXU_DIM
    assert w_pool.shape[0] == len(POOL_WINDOWS) and w_pool.shape[1] == MXU_DIM

    tm_in, tn_in, rc_in = min(2048, t), 512, min(256, t)
    tm_mix = min(256, seq)
    tm_cmb = min(256, t)
    tm_moe = min(128, t)
    tm_ple = min(512, t)

    row = lambda v: v.reshape(1, -1).astype(_F32)
    per_block = MXU_DIM // hd

    z = _inproj(x2d, row(norm1_g), w_in, tm_in, tn_in, rc_in)
    mixer_w = (conv_w.reshape(CONV_WIDTH, c), row(conv_b),
               _block_diag(w_rg_a, per_block).astype(_BF16), row(b_rg_a),
               _block_diag(w_rg_x, per_block).astype(_BF16), row(b_rg_x),
               row(lru_lambda), w_pool.astype(_BF16), row(pool_scale))

    n_rt = n_groups + n_exp
    wr = jnp.pad(jnp.concatenate([w_router_group, w_router_expert], axis=1),
                 ((0, 0), (0, LANES - n_rt))).astype(_BF16)
    br = jnp.pad(jnp.concatenate([b_router_group, b_router_expert]), (0, LANES - n_rt)).reshape(1, LANES)
    xe, counts, meta = _mix_combine(z, x2d, seq, mixer_w, w_branch_a, w_branch_b, w_out, row(norm2_g), wr, br,
                                    tm_cmb, n_groups, epg)

    n_buckets = n_groups * _N_PAIRS
    tok_bucket, tok_rank = meta[0], meta[1]
    bucket_cnt = counts[0, :n_buckets].astype(jnp.int32)

    x2s = _moe(tok_bucket, tok_rank, bucket_cnt, xe, row(norm2_g), w_e_gate, w_e_up, w_e_down,
               tm_moe, n_groups, epg)
    return _ple(tok_bucket, tok_rank, bucket_cnt, x2s, p2d, row(norm_ple_g), w_ple_gate,
                w_ple_proj.astype(_BF16), row(out_norm_g), tm_ple, t, tm_moe)


def kernel(x, p, norm1_g, w_in, conv_w, conv_b, w_rg_a, b_rg_a, w_rg_x, b_rg_x, lru_lambda, w_pool, pool_scale, w_branch_a, w_branch_b, w_out, norm2_g, w_router_group, b_router_group, w_router_expert, b_router_expert, w_e_gate, w_e_up, w_e_down, norm_ple_g, w_ple_gate, w_ple_proj, final_norm_g):
    bsz, seq, d = x.shape
    depth = p.shape[0]
    assert depth == 1, "the final RMSNorm is fused into the last layer's embedding kernel"
    out = _layer(x.reshape(bsz * seq, d), p[0].reshape(bsz * seq, -1), bsz, seq,
                 norm1_g[0], w_in[0], conv_w[0], conv_b[0], w_rg_a[0], b_rg_a[0], w_rg_x[0], b_rg_x[0],
                 lru_lambda[0], w_pool[0], pool_scale[0], w_branch_a[0], w_branch_b[0], w_out[0],
                 norm2_g[0], w_router_group[0], b_router_group[0], w_router_expert[0],
                 b_router_expert[0], w_e_gate[0], w_e_up[0], w_e_down[0], norm_ple_g[0],
                 w_ple_gate[0], w_ple_proj[0], final_norm_g)
    return out.reshape(bsz, seq, d)
```

```python
import functools

import jax
import jax.numpy as jnp
from jax import lax
from jax.experimental import pallas as pl
from jax.experimental.pallas import tpu as pltpu

EPS = 1e-6
LRU_C = 8.0
CONV_WIDTH = 4
POOL_WINDOWS = (2, 4, 8, 16)
TOP_K = 2
SQRT_GUARD = 1e-30
WEIGHT_STAGE_ROWS = 256

LANES = 128
SUBLANES = 8
MXU_DIM = 256
VMEM_LIMIT_BYTES = 56 * 1024 * 1024

_BF16 = jnp.bfloat16
_F32 = jnp.float32

_PAIR_SLOT_A = (0, 0, 0, 1, 1, 3)
_PAIR_SLOT_B = (1, 2, 3, 3, 2, 2)
_N_PAIRS = len(_PAIR_SLOT_A)


def _sigmoid(x):
    return 0.5 * jnp.tanh(0.5 * x) + 0.5


def _rms(x, g):
    ms = jnp.mean(x * x, axis=-1, keepdims=True)
    return x * lax.rsqrt(ms + EPS) * g


def _load_cast_weight(w_hbm, w_s, stage, sem):
    rows = stage.shape[1]
    n_chunks = w_hbm.shape[0] // rows

    def chunk_copy(c):
        return pltpu.make_async_copy(w_hbm.at[pl.ds(c * rows, rows)], stage.at[c % 2], sem.at[c % 2])

    chunk_copy(0).start()
    for c in range(n_chunks):
        if c + 1 < n_chunks:
            chunk_copy(c + 1).start()
        chunk_copy(c).wait()
        w_s[c * rows:(c + 1) * rows, :] = stage[c % 2].astype(_BF16)


def _const_spec(shape):
    nd = len(shape)
    return pl.BlockSpec(shape, lambda *_: (0,) * nd, pipeline_mode=pl.Buffered(1))


def _params(n_axes, flags=None):
    return pltpu.CompilerParams(dimension_semantics=("arbitrary",) * n_axes,
                                vmem_limit_bytes=VMEM_LIMIT_BYTES, flags=flags)


def _inproj_kernel(x_hbm, g_ref, w_ref, z_ref, h_ref, xs_ref, sem, *, tm, rc):
    i = pl.program_id(0)
    j = pl.program_id(1)
    n_chunks = tm // rc
    more_tiles = i + 1 < pl.num_programs(0)

    def chunk_copy(tile, c):
        return pltpu.make_async_copy(x_hbm.at[pl.ds(pl.multiple_of(tile * tm + c * rc, rc), rc)],
                                     xs_ref.at[c % 2], sem.at[c % 2])

    def normalise(tile, c):
        h_ref[tile % 2, pl.ds(pl.multiple_of(c * rc, rc), rc), :] = _rms(xs_ref[c % 2], g_ref[...]).astype(_BF16)

    @pl.when(jnp.logical_and(i == 0, j == 0))
    def _():
        chunk_copy(0, 0).start()
        for c in range(n_chunks):
            if c + 1 < n_chunks:
                chunk_copy(0, c + 1).start()
            chunk_copy(0, c).wait()
            normalise(0, c)

    @pl.when(jnp.logical_and(more_tiles, jnp.logical_and(j >= 1, j <= n_chunks)))
    def _():
        chunk_copy(i + 1, j - 1).wait()
        normalise(i + 1, j - 1)

    @pl.when(jnp.logical_and(more_tiles, j < n_chunks))
    def _():
        chunk_copy(i + 1, j).start()

    z_ref[...] = jnp.dot(h_ref[i % 2], w_ref[...].astype(_BF16), preferred_element_type=_F32)


def _inproj(x2d, g, w, tm, tn, rc):
    t, d = x2d.shape
    n = w.shape[1]
    kern = functools.partial(_inproj_kernel, tm=tm, rc=rc)
    assert n // tn > tm // rc, "a row tile's chunks are prepared during the column steps of the previous tile"
    return pl.pallas_call(
        kern,
        grid=(t // tm, n // tn),
        in_specs=[pl.BlockSpec(memory_space=pl.ANY),
                  pl.BlockSpec((1, d), lambda i, j: (0, 0)),
                  pl.BlockSpec((d, tn), lambda i, j: (0, j))],
        out_specs=pl.BlockSpec((tm, tn), lambda i, j: (i, j)),
        out_shape=jax.ShapeDtypeStruct((t, n), _F32),
        scratch_shapes=[pltpu.VMEM((2, tm, d), _BF16),
                        pltpu.VMEM((2, rc, d), _F32),
                        pltpu.SemaphoreType.DMA((2,))],
        compiler_params=_params(2),
        name="inproj",
    )(x2d, g, w)


def _scan_pitch(tm):
    p = -(-tm // SUBLANES)
    while p % SUBLANES != 4:
        p += 1
    return p


def _mixer_init(er_ref, ep_ref, a_ref, b_ref, car_ref, tm):
    er_ref[...] = jnp.zeros(er_ref.shape, _F32)
    ep_ref[...] = jnp.zeros(ep_ref.shape, _F32)
    car_ref[...] = jnp.zeros(car_ref.shape, _F32)
    a_ref[:, tm:, :] = jnp.ones((a_ref.shape[0], a_ref.shape[1] - tm, LANES), _F32)
    b_ref[:, tm:, :] = jnp.zeros((b_ref.shape[0], b_ref.shape[1] - tm, LANES), _F32)


def _mixer_tile(z_ref, cw_ref, cb_ref, wa_ref, ba_ref, wx_ref, bx_ref, lam_ref, wp_ref, ps_ref,
                ya_ref, yb_ref, er_ref, ep_ref, a_ref, b_ref, h_ref, car_ref, *, first, t0, tm, pitch):
    c = cb_ref.shape[1]
    n_slab = c // LANES
    hist_r = SUBLANES
    hist_p = 2 * SUBLANES

    er_ref[0:hist_r, :] = jnp.where(first, 0.0, er_ref[tm:tm + hist_r, :])
    ep_ref[0:hist_p, :] = jnp.where(first, 0.0, ep_ref[tm:tm + hist_p, :])
    er_ref[hist_r:hist_r + tm, :] = z_ref[:, 0:c]
    ep_ref[hist_p:hist_p + tm, :] = z_ref[:, 2 * c:3 * c]

    kvec = -LRU_C * jax.nn.softplus(-lam_ref[...])
    nblk = c // MXU_DIM
    for k in range(nblk):
        cs = slice(k * MXU_DIM, (k + 1) * MXU_DIM)
        xc = cb_ref[:, cs] + cw_ref[CONV_WIDTH - 1:CONV_WIDTH, cs] * er_ref[hist_r:hist_r + tm, cs]
        for j in range(1, CONV_WIDTH):
            xc = xc + cw_ref[CONV_WIDTH - 1 - j:CONV_WIDTH - j, cs] * er_ref[hist_r - j:hist_r - j + tm, cs]
        xcb = xc.astype(_BF16)
        r = _sigmoid(jnp.dot(xcb, wa_ref[k], preferred_element_type=_F32) + ba_ref[:, cs])
        ig = _sigmoid(jnp.dot(xcb, wx_ref[k], preferred_element_type=_F32) + bx_ref[:, cs])
        log_a = r * kvec[:, cs]
        a = jnp.exp(log_a)
        v = 1.0 - a * a
        mult = v * lax.rsqrt(jnp.maximum(v, SQRT_GUARD))
        bb = mult * ig * xc
        for q in range(MXU_DIM // LANES):
            slab = k * (MXU_DIM // LANES) + q
            a_ref[slab, 0:tm, :] = a[:, q * LANES:(q + 1) * LANES]
            b_ref[slab, 0:tm, :] = bb[:, q * LANES:(q + 1) * LANES]
        yield

    def seg(i):
        return pl.ds(i, SUBLANES, stride=pitch)

    row = lax.broadcasted_iota(jnp.int32, (SUBLANES, LANES), 0)
    for sl in range(n_slab):
        hh = jnp.zeros((SUBLANES, LANES), _F32)
        aa = jnp.ones((SUBLANES, LANES), _F32)
        for i in range(pitch):
            av = a_ref[sl, seg(i), :]
            hh = av * hh + b_ref[sl, seg(i), :]
            aa = av * aa
        d = 1
        while d < SUBLANES:
            hs_ = jnp.where(row >= d, pltpu.roll(hh, d, 0), 0.0)
            as_ = jnp.where(row >= d, pltpu.roll(aa, d, 0), 1.0)
            hh = aa * hs_ + hh
            aa = aa * as_
            d *= 2
        cs = slice(sl * LANES, (sl + 1) * LANES)
        cin = jnp.where(first, 0.0, car_ref[:, cs])
        full = hh + aa * cin
        hv = jnp.where(row >= 1, pltpu.roll(full, 1, 0), cin)
        car_ref[:, cs] = jnp.broadcast_to(full[SUBLANES - 1:SUBLANES, :], (SUBLANES, LANES))
        for i in range(pitch):
            hv = a_ref[sl, seg(i), :] * hv + b_ref[sl, seg(i), :]
            h_ref[sl, seg(i), :] = hv
        g = z_ref[:, c + sl * LANES:c + (sl + 1) * LANES]
        ya_ref[:, cs] = (h_ref[sl, 0:tm, :] * jax.nn.gelu(g)).astype(_BF16)
        yield

    t_idx = (t0 + lax.broadcasted_iota(jnp.int32, (tm, 1), 0) + 1).astype(_F32)
    n_grp = len(POOL_WINDOWS)
    gd = c // n_grp
    for gi, w in enumerate(POOL_WINDOWS):
        cs = slice(gi * gd, (gi + 1) * gd)
        e = ep_ref[:, cs]
        acc = e
        d = 1
        while d < w:
            acc = acc + pltpu.roll(acc, d, 0)
            d *= 2
        xt = e[hist_p:, :]
        cnt = jnp.minimum(t_idx, float(w))
        dd = acc[hist_p:, :] / cnt - xt
        yb = jnp.dot(dd.astype(_BF16), wp_ref[gi], preferred_element_type=_F32) * ps_ref[:, cs]
        yb_ref[:, cs] = yb.astype(_BF16)
        yield


def _combine_tile(ya_ref, yb_ref, ga0_ref, ga1_ref, gb0_ref, gb1_ref, x_ref,
                  wa_ref, wb_ref, wo_ref, n2_ref, wr_ref, br_ref,
                  xe_ref, cnt_ref, meta_ref, cnt_s, *, live, tm, n_groups, epg):
    d = x_ref.shape[1]
    half = d // 2
    piece = MXU_DIM * 2
    ya = ya_ref[...]
    yb = yb_ref[...]
    us = []
    for q in range(d // piece):
        cs = slice(q * piece, (q + 1) * piece)
        ga_ref, gb_ref = (ga0_ref, gb0_ref) if q * piece < half else (ga1_ref, gb1_ref)
        gs = slice((q * piece) % half, (q * piece) % half + piece)
        pa = jnp.dot(ya, wa_ref[:, cs], preferred_element_type=_F32)
        yield
        pb = jnp.dot(yb, wb_ref[:, cs], preferred_element_type=_F32)
        yield
        ta = jnp.tanh(0.5 * ga_ref[:, gs])
        tb = jnp.tanh(0.5 * gb_ref[:, gs])
        us.append((0.5 * ((pa + pb) + (ta * pa + tb * pb))).astype(_BF16))
    u = jnp.concatenate(us, axis=1)
    for q in range(d // MXU_DIM):
        cs = slice(q * MXU_DIM, (q + 1) * MXU_DIM)
        xe_ref[:, cs] = x_ref[:, cs] + jnp.dot(u, wo_ref[:, cs], preferred_element_type=_F32)
        yield
    x1 = xe_ref[:, 0:d]

    ht = _rms(x1, n2_ref[...]).astype(_BF16)
    logits = jnp.dot(ht, wr_ref[...], preferred_element_type=_F32) + br_ref[...]

    lane = lax.broadcasted_iota(jnp.int32, (tm, LANES), 1).astype(_F32)
    ninf = -jnp.inf
    big = float(LANES)

    def first_argmax(v):
        m = jnp.max(v, axis=-1, keepdims=True)
        return m, jnp.min(jnp.where(v == m, lane, big), axis=-1, keepdims=True)

    is_g = lane < float(n_groups)
    gmax, gidx = first_argmax(jnp.where(is_g, logits, ninf))
    g_w = 1.0 / jnp.sum(jnp.where(is_g, jnp.exp(logits - gmax), 0.0), axis=-1, keepdims=True)
    lo_lane = float(n_groups) + float(epg) * gidx
    in_grp = (lane >= lo_lane) & (lane < lo_lane + float(epg))
    le = jnp.where(in_grp, logits, ninf)
    m1, i1 = first_argmax(le)
    m2, i2 = first_argmax(jnp.where(lane == i1, ninf, le))
    e21 = jnp.exp(m2 - m1)
    w1 = g_w / (1.0 + e21)
    w2 = w1 * e21
    e1 = i1 - lo_lane
    e2 = i2 - lo_lane
    lo = jnp.minimum(e1, e2)
    hi = jnp.maximum(e1, e2)
    w_lo = jnp.where(e1 < e2, w1, w2)
    w_hi = jnp.where(e1 < e2, w2, w1)
    pair = jnp.where(lo == 0.0, hi - 1.0, jnp.where(lo == 1.0, 6.0 - hi, 5.0))
    swap = pair == 5.0
    w_a = jnp.where(swap, w_hi, w_lo)
    w_b = jnp.where(swap, w_lo, w_hi)
    bucket = float(_N_PAIRS) * gidx + pair

    onehot = lane == bucket
    oh_bf = jnp.where(onehot, 1.0, 0.0).astype(_BF16)
    rr = lax.broadcasted_iota(jnp.int32, (tm, tm), 0)
    cc = lax.broadcasted_iota(jnp.int32, (tm, tm), 1)
    tri = jnp.where(cc < rr, 1.0, 0.0).astype(_BF16)
    before = jnp.dot(tri, oh_bf, preferred_element_type=_F32) + cnt_s[...]
    rank = jnp.sum(jnp.where(onehot, before, 0.0), axis=-1, keepdims=True)
    cnt_s[...] = cnt_s[...] + jnp.where(live, jnp.sum(jnp.where(onehot, 1.0, 0.0), axis=0, keepdims=True), 0.0)
    cnt_ref[...] = cnt_s[...]

    info = jnp.where(lane == 0.0, bucket,
                     jnp.where(lane == 1.0, rank,
                               jnp.where(lane == 2.0, w_a, jnp.where(lane == 3.0, w_b, 0.0))))
    xe_ref[:, d:d + LANES] = info
    meta_ref[...] = jnp.transpose(info)[0:SUBLANES, :].astype(jnp.int32)


def _mix_combine_kernel(*refs, tm, pitch, ns, n_tiles, n_groups, epg):
    mix_in, cmb_in = refs[0:10], refs[10:21]
    xe_ref, cnt_ref, meta_ref = refs[21:24]
    (ynew_ref, yold_ref, er_ref, ep_ref, a_ref, b_ref, h_ref, car_ref, cnt_s,
     wa_s, wb_s, wo_s, wstage, wsem) = refs[24:]
    s = pl.program_id(0)
    dense_hbm = cmb_in[5:8]
    cmb_in = cmb_in[0:5] + (wa_s, wb_s, wo_s) + cmb_in[8:]

    @pl.when(s == 0)
    def _():
        for w_hbm, w_s in zip(dense_hbm, (wa_s, wb_s, wo_s)):
            _load_cast_weight(w_hbm, w_s, wstage, wsem)
        _mixer_init(er_ref, ep_ref, a_ref, b_ref, car_ref, tm)
        cnt_s[...] = jnp.zeros(cnt_s.shape, _F32)
        yold_ref[...] = jnp.zeros(yold_ref.shape, _BF16)

    seq_pos = lax.rem(jnp.minimum(s, n_tiles - 1), ns)
    mixer = _mixer_tile(*mix_in, ynew_ref.at[0], ynew_ref.at[1], er_ref, ep_ref, a_ref, b_ref, h_ref,
                        car_ref, first=seq_pos == 0, t0=seq_pos * tm, tm=tm, pitch=pitch)
    combine = _combine_tile(yold_ref.at[0], yold_ref.at[1], *cmb_in, xe_ref, cnt_ref, meta_ref,
                            cnt_s, live=s >= 1, tm=tm, n_groups=n_groups, epg=epg)
    stages = [mixer, combine]
    while stages:
        for stage in list(stages):
            if next(stage, stages) is stages:
                stages.remove(stage)
    yold_ref[...] = ynew_ref[...]


def _mix_combine(z, x2d, seq, mixer_w, wa, wb, wo, n2, wr, br, tm, n_groups, epg):
    t, d = x2d.shape
    c = wa.shape[0]
    half = d // 2
    n_tiles = t // tm
    ns = seq // tm
    pitch = _scan_pitch(tm)
    n_slab = c // LANES
    off = (3 * c) // half
    kern = functools.partial(_mix_combine_kernel, tm=tm, pitch=pitch, ns=ns, n_tiles=n_tiles,
                             n_groups=n_groups, epg=epg)

    def cur(s):
        return jnp.minimum(s, n_tiles - 1)

    def prev(s):
        return jnp.maximum(s - 1, 0)

    def zspec(j):
        return pl.BlockSpec((tm, half), lambda s, j=j: (prev(s), off + j))

    return pl.pallas_call(
        kern,
        grid=(n_tiles + 1,),
        in_specs=[pl.BlockSpec((tm, 3 * c), lambda s: (cur(s), 0))]
                 + [_const_spec(w.shape) for w in mixer_w]
                 + [zspec(0), zspec(1), zspec(2), zspec(3),
                    pl.BlockSpec((tm, d), lambda s: (prev(s), 0)),
                    pl.BlockSpec(memory_space=pl.ANY), pl.BlockSpec(memory_space=pl.ANY),
                    pl.BlockSpec(memory_space=pl.ANY),
                    _const_spec(n2.shape), _const_spec(wr.shape), _const_spec(br.shape)],
        out_specs=[pl.BlockSpec((tm, d + LANES), lambda s: (prev(s), 0)),
                   pl.BlockSpec((1, LANES), lambda s: (0, 0)),
                   pl.BlockSpec((SUBLANES, tm), lambda s: (0, prev(s)))],
        out_shape=[jax.ShapeDtypeStruct((t, d + LANES), _F32),
                   jax.ShapeDtypeStruct((1, LANES), _F32),
                   jax.ShapeDtypeStruct((SUBLANES, t), jnp.int32)],
        scratch_shapes=[pltpu.VMEM((2, tm, c), _BF16),
                        pltpu.VMEM((2, tm, c), _BF16),
                        pltpu.VMEM((SUBLANES + tm, c), _F32),
                        pltpu.VMEM((2 * SUBLANES + tm, c), _F32),
                        pltpu.VMEM((n_slab, SUBLANES * pitch, LANES), _F32),
                        pltpu.VMEM((n_slab, SUBLANES * pitch, LANES), _F32),
                        pltpu.VMEM((n_slab, SUBLANES * pitch, LANES), _F32),
                        pltpu.VMEM((SUBLANES, c), _F32),
                        pltpu.VMEM((1, LANES), _F32),
                        pltpu.VMEM(wa.shape, _BF16), pltpu.VMEM(wb.shape, _BF16), pltpu.VMEM(wo.shape, _BF16),
                        pltpu.VMEM((2, WEIGHT_STAGE_ROWS, d), _F32),
                        pltpu.SemaphoreType.DMA((2,))],
        compiler_params=_params(1),
        name="mix_combine",
    )(z, *mixer_w, z, z, z, z, x2d, wa, wb, wo, n2, wr, br)


def _bucket_starts(cnt_ref, start_ref, n_buckets, tm, on_tile=None):
    def bucket_body(b, tile_idx):
        start_ref[b] = tile_idx * tm
        nt = lax.div(cnt_ref[b] + (tm - 1), tm)
        if on_tile is not None:
            lax.fori_loop(0, nt, lambda k, carry: (on_tile(tile_idx + k, b), carry)[1], 0)
        return tile_idx + nt

    return lax.fori_loop(0, n_buckets, bucket_body, 0)


def _row_gather_start(row_of, src_hbm, buf, slot, sem, n_rows, unrolled):
    def start(r, priority):
        row = row_of(r)
        pltpu.make_async_copy(src_hbm.at[pl.ds(row, 1)], buf.at[slot, pl.ds(r, 1)],
                              sem.at[slot]).start(priority=priority)

    if unrolled:
        for r in range(n_rows):
            start(r, r % 2)
    else:
        def body(r, _):
            start(2 * r, 0)
            start(2 * r + 1, 1)
            return 0
        lax.fori_loop(0, n_rows // 2, body, 0, unroll=4)


def _row_gather_wait(src_hbm, buf, slot, sem, n_rows):
    pltpu.make_async_copy(src_hbm.at[pl.ds(0, n_rows)], buf.at[slot], sem.at[slot]).wait()


MOE_WEIGHT_SLOTS = 6
MOE_PARTS = 4
MOE_CHUNKS = 3 * MOE_PARTS
MOE_PUMP = 3
MOE_DEPTH = 4
MOE_GATHER_AHEAD = 1
MOE_PLACE_SEGMENT = 512


def _moe_weight_plan(n_groups, epg):
    first_use = {e: min(p for p in range(_N_PAIRS) if e in (_PAIR_SLOT_A[p], _PAIR_SLOT_B[p])) for e in range(epg)}
    last_use = {e: max(p for p in range(_N_PAIRS) if e in (_PAIR_SLOT_A[p], _PAIR_SLOT_B[p])) for e in range(epg)}
    free_after = [-1] * MOE_WEIGHT_SLOTS
    loads, slot_of = [], {}
    for g in range(n_groups):
        for e in sorted(range(epg), key=lambda e: (first_use[e], e)):
            needed_by = g * _N_PAIRS + first_use[e]
            s = min(range(MOE_WEIGHT_SLOTS), key=lambda s: (free_after[s], s))
            assert free_after[s] < needed_by
            loads.append((g * epg + e, s, free_after[s], needed_by))
            free_after[s] = g * _N_PAIRS + last_use[e]
            slot_of[(g, e)] = s
    n_buckets = n_groups * _N_PAIRS
    need = [sum(1 for l in loads if l[3] <= b) for b in range(n_buckets)]
    allow = [sum(1 for l in loads if l[2] < b) for b in range(n_buckets)]
    slot_a = [slot_of[(b // _N_PAIRS, _PAIR_SLOT_A[b % _N_PAIRS])] for b in range(n_buckets)]
    slot_b = [slot_of[(b // _N_PAIRS, _PAIR_SLOT_B[b % _N_PAIRS])] for b in range(n_buckets)]
    return [l[0] for l in loads], [l[1] for l in loads], need, allow, slot_a, slot_b


def _moe_kernel(tokb_ref, tokr_ref, cntb_ref, need_ref, allow_ref, sa_ref, sb_ref, le_ref, ls_ref,
                xe_hbm, n2_ref, wg_hbm, wu_hbm, wd_hbm,
                x2s_ref, xbuf, gsem, wg_s, wu_s, wd_s, st_gu, st_d, wsem, cnt, src_s, tbk_s, start_s, tot_s,
                *, tm):
    j = pl.program_id(0)
    slot = lax.rem(j, MOE_GATHER_AHEAD + 1)
    d = x2s_ref.shape[1]
    t_rows = xe_hbm.shape[0]
    n_buckets = start_s.shape[0]

    def src_row(tile):
        return lambda r: src_s[tile * tm + r]

    rows_gu = wg_hbm.shape[1] // MOE_PARTS
    rows_d = wd_hbm.shape[1] // MOE_PARTS

    def chunk_dma(c, kind):
        load = c // MOE_CHUNKS
        part = c % MOE_PARTS
        e = le_ref[load]
        if kind == 2:
            return pltpu.make_async_copy(wd_hbm.at[e, pl.ds(part * rows_d, rows_d)], st_d.at[c % MOE_DEPTH],
                                         wsem.at[c % MOE_DEPTH])
        src = wg_hbm if kind == 0 else wu_hbm
        return pltpu.make_async_copy(src.at[e, pl.ds(part * rows_gu, rows_gu)], st_gu.at[c % MOE_DEPTH],
                                     wsem.at[c % MOE_DEPTH])

    def for_kind(c, fn):
        kind = (c % MOE_CHUNKS) // MOE_PARTS
        for k in range(3):
            @pl.when(kind == k)
            def _(k=k):
                fn(k)

    def issue_one(allowed):
        @pl.when(cnt[0] < jnp.minimum(allowed, cnt[1] + MOE_DEPTH))
        def _():
            c = cnt[0]
            for_kind(c, lambda k: chunk_dma(c, k).start())
            cnt[0] = c + 1

    def retire():
        c = cnt[1]
        s = ls_ref[c // MOE_CHUNKS]
        part = c % MOE_PARTS

        def finish(k):
            chunk_dma(c, k).wait()
            if k == 2:
                wd_s[s, pl.ds(part * rows_d, rows_d), :] = st_d[c % MOE_DEPTH].astype(_BF16)
            else:
                dst = wg_s if k == 0 else wu_s
                dst[s, pl.ds(part * rows_gu, rows_gu), :] = st_gu[c % MOE_DEPTH].astype(_BF16)

        for_kind(c, finish)
        cnt[1] = c + 1

    def pump(required, allowed, extra):
        n_iter = jnp.maximum(required - cnt[1], jnp.minimum(extra, allowed - cnt[1]))
        n_fill = jnp.minimum(allowed, cnt[1] + MOE_DEPTH) - cnt[0]
        lax.fori_loop(0, jnp.maximum(n_fill, 0), lambda _, carry: (issue_one(allowed), carry)[1], 0)

        def body(_, carry):
            retire()
            issue_one(allowed)
            return carry

        lax.fori_loop(0, jnp.maximum(n_iter, 0), body, 0)

    @pl.when(j == 0)
    def _():
        cnt[0] = 0
        cnt[1] = 0

        def set_tile(tile, b):
            tbk_s[tile] = b

        total = _bucket_starts(cntb_ref, start_s, n_buckets, tm, on_tile=set_tile)
        tot_s[0] = total

        def fill(lo, hi):
            def body(q, carry):
                src_s[q] = jnp.minimum(jnp.where(q >= t_rows, q - t_rows, q), t_rows - 1)
                return carry
            lax.fori_loop(lo, hi, body, 0)

        def fill_bucket(b, carry):
            nxt = jnp.where(b + 1 < n_buckets, start_s[jnp.minimum(b + 1, n_buckets - 1)], total * tm)
            fill(start_s[b] + cntb_ref[b], nxt)
            return carry
        lax.fori_loop(0, n_buckets, fill_bucket, 0)
        fill(total * tm, (total + MOE_GATHER_AHEAD) * tm)

        def place(tok, carry):
            src_s[start_s[tokb_ref[tok]] + tokr_ref[tok]] = tok
            return carry

        first_allowed = allow_ref[tbk_s[0]] * MOE_CHUNKS
        seg = min(MOE_PLACE_SEGMENT, t_rows)

        def place_segment(sg, carry):
            lax.fori_loop(0, seg, lambda i, c: place(sg * seg + i, c), 0, unroll=16)
            pump(0, first_allowed, 2)
            return carry
        lax.fori_loop(0, t_rows // seg, place_segment, 0)

        for tile in range(MOE_GATHER_AHEAD):
            _row_gather_start(src_row(tile), xe_hbm, xbuf, tile, gsem, tm, unrolled=False)

    n_valid = tot_s[0]

    @pl.when(j < n_valid)
    def _():
        b = tbk_s[j]
        pump(need_ref[b] * MOE_CHUNKS, allow_ref[b] * MOE_CHUNKS, MOE_PUMP)
        s_a = sa_ref[b]
        s_b = sb_ref[b]

        def tile_body(cur):
            _row_gather_wait(xe_hbm, xbuf, cur, gsem, tm)
            _row_gather_start(src_row(j + MOE_GATHER_AHEAD), xe_hbm, xbuf,
                              (cur + MOE_GATHER_AHEAD) % (MOE_GATHER_AHEAD + 1), gsem, tm, unrolled=True)
            xe = xbuf[cur]
            x1 = xe[:, 0:d]
            w_a = xe[:, d + 2:d + 3]
            w_b = xe[:, d + 3:d + 4]
            ht = _rms(x1, n2_ref[...]).astype(_BF16)

            def expert(s, wgt):
                hg = jnp.dot(ht, wg_s[s], preferred_element_type=_F32)
                hu = jnp.dot(ht, wu_s[s], preferred_element_type=_F32)
                half = 0.5 * hg
                return ((half + half * jnp.tanh(half)) * hu * wgt).astype(_BF16)

            y = jnp.dot(expert(s_a, w_a), wd_s[s_a], preferred_element_type=_F32)
            y = y + jnp.dot(expert(s_b, w_b), wd_s[s_b], preferred_element_type=_F32)
            x2s_ref[...] = x1 + y

        for cur in range(MOE_GATHER_AHEAD + 1):
            pl.when(slot == cur)(functools.partial(tile_body, cur))

    @pl.when(j >= n_valid)
    def _():
        x2s_ref[...] = jnp.zeros(x2s_ref.shape, _F32)

        @pl.when(j == n_valid)
        def _():
            for ahead in range(MOE_GATHER_AHEAD):
                _row_gather_wait(xe_hbm, xbuf, lax.rem(j + ahead, MOE_GATHER_AHEAD + 1), gsem, tm)
            lax.fori_loop(0, cnt[0] - cnt[1], lambda _, carry: (retire(), carry)[1], 0)


def _moe(tok_bucket, tok_rank, bucket_cnt, xe, n2, wg, wu, wd, tm, n_groups, epg):
    n_buckets = bucket_cnt.shape[0]
    n_tiles = xe.shape[0] // tm + n_buckets - 1 + MOE_GATHER_AHEAD
    d = wg.shape[1]
    f = wg.shape[2]
    kern = functools.partial(_moe_kernel, tm=tm)
    le, ls, need, allow, slot_a, slot_b = (jnp.asarray(v, jnp.int32) for v in _moe_weight_plan(n_groups, epg))

    grid_spec = pltpu.PrefetchScalarGridSpec(
        num_scalar_prefetch=9,
        grid=(n_tiles,),
        in_specs=[pl.BlockSpec(memory_space=pl.ANY),
                  pl.BlockSpec((1, d), lambda j, *_: (0, 0)),
                  pl.BlockSpec(memory_space=pl.ANY),
                  pl.BlockSpec(memory_space=pl.ANY),
                  pl.BlockSpec(memory_space=pl.ANY)],
        out_specs=pl.BlockSpec((tm, d), lambda j, *_: (j, 0)),
        scratch_shapes=[pltpu.VMEM((MOE_GATHER_AHEAD + 1, tm, d + LANES), _F32),
                        pltpu.SemaphoreType.DMA((MOE_GATHER_AHEAD + 1,)),
                        pltpu.VMEM((MOE_WEIGHT_SLOTS, d, f), _BF16),
                        pltpu.VMEM((MOE_WEIGHT_SLOTS, d, f), _BF16),
                        pltpu.VMEM((MOE_WEIGHT_SLOTS, f, d), _BF16),
                        pltpu.VMEM((MOE_DEPTH, d // MOE_PARTS, f), _F32),
                        pltpu.VMEM((MOE_DEPTH, f // MOE_PARTS, d), _F32),
                        pltpu.SemaphoreType.DMA((MOE_DEPTH,)),
                        pltpu.SMEM((2,), jnp.int32),
                        pltpu.SMEM((n_tiles * tm,), jnp.int32),
                        pltpu.SMEM((n_tiles,), jnp.int32),
                        pltpu.SMEM((n_buckets,), jnp.int32),
                        pltpu.SMEM((1,), jnp.int32)],
    )
    return pl.pallas_call(
        kern,
        grid_spec=grid_spec,
        out_shape=jax.ShapeDtypeStruct((n_tiles * tm, d), _F32),
        compiler_params=_params(1),
        name="moe",
    )(tok_bucket, tok_rank, bucket_cnt, need, allow, slot_a, slot_b, le, ls, xe, n2, wg, wu, wd)


def _ple_kernel(tokb_ref, tokr_ref, cntb_ref, x2s_hbm, p_ref, ng_ref, wg_hbm, wp_ref, nf_ref, o_ref,
                xbuf, gsem, start_s, wg_ref, wstage, wsem, *, tm, n_steps, tm_sorted):
    i = pl.program_id(0)
    slot = i % 2

    def sorted_row(tile):
        return lambda r: start_s[tokb_ref[tile * tm + r]] + tokr_ref[tile * tm + r]

    @pl.when(i == 0)
    def _():
        _bucket_starts(cntb_ref, start_s, start_s.shape[0], tm_sorted)
        _row_gather_start(sorted_row(0), x2s_hbm, xbuf, 0, gsem, tm, unrolled=False)
        _load_cast_weight(wg_hbm, wg_ref, wstage, wsem)

    nxt = jnp.where(i + 1 == n_steps, 0, i + 1)

    def step(cur):
        _row_gather_wait(x2s_hbm, xbuf, cur, gsem, tm)
        _row_gather_start(sorted_row(nxt), x2s_hbm, xbuf, 1 - cur, gsem, tm, unrolled=True)
        x2 = xbuf[cur]
        g = _sigmoid(jnp.dot(_rms(x2, ng_ref[...]).astype(_BF16), wg_ref[...], preferred_element_type=_F32))
        e = jnp.dot(p_ref[...].astype(_BF16), wp_ref[...], preferred_element_type=_F32)
        o_ref[...] = _rms(x2 + g * e, nf_ref[...])

        @pl.when(i == n_steps - 1)
        def _():
            _row_gather_wait(x2s_hbm, xbuf, 1 - cur, gsem, tm)

    for parity in range(2):
        pl.when(slot == parity)(functools.partial(step, parity))


def _ple(tok_bucket, tok_rank, bucket_cnt, x2s, p2d, ng, wg, wp, nf, tm, t, tm_sorted):
    d = x2s.shape[1]
    pd = p2d.shape[1]
    n_steps = t // tm
    kern = functools.partial(_ple_kernel, tm=tm, n_steps=n_steps, tm_sorted=tm_sorted)

    def cspec(shape):
        nd = len(shape)
        return pl.BlockSpec(shape, lambda i, *_: (0,) * nd, pipeline_mode=pl.Buffered(1))

    grid_spec = pltpu.PrefetchScalarGridSpec(
        num_scalar_prefetch=3,
        grid=(n_steps,),
        in_specs=[pl.BlockSpec(memory_space=pl.ANY),
                  pl.BlockSpec((tm, pd), lambda i, *_: (i, 0)),
                  cspec(ng.shape), pl.BlockSpec(memory_space=pl.ANY), cspec(wp.shape), cspec(nf.shape)],
        out_specs=pl.BlockSpec((tm, d), lambda i, *_: (i, 0)),
        scratch_shapes=[pltpu.VMEM((2, tm, d), _F32),
                        pltpu.SemaphoreType.DMA((2,)),
                        pltpu.SMEM((bucket_cnt.shape[0],), jnp.int32),
                        pltpu.VMEM(wg.shape, _BF16),
                        pltpu.VMEM((2, WEIGHT_STAGE_ROWS, wg.shape[1]), _F32),
                        pltpu.SemaphoreType.DMA((2,))],
    )
    return pl.pallas_call(
        kern,
        grid_spec=grid_spec,
        out_shape=jax.ShapeDtypeStruct((t, d), _F32),
        compiler_params=_params(1),
        name="ple",
    )(tok_bucket, tok_rank, bucket_cnt, x2s, p2d, ng, wg, wp, nf)


def _block_diag(w, per_block):
    h, hd, _ = w.shape
    nb = h // per_block
    w4 = w.reshape(nb, per_block, hd, hd)
    rows = [jnp.pad(w4[:, p], ((0, 0), (0, 0), (p * hd, (per_block - 1 - p) * hd))) for p in range(per_block)]
    return jnp.concatenate(rows, axis=1)


def _layer(x2d, p2d, bsz, seq, norm1_g, w_in, conv_w, conv_b, w_rg_a, b_rg_a, w_rg_x, b_rg_x, lru_lambda,
           w_pool, pool_scale, w_branch_a, w_branch_b, w_out, norm2_g, w_router_group, b_router_group,
           w_router_expert, b_router_expert, w_e_gate, w_e_up, w_e_down, norm_ple_g, w_ple_gate,
           w_ple_proj, out_norm_g):
    t, d = x2d.shape
    c = conv_b.shape[0]
    heads, hd, _ = w_rg_a.shape
    n_groups = w_router_group.shape[1]
    n_exp = w_router_expert.shape[1]
    epg = n_exp // n_groups
    assert epg == 4 and TOP_K == 2 and hd * (MXU_DIM // hd) == MXU_DIM
    assert w_pool.shape[0] == len(POOL_WINDOWS) and w_pool.shape[1] == MXU_DIM

    tm_in, tn_in, rc_in = min(2048, t), 512, min(256, t)
    tm_mix = min(256, seq)
    tm_cmb = min(256, t)
    tm_moe = min(128, t)
    tm_ple = min(512, t)

    row = lambda v: v.reshape(1, -1).astype(_F32)
    per_block = MXU_DIM // hd

    z = _inproj(x2d, row(norm1_g), w_in, tm_in, tn_in, rc_in)
    mixer_w = (conv_w.reshape(CONV_WIDTH, c), row(conv_b),
               _block_diag(w_rg_a, per_block).astype(_BF16), row(b_rg_a),
               _block_diag(w_rg_x, per_block).astype(_BF16), row(b_rg_x),
               row(lru_lambda), w_pool.astype(_BF16), row(pool_scale))

    n_rt = n_groups + n_exp
    wr = jnp.pad(jnp.concatenate([w_router_group, w_router_expert], axis=1),
                 ((0, 0), (0, LANES - n_rt))).astype(_BF16)
    br = jnp.pad(jnp.concatenate([b_router_group, b_router_expert]), (0, LANES - n_rt)).reshape(1, LANES)
    xe, counts, meta = _mix_combine(z, x2d, seq, mixer_w, w_branch_a, w_branch_b, w_out, row(norm2_g), wr, br,
                                    tm_cmb, n_groups, epg)

    n_buckets = n_groups * _N_PAIRS
    tok_bucket, tok_rank = meta[0], meta[1]
    bucket_cnt = counts[0, :n_buckets].astype(jnp.int32)

    x2s = _moe(tok_bucket, tok_rank, bucket_cnt, xe, row(norm2_g), w_e_gate, w_e_up, w_e_down,
               tm_moe, n_groups, epg)
    return _ple(tok_bucket, tok_rank, bucket_cnt, x2s, p2d, row(norm_ple_g), w_ple_gate,
                w_ple_proj.astype(_BF16), row(out_norm_g), tm_ple, t, tm_moe)


def kernel(x, p, norm1_g, w_in, conv_w, conv_b, w_rg_a, b_rg_a, w_rg_x, b_rg_x, lru_lambda, w_pool, pool_scale, w_branch_a, w_branch_b, w_out, norm2_g, w_router_group, b_router_group, w_router_expert, b_router_expert, w_e_gate, w_e_up, w_e_down, norm_ple_g, w_ple_gate, w_ple_proj, final_norm_g):
    bsz, seq, d = x.shape
    depth = p.shape[0]
    assert depth == 1, "the final RMSNorm is fused into the last layer's embedding kernel"
    out = _layer(x.reshape(bsz * seq, d), p[0].reshape(bsz * seq, -1), bsz, seq,
                 norm1_g[0], w_in[0], conv_w[0], conv_b[0], w_rg_a[0], b_rg_a[0], w_rg_x[0], b_rg_x[0],
                 lru_lambda[0], w_pool[0], pool_scale[0], w_branch_a[0], w_branch_b[0], w_out[0],
                 norm2_g[0], w_router_group[0], b_router_group[0], w_router_expert[0],
                 b_router_expert[0], w_e_gate[0], w_e_up[0], w_e_down[0], norm_ple_g[0],
                 w_ple_gate[0], w_ple_proj[0], final_norm_g)
    return out.reshape(bsz, seq, d)
```

```python
import functools

import jax
import jax.numpy as jnp
from jax import lax
from jax.experimental import pallas as pl
from jax.experimental.pallas import tpu as pltpu

EPS = 1e-6
LRU_C = 8.0
CONV_WIDTH = 4
POOL_WINDOWS = (2, 4, 8, 16)
TOP_K = 2
SQRT_GUARD = 1e-30
WEIGHT_STAGE_ROWS = 256

LANES = 128
SUBLANES = 8
MXU_DIM = 256
VMEM_LIMIT_BYTES = 56 * 1024 * 1024

_BF16 = jnp.bfloat16
_F32 = jnp.float32

_PAIR_SLOT_A = (0, 0, 0, 1, 1, 3)
_PAIR_SLOT_B = (1, 2, 3, 3, 2, 2)
_N_PAIRS = len(_PAIR_SLOT_A)


def _sigmoid(x):
    return 0.5 * jnp.tanh(0.5 * x) + 0.5


def _rms(x, g):
    ms = jnp.mean(x * x, axis=-1, keepdims=True)
    return x * lax.rsqrt(ms + EPS) * g


def _load_cast_weight(w_hbm, w_s, stage, sem):
    rows = stage.shape[1]
    n_chunks = w_hbm.shape[0] // rows

    def chunk_copy(c):
        return pltpu.make_async_copy(w_hbm.at[pl.ds(c * rows, rows)], stage.at[c % 2], sem.at[c % 2])

    chunk_copy(0).start()
    for c in range(n_chunks):
        if c + 1 < n_chunks:
            chunk_copy(c + 1).start()
        chunk_copy(c).wait()
        w_s[c * rows:(c + 1) * rows, :] = stage[c % 2].astype(_BF16)


def _const_spec(shape):
    nd = len(shape)
    return pl.BlockSpec(shape, lambda *_: (0,) * nd, pipeline_mode=pl.Buffered(1))


def _params(n_axes):
    return pltpu.CompilerParams(dimension_semantics=("arbitrary",) * n_axes,
                                vmem_limit_bytes=VMEM_LIMIT_BYTES)


def _inproj_kernel(x_hbm, g_ref, w_ref, z_ref, h_ref, xs_ref, sem, *, tm, rc):
    i = pl.program_id(0)
    j = pl.program_id(1)
    n_chunks = tm // rc
    more_tiles = i + 1 < pl.num_programs(0)

    def chunk_copy(tile, c):
        return pltpu.make_async_copy(x_hbm.at[pl.ds(pl.multiple_of(tile * tm + c * rc, rc), rc)],
                                     xs_ref.at[c % 2], sem.at[c % 2])

    def normalise(tile, c):
        h_ref[tile % 2, pl.ds(pl.multiple_of(c * rc, rc), rc), :] = _rms(xs_ref[c % 2], g_ref[...]).astype(_BF16)

    @pl.when(jnp.logical_and(i == 0, j == 0))
    def _():
        chunk_copy(0, 0).start()
        for c in range(n_chunks):
            if c + 1 < n_chunks:
                chunk_copy(0, c + 1).start()
            chunk_copy(0, c).wait()
            normalise(0, c)

    @pl.when(jnp.logical_and(more_tiles, jnp.logical_and(j >= 1, j <= n_chunks)))
    def _():
        chunk_copy(i + 1, j - 1).wait()
        normalise(i + 1, j - 1)

    @pl.when(jnp.logical_and(more_tiles, j < n_chunks))
    def _():
        chunk_copy(i + 1, j).start()

    z_ref[...] = jnp.dot(h_ref[i % 2], w_ref[...].astype(_BF16), preferred_element_type=_F32)


def _inproj(x2d, g, w, tm, tn, rc):
    t, d = x2d.shape
    n = w.shape[1]
    kern = functools.partial(_inproj_kernel, tm=tm, rc=rc)
    assert n // tn > tm // rc, "a row tile's chunks are prepared during the column steps of the previous tile"
    return pl.pallas_call(
        kern,
        grid=(t // tm, n // tn),
        in_specs=[pl.BlockSpec(memory_space=pl.ANY),
                  pl.BlockSpec((1, d), lambda i, j: (0, 0)),
                  pl.BlockSpec((d, tn), lambda i, j: (0, j))],
        out_specs=pl.BlockSpec((tm, tn), lambda i, j: (i, j)),
        out_shape=jax.ShapeDtypeStruct((t, n), _F32),
        scratch_shapes=[pltpu.VMEM((2, tm, d), _BF16),
                        pltpu.VMEM((2, rc, d), _F32),
                        pltpu.SemaphoreType.DMA((2,))],
        compiler_params=_params(2),
        name="inproj",
    )(x2d, g, w)


def _scan_pitch(tm):
    p = -(-tm // SUBLANES)
    while p % SUBLANES != 4:
        p += 1
    return p


def _mixer_init(er_ref, ep_ref, a_ref, b_ref, car_ref, tm):
    er_ref[...] = jnp.zeros(er_ref.shape, _F32)
    ep_ref[...] = jnp.zeros(ep_ref.shape, _F32)
    car_ref[...] = jnp.zeros(car_ref.shape, _F32)
    a_ref[:, tm:, :] = jnp.ones((a_ref.shape[0], a_ref.shape[1] - tm, LANES), _F32)
    b_ref[:, tm:, :] = jnp.zeros((b_ref.shape[0], b_ref.shape[1] - tm, LANES), _F32)


def _mixer_tile(z_ref, cw_ref, cb_ref, wa_ref, ba_ref, wx_ref, bx_ref, lam_ref, wp_ref, ps_ref,
                ya_ref, yb_ref, er_ref, ep_ref, a_ref, b_ref, h_ref, car_ref, *, first, t0, tm, pitch):
    c = cb_ref.shape[1]
    n_slab = c // LANES
    hist_r = SUBLANES
    hist_p = 2 * SUBLANES

    er_ref[0:hist_r, :] = jnp.where(first, 0.0, er_ref[tm:tm + hist_r, :])
    ep_ref[0:hist_p, :] = jnp.where(first, 0.0, ep_ref[tm:tm + hist_p, :])
    er_ref[hist_r:hist_r + tm, :] = z_ref[:, 0:c]
    ep_ref[hist_p:hist_p + tm, :] = z_ref[:, 2 * c:3 * c]

    kvec = -LRU_C * jax.nn.softplus(-lam_ref[...])
    nblk = c // MXU_DIM
    for k in range(nblk):
        cs = slice(k * MXU_DIM, (k + 1) * MXU_DIM)
        xc = cb_ref[:, cs] + cw_ref[CONV_WIDTH - 1:CONV_WIDTH, cs] * er_ref[hist_r:hist_r + tm, cs]
        for j in range(1, CONV_WIDTH):
            xc = xc + cw_ref[CONV_WIDTH - 1 - j:CONV_WIDTH - j, cs] * er_ref[hist_r - j:hist_r - j + tm, cs]
        xcb = xc.astype(_BF16)
        r = _sigmoid(jnp.dot(xcb, wa_ref[k], preferred_element_type=_F32) + ba_ref[:, cs])
        ig = _sigmoid(jnp.dot(xcb, wx_ref[k], preferred_element_type=_F32) + bx_ref[:, cs])
        log_a = r * kvec[:, cs]
        a = jnp.exp(log_a)
        v = 1.0 - a * a
        mult = v * lax.rsqrt(jnp.maximum(v, SQRT_GUARD))
        bb = mult * ig * xc
        for q in range(MXU_DIM // LANES):
            slab = k * (MXU_DIM // LANES) + q
            a_ref[slab, 0:tm, :] = a[:, q * LANES:(q + 1) * LANES]
            b_ref[slab, 0:tm, :] = bb[:, q * LANES:(q + 1) * LANES]
        yield

    def seg(i):
        return pl.ds(i, SUBLANES, stride=pitch)

    row = lax.broadcasted_iota(jnp.int32, (SUBLANES, LANES), 0)
    for sl in range(n_slab):
        hh = jnp.zeros((SUBLANES, LANES), _F32)
        aa = jnp.ones((SUBLANES, LANES), _F32)
        for i in range(pitch):
            av = a_ref[sl, seg(i), :]
            hh = av * hh + b_ref[sl, seg(i), :]
            aa = av * aa
        d = 1
        while d < SUBLANES:
            hs_ = jnp.where(row >= d, pltpu.roll(hh, d, 0), 0.0)
            as_ = jnp.where(row >= d, pltpu.roll(aa, d, 0), 1.0)
            hh = aa * hs_ + hh
            aa = aa * as_
            d *= 2
        cs = slice(sl * LANES, (sl + 1) * LANES)
        cin = jnp.where(first, 0.0, car_ref[:, cs])
        full = hh + aa * cin
        hv = jnp.where(row >= 1, pltpu.roll(full, 1, 0), cin)
        car_ref[:, cs] = jnp.broadcast_to(full[SUBLANES - 1:SUBLANES, :], (SUBLANES, LANES))
        for i in range(pitch):
            hv = a_ref[sl, seg(i), :] * hv + b_ref[sl, seg(i), :]
            h_ref[sl, seg(i), :] = hv
        g = z_ref[:, c + sl * LANES:c + (sl + 1) * LANES]
        ya_ref[:, cs] = (h_ref[sl, 0:tm, :] * jax.nn.gelu(g)).astype(_BF16)
        yield

    t_idx = (t0 + lax.broadcasted_iota(jnp.int32, (tm, 1), 0) + 1).astype(_F32)
    n_grp = len(POOL_WINDOWS)
    gd = c // n_grp
    for gi, w in enumerate(POOL_WINDOWS):
        cs = slice(gi * gd, (gi + 1) * gd)
        e = ep_ref[:, cs]
        acc = e
        d = 1
        while d < w:
            acc = acc + pltpu.roll(acc, d, 0)
            d *= 2
        xt = e[hist_p:, :]
        cnt = jnp.minimum(t_idx, float(w))
        dd = acc[hist_p:, :] / cnt - xt
        yb = jnp.dot(dd.astype(_BF16), wp_ref[gi], preferred_element_type=_F32) * ps_ref[:, cs]
        yb_ref[:, cs] = yb.astype(_BF16)
        yield


def _combine_tile(ya_ref, yb_ref, ga0_ref, ga1_ref, gb0_ref, gb1_ref, x_ref,
                  wa_ref, wb_ref, wo_ref, n2_ref, wr_ref, br_ref,
                  xe_ref, cnt_ref, meta_ref, cnt_s, *, live, tm, n_groups, epg):
    d = x_ref.shape[1]
    half = d // 2
    piece = MXU_DIM * 2
    ya = ya_ref[...]
    yb = yb_ref[...]
    us = []
    for q in range(d // piece):
        cs = slice(q * piece, (q + 1) * piece)
        ga_ref, gb_ref = (ga0_ref, gb0_ref) if q * piece < half else (ga1_ref, gb1_ref)
        gs = slice((q * piece) % half, (q * piece) % half + piece)
        pa = jnp.dot(ya, wa_ref[:, cs], preferred_element_type=_F32)
        yield
        pb = jnp.dot(yb, wb_ref[:, cs], preferred_element_type=_F32)
        yield
        ta = jnp.tanh(0.5 * ga_ref[:, gs])
        tb = jnp.tanh(0.5 * gb_ref[:, gs])
        us.append((0.5 * ((pa + pb) + (ta * pa + tb * pb))).astype(_BF16))
    u = jnp.concatenate(us, axis=1)
    for q in range(d // MXU_DIM):
        cs = slice(q * MXU_DIM, (q + 1) * MXU_DIM)
        xe_ref[:, cs] = x_ref[:, cs] + jnp.dot(u, wo_ref[:, cs], preferred_element_type=_F32)
        yield
    x1 = xe_ref[:, 0:d]

    ht = _rms(x1, n2_ref[...]).astype(_BF16)
    logits = jnp.dot(ht, wr_ref[...], preferred_element_type=_F32) + br_ref[...]

    lane = lax.broadcasted_iota(jnp.int32, (tm, LANES), 1).astype(_F32)
    ninf = -jnp.inf
    big = float(LANES)

    def first_argmax(v):
        m = jnp.max(v, axis=-1, keepdims=True)
        return m, jnp.min(jnp.where(v == m, lane, big), axis=-1, keepdims=True)

    is_g = lane < float(n_groups)
    gmax, gidx = first_argmax(jnp.where(is_g, logits, ninf))
    g_w = 1.0 / jnp.sum(jnp.where(is_g, jnp.exp(logits - gmax), 0.0), axis=-1, keepdims=True)
    lo_lane = float(n_groups) + float(epg) * gidx
    in_grp = (lane >= lo_lane) & (lane < lo_lane + float(epg))
    le = jnp.where(in_grp, logits, ninf)
    m1, i1 = first_argmax(le)
    m2, i2 = first_argmax(jnp.where(lane == i1, ninf, le))
    e21 = jnp.exp(m2 - m1)
    w1 = g_w / (1.0 + e21)
    w2 = w1 * e21
    e1 = i1 - lo_lane
    e2 = i2 - lo_lane
    lo = jnp.minimum(e1, e2)
    hi = jnp.maximum(e1, e2)
    w_lo = jnp.where(e1 < e2, w1, w2)
    w_hi = jnp.where(e1 < e2, w2, w1)
    pair = jnp.where(lo == 0.0, hi - 1.0, jnp.where(lo == 1.0, 6.0 - hi, 5.0))
    swap = pair == 5.0
    w_a = jnp.where(swap, w_hi, w_lo)
    w_b = jnp.where(swap, w_lo, w_hi)
    bucket = float(_N_PAIRS) * gidx + pair

    onehot = lane == bucket
    oh_bf = jnp.where(onehot, 1.0, 0.0).astype(_BF16)
    rr = lax.broadcasted_iota(jnp.int32, (tm, tm), 0)
    cc = lax.broadcasted_iota(jnp.int32, (tm, tm), 1)
    tri = jnp.where(cc < rr, 1.0, 0.0).astype(_BF16)
    before = jnp.dot(tri, oh_bf, preferred_element_type=_F32) + cnt_s[...]
    rank = jnp.sum(jnp.where(onehot, before, 0.0), axis=-1, keepdims=True)
    cnt_s[...] = cnt_s[...] + jnp.where(live, jnp.sum(jnp.where(onehot, 1.0, 0.0), axis=0, keepdims=True), 0.0)
    cnt_ref[...] = cnt_s[...]

    info = jnp.where(lane == 0.0, bucket,
                     jnp.where(lane == 1.0, rank,
                               jnp.where(lane == 2.0, w_a, jnp.where(lane == 3.0, w_b, 0.0))))
    xe_ref[:, d:d + LANES] = info
    meta_ref[...] = jnp.transpose(info)[0:SUBLANES, :].astype(jnp.int32)


def _mix_combine_kernel(*refs, tm, pitch, ns, n_tiles, n_groups, epg):
    mix_in, cmb_in = refs[0:10], refs[10:21]
    xe_ref, cnt_ref, meta_ref = refs[21:24]
    (ynew_ref, yold_ref, er_ref, ep_ref, a_ref, b_ref, h_ref, car_ref, cnt_s,
     wa_s, wb_s, wo_s, wstage, wsem) = refs[24:]
    s = pl.program_id(0)
    dense_hbm = cmb_in[5:8]
    cmb_in = cmb_in[0:5] + (wa_s, wb_s, wo_s) + cmb_in[8:]

    @pl.when(s == 0)
    def _():
        for w_hbm, w_s in zip(dense_hbm, (wa_s, wb_s, wo_s)):
            _load_cast_weight(w_hbm, w_s, wstage, wsem)
        _mixer_init(er_ref, ep_ref, a_ref, b_ref, car_ref, tm)
        cnt_s[...] = jnp.zeros(cnt_s.shape, _F32)
        yold_ref[...] = jnp.zeros(yold_ref.shape, _BF16)

    seq_pos = lax.rem(jnp.minimum(s, n_tiles - 1), ns)
    mixer = _mixer_tile(*mix_in, ynew_ref.at[0], ynew_ref.at[1], er_ref, ep_ref, a_ref, b_ref, h_ref,
                        car_ref, first=seq_pos == 0, t0=seq_pos * tm, tm=tm, pitch=pitch)
    combine = _combine_tile(yold_ref.at[0], yold_ref.at[1], *cmb_in, xe_ref, cnt_ref, meta_ref,
                            cnt_s, live=s >= 1, tm=tm, n_groups=n_groups, epg=epg)
    stages = [mixer, combine]
    while stages:
        for stage in list(stages):
            if next(stage, stages) is stages:
                stages.remove(stage)
    yold_ref[...] = ynew_ref[...]


def _mix_combine(z, x2d, seq, mixer_w, wa, wb, wo, n2, wr, br, tm, n_groups, epg):
    t, d = x2d.shape
    c = wa.shape[0]
    half = d // 2
    n_tiles = t // tm
    ns = seq // tm
    pitch = _scan_pitch(tm)
    n_slab = c // LANES
    off = (3 * c) // half
    kern = functools.partial(_mix_combine_kernel, tm=tm, pitch=pitch, ns=ns, n_tiles=n_tiles,
                             n_groups=n_groups, epg=epg)

    def cur(s):
        return jnp.minimum(s, n_tiles - 1)

    def prev(s):
        return jnp.maximum(s - 1, 0)

    def zspec(j):
        return pl.BlockSpec((tm, half), lambda s, j=j: (prev(s), off + j))

    return pl.pallas_call(
        kern,
        grid=(n_tiles + 1,),
        in_specs=[pl.BlockSpec((tm, 3 * c), lambda s: (cur(s), 0))]
                 + [_const_spec(w.shape) for w in mixer_w]
                 + [zspec(0), zspec(1), zspec(2), zspec(3),
                    pl.BlockSpec((tm, d), lambda s: (prev(s), 0)),
                    pl.BlockSpec(memory_space=pl.ANY), pl.BlockSpec(memory_space=pl.ANY),
                    pl.BlockSpec(memory_space=pl.ANY),
                    _const_spec(n2.shape), _const_spec(wr.shape), _const_spec(br.shape)],
        out_specs=[pl.BlockSpec((tm, d + LANES), lambda s: (prev(s), 0)),
                   pl.BlockSpec((1, LANES), lambda s: (0, 0)),
                   pl.BlockSpec((SUBLANES, tm), lambda s: (0, prev(s)))],
        out_shape=[jax.ShapeDtypeStruct((t, d + LANES), _F32),
                   jax.ShapeDtypeStruct((1, LANES), _F32),
                   jax.ShapeDtypeStruct((SUBLANES, t), jnp.int32)],
        scratch_shapes=[pltpu.VMEM((2, tm, c), _BF16),
                        pltpu.VMEM((2, tm, c), _BF16),
                        pltpu.VMEM((SUBLANES + tm, c), _F32),
                        pltpu.VMEM((2 * SUBLANES + tm, c), _F32),
                        pltpu.VMEM((n_slab, SUBLANES * pitch, LANES), _F32),
                        pltpu.VMEM((n_slab, SUBLANES * pitch, LANES), _F32),
                        pltpu.VMEM((n_slab, SUBLANES * pitch, LANES), _F32),
                        pltpu.VMEM((SUBLANES, c), _F32),
                        pltpu.VMEM((1, LANES), _F32),
                        pltpu.VMEM(wa.shape, _BF16), pltpu.VMEM(wb.shape, _BF16), pltpu.VMEM(wo.shape, _BF16),
                        pltpu.VMEM((2, WEIGHT_STAGE_ROWS, d), _F32),
                        pltpu.SemaphoreType.DMA((2,))],
        compiler_params=_params(1),
        name="mix_combine",
    )(z, *mixer_w, z, z, z, z, x2d, wa, wb, wo, n2, wr, br)


def _bucket_starts(cnt_ref, start_ref, n_buckets, tm, on_tile=None):
    def bucket_body(b, tile_idx):
        start_ref[b] = tile_idx * tm
        nt = lax.div(cnt_ref[b] + (tm - 1), tm)
        if on_tile is not None:
            lax.fori_loop(0, nt, lambda k, carry: (on_tile(tile_idx + k, b), carry)[1], 0)
        return tile_idx + nt

    return lax.fori_loop(0, n_buckets, bucket_body, 0)


def _row_gather_start(row_of, src_hbm, buf, slot, sem, n_rows, unrolled):
    def start(r, priority):
        row = row_of(r)
        pltpu.make_async_copy(src_hbm.at[pl.ds(row, 1)], buf.at[slot, pl.ds(r, 1)],
                              sem.at[slot]).start(priority=priority)

    if unrolled:
        for r in range(n_rows):
            start(r, r % 2)
    else:
        def body(r, _):
            start(2 * r, 0)
            start(2 * r + 1, 1)
            return 0
        lax.fori_loop(0, n_rows // 2, body, 0, unroll=4)


def _row_gather_wait(src_hbm, buf, slot, sem, n_rows):
    pltpu.make_async_copy(src_hbm.at[pl.ds(0, n_rows)], buf.at[slot], sem.at[slot]).wait()


MOE_WEIGHT_SLOTS = 6
MOE_PARTS = 4
MOE_CHUNKS = 3 * MOE_PARTS
MOE_PUMP = 3
MOE_DEPTH = 4
MOE_GATHER_AHEAD = 1
MOE_PLACE_SEGMENT = 512


def _moe_weight_plan(n_groups, epg):
    first_use = {e: min(p for p in range(_N_PAIRS) if e in (_PAIR_SLOT_A[p], _PAIR_SLOT_B[p])) for e in range(epg)}
    last_use = {e: max(p for p in range(_N_PAIRS) if e in (_PAIR_SLOT_A[p], _PAIR_SLOT_B[p])) for e in range(epg)}
    free_after = [-1] * MOE_WEIGHT_SLOTS
    loads, slot_of = [], {}
    for g in range(n_groups):
        for e in sorted(range(epg), key=lambda e: (first_use[e], e)):
            needed_by = g * _N_PAIRS + first_use[e]
            s = min(range(MOE_WEIGHT_SLOTS), key=lambda s: (free_after[s], s))
            assert free_after[s] < needed_by
            loads.append((g * epg + e, s, free_after[s], needed_by))
            free_after[s] = g * _N_PAIRS + last_use[e]
            slot_of[(g, e)] = s
    n_buckets = n_groups * _N_PAIRS
    need = [sum(1 for l in loads if l[3] <= b) for b in range(n_buckets)]
    allow = [sum(1 for l in loads if l[2] < b) for b in range(n_buckets)]
    slot_a = [slot_of[(b // _N_PAIRS, _PAIR_SLOT_A[b % _N_PAIRS])] for b in range(n_buckets)]
    slot_b = [slot_of[(b // _N_PAIRS, _PAIR_SLOT_B[b % _N_PAIRS])] for b in range(n_buckets)]
    return [l[0] for l in loads], [l[1] for l in loads], need, allow, slot_a, slot_b


def _moe_kernel(tokb_ref, tokr_ref, cntb_ref, need_ref, allow_ref, sa_ref, sb_ref, le_ref, ls_ref,
                xe_hbm, n2_ref, wg_hbm, wu_hbm, wd_hbm,
                x2s_ref, xbuf, gsem, wg_s, wu_s, wd_s, st_gu, st_d, wsem, cnt, src_s, tbk_s, start_s, tot_s,
                *, tm):
    j = pl.program_id(0)
    slot = lax.rem(j, MOE_GATHER_AHEAD + 1)
    d = x2s_ref.shape[1]
    t_rows = xe_hbm.shape[0]
    n_buckets = start_s.shape[0]

    def src_row(tile):
        return lambda r: src_s[tile * tm + r]

    rows_gu = wg_hbm.shape[1] // MOE_PARTS
    rows_d = wd_hbm.shape[1] // MOE_PARTS

    def chunk_dma(c, kind):
        load = c // MOE_CHUNKS
        part = c % MOE_PARTS
        e = le_ref[load]
        if kind == 2:
            return pltpu.make_async_copy(wd_hbm.at[e, pl.ds(part * rows_d, rows_d)], st_d.at[c % MOE_DEPTH],
                                         wsem.at[c % MOE_DEPTH])
        src = wg_hbm if kind == 0 else wu_hbm
        return pltpu.make_async_copy(src.at[e, pl.ds(part * rows_gu, rows_gu)], st_gu.at[c % MOE_DEPTH],
                                     wsem.at[c % MOE_DEPTH])

    def for_kind(c, fn):
        kind = (c % MOE_CHUNKS) // MOE_PARTS
        for k in range(3):
            @pl.when(kind == k)
            def _(k=k):
                fn(k)

    def issue_one(allowed):
        @pl.when(cnt[0] < jnp.minimum(allowed, cnt[1] + MOE_DEPTH))
        def _():
            c = cnt[0]
            for_kind(c, lambda k: chunk_dma(c, k).start())
            cnt[0] = c + 1

    def retire():
        c = cnt[1]
        s = ls_ref[c // MOE_CHUNKS]
        part = c % MOE_PARTS

        def finish(k):
            chunk_dma(c, k).wait()
            if k == 2:
                wd_s[s, pl.ds(part * rows_d, rows_d), :] = st_d[c % MOE_DEPTH].astype(_BF16)
            else:
                dst = wg_s if k == 0 else wu_s
                dst[s, pl.ds(part * rows_gu, rows_gu), :] = st_gu[c % MOE_DEPTH].astype(_BF16)

        for_kind(c, finish)
        cnt[1] = c + 1

    def pump(required, allowed, extra):
        n_iter = jnp.maximum(required - cnt[1], jnp.minimum(extra, allowed - cnt[1]))
        n_fill = jnp.minimum(allowed, cnt[1] + MOE_DEPTH) - cnt[0]
        lax.fori_loop(0, jnp.maximum(n_fill, 0), lambda _, carry: (issue_one(allowed), carry)[1], 0)

        def body(_, carry):
            retire()
            issue_one(allowed)
            return carry

        lax.fori_loop(0, jnp.maximum(n_iter, 0), body, 0)

    @pl.when(j == 0)
    def _():
        cnt[0] = 0
        cnt[1] = 0

        def set_tile(tile, b):
            tbk_s[tile] = b

        total = _bucket_starts(cntb_ref, start_s, n_buckets, tm, on_tile=set_tile)
        tot_s[0] = total

        def fill(lo, hi):
            def body(q, carry):
                src_s[q] = jnp.minimum(jnp.where(q >= t_rows, q - t_rows, q), t_rows - 1)
                return carry
            lax.fori_loop(lo, hi, body, 0)

        def fill_bucket(b, carry):
            nxt = jnp.where(b + 1 < n_buckets, start_s[jnp.minimum(b + 1, n_buckets - 1)], total * tm)
            fill(start_s[b] + cntb_ref[b], nxt)
            return carry
        lax.fori_loop(0, n_buckets, fill_bucket, 0)
        fill(total * tm, (total + MOE_GATHER_AHEAD) * tm)

        def place(tok, carry):
            src_s[start_s[tokb_ref[tok]] + tokr_ref[tok]] = tok
            return carry

        first_allowed = allow_ref[tbk_s[0]] * MOE_CHUNKS
        seg = min(MOE_PLACE_SEGMENT, t_rows)

        def place_segment(sg, carry):
            lax.fori_loop(0, seg, lambda i, c: place(sg * seg + i, c), 0, unroll=16)
            pump(0, first_allowed, 2)
            return carry
        lax.fori_loop(0, t_rows // seg, place_segment, 0)

        for tile in range(MOE_GATHER_AHEAD):
            _row_gather_start(src_row(tile), xe_hbm, xbuf, tile, gsem, tm, unrolled=False)

    n_valid = tot_s[0]

    @pl.when(j < n_valid)
    def _():
        b = tbk_s[j]
        pump(need_ref[b] * MOE_CHUNKS, allow_ref[b] * MOE_CHUNKS, MOE_PUMP)
        s_a = sa_ref[b]
        s_b = sb_ref[b]

        def tile_body(cur):
            _row_gather_wait(xe_hbm, xbuf, cur, gsem, tm)
            _row_gather_start(src_row(j + MOE_GATHER_AHEAD), xe_hbm, xbuf,
                              (cur + MOE_GATHER_AHEAD) % (MOE_GATHER_AHEAD + 1), gsem, tm, unrolled=True)
            xe = xbuf[cur]
            x1 = xe[:, 0:d]
            w_a = xe[:, d + 2:d + 3]
            w_b = xe[:, d + 3:d + 4]
            ht = _rms(x1, n2_ref[...]).astype(_BF16)

            def expert(s, wgt):
                hg = jnp.dot(ht, wg_s[s], preferred_element_type=_F32)
                hu = jnp.dot(ht, wu_s[s], preferred_element_type=_F32)
                half = 0.5 * hg
                return ((half + half * jnp.tanh(half)) * hu * wgt).astype(_BF16)

            y = jnp.dot(expert(s_a, w_a), wd_s[s_a], preferred_element_type=_F32)
            y = y + jnp.dot(expert(s_b, w_b), wd_s[s_b], preferred_element_type=_F32)
            x2s_ref[...] = x1 + y

        for cur in range(MOE_GATHER_AHEAD + 1):
            pl.when(slot == cur)(functools.partial(tile_body, cur))

    @pl.when(j >= n_valid)
    def _():
        x2s_ref[...] = jnp.zeros(x2s_ref.shape, _F32)

        @pl.when(j == n_valid)
        def _():
            for ahead in range(MOE_GATHER_AHEAD):
                _row_gather_wait(xe_hbm, xbuf, lax.rem(j + ahead, MOE_GATHER_AHEAD + 1), gsem, tm)
            lax.fori_loop(0, cnt[0] - cnt[1], lambda _, carry: (retire(), carry)[1], 0)


def _moe(tok_bucket, tok_rank, bucket_cnt, xe, n2, wg, wu, wd, tm, n_groups, epg):
    n_buckets = bucket_cnt.shape[0]
    n_tiles = xe.shape[0] // tm + n_buckets - 1 + MOE_GATHER_AHEAD
    d = wg.shape[1]
    f = wg.shape[2]
    kern = functools.partial(_moe_kernel, tm=tm)
    le, ls, need, allow, slot_a, slot_b = (jnp.asarray(v, jnp.int32) for v in _moe_weight_plan(n_groups, epg))

    grid_spec = pltpu.PrefetchScalarGridSpec(
        num_scalar_prefetch=9,
        grid=(n_tiles,),
        in_specs=[pl.BlockSpec(memory_space=pl.ANY),
                  pl.BlockSpec((1, d), lambda j, *_: (0, 0)),
                  pl.BlockSpec(memory_space=pl.ANY),
                  pl.BlockSpec(memory_space=pl.ANY),
                  pl.BlockSpec(memory_space=pl.ANY)],
        out_specs=pl.BlockSpec((tm, d), lambda j, *_: (j, 0)),
        scratch_shapes=[pltpu.VMEM((MOE_GATHER_AHEAD + 1, tm, d + LANES), _F32),
                        pltpu.SemaphoreType.DMA((MOE_GATHER_AHEAD + 1,)),
                        pltpu.VMEM((MOE_WEIGHT_SLOTS, d, f), _BF16),
                        pltpu.VMEM((MOE_WEIGHT_SLOTS, d, f), _BF16),
                        pltpu.VMEM((MOE_WEIGHT_SLOTS, f, d), _BF16),
                        pltpu.VMEM((MOE_DEPTH, d // MOE_PARTS, f), _F32),
                        pltpu.VMEM((MOE_DEPTH, f // MOE_PARTS, d), _F32),
                        pltpu.SemaphoreType.DMA((MOE_DEPTH,)),
                        pltpu.SMEM((2,), jnp.int32),
                        pltpu.SMEM((n_tiles * tm,), jnp.int32),
                        pltpu.SMEM((n_tiles,), jnp.int32),
                        pltpu.SMEM((n_buckets,), jnp.int32),
                        pltpu.SMEM((1,), jnp.int32)],
    )
    return pl.pallas_call(
        kern,
        grid_spec=grid_spec,
        out_shape=jax.ShapeDtypeStruct((n_tiles * tm, d), _F32),
        compiler_params=_params(1),
        name="moe",
    )(tok_bucket, tok_rank, bucket_cnt, need, allow, slot_a, slot_b, le, ls, xe, n2, wg, wu, wd)


def _ple_kernel(tokb_ref, tokr_ref, cntb_ref, x2s_hbm, p_ref, ng_ref, wg_hbm, wp_ref, nf_ref, o_ref,
                xbuf, gsem, start_s, wg_ref, wstage, wsem, *, tm, n_steps, tm_sorted):
    i = pl.program_id(0)
    slot = i % 2

    def sorted_row(tile):
        return lambda r: start_s[tokb_ref[tile * tm + r]] + tokr_ref[tile * tm + r]

    @pl.when(i == 0)
    def _():
        _bucket_starts(cntb_ref, start_s, start_s.shape[0], tm_sorted)
        _row_gather_start(sorted_row(0), x2s_hbm, xbuf, 0, gsem, tm, unrolled=False)
        _load_cast_weight(wg_hbm, wg_ref, wstage, wsem)

    nxt = jnp.where(i + 1 == n_steps, 0, i + 1)

    def step(cur):
        _row_gather_wait(x2s_hbm, xbuf, cur, gsem, tm)
        _row_gather_start(sorted_row(nxt), x2s_hbm, xbuf, 1 - cur, gsem, tm, unrolled=True)
        x2 = xbuf[cur]
        g = _sigmoid(jnp.dot(_rms(x2, ng_ref[...]).astype(_BF16), wg_ref[...], preferred_element_type=_F32))
        e = jnp.dot(p_ref[...].astype(_BF16), wp_ref[...], preferred_element_type=_F32)
        o_ref[...] = _rms(x2 + g * e, nf_ref[...])

        @pl.when(i == n_steps - 1)
        def _():
            _row_gather_wait(x2s_hbm, xbuf, 1 - cur, gsem, tm)

    for parity in range(2):
        pl.when(slot == parity)(functools.partial(step, parity))


def _ple(tok_bucket, tok_rank, bucket_cnt, x2s, p2d, ng, wg, wp, nf, tm, t, tm_sorted):
    d = x2s.shape[1]
    pd = p2d.shape[1]
    n_steps = t // tm
    kern = functools.partial(_ple_kernel, tm=tm, n_steps=n_steps, tm_sorted=tm_sorted)

    def cspec(shape):
        nd = len(shape)
        return pl.BlockSpec(shape, lambda i, *_: (0,) * nd, pipeline_mode=pl.Buffered(1))

    grid_spec = pltpu.PrefetchScalarGridSpec(
        num_scalar_prefetch=3,
        grid=(n_steps,),
        in_specs=[pl.BlockSpec(memory_space=pl.ANY),
                  pl.BlockSpec((tm, pd), lambda i, *_: (i, 0)),
                  cspec(ng.shape), pl.BlockSpec(memory_space=pl.ANY), cspec(wp.shape), cspec(nf.shape)],
        out_specs=pl.BlockSpec((tm, d), lambda i, *_: (i, 0)),
        scratch_shapes=[pltpu.VMEM((2, tm, d), _F32),
                        pltpu.SemaphoreType.DMA((2,)),
                        pltpu.SMEM((bucket_cnt.shape[0],), jnp.int32),
                        pltpu.VMEM(wg.shape, _BF16),
                        pltpu.VMEM((2, WEIGHT_STAGE_ROWS, wg.shape[1]), _F32),
                        pltpu.SemaphoreType.DMA((2,))],
    )
    return pl.pallas_call(
        kern,
        grid_spec=grid_spec,
        out_shape=jax.ShapeDtypeStruct((t, d), _F32),
        compiler_params=_params(1),
        name="ple",
    )(tok_bucket, tok_rank, bucket_cnt, x2s, p2d, ng, wg, wp, nf)


def _block_diag(w, per_block):
    h, hd, _ = w.shape
    nb = h // per_block
    w4 = w.reshape(nb, per_block, hd, hd)
    rows = [jnp.pad(w4[:, p], ((0, 0), (0, 0), (p * hd, (per_block - 1 - p) * hd))) for p in range(per_block)]
    return jnp.concatenate(rows, axis=1)


def _layer(x2d, p2d, bsz, seq, norm1_g, w_in, conv_w, conv_b, w_rg_a, b_rg_a, w_rg_x, b_rg_x, lru_lambda,
           w_pool, pool_scale, w_branch_a, w_branch_b, w_out, norm2_g, w_router_group, b_router_group,
           w_router_expert, b_router_expert, w_e_gate, w_e_up, w_e_down, norm_ple_g, w_ple_gate,
           w_ple_proj, out_norm_g):
    t, d = x2d.shape
    c = conv_b.shape[0]
    heads, hd, _ = w_rg_a.shape
    n_groups = w_router_group.shape[1]
    n_exp = w_router_expert.shape[1]
    epg = n_exp // n_groups
    assert epg == 4 and TOP_K == 2 and hd * (MXU_DIM // hd) == MXU_DIM
    assert w_pool.shape[0] == len(POOL_WINDOWS) and w_pool.shape[1] == MXU_DIM

    tm_in, tn_in, rc_in = min(2048, t), 512, min(256, t)
    tm_cmb = min(256, seq)
    tm_moe = min(128, t)
    tm_ple = min(512, t)
    assert t % tm_in == 0 and seq % tm_cmb == 0 and t % tm_moe == 0 and t % tm_ple == 0 and tm_ple % 2 == 0
    assert t % min(MOE_PLACE_SEGMENT, t) == 0 and w_in.shape[1] % tn_in == 0

    row = lambda v: v.reshape(1, -1).astype(_F32)
    per_block = MXU_DIM // hd

    z = _inproj(x2d, row(norm1_g), w_in, tm_in, tn_in, rc_in)
    mixer_w = (conv_w.reshape(CONV_WIDTH, c), row(conv_b),
               _block_diag(w_rg_a, per_block).astype(_BF16), row(b_rg_a),
               _block_diag(w_rg_x, per_block).astype(_BF16), row(b_rg_x),
               row(lru_lambda), w_pool.astype(_BF16), row(pool_scale))

    n_rt = n_groups + n_exp
    wr = jnp.pad(jnp.concatenate([w_router_group, w_router_expert], axis=1),
                 ((0, 0), (0, LANES - n_rt))).astype(_BF16)
    br = jnp.pad(jnp.concatenate([b_router_group, b_router_expert]), (0, LANES - n_rt)).reshape(1, LANES)
    xe, counts, meta = _mix_combine(z, x2d, seq, mixer_w, w_branch_a, w_branch_b, w_out, row(norm2_g), wr, br,
                                    tm_cmb, n_groups, epg)

    n_buckets = n_groups * _N_PAIRS
    tok_bucket, tok_rank = meta[0], meta[1]
    bucket_cnt = counts[0, :n_buckets].astype(jnp.int32)

    x2s = _moe(tok_bucket, tok_rank, bucket_cnt, xe, row(norm2_g), w_e_gate, w_e_up, w_e_down,
               tm_moe, n_groups, epg)
    return _ple(tok_bucket, tok_rank, bucket_cnt, x2s, p2d, row(norm_ple_g), w_ple_gate,
                w_ple_proj.astype(_BF16), row(out_norm_g), tm_ple, t, tm_moe)


def kernel(x, p, norm1_g, w_in, conv_w, conv_b, w_rg_a, b_rg_a, w_rg_x, b_rg_x, lru_lambda, w_pool, pool_scale, w_branch_a, w_branch_b, w_out, norm2_g, w_router_group, b_router_group, w_router_expert, b_router_expert, w_e_gate, w_e_up, w_e_down, norm_ple_g, w_ple_gate, w_ple_proj, final_norm_g):
    bsz, seq, d = x.shape
    depth = p.shape[0]
    assert depth == 1, "the final RMSNorm is fused into the last layer's embedding kernel"
    out = _layer(x.reshape(bsz * seq, d), p[0].reshape(bsz * seq, -1), bsz, seq,
                 norm1_g[0], w_in[0], conv_w[0], conv_b[0], w_rg_a[0], b_rg_a[0], w_rg_x[0], b_rg_x[0],
                 lru_lambda[0], w_pool[0], pool_scale[0], w_branch_a[0], w_branch_b[0], w_out[0],
                 norm2_g[0], w_router_group[0], b_router_group[0], w_router_expert[0],
                 b_router_expert[0], w_e_gate[0], w_e_up[0], w_e_down[0], norm_ple_g[0],
                 w_ple_gate[0], w_ple_proj[0], final_norm_g)
    return out.reshape(bsz, seq, d)
```

```python
import functools

import jax
import jax.numpy as jnp
from jax import lax
from jax.experimental import pallas as pl
from jax.experimental.pallas import tpu as pltpu

EPS = 1e-6
LRU_C = 8.0
CONV_WIDTH = 4
POOL_WINDOWS = (2, 4, 8, 16)
TOP_K = 2
SQRT_GUARD = 1e-30
WEIGHT_STAGE_ROWS = 256

LANES = 128
SUBLANES = 8
MXU_DIM = 256
VMEM_LIMIT_BYTES = 56 * 1024 * 1024

_BF16 = jnp.bfloat16
_F32 = jnp.float32

_PAIR_SLOT_A = (0, 0, 0, 1, 1, 3)
_PAIR_SLOT_B = (1, 2, 3, 3, 2, 2)
_N_PAIRS = len(_PAIR_SLOT_A)


def _sigmoid(x):
    return 0.5 * jnp.tanh(0.5 * x) + 0.5


def _rms(x, g):
    ms = jnp.mean(x * x, axis=-1, keepdims=True)
    return x * lax.rsqrt(ms + EPS) * g


def _load_cast_weight(w_hbm, w_s, stage, sem):
    rows = stage.shape[1]
    n_chunks = w_hbm.shape[0] // rows

    def chunk_copy(c):
        return pltpu.make_async_copy(w_hbm.at[pl.ds(c * rows, rows)], stage.at[c % 2], sem.at[c % 2])

    chunk_copy(0).start()
    for c in range(n_chunks):
        if c + 1 < n_chunks:
            chunk_copy(c + 1).start()
        chunk_copy(c).wait()
        w_s[c * rows:(c + 1) * rows, :] = stage[c % 2].astype(_BF16)


def _const_spec(shape):
    nd = len(shape)
    return pl.BlockSpec(shape, lambda *_: (0,) * nd, pipeline_mode=pl.Buffered(1))


def _params(n_axes):
    return pltpu.CompilerParams(dimension_semantics=("arbitrary",) * n_axes,
                                vmem_limit_bytes=VMEM_LIMIT_BYTES)


def _inproj_kernel(x_hbm, g_ref, w_ref, z_ref, h_ref, xs_ref, sem, *, tm, rc):
    i = pl.program_id(0)
    j = pl.program_id(1)
    n_chunks = tm // rc
    more_tiles = i + 1 < pl.num_programs(0)

    def chunk_copy(tile, c):
        return pltpu.make_async_copy(x_hbm.at[pl.ds(pl.multiple_of(tile * tm + c * rc, rc), rc)],
                                     xs_ref.at[c % 2], sem.at[c % 2])

    def normalise(tile, c):
        h_ref[tile % 2, pl.ds(pl.multiple_of(c * rc, rc), rc), :] = _rms(xs_ref[c % 2], g_ref[...]).astype(_BF16)

    @pl.when(jnp.logical_and(i == 0, j == 0))
    def _():
        chunk_copy(0, 0).start()
        for c in range(n_chunks):
            if c + 1 < n_chunks:
                chunk_copy(0, c + 1).start()
            chunk_copy(0, c).wait()
            normalise(0, c)

    @pl.when(jnp.logical_and(more_tiles, jnp.logical_and(j >= 1, j <= n_chunks)))
    def _():
        chunk_copy(i + 1, j - 1).wait()
        normalise(i + 1, j - 1)

    @pl.when(jnp.logical_and(more_tiles, j < n_chunks))
    def _():
        chunk_copy(i + 1, j).start()

    z_ref[...] = jnp.dot(h_ref[i % 2], w_ref[...].astype(_BF16), preferred_element_type=_F32)


def _inproj(x2d, g, w, tm, tn, rc):
    t, d = x2d.shape
    n = w.shape[1]
    kern = functools.partial(_inproj_kernel, tm=tm, rc=rc)
    assert n // tn > tm // rc, "a row tile's chunks are prepared during the column steps of the previous tile"
    return pl.pallas_call(
        kern,
        grid=(t // tm, n // tn),
        in_specs=[pl.BlockSpec(memory_space=pl.ANY),
                  pl.BlockSpec((1, d), lambda i, j: (0, 0)),
                  pl.BlockSpec((d, tn), lambda i, j: (0, j))],
        out_specs=pl.BlockSpec((tm, tn), lambda i, j: (i, j)),
        out_shape=jax.ShapeDtypeStruct((t, n), _F32),
        scratch_shapes=[pltpu.VMEM((2, tm, d), _BF16),
                        pltpu.VMEM((2, rc, d), _F32),
                        pltpu.SemaphoreType.DMA((2,))],
        compiler_params=_params(2),
        name="inproj",
    )(x2d, g, w)


def _scan_pitch(tm):
    p = -(-tm // SUBLANES)
    while p % SUBLANES != 4:
        p += 1
    return p


def _mixer_init(er_ref, ep_ref, a_ref, b_ref, car_ref, tm):
    er_ref[...] = jnp.zeros(er_ref.shape, _F32)
    ep_ref[...] = jnp.zeros(ep_ref.shape, _F32)
    car_ref[...] = jnp.zeros(car_ref.shape, _F32)
    a_ref[:, tm:, :] = jnp.ones((a_ref.shape[0], a_ref.shape[1] - tm, LANES), _F32)
    b_ref[:, tm:, :] = jnp.zeros((b_ref.shape[0], b_ref.shape[1] - tm, LANES), _F32)


def _mixer_tile(z_ref, cw_ref, cb_ref, wa_ref, ba_ref, wx_ref, bx_ref, lam_ref, wp_ref, ps_ref,
                ya_ref, yb_ref, er_ref, ep_ref, a_ref, b_ref, h_ref, car_ref, *, first, t0, tm, pitch):
    c = cb_ref.shape[1]
    n_slab = c // LANES
    hist_r = SUBLANES
    hist_p = 2 * SUBLANES

    er_ref[0:hist_r, :] = jnp.where(first, 0.0, er_ref[tm:tm + hist_r, :])
    ep_ref[0:hist_p, :] = jnp.where(first, 0.0, ep_ref[tm:tm + hist_p, :])
    er_ref[hist_r:hist_r + tm, :] = z_ref[:, 0:c]
    ep_ref[hist_p:hist_p + tm, :] = z_ref[:, 2 * c:3 * c]

    kvec = -LRU_C * jax.nn.softplus(-lam_ref[...])
    nblk = c // MXU_DIM
    for k in range(nblk):
        cs = slice(k * MXU_DIM, (k + 1) * MXU_DIM)
        xc = cb_ref[:, cs] + cw_ref[CONV_WIDTH - 1:CONV_WIDTH, cs] * er_ref[hist_r:hist_r + tm, cs]
        for j in range(1, CONV_WIDTH):
            xc = xc + cw_ref[CONV_WIDTH - 1 - j:CONV_WIDTH - j, cs] * er_ref[hist_r - j:hist_r - j + tm, cs]
        xcb = xc.astype(_BF16)
        r = _sigmoid(jnp.dot(xcb, wa_ref[k], preferred_element_type=_F32) + ba_ref[:, cs])
        ig = _sigmoid(jnp.dot(xcb, wx_ref[k], preferred_element_type=_F32) + bx_ref[:, cs])
        log_a = r * kvec[:, cs]
        a = jnp.exp(log_a)
        v = 1.0 - a * a
        mult = v * lax.rsqrt(jnp.maximum(v, SQRT_GUARD))
        bb = mult * ig * xc
        for q in range(MXU_DIM // LANES):
            slab = k * (MXU_DIM // LANES) + q
            a_ref[slab, 0:tm, :] = a[:, q * LANES:(q + 1) * LANES]
            b_ref[slab, 0:tm, :] = bb[:, q * LANES:(q + 1) * LANES]
        yield

    def seg(i):
        return pl.ds(i, SUBLANES, stride=pitch)

    row = lax.broadcasted_iota(jnp.int32, (SUBLANES, LANES), 0)
    for sl in range(n_slab):
        hh = jnp.zeros((SUBLANES, LANES), _F32)
        aa = jnp.ones((SUBLANES, LANES), _F32)
        for i in range(pitch):
            av = a_ref[sl, seg(i), :]
            hh = av * hh + b_ref[sl, seg(i), :]
            aa = av * aa
        d = 1
        while d < SUBLANES:
            hs_ = jnp.where(row >= d, pltpu.roll(hh, d, 0), 0.0)
            as_ = jnp.where(row >= d, pltpu.roll(aa, d, 0), 1.0)
            hh = aa * hs_ + hh
            aa = aa * as_
            d *= 2
        cs = slice(sl * LANES, (sl + 1) * LANES)
        cin = jnp.where(first, 0.0, car_ref[:, cs])
        full = hh + aa * cin
        hv = jnp.where(row >= 1, pltpu.roll(full, 1, 0), cin)
        car_ref[:, cs] = jnp.broadcast_to(full[SUBLANES - 1:SUBLANES, :], (SUBLANES, LANES))
        for i in range(pitch):
            hv = a_ref[sl, seg(i), :] * hv + b_ref[sl, seg(i), :]
            h_ref[sl, seg(i), :] = hv
        g = z_ref[:, c + sl * LANES:c + (sl + 1) * LANES]
        ya_ref[:, cs] = (h_ref[sl, 0:tm, :] * jax.nn.gelu(g)).astype(_BF16)
        yield

    t_idx = (t0 + lax.broadcasted_iota(jnp.int32, (tm, 1), 0) + 1).astype(_F32)
    n_grp = len(POOL_WINDOWS)
    gd = c // n_grp
    for gi, w in enumerate(POOL_WINDOWS):
        cs = slice(gi * gd, (gi + 1) * gd)
        e = ep_ref[:, cs]
        acc = e
        d = 1
        while d < w:
            acc = acc + pltpu.roll(acc, d, 0)
            d *= 2
        xt = e[hist_p:, :]
        cnt = jnp.minimum(t_idx, float(w))
        dd = acc[hist_p:, :] / cnt - xt
        yb = jnp.dot(dd.astype(_BF16), wp_ref[gi], preferred_element_type=_F32) * ps_ref[:, cs]
        yb_ref[:, cs] = yb.astype(_BF16)
        yield


def _combine_tile(ya_ref, yb_ref, ga0_ref, ga1_ref, gb0_ref, gb1_ref, x_ref,
                  wa_ref, wb_ref, wo_ref, n2_ref, wr_ref, br_ref,
                  xe_ref, cnt_ref, meta_ref, cnt_s, *, live, tm, n_groups, epg):
    d = x_ref.shape[1]
    half = d // 2
    piece = MXU_DIM * 2
    ya = ya_ref[...]
    yb = yb_ref[...]
    us = []
    for q in range(d // piece):
        cs = slice(q * piece, (q + 1) * piece)
        ga_ref, gb_ref = (ga0_ref, gb0_ref) if q * piece < half else (ga1_ref, gb1_ref)
        gs = slice((q * piece) % half, (q * piece) % half + piece)
        pa = jnp.dot(ya, wa_ref[:, cs], preferred_element_type=_F32)
        yield
        pb = jnp.dot(yb, wb_ref[:, cs], preferred_element_type=_F32)
        yield
        ta = jnp.tanh(0.5 * ga_ref[:, gs])
        tb = jnp.tanh(0.5 * gb_ref[:, gs])
        us.append((0.5 * ((pa + pb) + (ta * pa + tb * pb))).astype(_BF16))
    u = jnp.concatenate(us, axis=1)
    for q in range(d // MXU_DIM):
        cs = slice(q * MXU_DIM, (q + 1) * MXU_DIM)
        xe_ref[:, cs] = x_ref[:, cs] + jnp.dot(u, wo_ref[:, cs], preferred_element_type=_F32)
        yield
    x1 = xe_ref[:, 0:d]

    ht = _rms(x1, n2_ref[...]).astype(_BF16)
    logits = jnp.dot(ht, wr_ref[...], preferred_element_type=_F32) + br_ref[...]

    lane = lax.broadcasted_iota(jnp.int32, (tm, LANES), 1).astype(_F32)
    ninf = -jnp.inf
    big = float(LANES)

    def first_argmax(v):
        m = jnp.max(v, axis=-1, keepdims=True)
        return m, jnp.min(jnp.where(v == m, lane, big), axis=-1, keepdims=True)

    is_g = lane < float(n_groups)
    gmax, gidx = first_argmax(jnp.where(is_g, logits, ninf))
    g_w = 1.0 / jnp.sum(jnp.where(is_g, jnp.exp(logits - gmax), 0.0), axis=-1, keepdims=True)
    lo_lane = float(n_groups) + float(epg) * gidx
    in_grp = (lane >= lo_lane) & (lane < lo_lane + float(epg))
    le = jnp.where(in_grp, logits, ninf)
    m1, i1 = first_argmax(le)
    m2, i2 = first_argmax(jnp.where(lane == i1, ninf, le))
    e21 = jnp.exp(m2 - m1)
    w1 = g_w / (1.0 + e21)
    w2 = w1 * e21
    e1 = i1 - lo_lane
    e2 = i2 - lo_lane
    lo = jnp.minimum(e1, e2)
    hi = jnp.maximum(e1, e2)
    w_lo = jnp.where(e1 < e2, w1, w2)
    w_hi = jnp.where(e1 < e2, w2, w1)
    pair = jnp.where(lo == 0.0, hi - 1.0, jnp.where(lo == 1.0, 6.0 - hi, 5.0))
    swap = pair == 5.0
    w_a = jnp.where(swap, w_hi, w_lo)
    w_b = jnp.where(swap, w_lo, w_hi)
    bucket = float(_N_PAIRS) * gidx + pair

    onehot = lane == bucket
    oh_bf = jnp.where(onehot, 1.0, 0.0).astype(_BF16)
    rr = lax.broadcasted_iota(jnp.int32, (tm, tm), 0)
    cc = lax.broadcasted_iota(jnp.int32, (tm, tm), 1)
    tri = jnp.where(cc < rr, 1.0, 0.0).astype(_BF16)
    before = jnp.dot(tri, oh_bf, preferred_element_type=_F32) + cnt_s[...]
    rank = jnp.sum(jnp.where(onehot, before, 0.0), axis=-1, keepdims=True)
    cnt_s[...] = cnt_s[...] + jnp.where(live, jnp.sum(jnp.where(onehot, 1.0, 0.0), axis=0, keepdims=True), 0.0)
    cnt_ref[...] = cnt_s[...]

    info = jnp.where(lane == 0.0, bucket,
                     jnp.where(lane == 1.0, rank,
                               jnp.where(lane == 2.0, w_a, jnp.where(lane == 3.0, w_b, 0.0))))
    xe_ref[:, d:d + LANES] = info
    meta_ref[...] = jnp.transpose(info)[0:SUBLANES, :].astype(jnp.int32)


def _mix_combine_kernel(*refs, tm, pitch, ns, n_tiles, n_groups, epg):
    mix_in, cmb_in = refs[0:10], refs[10:21]
    xe_ref, cnt_ref, meta_ref = refs[21:24]
    (ynew_ref, yold_ref, er_ref, ep_ref, a_ref, b_ref, h_ref, car_ref, cnt_s,
     wa_s, wb_s, wo_s, wstage, wsem) = refs[24:]
    s = pl.program_id(0)
    dense_hbm = cmb_in[5:8]
    cmb_in = cmb_in[0:5] + (wa_s, wb_s, wo_s) + cmb_in[8:]

    @pl.when(s == 0)
    def _():
        for w_hbm, w_s in zip(dense_hbm, (wa_s, wb_s, wo_s)):
            _load_cast_weight(w_hbm, w_s, wstage, wsem)
        _mixer_init(er_ref, ep_ref, a_ref, b_ref, car_ref, tm)
        cnt_s[...] = jnp.zeros(cnt_s.shape, _F32)
        yold_ref[...] = jnp.zeros(yold_ref.shape, _BF16)

    seq_pos = lax.rem(jnp.minimum(s, n_tiles - 1), ns)
    mixer = _mixer_tile(*mix_in, ynew_ref.at[0], ynew_ref.at[1], er_ref, ep_ref, a_ref, b_ref, h_ref,
                        car_ref, first=seq_pos == 0, t0=seq_pos * tm, tm=tm, pitch=pitch)
    combine = _combine_tile(yold_ref.at[0], yold_ref.at[1], *cmb_in, xe_ref, cnt_ref, meta_ref,
                            cnt_s, live=s >= 1, tm=tm, n_groups=n_groups, epg=epg)
    stages = [mixer, combine]
    while stages:
        for stage in list(stages):
            if next(stage, stages) is stages:
                stages.remove(stage)
    yold_ref[...] = ynew_ref[...]


def _mix_combine(z, x2d, seq, mixer_w, wa, wb, wo, n2, wr, br, tm, n_groups, epg):
    t, d = x2d.shape
    c = wa.shape[0]
    half = d // 2
    n_tiles = t // tm
    ns = seq // tm
    pitch = _scan_pitch(tm)
    n_slab = c // LANES
    off = (3 * c) // half
    kern = functools.partial(_mix_combine_kernel, tm=tm, pitch=pitch, ns=ns, n_tiles=n_tiles,
                             n_groups=n_groups, epg=epg)

    def cur(s):
        return jnp.minimum(s, n_tiles - 1)

    def prev(s):
        return jnp.maximum(s - 1, 0)

    def zspec(j):
        return pl.BlockSpec((tm, half), lambda s, j=j: (prev(s), off + j))

    return pl.pallas_call(
        kern,
        grid=(n_tiles + 1,),
        in_specs=[pl.BlockSpec((tm, 3 * c), lambda s: (cur(s), 0))]
                 + [_const_spec(w.shape) for w in mixer_w]
                 + [zspec(0), zspec(1), zspec(2), zspec(3),
                    pl.BlockSpec((tm, d), lambda s: (prev(s), 0)),
                    pl.BlockSpec(memory_space=pl.ANY), pl.BlockSpec(memory_space=pl.ANY),
                    pl.BlockSpec(memory_space=pl.ANY),
                    _const_spec(n2.shape), _const_spec(wr.shape), _const_spec(br.shape)],
        out_specs=[pl.BlockSpec((tm, d + LANES), lambda s: (prev(s), 0)),
                   pl.BlockSpec((1, LANES), lambda s: (0, 0)),
                   pl.BlockSpec((SUBLANES, tm), lambda s: (0, prev(s)))],
        out_shape=[jax.ShapeDtypeStruct((t, d + LANES), _F32),
                   jax.ShapeDtypeStruct((1, LANES), _F32),
                   jax.ShapeDtypeStruct((SUBLANES, t), jnp.int32)],
        scratch_shapes=[pltpu.VMEM((2, tm, c), _BF16),
                        pltpu.VMEM((2, tm, c), _BF16),
                        pltpu.VMEM((SUBLANES + tm, c), _F32),
                        pltpu.VMEM((2 * SUBLANES + tm, c), _F32),
                        pltpu.VMEM((n_slab, SUBLANES * pitch, LANES), _F32),
                        pltpu.VMEM((n_slab, SUBLANES * pitch, LANES), _F32),
                        pltpu.VMEM((n_slab, SUBLANES * pitch, LANES), _F32),
                        pltpu.VMEM((SUBLANES, c), _F32),
                        pltpu.VMEM((1, LANES), _F32),
                        pltpu.VMEM(wa.shape, _BF16), pltpu.VMEM(wb.shape, _BF16), pltpu.VMEM(wo.shape, _BF16),
                        pltpu.VMEM((2, WEIGHT_STAGE_ROWS, d), _F32),
                        pltpu.SemaphoreType.DMA((2,))],
        compiler_params=_params(1),
        name="mix_combine",
    )(z, *mixer_w, z, z, z, z, x2d, wa, wb, wo, n2, wr, br)


def _bucket_starts(cnt_ref, start_ref, n_buckets, tm, on_tile=None):
    def bucket_body(b, tile_idx):
        start_ref[b] = tile_idx * tm
        nt = lax.div(cnt_ref[b] + (tm - 1), tm)
        if on_tile is not None:
            lax.fori_loop(0, nt, lambda k, carry: (on_tile(tile_idx + k, b), carry)[1], 0)
        return tile_idx + nt

    return lax.fori_loop(0, n_buckets, bucket_body, 0)


def _row_gather_start(row_of, src_hbm, buf, slot, sem, n_rows, unrolled):
    def start(r, priority):
        row = row_of(r)
        pltpu.make_async_copy(src_hbm.at[pl.ds(row, 1)], buf.at[slot, pl.ds(r, 1)],
                              sem.at[slot]).start(priority=priority)

    if unrolled:
        for r in range(n_rows):
            start(r, r % 2)
    else:
        def body(r, _):
            start(2 * r, 0)
            start(2 * r + 1, 1)
            return 0
        lax.fori_loop(0, n_rows // 2, body, 0, unroll=4)


def _row_gather_wait(src_hbm, buf, slot, sem, n_rows):
    pltpu.make_async_copy(src_hbm.at[pl.ds(0, n_rows)], buf.at[slot], sem.at[slot]).wait()


MOE_WEIGHT_SLOTS = 6
MOE_PARTS = 2
MOE_CHUNKS = 3 * MOE_PARTS
MOE_PUMP = 2
MOE_DEPTH = 3
MOE_GATHER_AHEAD = 1
MOE_PLACE_SEGMENT = 512


def _moe_weight_plan(n_groups, epg):
    first_use = {e: min(p for p in range(_N_PAIRS) if e in (_PAIR_SLOT_A[p], _PAIR_SLOT_B[p])) for e in range(epg)}
    last_use = {e: max(p for p in range(_N_PAIRS) if e in (_PAIR_SLOT_A[p], _PAIR_SLOT_B[p])) for e in range(epg)}
    free_after = [-1] * MOE_WEIGHT_SLOTS
    loads, slot_of = [], {}
    for g in range(n_groups):
        for e in sorted(range(epg), key=lambda e: (first_use[e], e)):
            needed_by = g * _N_PAIRS + first_use[e]
            s = min(range(MOE_WEIGHT_SLOTS), key=lambda s: (free_after[s], s))
            assert free_after[s] < needed_by
            loads.append((g * epg + e, s, free_after[s], needed_by))
            free_after[s] = g * _N_PAIRS + last_use[e]
            slot_of[(g, e)] = s
    n_buckets = n_groups * _N_PAIRS
    need = [sum(1 for l in loads if l[3] <= b) for b in range(n_buckets)]
    allow = [sum(1 for l in loads if l[2] < b) for b in range(n_buckets)]
    slot_a = [slot_of[(b // _N_PAIRS, _PAIR_SLOT_A[b % _N_PAIRS])] for b in range(n_buckets)]
    slot_b = [slot_of[(b // _N_PAIRS, _PAIR_SLOT_B[b % _N_PAIRS])] for b in range(n_buckets)]
    return [l[0] for l in loads], [l[1] for l in loads], need, allow, slot_a, slot_b


def _moe_kernel(tokb_ref, tokr_ref, cntb_ref, need_ref, allow_ref, sa_ref, sb_ref, le_ref, ls_ref,
                xe_hbm, n2_ref, wg_hbm, wu_hbm, wd_hbm,
                x2s_ref, xbuf, gsem, wg_s, wu_s, wd_s, st_gu, st_d, wsem, cnt, src_s, tbk_s, start_s, tot_s,
                *, tm):
    j = pl.program_id(0)
    slot = lax.rem(j, MOE_GATHER_AHEAD + 1)
    d = x2s_ref.shape[1]
    t_rows = xe_hbm.shape[0]
    n_buckets = start_s.shape[0]

    def src_row(tile):
        return lambda r: src_s[tile * tm + r]

    rows_gu = wg_hbm.shape[1] // MOE_PARTS
    rows_d = wd_hbm.shape[1] // MOE_PARTS

    def chunk_dma(c, kind):
        load = c // MOE_CHUNKS
        part = c % MOE_PARTS
        e = le_ref[load]
        if kind == 2:
            return pltpu.make_async_copy(wd_hbm.at[e, pl.ds(part * rows_d, rows_d)], st_d.at[c % MOE_DEPTH],
                                         wsem.at[c % MOE_DEPTH])
        src = wg_hbm if kind == 0 else wu_hbm
        return pltpu.make_async_copy(src.at[e, pl.ds(part * rows_gu, rows_gu)], st_gu.at[c % MOE_DEPTH],
                                     wsem.at[c % MOE_DEPTH])

    def for_kind(c, fn):
        kind = (c % MOE_CHUNKS) // MOE_PARTS
        for k in range(3):
            @pl.when(kind == k)
            def _(k=k):
                fn(k)

    def issue_one(allowed):
        @pl.when(cnt[0] < jnp.minimum(allowed, cnt[1] + MOE_DEPTH))
        def _():
            c = cnt[0]
            for_kind(c, lambda k: chunk_dma(c, k).start())
            cnt[0] = c + 1

    def retire():
        c = cnt[1]
        s = ls_ref[c // MOE_CHUNKS]
        part = c % MOE_PARTS

        def finish(k):
            chunk_dma(c, k).wait()
            if k == 2:
                wd_s[s, pl.ds(part * rows_d, rows_d), :] = st_d[c % MOE_DEPTH].astype(_BF16)
            else:
                dst = wg_s if k == 0 else wu_s
                dst[s, pl.ds(part * rows_gu, rows_gu), :] = st_gu[c % MOE_DEPTH].astype(_BF16)

        for_kind(c, finish)
        cnt[1] = c + 1

    def pump(required, allowed, extra):
        n_iter = jnp.maximum(required - cnt[1], jnp.minimum(extra, allowed - cnt[1]))
        n_fill = jnp.minimum(allowed, cnt[1] + MOE_DEPTH) - cnt[0]
        lax.fori_loop(0, jnp.maximum(n_fill, 0), lambda _, carry: (issue_one(allowed), carry)[1], 0)

        def body(_, carry):
            retire()
            issue_one(allowed)
            return carry

        lax.fori_loop(0, jnp.maximum(n_iter, 0), body, 0)

    @pl.when(j == 0)
    def _():
        cnt[0] = 0
        cnt[1] = 0

        def set_tile(tile, b):
            tbk_s[tile] = b

        total = _bucket_starts(cntb_ref, start_s, n_buckets, tm, on_tile=set_tile)
        tot_s[0] = total

        def fill(lo, hi):
            def body(q, carry):
                src_s[q] = jnp.minimum(jnp.where(q >= t_rows, q - t_rows, q), t_rows - 1)
                return carry
            lax.fori_loop(lo, hi, body, 0)

        def fill_bucket(b, carry):
            nxt = jnp.where(b + 1 < n_buckets, start_s[jnp.minimum(b + 1, n_buckets - 1)], total * tm)
            fill(start_s[b] + cntb_ref[b], nxt)
            return carry
        lax.fori_loop(0, n_buckets, fill_bucket, 0)
        fill(total * tm, (total + MOE_GATHER_AHEAD) * tm)

        def place(tok, carry):
            src_s[start_s[tokb_ref[tok]] + tokr_ref[tok]] = tok
            return carry

        first_allowed = allow_ref[tbk_s[0]] * MOE_CHUNKS
        seg = min(MOE_PLACE_SEGMENT, t_rows)

        def place_segment(sg, carry):
            lax.fori_loop(0, seg, lambda i, c: place(sg * seg + i, c), 0, unroll=16)
            pump(0, first_allowed, 2)
            return carry
        lax.fori_loop(0, t_rows // seg, place_segment, 0)

        for tile in range(MOE_GATHER_AHEAD):
            _row_gather_start(src_row(tile), xe_hbm, xbuf, tile, gsem, tm, unrolled=False)

    n_valid = tot_s[0]

    @pl.when(j < n_valid)
    def _():
        b = tbk_s[j]
        pump(need_ref[b] * MOE_CHUNKS, allow_ref[b] * MOE_CHUNKS, MOE_PUMP)
        s_a = sa_ref[b]
        s_b = sb_ref[b]

        def tile_body(cur):
            _row_gather_wait(xe_hbm, xbuf, cur, gsem, tm)
            _row_gather_start(src_row(j + MOE_GATHER_AHEAD), xe_hbm, xbuf,
                              (cur + MOE_GATHER_AHEAD) % (MOE_GATHER_AHEAD + 1), gsem, tm, unrolled=True)
            xe = xbuf[cur]
            x1 = xe[:, 0:d]
            w_a = xe[:, d + 2:d + 3]
            w_b = xe[:, d + 3:d + 4]
            ht = _rms(x1, n2_ref[...]).astype(_BF16)

            def expert(s, wgt):
                hg = jnp.dot(ht, wg_s[s], preferred_element_type=_F32)
                hu = jnp.dot(ht, wu_s[s], preferred_element_type=_F32)
                half = 0.5 * hg
                return ((half + half * jnp.tanh(half)) * hu * wgt).astype(_BF16)

            y = jnp.dot(expert(s_a, w_a), wd_s[s_a], preferred_element_type=_F32)
            y = y + jnp.dot(expert(s_b, w_b), wd_s[s_b], preferred_element_type=_F32)
            x2s_ref[...] = x1 + y

        for cur in range(MOE_GATHER_AHEAD + 1):
            pl.when(slot == cur)(functools.partial(tile_body, cur))

    @pl.when(j >= n_valid)
    def _():
        x2s_ref[...] = jnp.zeros(x2s_ref.shape, _F32)

        @pl.when(j == n_valid)
        def _():
            for ahead in range(MOE_GATHER_AHEAD):
                _row_gather_wait(xe_hbm, xbuf, lax.rem(j + ahead, MOE_GATHER_AHEAD + 1), gsem, tm)
            lax.fori_loop(0, cnt[0] - cnt[1], lambda _, carry: (retire(), carry)[1], 0)


def _moe(tok_bucket, tok_rank, bucket_cnt, xe, n2, wg, wu, wd, tm, n_groups, epg):
    n_buckets = bucket_cnt.shape[0]
    n_tiles = xe.shape[0] // tm + n_buckets - 1 + MOE_GATHER_AHEAD
    d = wg.shape[1]
    f = wg.shape[2]
    kern = functools.partial(_moe_kernel, tm=tm)
    le, ls, need, allow, slot_a, slot_b = (jnp.asarray(v, jnp.int32) for v in _moe_weight_plan(n_groups, epg))

    grid_spec = pltpu.PrefetchScalarGridSpec(
        num_scalar_prefetch=9,
        grid=(n_tiles,),
        in_specs=[pl.BlockSpec(memory_space=pl.ANY),
                  pl.BlockSpec((1, d), lambda j, *_: (0, 0)),
                  pl.BlockSpec(memory_space=pl.ANY),
                  pl.BlockSpec(memory_space=pl.ANY),
                  pl.BlockSpec(memory_space=pl.ANY)],
        out_specs=pl.BlockSpec((tm, d), lambda j, *_: (j, 0)),
        scratch_shapes=[pltpu.VMEM((MOE_GATHER_AHEAD + 1, tm, d + LANES), _F32),
                        pltpu.SemaphoreType.DMA((MOE_GATHER_AHEAD + 1,)),
                        pltpu.VMEM((MOE_WEIGHT_SLOTS, d, f), _BF16),
                        pltpu.VMEM((MOE_WEIGHT_SLOTS, d, f), _BF16),
                        pltpu.VMEM((MOE_WEIGHT_SLOTS, f, d), _BF16),
                        pltpu.VMEM((MOE_DEPTH, d // MOE_PARTS, f), _F32),
                        pltpu.VMEM((MOE_DEPTH, f // MOE_PARTS, d), _F32),
                        pltpu.SemaphoreType.DMA((MOE_DEPTH,)),
                        pltpu.SMEM((2,), jnp.int32),
                        pltpu.SMEM((n_tiles * tm,), jnp.int32),
                        pltpu.SMEM((n_tiles,), jnp.int32),
                        pltpu.SMEM((n_buckets,), jnp.int32),
                        pltpu.SMEM((1,), jnp.int32)],
    )
    return pl.pallas_call(
        kern,
        grid_spec=grid_spec,
        out_shape=jax.ShapeDtypeStruct((n_tiles * tm, d), _F32),
        compiler_params=_params(1),
        name="moe",
    )(tok_bucket, tok_rank, bucket_cnt, need, allow, slot_a, slot_b, le, ls, xe, n2, wg, wu, wd)


def _ple_kernel(tokb_ref, tokr_ref, cntb_ref, x2s_hbm, p_ref, ng_ref, wg_hbm, wp_ref, nf_ref, o_ref,
                xbuf, gsem, start_s, wg_ref, wstage, wsem, *, tm, n_steps, tm_sorted):
    i = pl.program_id(0)
    slot = i % 2

    def sorted_row(tile):
        return lambda r: start_s[tokb_ref[tile * tm + r]] + tokr_ref[tile * tm + r]

    @pl.when(i == 0)
    def _():
        _bucket_starts(cntb_ref, start_s, start_s.shape[0], tm_sorted)
        _row_gather_start(sorted_row(0), x2s_hbm, xbuf, 0, gsem, tm, unrolled=False)
        _load_cast_weight(wg_hbm, wg_ref, wstage, wsem)

    nxt = jnp.where(i + 1 == n_steps, 0, i + 1)

    def step(cur):
        _row_gather_wait(x2s_hbm, xbuf, cur, gsem, tm)
        _row_gather_start(sorted_row(nxt), x2s_hbm, xbuf, 1 - cur, gsem, tm, unrolled=True)
        x2 = xbuf[cur]
        g = _sigmoid(jnp.dot(_rms(x2, ng_ref[...]).astype(_BF16), wg_ref[...], preferred_element_type=_F32))
        e = jnp.dot(p_ref[...].astype(_BF16), wp_ref[...], preferred_element_type=_F32)
        o_ref[...] = _rms(x2 + g * e, nf_ref[...])

        @pl.when(i == n_steps - 1)
        def _():
            _row_gather_wait(x2s_hbm, xbuf, 1 - cur, gsem, tm)

    for parity in range(2):
        pl.when(slot == parity)(functools.partial(step, parity))


def _ple(tok_bucket, tok_rank, bucket_cnt, x2s, p2d, ng, wg, wp, nf, tm, t, tm_sorted):
    d = x2s.shape[1]
    pd = p2d.shape[1]
    n_steps = t // tm
    kern = functools.partial(_ple_kernel, tm=tm, n_steps=n_steps, tm_sorted=tm_sorted)

    def cspec(shape):
        nd = len(shape)
        return pl.BlockSpec(shape, lambda i, *_: (0,) * nd, pipeline_mode=pl.Buffered(1))

    grid_spec = pltpu.PrefetchScalarGridSpec(
        num_scalar_prefetch=3,
        grid=(n_steps,),
        in_specs=[pl.BlockSpec(memory_space=pl.ANY),
                  pl.BlockSpec((tm, pd), lambda i, *_: (i, 0)),
                  cspec(ng.shape), pl.BlockSpec(memory_space=pl.ANY), cspec(wp.shape), cspec(nf.shape)],
        out_specs=pl.BlockSpec((tm, d), lambda i, *_: (i, 0)),
        scratch_shapes=[pltpu.VMEM((2, tm, d), _F32),
                        pltpu.SemaphoreType.DMA((2,)),
                        pltpu.SMEM((bucket_cnt.shape[0],), jnp.int32),
                        pltpu.VMEM(wg.shape, _BF16),
                        pltpu.VMEM((2, WEIGHT_STAGE_ROWS, wg.shape[1]), _F32),
                        pltpu.SemaphoreType.DMA((2,))],
    )
    return pl.pallas_call(
        kern,
        grid_spec=grid_spec,
        out_shape=jax.ShapeDtypeStruct((t, d), _F32),
        compiler_params=_params(1),
        name="ple",
    )(tok_bucket, tok_rank, bucket_cnt, x2s, p2d, ng, wg, wp, nf)


def _block_diag(w, per_block):
    h, hd, _ = w.shape
    nb = h // per_block
    w4 = w.reshape(nb, per_block, hd, hd)
    rows = [jnp.pad(w4[:, p], ((0, 0), (0, 0), (p * hd, (per_block - 1 - p) * hd))) for p in range(per_block)]
    return jnp.concatenate(rows, axis=1)


def _layer(x2d, p2d, bsz, seq, norm1_g, w_in, conv_w, conv_b, w_rg_a, b_rg_a, w_rg_x, b_rg_x, lru_lambda,
           w_pool, pool_scale, w_branch_a, w_branch_b, w_out, norm2_g, w_router_group, b_router_group,
           w_router_expert, b_router_expert, w_e_gate, w_e_up, w_e_down, norm_ple_g, w_ple_gate,
           w_ple_proj, out_norm_g):
    t, d = x2d.shape
    c = conv_b.shape[0]
    heads, hd, _ = w_rg_a.shape
    n_groups = w_router_group.shape[1]
    n_exp = w_router_expert.shape[1]
    epg = n_exp // n_groups
    assert epg == 4 and TOP_K == 2 and hd * (MXU_DIM // hd) == MXU_DIM
    assert w_pool.shape[0] == len(POOL_WINDOWS) and w_pool.shape[1] == MXU_DIM

    tm_in, tn_in, rc_in = min(2048, t), 512, min(256, t)
    tm_cmb = min(256, seq)
    tm_moe = min(128, t)
    tm_ple = min(512, t)
    assert t % tm_in == 0 and seq % tm_cmb == 0 and t % tm_moe == 0 and t % tm_ple == 0 and tm_ple % 2 == 0
    assert t % min(MOE_PLACE_SEGMENT, t) == 0 and w_in.shape[1] % tn_in == 0

    row = lambda v: v.reshape(1, -1).astype(_F32)
    per_block = MXU_DIM // hd

    z = _inproj(x2d, row(norm1_g), w_in, tm_in, tn_in, rc_in)
    mixer_w = (conv_w.reshape(CONV_WIDTH, c), row(conv_b),
               _block_diag(w_rg_a, per_block).astype(_BF16), row(b_rg_a),
               _block_diag(w_rg_x, per_block).astype(_BF16), row(b_rg_x),
               row(lru_lambda), w_pool.astype(_BF16), row(pool_scale))

    n_rt = n_groups + n_exp
    wr = jnp.pad(jnp.concatenate([w_router_group, w_router_expert], axis=1),
                 ((0, 0), (0, LANES - n_rt))).astype(_BF16)
    br = jnp.pad(jnp.concatenate([b_router_group, b_router_expert]), (0, LANES - n_rt)).reshape(1, LANES)
    xe, counts, meta = _mix_combine(z, x2d, seq, mixer_w, w_branch_a, w_branch_b, w_out, row(norm2_g), wr, br,
                                    tm_cmb, n_groups, epg)

    n_buckets = n_groups * _N_PAIRS
    tok_bucket, tok_rank = meta[0], meta[1]
    bucket_cnt = counts[0, :n_buckets].astype(jnp.int32)

    x2s = _moe(tok_bucket, tok_rank, bucket_cnt, xe, row(norm2_g), w_e_gate, w_e_up, w_e_down,
               tm_moe, n_groups, epg)
    return _ple(tok_bucket, tok_rank, bucket_cnt, x2s, p2d, row(norm_ple_g), w_ple_gate,
                w_ple_proj.astype(_BF16), row(out_norm_g), tm_ple, t, tm_moe)


def kernel(x, p, norm1_g, w_in, conv_w, conv_b, w_rg_a, b_rg_a, w_rg_x, b_rg_x, lru_lambda, w_pool, pool_scale, w_branch_a, w_branch_b, w_out, norm2_g, w_router_group, b_router_group, w_router_expert, b_router_expert, w_e_gate, w_e_up, w_e_down, norm_ple_g, w_ple_gate, w_ple_proj, final_norm_g):
    bsz, seq, d = x.shape
    depth = p.shape[0]
    assert depth == 1, "the final RMSNorm is fused into the last layer's embedding kernel"
    out = _layer(x.reshape(bsz * seq, d), p[0].reshape(bsz * seq, -1), bsz, seq,
                 norm1_g[0], w_in[0], conv_w[0], conv_b[0], w_rg_a[0], b_rg_a[0], w_rg_x[0], b_rg_x[0],
                 lru_lambda[0], w_pool[0], pool_scale[0], w_branch_a[0], w_branch_b[0], w_out[0],
                 norm2_g[0], w_router_group[0], b_router_group[0], w_router_expert[0],
                 b_router_expert[0], w_e_gate[0], w_e_up[0], w_e_down[0], norm_ple_g[0],
                 w_ple_gate[0], w_ple_proj[0], final_norm_g)
    return out.reshape(bsz, seq, d)
```

```python
import functools

import jax
import jax.numpy as jnp
from jax import lax
from jax.experimental import pallas as pl
from jax.experimental.pallas import tpu as pltpu

EPS = 1e-6
LRU_C = 8.0
CONV_WIDTH = 4
POOL_WINDOWS = (2, 4, 8, 16)
TOP_K = 2
SQRT_GUARD = 1e-30
WEIGHT_STAGE_ROWS = 256

LANES = 128
SUBLANES = 8
MXU_DIM = 256
VMEM_LIMIT_BYTES = 56 * 1024 * 1024

_BF16 = jnp.bfloat16
_F32 = jnp.float32

_PAIR_SLOT_A = (0, 0, 0, 1, 1, 3)
_PAIR_SLOT_B = (1, 2, 3, 3, 2, 2)
_N_PAIRS = len(_PAIR_SLOT_A)


def _sigmoid(x):
    return 0.5 * jnp.tanh(0.5 * x) + 0.5


def _rms(x, g):
    ms = jnp.mean(x * x, axis=-1, keepdims=True)
    return x * lax.rsqrt(ms + EPS) * g


def _load_cast_weight(w_hbm, w_s, stage, sem):
    rows = stage.shape[1]
    n_chunks = w_hbm.shape[0] // rows

    def chunk_copy(c):
        return pltpu.make_async_copy(w_hbm.at[pl.ds(c * rows, rows)], stage.at[c % 2], sem.at[c % 2])

    chunk_copy(0).start()
    for c in range(n_chunks):
        if c + 1 < n_chunks:
            chunk_copy(c + 1).start()
        chunk_copy(c).wait()
        w_s[c * rows:(c + 1) * rows, :] = stage[c % 2].astype(_BF16)


def _const_spec(shape):
    nd = len(shape)
    return pl.BlockSpec(shape, lambda *_: (0,) * nd, pipeline_mode=pl.Buffered(1))


def _params(n_axes):
    return pltpu.CompilerParams(dimension_semantics=("arbitrary",) * n_axes,
                                vmem_limit_bytes=VMEM_LIMIT_BYTES)


def _inproj_kernel(x_hbm, g_ref, w_ref, z_ref, h_ref, xs_ref, sem, *, tm, rc):
    i = pl.program_id(0)
    j = pl.program_id(1)
    n_chunks = tm // rc
    more_tiles = i + 1 < pl.num_programs(0)

    def chunk_copy(tile, c):
        return pltpu.make_async_copy(x_hbm.at[pl.ds(pl.multiple_of(tile * tm + c * rc, rc), rc)],
                                     xs_ref.at[c % 2], sem.at[c % 2])

    def normalise(tile, c):
        h_ref[tile % 2, pl.ds(pl.multiple_of(c * rc, rc), rc), :] = _rms(xs_ref[c % 2], g_ref[...]).astype(_BF16)

    @pl.when(jnp.logical_and(i == 0, j == 0))
    def _():
        chunk_copy(0, 0).start()
        for c in range(n_chunks):
            if c + 1 < n_chunks:
                chunk_copy(0, c + 1).start()
            chunk_copy(0, c).wait()
            normalise(0, c)

    @pl.when(jnp.logical_and(more_tiles, jnp.logical_and(j >= 1, j <= n_chunks)))
    def _():
        chunk_copy(i + 1, j - 1).wait()
        normalise(i + 1, j - 1)

    @pl.when(jnp.logical_and(more_tiles, j < n_chunks))
    def _():
        chunk_copy(i + 1, j).start()

    z_ref[...] = jnp.dot(h_ref[i % 2], w_ref[...].astype(_BF16), preferred_element_type=_F32)


def _inproj(x2d, g, w, tm, tn, rc):
    t, d = x2d.shape
    n = w.shape[1]
    kern = functools.partial(_inproj_kernel, tm=tm, rc=rc)
    assert n // tn > tm // rc, "a row tile's chunks are prepared during the column steps of the previous tile"
    return pl.pallas_call(
        kern,
        grid=(t // tm, n // tn),
        in_specs=[pl.BlockSpec(memory_space=pl.ANY),
                  pl.BlockSpec((1, d), lambda i, j: (0, 0)),
                  pl.BlockSpec((d, tn), lambda i, j: (0, j))],
        out_specs=pl.BlockSpec((tm, tn), lambda i, j: (i, j)),
        out_shape=jax.ShapeDtypeStruct((t, n), _F32),
        scratch_shapes=[pltpu.VMEM((2, tm, d), _BF16),
                        pltpu.VMEM((2, rc, d), _F32),
                        pltpu.SemaphoreType.DMA((2,))],
        compiler_params=_params(2),
        name="inproj",
    )(x2d, g, w)


def _scan_pitch(tm):
    p = -(-tm // SUBLANES)
    while p % SUBLANES != 4:
        p += 1
    return p


def _mixer_init(er_ref, ep_ref, a_ref, b_ref, car_ref, tm):
    er_ref[...] = jnp.zeros(er_ref.shape, _F32)
    ep_ref[...] = jnp.zeros(ep_ref.shape, _F32)
    car_ref[...] = jnp.zeros(car_ref.shape, _F32)
    a_ref[:, tm:, :] = jnp.ones((a_ref.shape[0], a_ref.shape[1] - tm, LANES), _F32)
    b_ref[:, tm:, :] = jnp.zeros((b_ref.shape[0], b_ref.shape[1] - tm, LANES), _F32)


def _mixer_tile(z_ref, cw_ref, cb_ref, wa_ref, ba_ref, wx_ref, bx_ref, lam_ref, wp_ref, ps_ref,
                ya_ref, yb_ref, er_ref, ep_ref, a_ref, b_ref, h_ref, car_ref, *, first, t0, tm, pitch):
    c = cb_ref.shape[1]
    n_slab = c // LANES
    hist_r = SUBLANES
    hist_p = 2 * SUBLANES

    er_ref[0:hist_r, :] = jnp.where(first, 0.0, er_ref[tm:tm + hist_r, :])
    ep_ref[0:hist_p, :] = jnp.where(first, 0.0, ep_ref[tm:tm + hist_p, :])
    er_ref[hist_r:hist_r + tm, :] = z_ref[:, 0:c]
    ep_ref[hist_p:hist_p + tm, :] = z_ref[:, 2 * c:3 * c]

    kvec = -LRU_C * jax.nn.softplus(-lam_ref[...])
    nblk = c // MXU_DIM
    for k in range(nblk):
        cs = slice(k * MXU_DIM, (k + 1) * MXU_DIM)
        xc = cb_ref[:, cs] + cw_ref[CONV_WIDTH - 1:CONV_WIDTH, cs] * er_ref[hist_r:hist_r + tm, cs]
        for j in range(1, CONV_WIDTH):
            xc = xc + cw_ref[CONV_WIDTH - 1 - j:CONV_WIDTH - j, cs] * er_ref[hist_r - j:hist_r - j + tm, cs]
        xcb = xc.astype(_BF16)
        r = _sigmoid(jnp.dot(xcb, wa_ref[k], preferred_element_type=_F32) + ba_ref[:, cs])
        ig = _sigmoid(jnp.dot(xcb, wx_ref[k], preferred_element_type=_F32) + bx_ref[:, cs])
        log_a = r * kvec[:, cs]
        a = jnp.exp(log_a)
        v = 1.0 - a * a
        mult = v * lax.rsqrt(jnp.maximum(v, SQRT_GUARD))
        bb = mult * ig * xc
        for q in range(MXU_DIM // LANES):
            slab = k * (MXU_DIM // LANES) + q
            a_ref[slab, 0:tm, :] = a[:, q * LANES:(q + 1) * LANES]
            b_ref[slab, 0:tm, :] = bb[:, q * LANES:(q + 1) * LANES]
        yield

    def seg(i):
        return pl.ds(i, SUBLANES, stride=pitch)

    row = lax.broadcasted_iota(jnp.int32, (SUBLANES, LANES), 0)
    for sl in range(n_slab):
        hh = jnp.zeros((SUBLANES, LANES), _F32)
        aa = jnp.ones((SUBLANES, LANES), _F32)
        for i in range(pitch):
            av = a_ref[sl, seg(i), :]
            hh = av * hh + b_ref[sl, seg(i), :]
            aa = av * aa
        d = 1
        while d < SUBLANES:
            hs_ = jnp.where(row >= d, pltpu.roll(hh, d, 0), 0.0)
            as_ = jnp.where(row >= d, pltpu.roll(aa, d, 0), 1.0)
            hh = aa * hs_ + hh
            aa = aa * as_
            d *= 2
        cs = slice(sl * LANES, (sl + 1) * LANES)
        cin = jnp.where(first, 0.0, car_ref[:, cs])
        full = hh + aa * cin
        hv = jnp.where(row >= 1, pltpu.roll(full, 1, 0), cin)
        car_ref[:, cs] = jnp.broadcast_to(full[SUBLANES - 1:SUBLANES, :], (SUBLANES, LANES))
        for i in range(pitch):
            hv = a_ref[sl, seg(i), :] * hv + b_ref[sl, seg(i), :]
            h_ref[sl, seg(i), :] = hv
        g = z_ref[:, c + sl * LANES:c + (sl + 1) * LANES]
        ya_ref[:, cs] = (h_ref[sl, 0:tm, :] * jax.nn.gelu(g)).astype(_BF16)
        yield

    t_idx = (t0 + lax.broadcasted_iota(jnp.int32, (tm, 1), 0) + 1).astype(_F32)
    n_grp = len(POOL_WINDOWS)
    gd = c // n_grp
    for gi, w in enumerate(POOL_WINDOWS):
        cs = slice(gi * gd, (gi + 1) * gd)
        e = ep_ref[:, cs]
        acc = e
        d = 1
        while d < w:
            acc = acc + pltpu.roll(acc, d, 0)
            d *= 2
        xt = e[hist_p:, :]
        cnt = jnp.minimum(t_idx, float(w))
        dd = acc[hist_p:, :] / cnt - xt
        yb = jnp.dot(dd.astype(_BF16), wp_ref[gi], preferred_element_type=_F32) * ps_ref[:, cs]
        yb_ref[:, cs] = yb.astype(_BF16)
        yield


def _combine_tile(ya_ref, yb_ref, ga0_ref, ga1_ref, gb0_ref, gb1_ref, x_ref,
                  wa_ref, wb_ref, wo_ref, n2_ref, wr_ref, br_ref,
                  xe_ref, cnt_ref, meta_ref, cnt_s, *, live, tm, n_groups, epg):
    d = x_ref.shape[1]
    half = d // 2
    piece = MXU_DIM * 2
    ya = ya_ref[...]
    yb = yb_ref[...]
    us = []
    for q in range(d // piece):
        cs = slice(q * piece, (q + 1) * piece)
        ga_ref, gb_ref = (ga0_ref, gb0_ref) if q * piece < half else (ga1_ref, gb1_ref)
        gs = slice((q * piece) % half, (q * piece) % half + piece)
        pa = jnp.dot(ya, wa_ref[:, cs], preferred_element_type=_F32)
        yield
        pb = jnp.dot(yb, wb_ref[:, cs], preferred_element_type=_F32)
        yield
        ta = jnp.tanh(0.5 * ga_ref[:, gs])
        tb = jnp.tanh(0.5 * gb_ref[:, gs])
        us.append((0.5 * ((pa + pb) + (ta * pa + tb * pb))).astype(_BF16))
    u = jnp.concatenate(us, axis=1)
    for q in range(d // MXU_DIM):
        cs = slice(q * MXU_DIM, (q + 1) * MXU_DIM)
        xe_ref[:, cs] = x_ref[:, cs] + jnp.dot(u, wo_ref[:, cs], preferred_element_type=_F32)
        yield
    x1 = xe_ref[:, 0:d]

    ht = _rms(x1, n2_ref[...]).astype(_BF16)
    logits = jnp.dot(ht, wr_ref[...], preferred_element_type=_F32) + br_ref[...]

    lane = lax.broadcasted_iota(jnp.int32, (tm, LANES), 1).astype(_F32)
    ninf = -jnp.inf
    big = float(LANES)

    def first_argmax(v):
        m = jnp.max(v, axis=-1, keepdims=True)
        return m, jnp.min(jnp.where(v == m, lane, big), axis=-1, keepdims=True)

    is_g = lane < float(n_groups)
    gmax, gidx = first_argmax(jnp.where(is_g, logits, ninf))
    g_w = 1.0 / jnp.sum(jnp.where(is_g, jnp.exp(logits - gmax), 0.0), axis=-1, keepdims=True)
    lo_lane = float(n_groups) + float(epg) * gidx
    in_grp = (lane >= lo_lane) & (lane < lo_lane + float(epg))
    le = jnp.where(in_grp, logits, ninf)
    m1, i1 = first_argmax(le)
    m2, i2 = first_argmax(jnp.where(lane == i1, ninf, le))
    e21 = jnp.exp(m2 - m1)
    w1 = g_w / (1.0 + e21)
    w2 = w1 * e21
    e1 = i1 - lo_lane
    e2 = i2 - lo_lane
    lo = jnp.minimum(e1, e2)
    hi = jnp.maximum(e1, e2)
    w_lo = jnp.where(e1 < e2, w1, w2)
    w_hi = jnp.where(e1 < e2, w2, w1)
    pair = jnp.where(lo == 0.0, hi - 1.0, jnp.where(lo == 1.0, 6.0 - hi, 5.0))
    swap = pair == 5.0
    w_a = jnp.where(swap, w_hi, w_lo)
    w_b = jnp.where(swap, w_lo, w_hi)
    bucket = float(_N_PAIRS) * gidx + pair

    onehot = lane == bucket
    oh_bf = jnp.where(onehot, 1.0, 0.0).astype(_BF16)
    rr = lax.broadcasted_iota(jnp.int32, (tm, tm), 0)
    cc = lax.broadcasted_iota(jnp.int32, (tm, tm), 1)
    tri = jnp.where(cc < rr, 1.0, 0.0).astype(_BF16)
    before = jnp.dot(tri, oh_bf, preferred_element_type=_F32) + cnt_s[...]
    rank = jnp.sum(jnp.where(onehot, before, 0.0), axis=-1, keepdims=True)
    cnt_s[...] = cnt_s[...] + jnp.where(live, jnp.sum(jnp.where(onehot, 1.0, 0.0), axis=0, keepdims=True), 0.0)
    cnt_ref[...] = cnt_s[...]

    info = jnp.where(lane == 0.0, bucket,
                     jnp.where(lane == 1.0, rank,
                               jnp.where(lane == 2.0, w_a, jnp.where(lane == 3.0, w_b, 0.0))))
    xe_ref[:, d:d + LANES] = info
    meta_ref[...] = jnp.transpose(info)[0:SUBLANES, :].astype(jnp.int32)


def _mix_combine_kernel(*refs, tm, pitch, ns, n_tiles, n_groups, epg):
    mix_in, cmb_in = refs[0:10], refs[10:21]
    xe_ref, cnt_ref, meta_ref = refs[21:24]
    (ynew_ref, yold_ref, er_ref, ep_ref, a_ref, b_ref, h_ref, car_ref, cnt_s,
     wa_s, wb_s, wo_s, wstage, wsem) = refs[24:]
    s = pl.program_id(0)
    dense_hbm = cmb_in[5:8]
    cmb_in = cmb_in[0:5] + (wa_s, wb_s, wo_s) + cmb_in[8:]

    @pl.when(s == 0)
    def _():
        for w_hbm, w_s in zip(dense_hbm, (wa_s, wb_s, wo_s)):
            _load_cast_weight(w_hbm, w_s, wstage, wsem)
        _mixer_init(er_ref, ep_ref, a_ref, b_ref, car_ref, tm)
        cnt_s[...] = jnp.zeros(cnt_s.shape, _F32)

    seq_pos = lax.rem(jnp.minimum(s, n_tiles - 1), ns)

    def run(with_mixer, with_combine):
        stages = []
        if with_mixer:
            stages.append(_mixer_tile(*mix_in, ynew_ref.at[0], ynew_ref.at[1], er_ref, ep_ref, a_ref, b_ref,
                                      h_ref, car_ref, first=seq_pos == 0, t0=seq_pos * tm, tm=tm, pitch=pitch))
        if with_combine:
            stages.append(_combine_tile(yold_ref.at[0], yold_ref.at[1], *cmb_in, xe_ref, cnt_ref, meta_ref,
                                        cnt_s, live=True, tm=tm, n_groups=n_groups, epg=epg))
        while stages:
            for stage in list(stages):
                if next(stage, stages) is stages:
                    stages.remove(stage)
        if with_mixer:
            yold_ref[...] = ynew_ref[...]

    pl.when(s == 0)(functools.partial(run, True, False))
    pl.when(jnp.logical_and(s > 0, s < n_tiles))(functools.partial(run, True, True))
    pl.when(s == n_tiles)(functools.partial(run, False, True))


def _mix_combine(z, x2d, seq, mixer_w, wa, wb, wo, n2, wr, br, tm, n_groups, epg):
    t, d = x2d.shape
    c = wa.shape[0]
    half = d // 2
    n_tiles = t // tm
    ns = seq // tm
    pitch = _scan_pitch(tm)
    n_slab = c // LANES
    off = (3 * c) // half
    kern = functools.partial(_mix_combine_kernel, tm=tm, pitch=pitch, ns=ns, n_tiles=n_tiles,
                             n_groups=n_groups, epg=epg)

    def cur(s):
        return jnp.minimum(s, n_tiles - 1)

    def prev(s):
        return jnp.maximum(s - 1, 0)

    def zspec(j):
        return pl.BlockSpec((tm, half), lambda s, j=j: (prev(s), off + j))

    return pl.pallas_call(
        kern,
        grid=(n_tiles + 1,),
        in_specs=[pl.BlockSpec((tm, 3 * c), lambda s: (cur(s), 0))]
                 + [_const_spec(w.shape) for w in mixer_w]
                 + [zspec(0), zspec(1), zspec(2), zspec(3),
                    pl.BlockSpec((tm, d), lambda s: (prev(s), 0)),
                    pl.BlockSpec(memory_space=pl.ANY), pl.BlockSpec(memory_space=pl.ANY),
                    pl.BlockSpec(memory_space=pl.ANY),
                    _const_spec(n2.shape), _const_spec(wr.shape), _const_spec(br.shape)],
        out_specs=[pl.BlockSpec((tm, d + LANES), lambda s: (prev(s), 0)),
                   pl.BlockSpec((1, LANES), lambda s: (0, 0)),
                   pl.BlockSpec((SUBLANES, tm), lambda s: (0, prev(s)))],
        out_shape=[jax.ShapeDtypeStruct((t, d + LANES), _F32),
                   jax.ShapeDtypeStruct((1, LANES), _F32),
                   jax.ShapeDtypeStruct((SUBLANES, t), jnp.int32)],
        scratch_shapes=[pltpu.VMEM((2, tm, c), _BF16),
                        pltpu.VMEM((2, tm, c), _BF16),
                        pltpu.VMEM((SUBLANES + tm, c), _F32),
                        pltpu.VMEM((2 * SUBLANES + tm, c), _F32),
                        pltpu.VMEM((n_slab, SUBLANES * pitch, LANES), _F32),
                        pltpu.VMEM((n_slab, SUBLANES * pitch, LANES), _F32),
                        pltpu.VMEM((n_slab, SUBLANES * pitch, LANES), _F32),
                        pltpu.VMEM((SUBLANES, c), _F32),
                        pltpu.VMEM((1, LANES), _F32),
                        pltpu.VMEM(wa.shape, _BF16), pltpu.VMEM(wb.shape, _BF16), pltpu.VMEM(wo.shape, _BF16),
                        pltpu.VMEM((2, WEIGHT_STAGE_ROWS, d), _F32),
                        pltpu.SemaphoreType.DMA((2,))],
        compiler_params=_params(1),
        name="mix_combine",
    )(z, *mixer_w, z, z, z, z, x2d, wa, wb, wo, n2, wr, br)


def _bucket_starts(cnt_ref, start_ref, n_buckets, tm, on_tile=None):
    def bucket_body(b, tile_idx):
        start_ref[b] = tile_idx * tm
        nt = lax.div(cnt_ref[b] + (tm - 1), tm)
        if on_tile is not None:
            lax.fori_loop(0, nt, lambda k, carry: (on_tile(tile_idx + k, b), carry)[1], 0)
        return tile_idx + nt

    return lax.fori_loop(0, n_buckets, bucket_body, 0)


def _row_gather_start(row_of, src_hbm, buf, slot, sem, n_rows, unrolled):
    def start(r, priority):
        row = row_of(r)
        pltpu.make_async_copy(src_hbm.at[pl.ds(row, 1)], buf.at[slot, pl.ds(r, 1)],
                              sem.at[slot]).start(priority=priority)

    if unrolled:
        for r in range(n_rows):
            start(r, r % 2)
    else:
        def body(r, _):
            start(2 * r, 0)
            start(2 * r + 1, 1)
            return 0
        lax.fori_loop(0, n_rows // 2, body, 0, unroll=4)


def _row_gather_wait(src_hbm, buf, slot, sem, n_rows):
    pltpu.make_async_copy(src_hbm.at[pl.ds(0, n_rows)], buf.at[slot], sem.at[slot]).wait()


MOE_WEIGHT_SLOTS = 6
MOE_PARTS = 4
MOE_CHUNKS = 3 * MOE_PARTS
MOE_PUMP = 3
MOE_DEPTH = 4
MOE_GATHER_AHEAD = 1
MOE_PLACE_SEGMENT = 512


def _moe_weight_plan(n_groups, epg):
    first_use = {e: min(p for p in range(_N_PAIRS) if e in (_PAIR_SLOT_A[p], _PAIR_SLOT_B[p])) for e in range(epg)}
    last_use = {e: max(p for p in range(_N_PAIRS) if e in (_PAIR_SLOT_A[p], _PAIR_SLOT_B[p])) for e in range(epg)}
    free_after = [-1] * MOE_WEIGHT_SLOTS
    loads, slot_of = [], {}
    for g in range(n_groups):
        for e in sorted(range(epg), key=lambda e: (first_use[e], e)):
            needed_by = g * _N_PAIRS + first_use[e]
            s = min(range(MOE_WEIGHT_SLOTS), key=lambda s: (free_after[s], s))
            assert free_after[s] < needed_by
            loads.append((g * epg + e, s, free_after[s], needed_by))
            free_after[s] = g * _N_PAIRS + last_use[e]
            slot_of[(g, e)] = s
    n_buckets = n_groups * _N_PAIRS
    need = [sum(1 for l in loads if l[3] <= b) for b in range(n_buckets)]
    allow = [sum(1 for l in loads if l[2] < b) for b in range(n_buckets)]
    slot_a = [slot_of[(b // _N_PAIRS, _PAIR_SLOT_A[b % _N_PAIRS])] for b in range(n_buckets)]
    slot_b = [slot_of[(b // _N_PAIRS, _PAIR_SLOT_B[b % _N_PAIRS])] for b in range(n_buckets)]
    return [l[0] for l in loads], [l[1] for l in loads], need, allow, slot_a, slot_b


def _moe_kernel(tokb_ref, tokr_ref, cntb_ref, need_ref, allow_ref, sa_ref, sb_ref, le_ref, ls_ref,
                xe_hbm, n2_ref, wg_hbm, wu_hbm, wd_hbm,
                x2s_ref, xbuf, gsem, wg_s, wu_s, wd_s, st_gu, st_d, wsem, cnt, src_s, tbk_s, start_s, tot_s,
                *, tm):
    j = pl.program_id(0)
    slot = lax.rem(j, MOE_GATHER_AHEAD + 1)
    d = x2s_ref.shape[1]
    t_rows = xe_hbm.shape[0]
    n_buckets = start_s.shape[0]

    def src_row(tile):
        return lambda r: src_s[tile * tm + r]

    rows_gu = wg_hbm.shape[1] // MOE_PARTS
    rows_d = wd_hbm.shape[1] // MOE_PARTS

    def chunk_dma(c, kind):
        load = c // MOE_CHUNKS
        part = c % MOE_PARTS
        e = le_ref[load]
        if kind == 2:
            return pltpu.make_async_copy(wd_hbm.at[e, pl.ds(part * rows_d, rows_d)], st_d.at[c % MOE_DEPTH],
                                         wsem.at[c % MOE_DEPTH])
        src = wg_hbm if kind == 0 else wu_hbm
        return pltpu.make_async_copy(src.at[e, pl.ds(part * rows_gu, rows_gu)], st_gu.at[c % MOE_DEPTH],
                                     wsem.at[c % MOE_DEPTH])

    def for_kind(c, fn):
        kind = (c % MOE_CHUNKS) // MOE_PARTS
        for k in range(3):
            @pl.when(kind == k)
            def _(k=k):
                fn(k)

    def issue_one(allowed):
        @pl.when(cnt[0] < jnp.minimum(allowed, cnt[1] + MOE_DEPTH))
        def _():
            c = cnt[0]
            for_kind(c, lambda k: chunk_dma(c, k).start())
            cnt[0] = c + 1

    def retire():
        c = cnt[1]
        s = ls_ref[c // MOE_CHUNKS]
        part = c % MOE_PARTS

        def finish(k):
            chunk_dma(c, k).wait()
            if k == 2:
                wd_s[s, pl.ds(part * rows_d, rows_d), :] = st_d[c % MOE_DEPTH].astype(_BF16)
            else:
                dst = wg_s if k == 0 else wu_s
                dst[s, pl.ds(part * rows_gu, rows_gu), :] = st_gu[c % MOE_DEPTH].astype(_BF16)

        for_kind(c, finish)
        cnt[1] = c + 1

    def pump(required, allowed, extra):
        n_iter = jnp.maximum(required - cnt[1], jnp.minimum(extra, allowed - cnt[1]))
        n_fill = jnp.minimum(allowed, cnt[1] + MOE_DEPTH) - cnt[0]
        lax.fori_loop(0, jnp.maximum(n_fill, 0), lambda _, carry: (issue_one(allowed), carry)[1], 0)

        def body(_, carry):
            retire()
            issue_one(allowed)
            return carry

        lax.fori_loop(0, jnp.maximum(n_iter, 0), body, 0)

    @pl.when(j == 0)
    def _():
        cnt[0] = 0
        cnt[1] = 0

        def set_tile(tile, b):
            tbk_s[tile] = b

        total = _bucket_starts(cntb_ref, start_s, n_buckets, tm, on_tile=set_tile)
        tot_s[0] = total

        def fill(lo, hi):
            def body(q, carry):
                src_s[q] = jnp.minimum(jnp.where(q >= t_rows, q - t_rows, q), t_rows - 1)
                return carry
            lax.fori_loop(lo, hi, body, 0)

        def fill_bucket(b, carry):
            nxt = jnp.where(b + 1 < n_buckets, start_s[jnp.minimum(b + 1, n_buckets - 1)], total * tm)
            fill(start_s[b] + cntb_ref[b], nxt)
            return carry
        lax.fori_loop(0, n_buckets, fill_bucket, 0)
        fill(total * tm, (total + MOE_GATHER_AHEAD) * tm)

        def place(tok, carry):
            src_s[start_s[tokb_ref[tok]] + tokr_ref[tok]] = tok
            return carry

        first_allowed = allow_ref[tbk_s[0]] * MOE_CHUNKS
        seg = min(MOE_PLACE_SEGMENT, t_rows)

        def place_segment(sg, carry):
            lax.fori_loop(0, seg, lambda i, c: place(sg * seg + i, c), 0, unroll=16)
            pump(0, first_allowed, 2)
            return carry
        lax.fori_loop(0, t_rows // seg, place_segment, 0)

        for tile in range(MOE_GATHER_AHEAD):
            _row_gather_start(src_row(tile), xe_hbm, xbuf, tile, gsem, tm, unrolled=False)

    n_valid = tot_s[0]

    @pl.when(j < n_valid)
    def _():
        b = tbk_s[j]
        pump(need_ref[b] * MOE_CHUNKS, allow_ref[b] * MOE_CHUNKS, MOE_PUMP)
        s_a = sa_ref[b]
        s_b = sb_ref[b]

        def tile_body(cur):
            _row_gather_wait(xe_hbm, xbuf, cur, gsem, tm)
            _row_gather_start(src_row(j + MOE_GATHER_AHEAD), xe_hbm, xbuf,
                              (cur + MOE_GATHER_AHEAD) % (MOE_GATHER_AHEAD + 1), gsem, tm, unrolled=True)
            xe = xbuf[cur]
            x1 = xe[:, 0:d]
            w_a = xe[:, d + 2:d + 3]
            w_b = xe[:, d + 3:d + 4]
            ht = _rms(x1, n2_ref[...]).astype(_BF16)

            def expert(s, wgt):
                hg = jnp.dot(ht, wg_s[s], preferred_element_type=_F32)
                hu = jnp.dot(ht, wu_s[s], preferred_element_type=_F32)
                half = 0.5 * hg
                return ((half + half * jnp.tanh(half)) * hu * wgt).astype(_BF16)

            y = jnp.dot(expert(s_a, w_a), wd_s[s_a], preferred_element_type=_F32)
            y = y + jnp.dot(expert(s_b, w_b), wd_s[s_b], preferred_element_type=_F32)
            x2s_ref[...] = x1 + y

        for cur in range(MOE_GATHER_AHEAD + 1):
            pl.when(slot == cur)(functools.partial(tile_body, cur))

    @pl.when(j >= n_valid)
    def _():
        x2s_ref[...] = jnp.zeros(x2s_ref.shape, _F32)

        @pl.when(j == n_valid)
        def _():
            for ahead in range(MOE_GATHER_AHEAD):
                _row_gather_wait(xe_hbm, xbuf, lax.rem(j + ahead, MOE_GATHER_AHEAD + 1), gsem, tm)
            lax.fori_loop(0, cnt[0] - cnt[1], lambda _, carry: (retire(), carry)[1], 0)


def _moe(tok_bucket, tok_rank, bucket_cnt, xe, n2, wg, wu, wd, tm, n_groups, epg):
    n_buckets = bucket_cnt.shape[0]
    n_tiles = xe.shape[0] // tm + n_buckets - 1 + MOE_GATHER_AHEAD
    d = wg.shape[1]
    f = wg.shape[2]
    kern = functools.partial(_moe_kernel, tm=tm)
    le, ls, need, allow, slot_a, slot_b = (jnp.asarray(v, jnp.int32) for v in _moe_weight_plan(n_groups, epg))

    grid_spec = pltpu.PrefetchScalarGridSpec(
        num_scalar_prefetch=9,
        grid=(n_tiles,),
        in_specs=[pl.BlockSpec(memory_space=pl.ANY),
                  pl.BlockSpec((1, d), lambda j, *_: (0, 0)),
                  pl.BlockSpec(memory_space=pl.ANY),
                  pl.BlockSpec(memory_space=pl.ANY),
                  pl.BlockSpec(memory_space=pl.ANY)],
        out_specs=pl.BlockSpec((tm, d), lambda j, *_: (j, 0)),
        scratch_shapes=[pltpu.VMEM((MOE_GATHER_AHEAD + 1, tm, d + LANES), _F32),
                        pltpu.SemaphoreType.DMA((MOE_GATHER_AHEAD + 1,)),
                        pltpu.VMEM((MOE_WEIGHT_SLOTS, d, f), _BF16),
                        pltpu.VMEM((MOE_WEIGHT_SLOTS, d, f), _BF16),
                        pltpu.VMEM((MOE_WEIGHT_SLOTS, f, d), _BF16),
                        pltpu.VMEM((MOE_DEPTH, d // MOE_PARTS, f), _F32),
                        pltpu.VMEM((MOE_DEPTH, f // MOE_PARTS, d), _F32),
                        pltpu.SemaphoreType.DMA((MOE_DEPTH,)),
                        pltpu.SMEM((2,), jnp.int32),
                        pltpu.SMEM((n_tiles * tm,), jnp.int32),
                        pltpu.SMEM((n_tiles,), jnp.int32),
                        pltpu.SMEM((n_buckets,), jnp.int32),
                        pltpu.SMEM((1,), jnp.int32)],
    )
    return pl.pallas_call(
        kern,
        grid_spec=grid_spec,
        out_shape=jax.ShapeDtypeStruct((n_tiles * tm, d), _F32),
        compiler_params=_params(1),
        name="moe",
    )(tok_bucket, tok_rank, bucket_cnt, need, allow, slot_a, slot_b, le, ls, xe, n2, wg, wu, wd)


def _ple_kernel(tokb_ref, tokr_ref, cntb_ref, x2s_hbm, p_ref, ng_ref, wg_hbm, wp_ref, nf_ref, o_ref,
                xbuf, gsem, start_s, wg_ref, wstage, wsem, *, tm, n_steps, tm_sorted):
    i = pl.program_id(0)
    slot = i % 2

    def sorted_row(tile):
        return lambda r: start_s[tokb_ref[tile * tm + r]] + tokr_ref[tile * tm + r]

    @pl.when(i == 0)
    def _():
        _bucket_starts(cntb_ref, start_s, start_s.shape[0], tm_sorted)
        _row_gather_start(sorted_row(0), x2s_hbm, xbuf, 0, gsem, tm, unrolled=False)
        _load_cast_weight(wg_hbm, wg_ref, wstage, wsem)

    nxt = jnp.where(i + 1 == n_steps, 0, i + 1)

    def step(cur):
        _row_gather_wait(x2s_hbm, xbuf, cur, gsem, tm)
        _row_gather_start(sorted_row(nxt), x2s_hbm, xbuf, 1 - cur, gsem, tm, unrolled=True)
        x2 = xbuf[cur]
        g = _sigmoid(jnp.dot(_rms(x2, ng_ref[...]).astype(_BF16), wg_ref[...], preferred_element_type=_F32))
        e = jnp.dot(p_ref[...].astype(_BF16), wp_ref[...], preferred_element_type=_F32)
        o_ref[...] = _rms(x2 + g * e, nf_ref[...])

        @pl.when(i == n_steps - 1)
        def _():
            _row_gather_wait(x2s_hbm, xbuf, 1 - cur, gsem, tm)

    for parity in range(2):
        pl.when(slot == parity)(functools.partial(step, parity))


def _ple(tok_bucket, tok_rank, bucket_cnt, x2s, p2d, ng, wg, wp, nf, tm, t, tm_sorted):
    d = x2s.shape[1]
    pd = p2d.shape[1]
    n_steps = t // tm
    kern = functools.partial(_ple_kernel, tm=tm, n_steps=n_steps, tm_sorted=tm_sorted)

    def cspec(shape):
        nd = len(shape)
        return pl.BlockSpec(shape, lambda i, *_: (0,) * nd, pipeline_mode=pl.Buffered(1))

    grid_spec = pltpu.PrefetchScalarGridSpec(
        num_scalar_prefetch=3,
        grid=(n_steps,),
        in_specs=[pl.BlockSpec(memory_space=pl.ANY),
                  pl.BlockSpec((tm, pd), lambda i, *_: (i, 0)),
                  cspec(ng.shape), pl.BlockSpec(memory_space=pl.ANY), cspec(wp.shape), cspec(nf.shape)],
        out_specs=pl.BlockSpec((tm, d), lambda i, *_: (i, 0)),
        scratch_shapes=[pltpu.VMEM((2, tm, d), _F32),
                        pltpu.SemaphoreType.DMA((2,)),
                        pltpu.SMEM((bucket_cnt.shape[0],), jnp.int32),
                        pltpu.VMEM(wg.shape, _BF16),
                        pltpu.VMEM((2, WEIGHT_STAGE_ROWS, wg.shape[1]), _F32),
                        pltpu.SemaphoreType.DMA((2,))],
    )
    return pl.pallas_call(
        kern,
        grid_spec=grid_spec,
        out_shape=jax.ShapeDtypeStruct((t, d), _F32),
        compiler_params=_params(1),
        name="ple",
    )(tok_bucket, tok_rank, bucket_cnt, x2s, p2d, ng, wg, wp, nf)


def _block_diag(w, per_block):
    h, hd, _ = w.shape
    nb = h // per_block
    w4 = w.reshape(nb, per_block, hd, hd)
    rows = [jnp.pad(w4[:, p], ((0, 0), (0, 0), (p * hd, (per_block - 1 - p) * hd))) for p in range(per_block)]
    return jnp.concatenate(rows, axis=1)


def _layer(x2d, p2d, bsz, seq, norm1_g, w_in, conv_w, conv_b, w_rg_a, b_rg_a, w_rg_x, b_rg_x, lru_lambda,
           w_pool, pool_scale, w_branch_a, w_branch_b, w_out, norm2_g, w_router_group, b_router_group,
           w_router_expert, b_router_expert, w_e_gate, w_e_up, w_e_down, norm_ple_g, w_ple_gate,
           w_ple_proj, out_norm_g):
    t, d = x2d.shape
    c = conv_b.shape[0]
    heads, hd, _ = w_rg_a.shape
    n_groups = w_router_group.shape[1]
    n_exp = w_router_expert.shape[1]
    epg = n_exp // n_groups
    assert epg == 4 and TOP_K == 2 and hd * (MXU_DIM // hd) == MXU_DIM
    assert w_pool.shape[0] == len(POOL_WINDOWS) and w_pool.shape[1] == MXU_DIM

    tm_in, tn_in, rc_in = min(2048, t), 512, min(256, t)
    tm_cmb = min(256, seq)
    tm_moe = min(128, t)
    tm_ple = min(512, t)
    assert t % tm_in == 0 and seq % tm_cmb == 0 and t % tm_moe == 0 and t % tm_ple == 0 and tm_ple % 2 == 0
    assert t % min(MOE_PLACE_SEGMENT, t) == 0 and w_in.shape[1] % tn_in == 0

    row = lambda v: v.reshape(1, -1).astype(_F32)
    per_block = MXU_DIM // hd

    z = _inproj(x2d, row(norm1_g), w_in, tm_in, tn_in, rc_in)
    mixer_w = (conv_w.reshape(CONV_WIDTH, c), row(conv_b),
               _block_diag(w_rg_a, per_block).astype(_BF16), row(b_rg_a),
               _block_diag(w_rg_x, per_block).astype(_BF16), row(b_rg_x),
               row(lru_lambda), w_pool.astype(_BF16), row(pool_scale))

    n_rt = n_groups + n_exp
    wr = jnp.pad(jnp.concatenate([w_router_group, w_router_expert], axis=1),
                 ((0, 0), (0, LANES - n_rt))).astype(_BF16)
    br = jnp.pad(jnp.concatenate([b_router_group, b_router_expert]), (0, LANES - n_rt)).reshape(1, LANES)
    xe, counts, meta = _mix_combine(z, x2d, seq, mixer_w, w_branch_a, w_branch_b, w_out, row(norm2_g), wr, br,
                                    tm_cmb, n_groups, epg)

    n_buckets = n_groups * _N_PAIRS
    tok_bucket, tok_rank = meta[0], meta[1]
    bucket_cnt = counts[0, :n_buckets].astype(jnp.int32)

    x2s = _moe(tok_bucket, tok_rank, bucket_cnt, xe, row(norm2_g), w_e_gate, w_e_up, w_e_down,
               tm_moe, n_groups, epg)
    return _ple(tok_bucket, tok_rank, bucket_cnt, x2s, p2d, row(norm_ple_g), w_ple_gate,
                w_ple_proj.astype(_BF16), row(out_norm_g), tm_ple, t, tm_moe)


def kernel(x, p, norm1_g, w_in, conv_w, conv_b, w_rg_a, b_rg_a, w_rg_x, b_rg_x, lru_lambda, w_pool, pool_scale, w_branch_a, w_branch_b, w_out, norm2_g, w_router_group, b_router_group, w_router_expert, b_router_expert, w_e_gate, w_e_up, w_e_down, norm_ple_g, w_ple_gate, w_ple_proj, final_norm_g):
    bsz, seq, d = x.shape
    depth = p.shape[0]
    assert depth == 1, "the final RMSNorm is fused into the last layer's embedding kernel"
    out = _layer(x.reshape(bsz * seq, d), p[0].reshape(bsz * seq, -1), bsz, seq,
                 norm1_g[0], w_in[0], conv_w[0], conv_b[0], w_rg_a[0], b_rg_a[0], w_rg_x[0], b_rg_x[0],
                 lru_lambda[0], w_pool[0], pool_scale[0], w_branch_a[0], w_branch_b[0], w_out[0],
                 norm2_g[0], w_router_group[0], b_router_group[0], w_router_expert[0],
                 b_router_expert[0], w_e_gate[0], w_e_up[0], w_e_down[0], norm_ple_g[0],
                 w_ple_gate[0], w_ple_proj[0], final_norm_g)
    return out.reshape(bsz, seq, d)
```

```python
import functools

import jax
import jax.numpy as jnp
from jax import lax
from jax.experimental import pallas as pl
from jax.experimental.pallas import tpu as pltpu

EPS = 1e-6
LRU_C = 8.0
CONV_WIDTH = 4
POOL_WINDOWS = (2, 4, 8, 16)
TOP_K = 2
SQRT_GUARD = 1e-30
WEIGHT_STAGE_ROWS = 256

LANES = 128
SUBLANES = 8
MXU_DIM = 256
VMEM_LIMIT_BYTES = 56 * 1024 * 1024

_BF16 = jnp.bfloat16
_F32 = jnp.float32

_PAIR_SLOT_A = (0, 0, 0, 1, 1, 3)
_PAIR_SLOT_B = (1, 2, 3, 3, 2, 2)
_N_PAIRS = len(_PAIR_SLOT_A)


def _sigmoid(x):
    return 0.5 * jnp.tanh(0.5 * x) + 0.5


def _rms(x, g):
    ms = jnp.mean(x * x, axis=-1, keepdims=True)
    return x * lax.rsqrt(ms + EPS) * g


def _load_cast_weight(w_hbm, w_s, stage, sem):
    rows = stage.shape[1]
    n_chunks = w_hbm.shape[0] // rows

    def chunk_copy(c):
        return pltpu.make_async_copy(w_hbm.at[pl.ds(c * rows, rows)], stage.at[c % 2], sem.at[c % 2])

    chunk_copy(0).start()
    for c in range(n_chunks):
        if c + 1 < n_chunks:
            chunk_copy(c + 1).start()
        chunk_copy(c).wait()
        w_s[c * rows:(c + 1) * rows, :] = stage[c % 2].astype(_BF16)


def _const_spec(shape):
    nd = len(shape)
    return pl.BlockSpec(shape, lambda *_: (0,) * nd, pipeline_mode=pl.Buffered(1))


def _params(n_axes):
    return pltpu.CompilerParams(dimension_semantics=("arbitrary",) * n_axes,
                                vmem_limit_bytes=VMEM_LIMIT_BYTES)


def _inproj_kernel(x_hbm, g_ref, w_ref, z_ref, h_ref, xs_ref, sem, *, tm, rc):
    i = pl.program_id(0)
    j = pl.program_id(1)
    n_chunks = tm // rc
    more_tiles = i + 1 < pl.num_programs(0)

    def chunk_copy(tile, c):
        return pltpu.make_async_copy(x_hbm.at[pl.ds(pl.multiple_of(tile * tm + c * rc, rc), rc)],
                                     xs_ref.at[c % 2], sem.at[c % 2])

    def normalise(tile, c):
        h_ref[tile % 2, pl.ds(pl.multiple_of(c * rc, rc), rc), :] = _rms(xs_ref[c % 2], g_ref[...]).astype(_BF16)

    @pl.when(jnp.logical_and(i == 0, j == 0))
    def _():
        chunk_copy(0, 0).start()
        for c in range(n_chunks):
            if c + 1 < n_chunks:
                chunk_copy(0, c + 1).start()
            chunk_copy(0, c).wait()
            normalise(0, c)

    @pl.when(jnp.logical_and(more_tiles, jnp.logical_and(j >= 1, j <= n_chunks)))
    def _():
        chunk_copy(i + 1, j - 1).wait()
        normalise(i + 1, j - 1)

    @pl.when(jnp.logical_and(more_tiles, j < n_chunks))
    def _():
        chunk_copy(i + 1, j).start()

    z_ref[...] = jnp.dot(h_ref[i % 2], w_ref[...].astype(_BF16), preferred_element_type=_F32)


def _inproj(x2d, g, w, tm, tn, rc):
    t, d = x2d.shape
    n = w.shape[1]
    kern = functools.partial(_inproj_kernel, tm=tm, rc=rc)
    assert n // tn > tm // rc, "a row tile's chunks are prepared during the column steps of the previous tile"
    return pl.pallas_call(
        kern,
        grid=(t // tm, n // tn),
        in_specs=[pl.BlockSpec(memory_space=pl.ANY),
                  pl.BlockSpec((1, d), lambda i, j: (0, 0)),
                  pl.BlockSpec((d, tn), lambda i, j: (0, j))],
        out_specs=pl.BlockSpec((tm, tn), lambda i, j: (i, j)),
        out_shape=jax.ShapeDtypeStruct((t, n), _F32),
        scratch_shapes=[pltpu.VMEM((2, tm, d), _BF16),
                        pltpu.VMEM((2, rc, d), _F32),
                        pltpu.SemaphoreType.DMA((2,))],
        compiler_params=_params(2),
        name="inproj",
    )(x2d, g, w)


def _scan_pitch(tm):
    p = -(-tm // SUBLANES)
    while p % SUBLANES != 4:
        p += 1
    return p


def _mixer_init(er_ref, ep_ref, a_ref, b_ref, car_ref, tm):
    er_ref[...] = jnp.zeros(er_ref.shape, _F32)
    ep_ref[...] = jnp.zeros(ep_ref.shape, _F32)
    car_ref[...] = jnp.zeros(car_ref.shape, _F32)
    a_ref[:, tm:, :] = jnp.ones((a_ref.shape[0], a_ref.shape[1] - tm, LANES), _F32)
    b_ref[:, tm:, :] = jnp.zeros((b_ref.shape[0], b_ref.shape[1] - tm, LANES), _F32)


def _mixer_tile(z_ref, cw_ref, cb_ref, wa_ref, ba_ref, wx_ref, bx_ref, lam_ref, wp_ref, ps_ref,
                ya_ref, yb_ref, er_ref, ep_ref, a_ref, b_ref, h_ref, car_ref, *, first, t0, tm, pitch):
    c = cb_ref.shape[1]
    n_slab = c // LANES
    hist_r = SUBLANES
    hist_p = 2 * SUBLANES

    er_ref[0:hist_r, :] = jnp.where(first, 0.0, er_ref[tm:tm + hist_r, :])
    ep_ref[0:hist_p, :] = jnp.where(first, 0.0, ep_ref[tm:tm + hist_p, :])
    er_ref[hist_r:hist_r + tm, :] = z_ref[:, 0:c]
    ep_ref[hist_p:hist_p + tm, :] = z_ref[:, 2 * c:3 * c]

    kvec = -LRU_C * jax.nn.softplus(-lam_ref[...])
    nblk = c // MXU_DIM
    for k in range(nblk):
        cs = slice(k * MXU_DIM, (k + 1) * MXU_DIM)
        xc = cb_ref[:, cs] + cw_ref[CONV_WIDTH - 1:CONV_WIDTH, cs] * er_ref[hist_r:hist_r + tm, cs]
        for j in range(1, CONV_WIDTH):
            xc = xc + cw_ref[CONV_WIDTH - 1 - j:CONV_WIDTH - j, cs] * er_ref[hist_r - j:hist_r - j + tm, cs]
        xcb = xc.astype(_BF16)
        r = _sigmoid(jnp.dot(xcb, wa_ref[k], preferred_element_type=_F32) + ba_ref[:, cs])
        ig = _sigmoid(jnp.dot(xcb, wx_ref[k], preferred_element_type=_F32) + bx_ref[:, cs])
        log_a = r * kvec[:, cs]
        a = jnp.exp(log_a)
        v = 1.0 - a * a
        mult = v * lax.rsqrt(jnp.maximum(v, SQRT_GUARD))
        bb = mult * ig * xc
        for q in range(MXU_DIM // LANES):
            slab = k * (MXU_DIM // LANES) + q
            a_ref[slab, 0:tm, :] = a[:, q * LANES:(q + 1) * LANES]
            b_ref[slab, 0:tm, :] = bb[:, q * LANES:(q + 1) * LANES]
        yield

    def seg(i):
        return pl.ds(i, SUBLANES, stride=pitch)

    row = lax.broadcasted_iota(jnp.int32, (SUBLANES, LANES), 0)
    for sl in range(n_slab):
        hh = jnp.zeros((SUBLANES, LANES), _F32)
        aa = jnp.ones((SUBLANES, LANES), _F32)
        for i in range(pitch):
            av = a_ref[sl, seg(i), :]
            hh = av * hh + b_ref[sl, seg(i), :]
            aa = av * aa
        d = 1
        while d < SUBLANES:
            hs_ = jnp.where(row >= d, pltpu.roll(hh, d, 0), 0.0)
            as_ = jnp.where(row >= d, pltpu.roll(aa, d, 0), 1.0)
            hh = aa * hs_ + hh
            aa = aa * as_
            d *= 2
        cs = slice(sl * LANES, (sl + 1) * LANES)
        cin = jnp.where(first, 0.0, car_ref[:, cs])
        full = hh + aa * cin
        hv = jnp.where(row >= 1, pltpu.roll(full, 1, 0), cin)
        car_ref[:, cs] = jnp.broadcast_to(full[SUBLANES - 1:SUBLANES, :], (SUBLANES, LANES))
        for i in range(pitch):
            hv = a_ref[sl, seg(i), :] * hv + b_ref[sl, seg(i), :]
            h_ref[sl, seg(i), :] = hv
        g = z_ref[:, c + sl * LANES:c + (sl + 1) * LANES]
        ya_ref[:, cs] = (h_ref[sl, 0:tm, :] * jax.nn.gelu(g)).astype(_BF16)
        yield

    t_idx = (t0 + lax.broadcasted_iota(jnp.int32, (tm, 1), 0) + 1).astype(_F32)
    n_grp = len(POOL_WINDOWS)
    gd = c // n_grp
    for gi, w in enumerate(POOL_WINDOWS):
        cs = slice(gi * gd, (gi + 1) * gd)
        e = ep_ref[:, cs]
        acc = e
        d = 1
        while d < w:
            acc = acc + pltpu.roll(acc, d, 0)
            d *= 2
        xt = e[hist_p:, :]
        cnt = jnp.minimum(t_idx, float(w))
        dd = acc[hist_p:, :] / cnt - xt
        yb = jnp.dot(dd.astype(_BF16), wp_ref[gi], preferred_element_type=_F32) * ps_ref[:, cs]
        yb_ref[:, cs] = yb.astype(_BF16)
        yield


def _combine_tile(ya_ref, yb_ref, ga0_ref, ga1_ref, gb0_ref, gb1_ref, x_ref,
                  wa_ref, wb_ref, wo_ref, n2_ref, wr_ref, br_ref,
                  xe_ref, cnt_ref, meta_ref, cnt_s, *, live, tm, n_groups, epg):
    d = x_ref.shape[1]
    half = d // 2
    piece = MXU_DIM
    ya = ya_ref[...]
    yb = yb_ref[...]
    us = []
    for q in range(d // piece):
        cs = slice(q * piece, (q + 1) * piece)
        ga_ref, gb_ref = (ga0_ref, gb0_ref) if q * piece < half else (ga1_ref, gb1_ref)
        gs = slice((q * piece) % half, (q * piece) % half + piece)
        pa = jnp.dot(ya, wa_ref[:, cs], preferred_element_type=_F32)
        yield
        pb = jnp.dot(yb, wb_ref[:, cs], preferred_element_type=_F32)
        yield
        ta = jnp.tanh(0.5 * ga_ref[:, gs])
        tb = jnp.tanh(0.5 * gb_ref[:, gs])
        us.append((0.5 * ((pa + pb) + (ta * pa + tb * pb))).astype(_BF16))
    u = jnp.concatenate(us, axis=1)
    for q in range(d // MXU_DIM):
        cs = slice(q * MXU_DIM, (q + 1) * MXU_DIM)
        xe_ref[:, cs] = x_ref[:, cs] + jnp.dot(u, wo_ref[:, cs], preferred_element_type=_F32)
        yield
    x1 = xe_ref[:, 0:d]

    ht = _rms(x1, n2_ref[...]).astype(_BF16)
    logits = jnp.dot(ht, wr_ref[...], preferred_element_type=_F32) + br_ref[...]

    lane = lax.broadcasted_iota(jnp.int32, (tm, LANES), 1).astype(_F32)
    ninf = -jnp.inf
    big = float(LANES)

    def first_argmax(v):
        m = jnp.max(v, axis=-1, keepdims=True)
        return m, jnp.min(jnp.where(v == m, lane, big), axis=-1, keepdims=True)

    is_g = lane < float(n_groups)
    gmax, gidx = first_argmax(jnp.where(is_g, logits, ninf))
    g_w = 1.0 / jnp.sum(jnp.where(is_g, jnp.exp(logits - gmax), 0.0), axis=-1, keepdims=True)
    lo_lane = float(n_groups) + float(epg) * gidx
    in_grp = (lane >= lo_lane) & (lane < lo_lane + float(epg))
    le = jnp.where(in_grp, logits, ninf)
    m1, i1 = first_argmax(le)
    m2, i2 = first_argmax(jnp.where(lane == i1, ninf, le))
    e21 = jnp.exp(m2 - m1)
    w1 = g_w / (1.0 + e21)
    w2 = w1 * e21
    e1 = i1 - lo_lane
    e2 = i2 - lo_lane
    lo = jnp.minimum(e1, e2)
    hi = jnp.maximum(e1, e2)
    w_lo = jnp.where(e1 < e2, w1, w2)
    w_hi = jnp.where(e1 < e2, w2, w1)
    pair = jnp.where(lo == 0.0, hi - 1.0, jnp.where(lo == 1.0, 6.0 - hi, 5.0))
    swap = pair == 5.0
    w_a = jnp.where(swap, w_hi, w_lo)
    w_b = jnp.where(swap, w_lo, w_hi)
    bucket = float(_N_PAIRS) * gidx + pair

    onehot = lane == bucket
    oh_bf = jnp.where(onehot, 1.0, 0.0).astype(_BF16)
    rr = lax.broadcasted_iota(jnp.int32, (tm, tm), 0)
    cc = lax.broadcasted_iota(jnp.int32, (tm, tm), 1)
    tri = jnp.where(cc < rr, 1.0, 0.0).astype(_BF16)
    before = jnp.dot(tri, oh_bf, preferred_element_type=_F32) + cnt_s[...]
    rank = jnp.sum(jnp.where(onehot, before, 0.0), axis=-1, keepdims=True)
    cnt_s[...] = cnt_s[...] + jnp.where(live, jnp.sum(jnp.where(onehot, 1.0, 0.0), axis=0, keepdims=True), 0.0)
    cnt_ref[...] = cnt_s[...]

    info = jnp.where(lane == 0.0, bucket,
                     jnp.where(lane == 1.0, rank,
                               jnp.where(lane == 2.0, w_a, jnp.where(lane == 3.0, w_b, 0.0))))
    xe_ref[:, d:d + LANES] = info
    meta_ref[...] = jnp.transpose(info)[0:SUBLANES, :].astype(jnp.int32)


def _mix_combine_kernel(*refs, tm, pitch, ns, n_tiles, n_groups, epg):
    mix_in, cmb_in = refs[0:10], refs[10:21]
    xe_ref, cnt_ref, meta_ref = refs[21:24]
    (ynew_ref, yold_ref, er_ref, ep_ref, a_ref, b_ref, h_ref, car_ref, cnt_s,
     wa_s, wb_s, wo_s, wstage, wsem) = refs[24:]
    s = pl.program_id(0)
    dense_hbm = cmb_in[5:8]
    cmb_in = cmb_in[0:5] + (wa_s, wb_s, wo_s) + cmb_in[8:]

    @pl.when(s == 0)
    def _():
        for w_hbm, w_s in zip(dense_hbm, (wa_s, wb_s, wo_s)):
            _load_cast_weight(w_hbm, w_s, wstage, wsem)
        _mixer_init(er_ref, ep_ref, a_ref, b_ref, car_ref, tm)
        cnt_s[...] = jnp.zeros(cnt_s.shape, _F32)

    seq_pos = lax.rem(jnp.minimum(s, n_tiles - 1), ns)

    def run(with_mixer, with_combine):
        stages = []
        if with_mixer:
            stages.append(_mixer_tile(*mix_in, ynew_ref.at[0], ynew_ref.at[1], er_ref, ep_ref, a_ref, b_ref,
                                      h_ref, car_ref, first=seq_pos == 0, t0=seq_pos * tm, tm=tm, pitch=pitch))
        if with_combine:
            stages.append(_combine_tile(yold_ref.at[0], yold_ref.at[1], *cmb_in, xe_ref, cnt_ref, meta_ref,
                                        cnt_s, live=True, tm=tm, n_groups=n_groups, epg=epg))
        while stages:
            for stage in list(stages):
                if next(stage, stages) is stages:
                    stages.remove(stage)
        if with_mixer:
            yold_ref[...] = ynew_ref[...]

    pl.when(s == 0)(functools.partial(run, True, False))
    pl.when(jnp.logical_and(s > 0, s < n_tiles))(functools.partial(run, True, True))
    pl.when(s == n_tiles)(functools.partial(run, False, True))


def _mix_combine(z, x2d, seq, mixer_w, wa, wb, wo, n2, wr, br, tm, n_groups, epg):
    t, d = x2d.shape
    c = wa.shape[0]
    half = d // 2
    n_tiles = t // tm
    ns = seq // tm
    pitch = _scan_pitch(tm)
    n_slab = c // LANES
    off = (3 * c) // half
    kern = functools.partial(_mix_combine_kernel, tm=tm, pitch=pitch, ns=ns, n_tiles=n_tiles,
                             n_groups=n_groups, epg=epg)

    def cur(s):
        return jnp.minimum(s, n_tiles - 1)

    def prev(s):
        return jnp.maximum(s - 1, 0)

    def zspec(j):
        return pl.BlockSpec((tm, half), lambda s, j=j: (prev(s), off + j))

    return pl.pallas_call(
        kern,
        grid=(n_tiles + 1,),
        in_specs=[pl.BlockSpec((tm, 3 * c), lambda s: (cur(s), 0))]
                 + [_const_spec(w.shape) for w in mixer_w]
                 + [zspec(0), zspec(1), zspec(2), zspec(3),
                    pl.BlockSpec((tm, d), lambda s: (prev(s), 0)),
                    pl.BlockSpec(memory_space=pl.ANY), pl.BlockSpec(memory_space=pl.ANY),
                    pl.BlockSpec(memory_space=pl.ANY),
                    _const_spec(n2.shape), _const_spec(wr.shape), _const_spec(br.shape)],
        out_specs=[pl.BlockSpec((tm, d + LANES), lambda s: (prev(s), 0)),
                   pl.BlockSpec((1, LANES), lambda s: (0, 0)),
                   pl.BlockSpec((SUBLANES, tm), lambda s: (0, prev(s)))],
        out_shape=[jax.ShapeDtypeStruct((t, d + LANES), _F32),
                   jax.ShapeDtypeStruct((1, LANES), _F32),
                   jax.ShapeDtypeStruct((SUBLANES, t), jnp.int32)],
        scratch_shapes=[pltpu.VMEM((2, tm, c), _BF16),
                        pltpu.VMEM((2, tm, c), _BF16),
                        pltpu.VMEM((SUBLANES + tm, c), _F32),
                        pltpu.VMEM((2 * SUBLANES + tm, c), _F32),
                        pltpu.VMEM((n_slab, SUBLANES * pitch, LANES), _F32),
                        pltpu.VMEM((n_slab, SUBLANES * pitch, LANES), _F32),
                        pltpu.VMEM((n_slab, SUBLANES * pitch, LANES), _F32),
                        pltpu.VMEM((SUBLANES, c), _F32),
                        pltpu.VMEM((1, LANES), _F32),
                        pltpu.VMEM(wa.shape, _BF16), pltpu.VMEM(wb.shape, _BF16), pltpu.VMEM(wo.shape, _BF16),
                        pltpu.VMEM((2, WEIGHT_STAGE_ROWS, d), _F32),
                        pltpu.SemaphoreType.DMA((2,))],
        compiler_params=_params(1),
        name="mix_combine",
    )(z, *mixer_w, z, z, z, z, x2d, wa, wb, wo, n2, wr, br)


def _bucket_starts(cnt_ref, start_ref, n_buckets, tm, on_tile=None):
    def bucket_body(b, tile_idx):
        start_ref[b] = tile_idx * tm
        nt = lax.div(cnt_ref[b] + (tm - 1), tm)
        if on_tile is not None:
            lax.fori_loop(0, nt, lambda k, carry: (on_tile(tile_idx + k, b), carry)[1], 0)
        return tile_idx + nt

    return lax.fori_loop(0, n_buckets, bucket_body, 0)


def _row_gather_start(row_of, src_hbm, buf, slot, sem, n_rows, unrolled):
    def start(r, priority):
        row = row_of(r)
        pltpu.make_async_copy(src_hbm.at[pl.ds(row, 1)], buf.at[slot, pl.ds(r, 1)],
                              sem.at[slot]).start(priority=priority)

    if unrolled:
        for r in range(n_rows):
            start(r, r % 2)
    else:
        def body(r, _):
            start(2 * r, 0)
            start(2 * r + 1, 1)
            return 0
        lax.fori_loop(0, n_rows // 2, body, 0, unroll=4)


def _row_gather_wait(src_hbm, buf, slot, sem, n_rows):
    pltpu.make_async_copy(src_hbm.at[pl.ds(0, n_rows)], buf.at[slot], sem.at[slot]).wait()


MOE_WEIGHT_SLOTS = 6
MOE_PARTS = 4
MOE_CHUNKS = 3 * MOE_PARTS
MOE_PUMP = 3
MOE_DEPTH = 4
MOE_GATHER_AHEAD = 1
MOE_PLACE_SEGMENT = 512


def _moe_weight_plan(n_groups, epg):
    first_use = {e: min(p for p in range(_N_PAIRS) if e in (_PAIR_SLOT_A[p], _PAIR_SLOT_B[p])) for e in range(epg)}
    last_use = {e: max(p for p in range(_N_PAIRS) if e in (_PAIR_SLOT_A[p], _PAIR_SLOT_B[p])) for e in range(epg)}
    free_after = [-1] * MOE_WEIGHT_SLOTS
    loads, slot_of = [], {}
    for g in range(n_groups):
        for e in sorted(range(epg), key=lambda e: (first_use[e], e)):
            needed_by = g * _N_PAIRS + first_use[e]
            s = min(range(MOE_WEIGHT_SLOTS), key=lambda s: (free_after[s], s))
            assert free_after[s] < needed_by
            loads.append((g * epg + e, s, free_after[s], needed_by))
            free_after[s] = g * _N_PAIRS + last_use[e]
            slot_of[(g, e)] = s
    n_buckets = n_groups * _N_PAIRS
    need = [sum(1 for l in loads if l[3] <= b) for b in range(n_buckets)]
    allow = [sum(1 for l in loads if l[2] < b) for b in range(n_buckets)]
    slot_a = [slot_of[(b // _N_PAIRS, _PAIR_SLOT_A[b % _N_PAIRS])] for b in range(n_buckets)]
    slot_b = [slot_of[(b // _N_PAIRS, _PAIR_SLOT_B[b % _N_PAIRS])] for b in range(n_buckets)]
    return [l[0] for l in loads], [l[1] for l in loads], need, allow, slot_a, slot_b


def _moe_kernel(tokb_ref, tokr_ref, cntb_ref, need_ref, allow_ref, sa_ref, sb_ref, le_ref, ls_ref,
                xe_hbm, n2_ref, wg_hbm, wu_hbm, wd_hbm,
                x2s_ref, xbuf, gsem, wg_s, wu_s, wd_s, st_gu, st_d, wsem, cnt, src_s, tbk_s, start_s, tot_s,
                *, tm):
    j = pl.program_id(0)
    slot = lax.rem(j, MOE_GATHER_AHEAD + 1)
    d = x2s_ref.shape[1]
    t_rows = xe_hbm.shape[0]
    n_buckets = start_s.shape[0]

    def src_row(tile):
        return lambda r: src_s[tile * tm + r]

    rows_gu = wg_hbm.shape[1] // MOE_PARTS
    rows_d = wd_hbm.shape[1] // MOE_PARTS

    def chunk_dma(c, kind):
        load = c // MOE_CHUNKS
        part = c % MOE_PARTS
        e = le_ref[load]
        if kind == 2:
            return pltpu.make_async_copy(wd_hbm.at[e, pl.ds(part * rows_d, rows_d)], st_d.at[c % MOE_DEPTH],
                                         wsem.at[c % MOE_DEPTH])
        src = wg_hbm if kind == 0 else wu_hbm
        return pltpu.make_async_copy(src.at[e, pl.ds(part * rows_gu, rows_gu)], st_gu.at[c % MOE_DEPTH],
                                     wsem.at[c % MOE_DEPTH])

    def for_kind(c, fn):
        kind = (c % MOE_CHUNKS) // MOE_PARTS
        for k in range(3):
            @pl.when(kind == k)
            def _(k=k):
                fn(k)

    def issue_one(allowed):
        @pl.when(cnt[0] < jnp.minimum(allowed, cnt[1] + MOE_DEPTH))
        def _():
            c = cnt[0]
            for_kind(c, lambda k: chunk_dma(c, k).start())
            cnt[0] = c + 1

    def retire():
        c = cnt[1]
        s = ls_ref[c // MOE_CHUNKS]
        part = c % MOE_PARTS

        def finish(k):
            chunk_dma(c, k).wait()
            if k == 2:
                wd_s[s, pl.ds(part * rows_d, rows_d), :] = st_d[c % MOE_DEPTH].astype(_BF16)
            else:
                dst = wg_s if k == 0 else wu_s
                dst[s, pl.ds(part * rows_gu, rows_gu), :] = st_gu[c % MOE_DEPTH].astype(_BF16)

        for_kind(c, finish)
        cnt[1] = c + 1

    def pump(required, allowed, extra):
        n_iter = jnp.maximum(required - cnt[1], jnp.minimum(extra, allowed - cnt[1]))
        n_fill = jnp.minimum(allowed, cnt[1] + MOE_DEPTH) - cnt[0]
        lax.fori_loop(0, jnp.maximum(n_fill, 0), lambda _, carry: (issue_one(allowed), carry)[1], 0)

        def body(_, carry):
            retire()
            issue_one(allowed)
            return carry

        lax.fori_loop(0, jnp.maximum(n_iter, 0), body, 0)

    @pl.when(j == 0)
    def _():
        cnt[0] = 0
        cnt[1] = 0

        def set_tile(tile, b):
            tbk_s[tile] = b

        total = _bucket_starts(cntb_ref, start_s, n_buckets, tm, on_tile=set_tile)
        tot_s[0] = total

        def fill(lo, hi):
            def body(q, carry):
                src_s[q] = jnp.minimum(jnp.where(q >= t_rows, q - t_rows, q), t_rows - 1)
                return carry
            lax.fori_loop(lo, hi, body, 0)

        def fill_bucket(b, carry):
            nxt = jnp.where(b + 1 < n_buckets, start_s[jnp.minimum(b + 1, n_buckets - 1)], total * tm)
            fill(start_s[b] + cntb_ref[b], nxt)
            return carry
        lax.fori_loop(0, n_buckets, fill_bucket, 0)
        fill(total * tm, (total + MOE_GATHER_AHEAD) * tm)

        def place(tok, carry):
            src_s[start_s[tokb_ref[tok]] + tokr_ref[tok]] = tok
            return carry

        first_allowed = allow_ref[tbk_s[0]] * MOE_CHUNKS
        seg = min(MOE_PLACE_SEGMENT, t_rows)

        def place_segment(sg, carry):
            lax.fori_loop(0, seg, lambda i, c: place(sg * seg + i, c), 0, unroll=16)
            pump(0, first_allowed, 2)
            return carry
        lax.fori_loop(0, t_rows // seg, place_segment, 0)

        for tile in range(MOE_GATHER_AHEAD):
            _row_gather_start(src_row(tile), xe_hbm, xbuf, tile, gsem, tm, unrolled=False)

    n_valid = tot_s[0]

    @pl.when(j < n_valid)
    def _():
        b = tbk_s[j]
        pump(need_ref[b] * MOE_CHUNKS, allow_ref[b] * MOE_CHUNKS, MOE_PUMP)
        s_a = sa_ref[b]
        s_b = sb_ref[b]

        def tile_body(cur):
            _row_gather_wait(xe_hbm, xbuf, cur, gsem, tm)
            _row_gather_start(src_row(j + MOE_GATHER_AHEAD), xe_hbm, xbuf,
                              (cur + MOE_GATHER_AHEAD) % (MOE_GATHER_AHEAD + 1), gsem, tm, unrolled=True)
            xe = xbuf[cur]
            x1 = xe[:, 0:d]
            w_a = xe[:, d + 2:d + 3]
            w_b = xe[:, d + 3:d + 4]
            ht = _rms(x1, n2_ref[...]).astype(_BF16)

            def expert(s, wgt):
                hg = jnp.dot(ht, wg_s[s], preferred_element_type=_F32)
                hu = jnp.dot(ht, wu_s[s], preferred_element_type=_F32)
                half = 0.5 * hg
                return ((half + half * jnp.tanh(half)) * hu * wgt).astype(_BF16)

            y = jnp.dot(expert(s_a, w_a), wd_s[s_a], preferred_element_type=_F32)
            y = y + jnp.dot(expert(s_b, w_b), wd_s[s_b], preferred_element_type=_F32)
            x2s_ref[...] = x1 + y

        for cur in range(MOE_GATHER_AHEAD + 1):
            pl.when(slot == cur)(functools.partial(tile_body, cur))

    @pl.when(j >= n_valid)
    def _():
        x2s_ref[...] = jnp.zeros(x2s_ref.shape, _F32)

        @pl.when(j == n_valid)
        def _():
            for ahead in range(MOE_GATHER_AHEAD):
                _row_gather_wait(xe_hbm, xbuf, lax.rem(j + ahead, MOE_GATHER_AHEAD + 1), gsem, tm)
            lax.fori_loop(0, cnt[0] - cnt[1], lambda _, carry: (retire(), carry)[1], 0)


def _moe(tok_bucket, tok_rank, bucket_cnt, xe, n2, wg, wu, wd, tm, n_groups, epg):
    n_buckets = bucket_cnt.shape[0]
    n_tiles = xe.shape[0] // tm + n_buckets - 1 + MOE_GATHER_AHEAD
    d = wg.shape[1]
    f = wg.shape[2]
    kern = functools.partial(_moe_kernel, tm=tm)
    le, ls, need, allow, slot_a, slot_b = (jnp.asarray(v, jnp.int32) for v in _moe_weight_plan(n_groups, epg))

    grid_spec = pltpu.PrefetchScalarGridSpec(
        num_scalar_prefetch=9,
        grid=(n_tiles,),
        in_specs=[pl.BlockSpec(memory_space=pl.ANY),
                  pl.BlockSpec((1, d), lambda j, *_: (0, 0)),
                  pl.BlockSpec(memory_space=pl.ANY),
                  pl.BlockSpec(memory_space=pl.ANY),
                  pl.BlockSpec(memory_space=pl.ANY)],
        out_specs=pl.BlockSpec((tm, d), lambda j, *_: (j, 0)),
        scratch_shapes=[pltpu.VMEM((MOE_GATHER_AHEAD + 1, tm, d + LANES), _F32),
                        pltpu.SemaphoreType.DMA((MOE_GATHER_AHEAD + 1,)),
                        pltpu.VMEM((MOE_WEIGHT_SLOTS, d, f), _BF16),
                        pltpu.VMEM((MOE_WEIGHT_SLOTS, d, f), _BF16),
                        pltpu.VMEM((MOE_WEIGHT_SLOTS, f, d), _BF16),
                        pltpu.VMEM((MOE_DEPTH, d // MOE_PARTS, f), _F32),
                        pltpu.VMEM((MOE_DEPTH, f // MOE_PARTS, d), _F32),
                        pltpu.SemaphoreType.DMA((MOE_DEPTH,)),
                        pltpu.SMEM((2,), jnp.int32),
                        pltpu.SMEM((n_tiles * tm,), jnp.int32),
                        pltpu.SMEM((n_tiles,), jnp.int32),
                        pltpu.SMEM((n_buckets,), jnp.int32),
                        pltpu.SMEM((1,), jnp.int32)],
    )
    return pl.pallas_call(
        kern,
        grid_spec=grid_spec,
        out_shape=jax.ShapeDtypeStruct((n_tiles * tm, d), _F32),
        compiler_params=_params(1),
        name="moe",
    )(tok_bucket, tok_rank, bucket_cnt, need, allow, slot_a, slot_b, le, ls, xe, n2, wg, wu, wd)


def _ple_kernel(tokb_ref, tokr_ref, cntb_ref, x2s_hbm, p_ref, ng_ref, wg_hbm, wp_ref, nf_ref, o_ref,
                xbuf, gsem, start_s, wg_ref, wstage, wsem, *, tm, n_steps, tm_sorted):
    i = pl.program_id(0)
    slot = i % 2

    def sorted_row(tile):
        return lambda r: start_s[tokb_ref[tile * tm + r]] + tokr_ref[tile * tm + r]

    @pl.when(i == 0)
    def _():
        _bucket_starts(cntb_ref, start_s, start_s.shape[0], tm_sorted)
        _row_gather_start(sorted_row(0), x2s_hbm, xbuf, 0, gsem, tm, unrolled=False)
        _load_cast_weight(wg_hbm, wg_ref, wstage, wsem)

    nxt = jnp.where(i + 1 == n_steps, 0, i + 1)

    def step(cur):
        _row_gather_wait(x2s_hbm, xbuf, cur, gsem, tm)
        _row_gather_start(sorted_row(nxt), x2s_hbm, xbuf, 1 - cur, gsem, tm, unrolled=True)
        x2 = xbuf[cur]
        g = _sigmoid(jnp.dot(_rms(x2, ng_ref[...]).astype(_BF16), wg_ref[...], preferred_element_type=_F32))
        e = jnp.dot(p_ref[...].astype(_BF16), wp_ref[...], preferred_element_type=_F32)
        o_ref[...] = _rms(x2 + g * e, nf_ref[...])

        @pl.when(i == n_steps - 1)
        def _():
            _row_gather_wait(x2s_hbm, xbuf, 1 - cur, gsem, tm)

    for parity in range(2):
        pl.when(slot == parity)(functools.partial(step, parity))


def _ple(tok_bucket, tok_rank, bucket_cnt, x2s, p2d, ng, wg, wp, nf, tm, t, tm_sorted):
    d = x2s.shape[1]
    pd = p2d.shape[1]
    n_steps = t // tm
    kern = functools.partial(_ple_kernel, tm=tm, n_steps=n_steps, tm_sorted=tm_sorted)

    def cspec(shape):
        nd = len(shape)
        return pl.BlockSpec(shape, lambda i, *_: (0,) * nd, pipeline_mode=pl.Buffered(1))

    grid_spec = pltpu.PrefetchScalarGridSpec(
        num_scalar_prefetch=3,
        grid=(n_steps,),
        in_specs=[pl.BlockSpec(memory_space=pl.ANY),
                  pl.BlockSpec((tm, pd), lambda i, *_: (i, 0)),
                  cspec(ng.shape), pl.BlockSpec(memory_space=pl.ANY), cspec(wp.shape), cspec(nf.shape)],
        out_specs=pl.BlockSpec((tm, d), lambda i, *_: (i, 0)),
        scratch_shapes=[pltpu.VMEM((2, tm, d), _F32),
                        pltpu.SemaphoreType.DMA((2,)),
                        pltpu.SMEM((bucket_cnt.shape[0],), jnp.int32),
                        pltpu.VMEM(wg.shape, _BF16),
                        pltpu.VMEM((2, WEIGHT_STAGE_ROWS, wg.shape[1]), _F32),
                        pltpu.SemaphoreType.DMA((2,))],
    )
    return pl.pallas_call(
        kern,
        grid_spec=grid_spec,
        out_shape=jax.ShapeDtypeStruct((t, d), _F32),
        compiler_params=_params(1),
        name="ple",
    )(tok_bucket, tok_rank, bucket_cnt, x2s, p2d, ng, wg, wp, nf)


def _block_diag(w, per_block):
    h, hd, _ = w.shape
    nb = h // per_block
    w4 = w.reshape(nb, per_block, hd, hd)
    rows = [jnp.pad(w4[:, p], ((0, 0), (0, 0), (p * hd, (per_block - 1 - p) * hd))) for p in range(per_block)]
    return jnp.concatenate(rows, axis=1)


def _layer(x2d, p2d, bsz, seq, norm1_g, w_in, conv_w, conv_b, w_rg_a, b_rg_a, w_rg_x, b_rg_x, lru_lambda,
           w_pool, pool_scale, w_branch_a, w_branch_b, w_out, norm2_g, w_router_group, b_router_group,
           w_router_expert, b_router_expert, w_e_gate, w_e_up, w_e_down, norm_ple_g, w_ple_gate,
           w_ple_proj, out_norm_g):
    t, d = x2d.shape
    c = conv_b.shape[0]
    heads, hd, _ = w_rg_a.shape
    n_groups = w_router_group.shape[1]
    n_exp = w_router_expert.shape[1]
    epg = n_exp // n_groups
    assert epg == 4 and TOP_K == 2 and hd * (MXU_DIM // hd) == MXU_DIM
    assert w_pool.shape[0] == len(POOL_WINDOWS) and w_pool.shape[1] == MXU_DIM

    tm_in, tn_in, rc_in = min(2048, t), 512, min(256, t)
    tm_cmb = min(256, seq)
    tm_moe = min(128, t)
    tm_ple = min(512, t)
    assert t % tm_in == 0 and seq % tm_cmb == 0 and t % tm_moe == 0 and t % tm_ple == 0 and tm_ple % 2 == 0
    assert t % min(MOE_PLACE_SEGMENT, t) == 0 and w_in.shape[1] % tn_in == 0

    row = lambda v: v.reshape(1, -1).astype(_F32)
    per_block = MXU_DIM // hd

    z = _inproj(x2d, row(norm1_g), w_in, tm_in, tn_in, rc_in)
    mixer_w = (conv_w.reshape(CONV_WIDTH, c), row(conv_b),
               _block_diag(w_rg_a, per_block).astype(_BF16), row(b_rg_a),
               _block_diag(w_rg_x, per_block).astype(_BF16), row(b_rg_x),
               row(lru_lambda), w_pool.astype(_BF16), row(pool_scale))

    n_rt = n_groups + n_exp
    wr = jnp.pad(jnp.concatenate([w_router_group, w_router_expert], axis=1),
                 ((0, 0), (0, LANES - n_rt))).astype(_BF16)
    br = jnp.pad(jnp.concatenate([b_router_group, b_router_expert]), (0, LANES - n_rt)).reshape(1, LANES)
    xe, counts, meta = _mix_combine(z, x2d, seq, mixer_w, w_branch_a, w_branch_b, w_out, row(norm2_g), wr, br,
                                    tm_cmb, n_groups, epg)

    n_buckets = n_groups * _N_PAIRS
    tok_bucket, tok_rank = meta[0], meta[1]
    bucket_cnt = counts[0, :n_buckets].astype(jnp.int32)

    x2s = _moe(tok_bucket, tok_rank, bucket_cnt, xe, row(norm2_g), w_e_gate, w_e_up, w_e_down,
               tm_moe, n_groups, epg)
    return _ple(tok_bucket, tok_rank, bucket_cnt, x2s, p2d, row(norm_ple_g), w_ple_gate,
                w_ple_proj.astype(_BF16), row(out_norm_g), tm_ple, t, tm_moe)


def kernel(x, p, norm1_g, w_in, conv_w, conv_b, w_rg_a, b_rg_a, w_rg_x, b_rg_x, lru_lambda, w_pool, pool_scale, w_branch_a, w_branch_b, w_out, norm2_g, w_router_group, b_router_group, w_router_expert, b_router_expert, w_e_gate, w_e_up, w_e_down, norm_ple_g, w_ple_gate, w_ple_proj, final_norm_g):
    bsz, seq, d = x.shape
    depth = p.shape[0]
    assert depth == 1, "the final RMSNorm is fused into the last layer's embedding kernel"
    out = _layer(x.reshape(bsz * seq, d), p[0].reshape(bsz * seq, -1), bsz, seq,
                 norm1_g[0], w_in[0], conv_w[0], conv_b[0], w_rg_a[0], b_rg_a[0], w_rg_x[0], b_rg_x[0],
                 lru_lambda[0], w_pool[0], pool_scale[0], w_branch_a[0], w_branch_b[0], w_out[0],
                 norm2_g[0], w_router_group[0], b_router_group[0], w_router_expert[0],
                 b_router_expert[0], w_e_gate[0], w_e_up[0], w_e_down[0], norm_ple_g[0],
                 w_ple_gate[0], w_ple_proj[0], final_norm_g)
    return out.reshape(bsz, seq, d)
```
